```python
import math
import jax
import jax.numpy as jnp
from jax import lax
import numpy as np

D_MODEL = 1024
BATCH = 8
SEQ = 4096
DEPTH = 1

D_INNER = 2 * D_MODEL
D_ATTN = D_INNER // 2
D_SSM = D_INNER - D_ATTN
SB_HEAD_DIM = 64
SB_HEADS = D_ATTN // SB_HEAD_DIM
SSM_HEAD_DIM = 64
SSM_HEADS = D_SSM // SSM_HEAD_DIM
SSM_GROUPS = 2
SSM_STATE = 128
CONV_WIDTH = 4
SSD_CHUNK = 128
Q_BLOCK = 128
NORM_EPS = 1e-6
D_XBC = D_SSM + 2 * SSM_GROUPS * SSM_STATE
D_PROJ = 4 * D_ATTN + D_XBC + SSM_HEADS + D_SSM
DT_MIN = 1e-3
DT_MAX = 1e-1

kernel_name = "hybrid_stickbreak_ssd_layer"


def rms_norm(x, gain):
    xf = x.astype(jnp.float32)
    y = xf * lax.rsqrt(jnp.mean(xf * xf, axis=-1, keepdims=True) + NORM_EPS)
    return (y * gain.astype(jnp.float32)).astype(x.dtype)


def stick_breaking_attention(q, k, v):
    seq = q.shape[1]
    scale = q.shape[-1] ** -0.5
    outs = []
    for blk in range(seq // Q_BLOCK):
        start = blk * Q_BLOCK
        end = start + Q_BLOCK
        z = jnp.einsum("bqhd,bkhd->bhqk", q[:, start:end], k[:, :end]).astype(jnp.float32) * scale
        t_idx = start + jnp.arange(Q_BLOCK)[:, None]
        s_idx = jnp.arange(end)[None, :]
        mask = s_idx < t_idx
        log_beta = jax.nn.log_sigmoid(z)
        log_one_minus = jnp.where(mask, jax.nn.log_sigmoid(-z), 0.0)
        tail = lax.cumsum(log_one_minus, axis=3, reverse=True) - log_one_minus
        weights = jnp.exp(jnp.where(mask, log_beta + tail, -jnp.inf))
        outs.append(jnp.einsum("bhqk,bkhd->bqhd", weights.astype(v.dtype), v[:, :end]))
    return jnp.concatenate(outs, axis=1)


def causal_depthwise_conv(u, w, b):
    ch = u.shape[-1]
    y = lax.conv_general_dilated(
        u, w[:, None, :].astype(u.dtype), window_strides=(1,),
        padding=[(CONV_WIDTH - 1, 0)], dimension_numbers=("NWC", "WIO", "NWC"),
        feature_group_count=ch)
    return y + b.astype(u.dtype)


def ssd_scan(xs, dt, a, b_in, c_in, d_skip):
    bsz, seq, nh, hp = xs.shape
    ng, ns = b_in.shape[-2:]
    hpg = nh // ng
    nc = seq // SSD_CHUNK
    cl = SSD_CHUNK
    xf = xs.astype(jnp.float32)
    xdt = (xf * dt[..., None]).reshape(bsz, nc, cl, ng, hpg, hp)
    bc = b_in.astype(jnp.float32).reshape(bsz, nc, cl, ng, ns)
    cc = c_in.astype(jnp.float32).reshape(bsz, nc, cl, ng, ns)
    log_decay = (dt * a).reshape(bsz, nc, cl, ng, hpg).transpose(0, 1, 3, 4, 2)
    a_cum = jnp.cumsum(log_decay, axis=-1)
    causal = jnp.tril(jnp.ones((cl, cl), dtype=bool))
    seg = a_cum[..., :, None] - a_cum[..., None, :]
    decay = jnp.exp(jnp.where(causal, seg, -jnp.inf))
    cb = jnp.einsum("bclgn,bcsgn->bcgls", cc, bc)
    y_diag = jnp.einsum("bcgrls,bcsgrp->bclgrp", cb[:, :, :, None] * decay, xdt)
    decay_to_end = jnp.exp(a_cum[..., -1:] - a_cum)
    chunk_states = jnp.einsum("bclgn,bcgrl,bclgrp->bcgrpn", bc, decay_to_end, xdt)
    chunk_decay = jnp.exp(a_cum[..., -1])

    def step(h, inp):
        st, dec = inp
        return h * dec[..., None, None] + st, h

    h0 = jnp.zeros((bsz, ng, hpg, hp, ns), jnp.float32)
    _, prev = lax.scan(step, h0, (jnp.moveaxis(chunk_states, 1, 0), jnp.moveaxis(chunk_decay, 1, 0)))
    prev = jnp.moveaxis(prev, 0, 1)
    y_off = jnp.einsum("bclgn,bcgrpn,bcgrl->bclgrp", cc, prev, jnp.exp(a_cum))
    y = (y_diag + y_off).reshape(bsz, seq, nh, hp)
    return y + xf * d_skip.astype(jnp.float32)[:, None]


def _fwd_setup_inputs(seed: int = 0) -> dict:
    key = jax.random.key(seed)
    ks = jax.random.split(key, 16)
    f = jnp.float32
    nrm = jax.random.normal
    x = nrm(ks[0], (BATCH, SEQ, D_MODEL), f)
    c = nrm(ks[1], (BATCH, D_MODEL), f)
    w_ada = nrm(ks[2], (DEPTH, D_MODEL, 3 * D_MODEL), f) * D_MODEL ** -0.5
    b_ada = 0.02 * nrm(ks[3], (DEPTH, 3 * D_MODEL), f)
    norm_in_gain = 1.0 + 0.1 * nrm(ks[4], (DEPTH, D_MODEL), f)
    w_in = nrm(ks[5], (DEPTH, D_MODEL, D_PROJ), f) * D_MODEL ** -0.5
    conv_w = nrm(ks[6], (DEPTH, CONV_WIDTH, D_XBC), f) * CONV_WIDTH ** -0.5
    conv_b = 0.02 * nrm(ks[7], (DEPTH, D_XBC), f)
    dt0 = jnp.exp(jax.random.uniform(ks[8], (DEPTH, SSM_HEADS), f, math.log(DT_MIN), math.log(DT_MAX)))
    dt_bias = dt0 + jnp.log(-jnp.expm1(-dt0))
    a_log = jnp.log(jax.random.uniform(ks[9], (DEPTH, SSM_HEADS), f, 1.0, 16.0))
    d_skip = 1.0 + 0.1 * nrm(ks[10], (DEPTH, SSM_HEADS), f)
    sb_norm_gain = 1.0 + 0.1 * nrm(ks[11], (DEPTH, D_ATTN), f)
    ssm_norm_gain = 1.0 + 0.1 * nrm(ks[12], (DEPTH, D_SSM), f)
    w_out = nrm(ks[13], (DEPTH, D_INNER, D_MODEL), f) * D_INNER ** -0.5
    norm_f_gain = 1.0 + 0.1 * nrm(ks[14], (D_MODEL,), f)
    return {"x": x, "c": c, "w_ada": w_ada, "b_ada": b_ada, "norm_in_gain": norm_in_gain,
            "w_in": w_in, "conv_w": conv_w, "conv_b": conv_b, "dt_bias": dt_bias,
            "a_log": a_log, "d_skip": d_skip, "sb_norm_gain": sb_norm_gain,
            "ssm_norm_gain": ssm_norm_gain, "w_out": w_out, "norm_f_gain": norm_f_gain}


def _fwd_reference(x, c, w_ada, b_ada, norm_in_gain, w_in, conv_w, conv_b, dt_bias, a_log,
              d_skip, sb_norm_gain, ssm_norm_gain, w_out, norm_f_gain):
    bsz, seq, _ = x.shape
    splits = [D_ATTN, 2 * D_ATTN, 3 * D_ATTN, 4 * D_ATTN, 4 * D_ATTN + D_XBC,
              4 * D_ATTN + D_XBC + SSM_HEADS]
    c_act = jax.nn.silu(c)
    for layer in range(DEPTH):
        mod = c_act @ w_ada[layer] + b_ada[layer]
        shift, scale, gate = jnp.split(mod, 3, axis=-1)
        h = rms_norm(x, norm_in_gain[layer]) * (1.0 + scale[:, None, :]) + shift[:, None, :]

        proj = h @ w_in[layer]
        q, k, v, z_attn, xbc, dt_raw, z_ssm = jnp.split(proj, splits, axis=-1)

        o = stick_breaking_attention(
            q.reshape(bsz, seq, SB_HEADS, SB_HEAD_DIM),
            k.reshape(bsz, seq, SB_HEADS, SB_HEAD_DIM),
            v.reshape(bsz, seq, SB_HEADS, SB_HEAD_DIM)).reshape(bsz, seq, D_ATTN)
        y_attn = rms_norm(o, sb_norm_gain[layer]) * jax.nn.silu(z_attn)

        xbc = jax.nn.silu(causal_depthwise_conv(xbc, conv_w[layer], conv_b[layer]))
        xs, b_ssm, c_ssm = jnp.split(xbc, [D_SSM, D_SSM + SSM_GROUPS * SSM_STATE], axis=-1)
        dt = jax.nn.softplus((dt_raw + dt_bias[layer]).astype(jnp.float32))
        a = -jnp.exp(a_log[layer].astype(jnp.float32))
        y = ssd_scan(xs.reshape(bsz, seq, SSM_HEADS, SSM_HEAD_DIM), dt, a,
                     b_ssm.reshape(bsz, seq, SSM_GROUPS, SSM_STATE),
                     c_ssm.reshape(bsz, seq, SSM_GROUPS, SSM_STATE), d_skip[layer])
        y = y.reshape(bsz, seq, D_SSM).astype(x.dtype)
        y_ssm = rms_norm(y * jax.nn.silu(z_ssm), ssm_norm_gain[layer])

        mixed = jnp.concatenate([y_attn, y_ssm], axis=-1) @ w_out[layer]
        x = x + gate[:, None, :] * mixed
    return rms_norm(x, norm_f_gain)


import jax as _jax
import jax.numpy as _jnp

TWIN_FORMAT = 'train_step'
FWD_PARAMS = ['x', 'c', 'w_ada', 'b_ada', 'norm_in_gain', 'w_in', 'conv_w', 'conv_b', 'dt_bias', 'a_log', 'd_skip', 'sb_norm_gain', 'ssm_norm_gain', 'w_out', 'norm_f_gain']
TWIN_WEIGHTS = ['w_ada', 'b_ada', 'norm_in_gain', 'w_in', 'conv_w', 'conv_b', 'dt_bias', 'a_log', 'd_skip', 'sb_norm_gain', 'ssm_norm_gain', 'w_out', 'norm_f_gain']
TWIN_DIFF_INPUT = 'x'
TWIN_INPUTS = ['x', 'c', 'w_ada', 'b_ada', 'norm_in_gain', 'w_in', 'conv_w', 'conv_b', 'dt_bias', 'a_log', 'd_skip', 'sb_norm_gain', 'ssm_norm_gain', 'w_out', 'norm_f_gain', 'loss_target', 'm_w_ada', 'm_b_ada', 'm_norm_in_gain', 'm_w_in', 'm_conv_w', 'm_conv_b', 'm_dt_bias', 'm_a_log', 'm_d_skip', 'm_sb_norm_gain', 'm_ssm_norm_gain', 'm_w_out', 'm_norm_f_gain', 'v_w_ada', 'v_b_ada', 'v_norm_in_gain', 'v_w_in', 'v_conv_w', 'v_conv_b', 'v_dt_bias', 'v_a_log', 'v_d_skip', 'v_sb_norm_gain', 'v_ssm_norm_gain', 'v_w_out', 'v_norm_f_gain']
TWIN_OUTPUTS = ['loss', 'grad_x', 'grad_w_ada', 'grad_b_ada', 'grad_norm_in_gain', 'grad_w_in', 'grad_conv_w', 'grad_conv_b', 'grad_dt_bias', 'grad_a_log', 'grad_d_skip', 'grad_sb_norm_gain', 'grad_ssm_norm_gain', 'grad_w_out', 'grad_norm_f_gain', 'delta_w_ada', 'delta_b_ada', 'delta_norm_in_gain', 'delta_w_in', 'delta_conv_w', 'delta_conv_b', 'delta_dt_bias', 'delta_a_log', 'delta_d_skip', 'delta_sb_norm_gain', 'delta_ssm_norm_gain', 'delta_w_out', 'delta_norm_f_gain', 'new_m_w_ada', 'new_m_b_ada', 'new_m_norm_in_gain', 'new_m_w_in', 'new_m_conv_w', 'new_m_conv_b', 'new_m_dt_bias', 'new_m_a_log', 'new_m_d_skip', 'new_m_sb_norm_gain', 'new_m_ssm_norm_gain', 'new_m_w_out', 'new_m_norm_f_gain', 'new_v_w_ada', 'new_v_b_ada', 'new_v_norm_in_gain', 'new_v_w_in', 'new_v_conv_w', 'new_v_conv_b', 'new_v_dt_bias', 'new_v_a_log', 'new_v_d_skip', 'new_v_sb_norm_gain', 'new_v_ssm_norm_gain', 'new_v_w_out', 'new_v_norm_f_gain']
TWIN_LEAF_KINDS = {'loss': 'loss', 'grad_x': 'grad_x', 'grad_w_ada': 'grad_w', 'grad_b_ada': 'grad_w', 'grad_norm_in_gain': 'grad_w', 'grad_w_in': 'grad_w', 'grad_conv_w': 'grad_w', 'grad_conv_b': 'grad_w', 'grad_dt_bias': 'grad_w', 'grad_a_log': 'grad_w', 'grad_d_skip': 'grad_w', 'grad_sb_norm_gain': 'grad_w', 'grad_ssm_norm_gain': 'grad_w', 'grad_w_out': 'grad_w', 'grad_norm_f_gain': 'grad_w', 'delta_w_ada': 'delta_w', 'delta_b_ada': 'delta_w', 'delta_norm_in_gain': 'delta_w', 'delta_w_in': 'delta_w', 'delta_conv_w': 'delta_w', 'delta_conv_b': 'delta_w', 'delta_dt_bias': 'delta_w', 'delta_a_log': 'delta_w', 'delta_d_skip': 'delta_w', 'delta_sb_norm_gain': 'delta_w', 'delta_ssm_norm_gain': 'delta_w', 'delta_w_out': 'delta_w', 'delta_norm_f_gain': 'delta_w', 'new_m_w_ada': 'new_m', 'new_m_b_ada': 'new_m', 'new_m_norm_in_gain': 'new_m', 'new_m_w_in': 'new_m', 'new_m_conv_w': 'new_m', 'new_m_conv_b': 'new_m', 'new_m_dt_bias': 'new_m', 'new_m_a_log': 'new_m', 'new_m_d_skip': 'new_m', 'new_m_sb_norm_gain': 'new_m', 'new_m_ssm_norm_gain': 'new_m', 'new_m_w_out': 'new_m', 'new_m_norm_f_gain': 'new_m', 'new_v_w_ada': 'new_v', 'new_v_b_ada': 'new_v', 'new_v_norm_in_gain': 'new_v', 'new_v_w_in': 'new_v', 'new_v_conv_w': 'new_v', 'new_v_conv_b': 'new_v', 'new_v_dt_bias': 'new_v', 'new_v_a_log': 'new_v', 'new_v_d_skip': 'new_v', 'new_v_sb_norm_gain': 'new_v', 'new_v_ssm_norm_gain': 'new_v', 'new_v_w_out': 'new_v', 'new_v_norm_f_gain': 'new_v'}


def _forward(args):
    return _fwd_reference(*[args[k] for k in FWD_PARAMS])


def _output_shape():
    out = _jax.eval_shape(lambda: _forward(_fwd_setup_inputs(0)))
    return out.shape, out.dtype

N_MICROBATCH = 1
ADAM_LR = 0.001
ADAM_B1 = 0.9
ADAM_B2 = 0.999
ADAM_EPS = 1e-08
ADAM_WD = 0.01
ADAM_STEP = 10
PER_EXAMPLE_BATCH_AXIS = {'x': 0, 'c': 0, 'loss_target': 0}
SHARED_INPUTS = []
_WEIGHT_DTYPES = {'w_ada': _jnp.float32, 'b_ada': _jnp.float32, 'norm_in_gain': _jnp.float32, 'w_in': _jnp.float32, 'conv_w': _jnp.float32, 'conv_b': _jnp.float32, 'dt_bias': _jnp.float32, 'a_log': _jnp.float32, 'd_skip': _jnp.float32, 'sb_norm_gain': _jnp.float32, 'ssm_norm_gain': _jnp.float32, 'w_out': _jnp.float32, 'norm_f_gain': _jnp.float32}
MOMENT_SCALE = {'w_ada': 3.709873e-01, 'b_ada': 6.968652e-01, 'norm_in_gain': 1.394317e-01, 'w_in': 7.031879e-02, 'conv_w': 1.087871e-01, 'conv_b': 1.268602e-01, 'dt_bias': 3.347294e-01, 'a_log': 8.540482e-01, 'd_skip': 2.946604e-01, 'sb_norm_gain': 7.686524e-02, 'ssm_norm_gain': 1.393635e-01, 'w_out': 1.609077e-01, 'norm_f_gain': 3.268178e+01}


def _to_microbatches(a, axis):
    t = _jnp.moveaxis(a, axis, 0)
    t = t.reshape((N_MICROBATCH, t.shape[0] // N_MICROBATCH) + t.shape[1:])
    return _jnp.moveaxis(t, 1, axis + 1)


def setup_inputs(seed: int = 0) -> dict:
    inp = _fwd_setup_inputs(seed)
    key = _jax.random.fold_in(_jax.random.key(seed), 7919)
    shape, _ = _output_shape()
    out = dict(inp)
    out["loss_target"] = _jax.random.normal(_jax.random.fold_in(key, 0), shape, _jnp.float32)
    for i, name in enumerate(TWIN_WEIGHTS):
        w = inp[name].astype(_jnp.float32)
        if MOMENT_SCALE is None:
            s = _jnp.sqrt(_jnp.mean(_jnp.square(w)) + 1e-30)
        else:
            s = MOMENT_SCALE[name]
        km, kv = _jax.random.split(_jax.random.fold_in(key, i + 1))
        out[name] = w
        out["m_" + name] = s * _jax.random.normal(km, w.shape, _jnp.float32)
        out["v_" + name] = (s * s) * _jax.random.uniform(kv, w.shape, _jnp.float32, 0.5, 1.5)
    if N_MICROBATCH > 1:
        for name, axis in PER_EXAMPLE_BATCH_AXIS.items():
            out[name] = _to_microbatches(out[name], axis)
    return {'x': out['x'], 'c': out['c'], 'w_ada': out['w_ada'], 'b_ada': out['b_ada'], 'norm_in_gain': out['norm_in_gain'], 'w_in': out['w_in'], 'conv_w': out['conv_w'], 'conv_b': out['conv_b'], 'dt_bias': out['dt_bias'], 'a_log': out['a_log'], 'd_skip': out['d_skip'], 'sb_norm_gain': out['sb_norm_gain'], 'ssm_norm_gain': out['ssm_norm_gain'], 'w_out': out['w_out'], 'norm_f_gain': out['norm_f_gain'], 'loss_target': out['loss_target'], 'm_w_ada': out['m_w_ada'], 'm_b_ada': out['m_b_ada'], 'm_norm_in_gain': out['m_norm_in_gain'], 'm_w_in': out['m_w_in'], 'm_conv_w': out['m_conv_w'], 'm_conv_b': out['m_conv_b'], 'm_dt_bias': out['m_dt_bias'], 'm_a_log': out['m_a_log'], 'm_d_skip': out['m_d_skip'], 'm_sb_norm_gain': out['m_sb_norm_gain'], 'm_ssm_norm_gain': out['m_ssm_norm_gain'], 'm_w_out': out['m_w_out'], 'm_norm_f_gain': out['m_norm_f_gain'], 'v_w_ada': out['v_w_ada'], 'v_b_ada': out['v_b_ada'], 'v_norm_in_gain': out['v_norm_in_gain'], 'v_w_in': out['v_w_in'], 'v_conv_w': out['v_conv_w'], 'v_conv_b': out['v_conv_b'], 'v_dt_bias': out['v_dt_bias'], 'v_a_log': out['v_a_log'], 'v_d_skip': out['v_d_skip'], 'v_sb_norm_gain': out['v_sb_norm_gain'], 'v_ssm_norm_gain': out['v_ssm_norm_gain'], 'v_w_out': out['v_w_out'], 'v_norm_f_gain': out['v_norm_f_gain']}


def _loss(weights, diff, rest, loss_target):
    with _jax.named_scope("forward"):
        args = {**rest, TWIN_DIFF_INPUT: diff, **{k: w.astype(_WEIGHT_DTYPES[k]) for k, w in weights.items()}}
        y = _forward(args)
    with _jax.named_scope("loss_head"):
        err = _jnp.square(y.astype(_jnp.float32) - loss_target)
        return 0.5 * _jnp.sum(_jnp.mean(err, axis=-1)) if err.ndim else 0.5 * err


def _adamw(w, g, m, v):
    m = ADAM_B1 * m + (1.0 - ADAM_B1) * g
    v = ADAM_B2 * v + (1.0 - ADAM_B2) * _jnp.square(g)
    m_hat = m / (1.0 - ADAM_B1 ** ADAM_STEP)
    v_hat = v / (1.0 - ADAM_B2 ** ADAM_STEP)
    delta = -ADAM_LR * (m_hat / (_jnp.sqrt(v_hat) + ADAM_EPS) + ADAM_WD * w)
    return delta, m, v


def reference(x, c, w_ada, b_ada, norm_in_gain, w_in, conv_w, conv_b, dt_bias, a_log, d_skip, sb_norm_gain, ssm_norm_gain, w_out, norm_f_gain, loss_target, m_w_ada, m_b_ada, m_norm_in_gain, m_w_in, m_conv_w, m_conv_b, m_dt_bias, m_a_log, m_d_skip, m_sb_norm_gain, m_ssm_norm_gain, m_w_out, m_norm_f_gain, v_w_ada, v_b_ada, v_norm_in_gain, v_w_in, v_conv_w, v_conv_b, v_dt_bias, v_a_log, v_d_skip, v_sb_norm_gain, v_ssm_norm_gain, v_w_out, v_norm_f_gain):
    given = dict(x=x, c=c, w_ada=w_ada, b_ada=b_ada, norm_in_gain=norm_in_gain, w_in=w_in, conv_w=conv_w, conv_b=conv_b, dt_bias=dt_bias, a_log=a_log, d_skip=d_skip, sb_norm_gain=sb_norm_gain, ssm_norm_gain=ssm_norm_gain, w_out=w_out, norm_f_gain=norm_f_gain, loss_target=loss_target, m_w_ada=m_w_ada, m_b_ada=m_b_ada, m_norm_in_gain=m_norm_in_gain, m_w_in=m_w_in, m_conv_w=m_conv_w, m_conv_b=m_conv_b, m_dt_bias=m_dt_bias, m_a_log=m_a_log, m_d_skip=m_d_skip, m_sb_norm_gain=m_sb_norm_gain, m_ssm_norm_gain=m_ssm_norm_gain, m_w_out=m_w_out, m_norm_f_gain=m_norm_f_gain, v_w_ada=v_w_ada, v_b_ada=v_b_ada, v_norm_in_gain=v_norm_in_gain, v_w_in=v_w_in, v_conv_w=v_conv_w, v_conv_b=v_conv_b, v_dt_bias=v_dt_bias, v_a_log=v_a_log, v_d_skip=v_d_skip, v_sb_norm_gain=v_sb_norm_gain, v_ssm_norm_gain=v_ssm_norm_gain, v_w_out=v_w_out, v_norm_f_gain=v_norm_f_gain)
    weights = {n: given[n] for n in TWIN_WEIGHTS}
    shared = {n: given[n] for n in SHARED_INPUTS}
    per_example = {n: given[n] for n in ['x', 'c']}
    grad_fn = _jax.value_and_grad(_loss, argnums=(0, 1))

    def one_microbatch(ex, loss_target):
        ex = dict(ex)
        diff = ex.pop(TWIN_DIFF_INPUT)
        return grad_fn(weights, diff, {**shared, **ex}, loss_target)

    if N_MICROBATCH == 1:
        loss, (grad_w, grad_x) = one_microbatch(per_example, given["loss_target"])
    else:
        def body(carry, xs):
            loss_sum, grad_sum = carry
            l_k, (gw_k, gx_k) = one_microbatch(xs[0], xs[1])
            with _jax.named_scope("update"):
                return (loss_sum + l_k, _jax.tree.map(_jnp.add, grad_sum, gw_k)), gx_k

        init = (_jnp.zeros((), _jnp.float32), _jax.tree.map(_jnp.zeros_like, weights))
        (loss, grad_w), grad_x = _jax.lax.scan(body, init, (per_example, given["loss_target"]))
    with _jax.named_scope("update"):
        delta_w, new_m, new_v = {}, {}, {}
        for n in TWIN_WEIGHTS:
            delta_w[n], new_m[n], new_v[n] = _adamw(weights[n], grad_w[n], given["m_" + n], given["v_" + n])
    return (loss, grad_x, *[grad_w[n] for n in TWIN_WEIGHTS], *[delta_w[n] for n in TWIN_WEIGHTS],
            *[new_m[n] for n in TWIN_WEIGHTS], *[new_v[n] for n in TWIN_WEIGHTS])
```

```python
import functools

import jax
import jax.numpy as jnp
from jax import lax
from jax.experimental import pallas as pl
from jax.experimental.pallas import tpu as pltpu

F32, BF16 = jnp.float32, jnp.bfloat16
MESH = pl.DeviceIdType.MESH
HI = lax.Precision.HIGHEST
NN = (((1,), (0,)), ((), ()))
NT = (((1,), (1,)), ((), ()))
TN = (((0,), (0,)), ((), ()))

D_MODEL = 1024
D_ATTN = 1024
D_SSM = 1024
HEAD_DIM = 64
N_HEADS = 16
N_PAIRS = 8
N_GROUPS = 2
D_STATE = 128
D_XBC = 1536
D_PROJ = 6672
D_MAIN = 6656
CONV_K = 4
CHUNK = 128
LANE = 128
N_CHIPS = 4
N_DEV = 8
NORM_EPS = 1e-6
ATTN_SCALE = HEAD_DIM ** -0.5
ATTN_TILE = 256
ADAM_LR, ADAM_B1, ADAM_B2, ADAM_EPS, ADAM_WD, ADAM_STEP = 0.001, 0.9, 0.999, 1e-08, 0.01, 10
VMEM_LIMIT = 56 * 1024 * 1024

OFF_Q, OFF_K, OFF_V, OFF_ZA, OFF_ZS, OFF_XBC = 0, 1024, 2048, 3072, 4096, 5120


def _cparams(sem=None):
    return pltpu.CompilerParams(dimension_semantics=sem, vmem_limit_bytes=VMEM_LIMIT)


def _sigmoid(x):
    return 1.0 / (1.0 + jnp.exp(-x))


def _softplus(x):
    return jnp.maximum(x, 0.0) + jnp.log(1.0 + jnp.exp(-jnp.abs(x)))


def _coords():
    return lax.axis_index("x"), lax.axis_index("y"), lax.axis_index("c")


def _allgather8(v, name):
    n = v.shape[-1]

    def body(v_ref, out_ref, send_sems, recv_sems, local_sem):
        x, y, c = _coords()
        me = 4 * x + 2 * y + c
        mine = pltpu.make_async_copy(v_ref, out_ref.at[me], local_sem)
        mine.start()
        sends, recvs = [], []
        for j in range(1, N_DEV):
            px = 1 - x if (j >> 2) & 1 else x
            py = 1 - y if (j >> 1) & 1 else y
            pc = 1 - c if j & 1 else c
            peer = (px, py, pc)
            sends.append(pltpu.make_async_remote_copy(
                src_ref=v_ref, dst_ref=out_ref.at[me], send_sem=send_sems.at[j - 1],
                recv_sem=recv_sems.at[j - 1], device_id=peer, device_id_type=MESH))
            recvs.append(pltpu.make_async_remote_copy(
                src_ref=v_ref, dst_ref=out_ref.at[4 * px + 2 * py + pc], send_sem=send_sems.at[j - 1],
                recv_sem=recv_sems.at[j - 1], device_id=peer, device_id_type=MESH))
        for s in sends:
            s.start()
        for r in recvs:
            r.wait_recv()
        for s in sends:
            s.wait_send()
        mine.wait()

    vm = pl.BlockSpec(memory_space=pltpu.VMEM)
    return pl.pallas_call(
        body, name=name, out_shape=jax.ShapeDtypeStruct((N_DEV, 1, n), F32),
        in_specs=[vm], out_specs=vm,
        scratch_shapes=[pltpu.SemaphoreType.DMA((N_DEV - 1,)), pltpu.SemaphoreType.DMA((N_DEV - 1,)),
                        pltpu.SemaphoreType.DMA(())],
    )(v)


def _other_chips(x, y):
    chips = [(1 - x, y), (x, 1 - y), (1 - x, 1 - y)]
    return chips, [2 * cx + cy for cx, cy in chips]


def _gather_shards(arrs, name):
    n = len(arrs)

    def body(*refs):
        ins, outs = refs[:n], refs[n:2 * n]
        send_sems, recv_sems, local_sems = refs[2 * n:]
        x, y, c = _coords()
        k = 2 * x + y
        chips, chip_idx = _other_chips(x, y)
        sibling = (x, y, 1 - c)
        locals_ = []
        for a in range(n):
            cp = pltpu.make_async_copy(ins[a], outs[a].at[k], local_sems.at[a])
            cp.start()
            locals_.append(cp)
        sends = []
        for a in range(n):
            rh = arrs[a].shape[0] // 2
            mine = pl.ds(pl.multiple_of(c * rh, rh), rh)
            for j in range(3):
                cp = pltpu.make_async_remote_copy(
                    src_ref=ins[a].at[mine], dst_ref=outs[a].at[k, mine], send_sem=send_sems.at[6 * a + j],
                    recv_sem=recv_sems.at[6 * a + j], device_id=(*chips[j], c), device_id_type=MESH)
                cp.start()
                sends.append(cp)
        for a in range(n):
            rh = arrs[a].shape[0] // 2
            mine = pl.ds(pl.multiple_of(c * rh, rh), rh)
            for j in range(3):
                landed = outs[a].at[chip_idx[j], mine]
                pltpu.make_async_remote_copy(
                    src_ref=landed, dst_ref=landed, send_sem=send_sems.at[6 * a + j],
                    recv_sem=recv_sems.at[6 * a + j], device_id=(*chips[j], c), device_id_type=MESH).wait_recv()
                fwd = pltpu.make_async_remote_copy(
                    src_ref=landed, dst_ref=landed, send_sem=send_sems.at[6 * a + 3 + j],
                    recv_sem=recv_sems.at[6 * a + 3 + j], device_id=sibling, device_id_type=MESH)
                fwd.start()
                sends.append(fwd)
        for a in range(n):
            rh = arrs[a].shape[0] // 2
            theirs = pl.ds(pl.multiple_of((1 - c) * rh, rh), rh)
            for j in range(3):
                landed = outs[a].at[chip_idx[j], theirs]
                pltpu.make_async_remote_copy(
                    src_ref=landed, dst_ref=landed, send_sem=send_sems.at[6 * a + 3 + j],
                    recv_sem=recv_sems.at[6 * a + 3 + j], device_id=sibling, device_id_type=MESH).wait_recv()
        for cp in sends:
            cp.wait_send()
        for cp in locals_:
            cp.wait()

    hbm = pl.BlockSpec(memory_space=pl.ANY)
    return pl.pallas_call(
        body, name=name,
        out_shape=tuple(jax.ShapeDtypeStruct((N_CHIPS,) + a.shape, a.dtype) for a in arrs),
        in_specs=[hbm] * n, out_specs=tuple([hbm] * n),
        scratch_shapes=[pltpu.SemaphoreType.DMA((6 * n,)), pltpu.SemaphoreType.DMA((6 * n,)),
                        pltpu.SemaphoreType.DMA((n,))],
    )(*arrs)


def _send_to_sibling(arrs, name):
    n = len(arrs)

    def body(*refs):
        ins, outs = refs[:n], refs[n:2 * n]
        send_sems, recv_sems = refs[2 * n:]
        x, y, c = _coords()
        cps = []
        for a in range(n):
            cp = pltpu.make_async_remote_copy(
                src_ref=ins[a].at[1 - c], dst_ref=outs[a], send_sem=send_sems.at[a], recv_sem=recv_sems.at[a],
                device_id=(x, y, 1 - c), device_id_type=MESH)
            cp.start()
            cps.append(cp)
        for cp in cps:
            cp.wait()

    hbm = pl.BlockSpec(memory_space=pl.ANY)
    return pl.pallas_call(
        body, name=name,
        out_shape=tuple(jax.ShapeDtypeStruct(a.shape[1:], a.dtype) for a in arrs),
        in_specs=[hbm] * n, out_specs=tuple([hbm] * n),
        scratch_shapes=[pltpu.SemaphoreType.DMA((n,)), pltpu.SemaphoreType.DMA((n,))],
    )(*arrs)


def _exchange_chips(arrs, name):
    n = len(arrs)

    def body(*refs):
        ins, outs = refs[:n], refs[n:2 * n]
        send_sems, recv_sems, local_sems = refs[2 * n:]
        x, y, c = _coords()
        k = 2 * x + y
        chips, chip_idx = _other_chips(x, y)
        cps = []
        for a in range(n):
            cp = pltpu.make_async_copy(ins[a].at[k], outs[a].at[k], local_sems.at[a])
            cp.start()
            cps.append(cp)
        sends = []
        for a in range(n):
            for j in range(3):
                cp = pltpu.make_async_remote_copy(
                    src_ref=ins[a].at[chip_idx[j]], dst_ref=outs[a].at[k], send_sem=send_sems.at[3 * a + j],
                    recv_sem=recv_sems.at[3 * a + j], device_id=(*chips[j], c), device_id_type=MESH)
                cp.start()
                sends.append(cp)
        for a in range(n):
            for j in range(3):
                landed = outs[a].at[chip_idx[j]]
                pltpu.make_async_remote_copy(
                    src_ref=landed, dst_ref=landed, send_sem=send_sems.at[3 * a + j],
                    recv_sem=recv_sems.at[3 * a + j], device_id=(*chips[j], c), device_id_type=MESH).wait_recv()
        for cp in sends:
            cp.wait_send()
        for cp in cps:
            cp.wait()

    hbm = pl.BlockSpec(memory_space=pl.ANY)
    return pl.pallas_call(
        body, name=name,
        out_shape=tuple(jax.ShapeDtypeStruct(a.shape, a.dtype) for a in arrs),
        in_specs=[hbm] * n, out_specs=tuple([hbm] * n),
        scratch_shapes=[pltpu.SemaphoreType.DMA((3 * n,)), pltpu.SemaphoreType.DMA((3 * n,)),
                        pltpu.SemaphoreType.DMA((n,))],
    )(*arrs)


def _share_with_sibling(arrs, name):
    n = len(arrs)

    def body(*refs):
        ins, outs = refs[:n], refs[n:2 * n]
        send_sems, recv_sems, local_sems = refs[2 * n:]
        x, y, c = _coords()
        cps = []
        for a in range(n):
            lc = pltpu.make_async_copy(ins[a], outs[a].at[c], local_sems.at[a])
            lc.start()
            cps.append(lc)
        sends = []
        for a in range(n):
            cp = pltpu.make_async_remote_copy(
                src_ref=ins[a], dst_ref=outs[a].at[c], send_sem=send_sems.at[a], recv_sem=recv_sems.at[a],
                device_id=(x, y, 1 - c), device_id_type=MESH)
            cp.start()
            sends.append(cp)
        for a in range(n):
            landed = outs[a].at[1 - c]
            pltpu.make_async_remote_copy(
                src_ref=ins[a], dst_ref=landed, send_sem=send_sems.at[a], recv_sem=recv_sems.at[a],
                device_id=(x, y, 1 - c), device_id_type=MESH).wait_recv()
        for cp in sends:
            cp.wait_send()
        for cp in cps:
            cp.wait()

    hbm = pl.BlockSpec(memory_space=pl.ANY)
    return pl.pallas_call(
        body, name=name,
        out_shape=tuple(jax.ShapeDtypeStruct((2,) + a.shape, a.dtype) for a in arrs),
        in_specs=[hbm] * n, out_specs=tuple([hbm] * n),
        scratch_shapes=[pltpu.SemaphoreType.DMA((n,)), pltpu.SemaphoreType.DMA((n,)),
                        pltpu.SemaphoreType.DMA((n,))],
    )(*arrs)


def _row_tile(rows, cols, n_arrays):
    budget = VMEM_LIMIT // 3
    t = rows
    while t > 8 and (t * cols * 4 * n_arrays * 2 > budget or rows % t):
        t //= 2
    return t


def _add_my_half(g, landed, my_c, name):
    _, nb, r, cdim = g.shape
    tr = _row_tile(r, cdim, 3)

    def body(c_ref, g_ref, l_ref, o_ref):
        o_ref[...] = g_ref[...] + l_ref[...]

    spec = pl.BlockSpec((None, tr, cdim), lambda b, i, c_ref: (b, i, 0))
    return pl.pallas_call(
        body, name=name, out_shape=jax.ShapeDtypeStruct((nb, r, cdim), F32),
        grid_spec=pltpu.PrefetchScalarGridSpec(
            num_scalar_prefetch=1, grid=(nb, r // tr),
            in_specs=[pl.BlockSpec((None, None, tr, cdim), lambda b, i, c_ref: (c_ref[0], b, i, 0)), spec],
            out_specs=spec),
        compiler_params=_cparams(("parallel", "parallel")),
    )(my_c, g, landed)


def _sum_slots(a, name):
    nb, r, cdim = a.shape
    tr = _row_tile(r, cdim, 5)

    def body(a_ref, o_ref):
        o_ref[...] = ((a_ref[0] + a_ref[1]) + a_ref[2]) + a_ref[3]

    return pl.pallas_call(
        body, name=name, out_shape=jax.ShapeDtypeStruct((r, cdim), F32), grid=(r // tr,),
        in_specs=[pl.BlockSpec((nb, tr, cdim), lambda i: (0, i, 0))],
        out_specs=pl.BlockSpec((tr, cdim), lambda i: (i, 0)),
        compiler_params=_cparams(("parallel",)),
    )(a)


def _adamw(w, g, m, v, name):
    r, cdim = w.shape
    tr = _row_tile(r, cdim, 7)

    def body(w_ref, g_ref, m_ref, v_ref, d_ref, nm_ref, nv_ref):
        gv = g_ref[...]
        nm = ADAM_B1 * m_ref[...] + (1.0 - ADAM_B1) * gv
        nv = ADAM_B2 * v_ref[...] + (1.0 - ADAM_B2) * (gv * gv)
        m_hat = nm / (1.0 - ADAM_B1 ** ADAM_STEP)
        v_hat = nv / (1.0 - ADAM_B2 ** ADAM_STEP)
        d_ref[...] = -ADAM_LR * (m_hat / (jnp.sqrt(v_hat) + ADAM_EPS) + ADAM_WD * w_ref[...])
        nm_ref[...] = nm
        nv_ref[...] = nv

    spec = pl.BlockSpec((tr, cdim), lambda i: (i, 0))
    shp = jax.ShapeDtypeStruct((r, cdim), F32)
    return pl.pallas_call(
        body, name=name, out_shape=(shp, shp, shp), grid=(r // tr,),
        in_specs=[spec] * 4, out_specs=(spec, spec, spec),
        compiler_params=_cparams(("parallel",)),
    )(w, g, m, v)


def _matmul(a, b, out_dtype, name, mode, tm, tn, tk, extra=None):
    dims = {"nn": NN, "nt": NT, "tn": TN}[mode]
    if mode == "tn":
        kdim, m = a.shape
    else:
        m, kdim = a.shape
    n = b.shape[0] if mode == "nt" else b.shape[1]
    tm, tn, tk = min(tm, m), min(tn, n), min(tk, kdim)
    nk = kdim // tk
    a_spec = (pl.BlockSpec((tk, tm), lambda i, j, k: (k, i)) if mode == "tn"
              else pl.BlockSpec((tm, tk), lambda i, j, k: (i, k)))
    b_spec = (pl.BlockSpec((tn, tk), lambda i, j, k: (j, k)) if mode == "nt"
              else pl.BlockSpec((tk, tn), lambda i, j, k: (k, j)))
    in_specs, operands = [a_spec, b_spec], [a, b]
    if extra is not None:
        a2, b2 = extra
        k2 = a2.shape[0] if mode == "tn" else a2.shape[1]
        in_specs.append(pl.BlockSpec((k2, tm), lambda i, j, k: (0, i)) if mode == "tn"
                        else pl.BlockSpec((tm, k2), lambda i, j, k: (i, 0)))
        in_specs.append(pl.BlockSpec((tn, k2), lambda i, j, k: (j, 0)) if mode == "nt"
                        else pl.BlockSpec((k2, tn), lambda i, j, k: (0, j)))
        operands += [a2, b2]

    def body(*refs):
        if extra is not None:
            a_ref, b_ref, a2_ref, b2_ref, o_ref, acc_ref = refs
        else:
            a_ref, b_ref, o_ref, acc_ref = refs
        k = pl.program_id(2)

        @pl.when(k == 0)
        def _():
            if extra is not None:
                acc_ref[...] = lax.dot_general(a2_ref[...], b2_ref[...], dims, preferred_element_type=F32)
            else:
                acc_ref[...] = jnp.zeros_like(acc_ref)

        acc_ref[...] += lax.dot_general(a_ref[...], b_ref[...], dims, preferred_element_type=F32)

        @pl.when(k == nk - 1)
        def _():
            o_ref[...] = acc_ref[...].astype(out_dtype)

    return pl.pallas_call(
        body, name=name, out_shape=jax.ShapeDtypeStruct((m, n), out_dtype), grid=(m // tm, n // tn, nk),
        in_specs=in_specs, out_specs=pl.BlockSpec((tm, tn), lambda i, j, k: (i, j)),
        scratch_shapes=[pltpu.VMEM((tm, tn), F32)],
        compiler_params=_cparams(("parallel", "parallel", "arbitrary")),
    )(*operands)


def _ada_mod(c_all, w_shard, b_shard):
    nb, d = c_all.shape
    cols = w_shard.shape[1]

    def body(c_ref, w_ref, b_ref, mod_ref, act_ref):
        cv = c_ref[...]
        act = cv * _sigmoid(cv)
        act_ref[...] = act
        mod_ref[...] = jnp.dot(act, w_ref[...], preferred_element_type=F32, precision=HI) + b_ref[...]

    return pl.pallas_call(
        body, name="ada_mod",
        out_shape=(jax.ShapeDtypeStruct((nb, cols), F32), jax.ShapeDtypeStruct((nb, d), F32)),
        compiler_params=_cparams(),
    )(c_all, w_shard, b_shard)


def _rms_mod_fwd(x, gain, scale, shift):
    s, d = x.shape
    tm = min(512, s)

    def body(x_ref, g_ref, sc_ref, sh_ref, h_ref):
        xv = x_ref[...]
        r = lax.rsqrt(jnp.mean(xv * xv, axis=-1, keepdims=True) + NORM_EPS)
        h_ref[...] = (xv * r * g_ref[...] * (1.0 + sc_ref[...]) + sh_ref[...]).astype(BF16)

    row = pl.BlockSpec((1, d), lambda i: (0, 0))
    tile = pl.BlockSpec((tm, d), lambda i: (i, 0))
    return pl.pallas_call(
        body, name="rms_mod_fwd", out_shape=jax.ShapeDtypeStruct((s, d), BF16), grid=(s // tm,),
        in_specs=[tile, row, row, row], out_specs=tile, compiler_params=_cparams(("parallel",)),
    )(x, gain, scale, shift)


def _rms_mod_bwd(x, dh, dres, gain, scale):
    s, d = x.shape
    tm = min(512, s)

    def body(x_ref, dh_ref, dres_ref, g_ref, sc_ref, dx_ref, sums_ref):
        @pl.when(pl.program_id(0) == 0)
        def _():
            sums_ref[...] = jnp.zeros_like(sums_ref)

        xv, dhv = x_ref[...], dh_ref[...]
        r = lax.rsqrt(jnp.mean(xv * xv, axis=-1, keepdims=True) + NORM_EPS)
        nrm = xv * r
        g, one_sc = g_ref[...], 1.0 + sc_ref[...]
        dn = dhv * g * one_sc
        dx_ref[...] = r * (dn - nrm * jnp.mean(dn * nrm, axis=-1, keepdims=True)) + dres_ref[...]
        dhn = dhv * nrm
        sums_ref[0:1, :] += jnp.sum(dhv, axis=0, keepdims=True)
        sums_ref[1:2, :] += jnp.sum(dhn * g, axis=0, keepdims=True)
        sums_ref[2:3, :] += jnp.sum(dhn * one_sc, axis=0, keepdims=True)

    row = pl.BlockSpec((1, d), lambda i: (0, 0))
    tile = pl.BlockSpec((tm, d), lambda i: (i, 0))
    return pl.pallas_call(
        body, name="rms_mod_bwd",
        out_shape=(jax.ShapeDtypeStruct((s, d), F32), jax.ShapeDtypeStruct((3, d), F32)), grid=(s // tm,),
        in_specs=[tile, tile, tile, row, row], out_specs=(tile, pl.BlockSpec((3, d), lambda i: (0, 0))),
        compiler_params=_cparams(("arbitrary",)),
    )(x, dh, dres, gain, scale)


def _loss_head(x, mixed, gate, gain_f, target):
    s, d = x.shape
    tm = min(512, s)

    def body(x_ref, mx_ref, gt_ref, gf_ref, t_ref, dx2_ref, dmx_ref, sums_ref):
        @pl.when(pl.program_id(0) == 0)
        def _():
            sums_ref[...] = jnp.zeros_like(sums_ref)

        mx, gt, gf = mx_ref[...], gt_ref[...], gf_ref[...]
        x2 = x_ref[...] + gt * mx
        r = lax.rsqrt(jnp.mean(x2 * x2, axis=-1, keepdims=True) + NORM_EPS)
        nrm = x2 * r
        err = nrm * gf - t_ref[...]
        dyf = err * (1.0 / d)
        dn = dyf * gf
        dx2 = r * (dn - nrm * jnp.mean(dn * nrm, axis=-1, keepdims=True))
        dx2_ref[...] = dx2
        dmx_ref[...] = (dx2 * gt).astype(BF16)
        sums_ref[0:1, :] += jnp.sum(err * err, axis=0, keepdims=True)
        sums_ref[1:2, :] += jnp.sum(dyf * nrm, axis=0, keepdims=True)
        sums_ref[2:3, :] += jnp.sum(dx2 * mx, axis=0, keepdims=True)

    row = pl.BlockSpec((1, d), lambda i: (0, 0))
    tile = pl.BlockSpec((tm, d), lambda i: (i, 0))
    return pl.pallas_call(
        body, name="loss_head",
        out_shape=(jax.ShapeDtypeStruct((s, d), F32), jax.ShapeDtypeStruct((s, d), BF16),
                   jax.ShapeDtypeStruct((3, d), F32)),
        grid=(s // tm,), in_specs=[tile, tile, row, row, tile],
        out_specs=(tile, tile, pl.BlockSpec((3, d), lambda i: (0, 0))),
        compiler_params=_cparams(("arbitrary",)),
    )(x, mixed, gate, gain_f, target)


def _silu_grad(z, sg):
    return sg * (1.0 + z * (1.0 - sg))


def _gated_norm_fwd(o, proj, z_off, gain, gate_inside, name):
    s, d = o.shape
    tm = min(512, s)
    zb = z_off // d

    def body(o_ref, z_ref, g_ref, y_ref):
        z = z_ref[...].astype(F32)
        sz = z * _sigmoid(z)
        u = o_ref[...] * sz if gate_inside else o_ref[...]
        r = lax.rsqrt(jnp.mean(u * u, axis=-1, keepdims=True) + NORM_EPS)
        y = u * r * g_ref[...]
        y_ref[...] = (y if gate_inside else y * sz).astype(BF16)

    tile = pl.BlockSpec((tm, d), lambda i: (i, 0))
    return pl.pallas_call(
        body, name=name, out_shape=jax.ShapeDtypeStruct((s, d), BF16), grid=(s // tm,),
        in_specs=[tile, pl.BlockSpec((tm, d), lambda i: (i, zb)), pl.BlockSpec((1, d), lambda i: (0, 0))],
        out_specs=tile, compiler_params=_cparams(("parallel",)),
    )(o, proj, gain)


def _gated_norm_bwd(dy_all, dy_blk, o, proj, z_off, gain, gate_inside, name):
    s, d = o.shape
    tm = min(512, s)
    zb = z_off // d

    def body(dy_ref, o_ref, z_ref, g_ref, do_ref, dz_ref, dg_ref):
        @pl.when(pl.program_id(0) == 0)
        def _():
            dg_ref[...] = jnp.zeros_like(dg_ref)

        z = z_ref[...].astype(F32)
        sg = _sigmoid(z)
        sz = z * sg
        ov, dy, g = o_ref[...], dy_ref[...], g_ref[...]
        u = ov * sz if gate_inside else ov
        r = lax.rsqrt(jnp.mean(u * u, axis=-1, keepdims=True) + NORM_EPS)
        nrm = u * r
        if gate_inside:
            dg_ref[...] += jnp.sum(dy * nrm, axis=0, keepdims=True)
            dn = dy * g
        else:
            dg_ref[...] += jnp.sum(dy * nrm * sz, axis=0, keepdims=True)
            dn = dy * g * sz
        du = r * (dn - nrm * jnp.mean(dn * nrm, axis=-1, keepdims=True))
        if gate_inside:
            do_ref[...] = du * sz
            dz_ref[...] = (du * ov * _silu_grad(z, sg)).astype(BF16)
        else:
            do_ref[...] = du
            dz_ref[...] = (dy * nrm * g * _silu_grad(z, sg)).astype(BF16)

    tile = pl.BlockSpec((tm, d), lambda i: (i, 0))
    row = pl.BlockSpec((1, d), lambda i: (0, 0))
    return pl.pallas_call(
        body, name=name,
        out_shape=(jax.ShapeDtypeStruct((s, d), F32), jax.ShapeDtypeStruct((s, d), BF16),
                   jax.ShapeDtypeStruct((1, d), F32)),
        grid=(s // tm,),
        in_specs=[pl.BlockSpec((tm, d), lambda i: (i, dy_blk)), tile, pl.BlockSpec((tm, d), lambda i: (i, zb)), row],
        out_specs=(tile, tile, row), compiler_params=_cparams(("arbitrary",)),
    )(dy_all, o, proj, gain)


def _sb_logits(qh, kb):
    z = lax.dot_general(qh, kb, NT, preferred_element_type=F32) * ATTN_SCALE
    sp = jnp.log(1.0 + jnp.exp(-jnp.abs(z)))
    return z, jnp.minimum(z, 0.0) - sp, jnp.minimum(-z, 0.0) - sp


def _attn_fwd(proj):
    s = proj.shape[0]
    t = min(ATTN_TILE, s)

    def body(q_ref, k_ref, v_ref, o_ref, l_ref, acc_ref, run_ref):
        i = pl.program_id(1)
        lane = lax.broadcasted_iota(jnp.int32, (1, LANE), 1)
        head_mask = (lane < HEAD_DIM, lane >= HEAD_DIM)
        row = lax.broadcasted_iota(jnp.int32, (t, t), 0)
        col = lax.broadcasted_iota(jnp.int32, (t, t), 1)
        causal = col < row
        later = (row > col).astype(BF16)
        q = q_ref[...]
        qh = [jnp.where(m, q, jnp.zeros_like(q)) for m in head_mask]
        acc_ref[...] = jnp.zeros_like(acc_ref)
        run_ref[...] = jnp.zeros_like(run_ref)

        def block(j, diag):
            start = pl.multiple_of(j * t, t)
            kb = k_ref[pl.ds(start, t), :]
            vb = v_ref[pl.ds(start, t), :]
            for h in range(2):
                _, lb, l1m = _sb_logits(qh[h], kb)
                if diag:
                    l1m = jnp.where(causal, l1m, 0.0)
                tail = jnp.dot(l1m.astype(BF16), later, preferred_element_type=F32) + run_ref[h]
                w = jnp.exp(lb + tail)
                if diag:
                    w = jnp.where(causal, w, 0.0)
                vh = jnp.where(head_mask[h], vb, jnp.zeros_like(vb))
                acc_ref[...] += jnp.dot(w.astype(BF16), vh, preferred_element_type=F32)
                run_ref[h] += jnp.sum(l1m, axis=1, keepdims=True)

        block(i, True)

        def step(jj, carry):
            block(i - 1 - jj, False)
            return carry

        lax.fori_loop(0, i, step, 0)
        o_ref[...] = acc_ref[...]
        l_ref[...] = jnp.where(head_mask[0], run_ref[0], run_ref[1])

    kq, kk, kv = OFF_Q // LANE, OFF_K // LANE, OFF_V // LANE
    tile = pl.BlockSpec((t, LANE), lambda p, i: (i, p))
    return pl.pallas_call(
        body, name="attn_fwd",
        out_shape=(jax.ShapeDtypeStruct((s, D_ATTN), F32), jax.ShapeDtypeStruct((s, D_ATTN), F32)),
        grid=(N_PAIRS, s // t),
        in_specs=[pl.BlockSpec((t, LANE), lambda p, i: (i, kq + p)),
                  pl.BlockSpec((s, LANE), lambda p, i: (0, kk + p)),
                  pl.BlockSpec((s, LANE), lambda p, i: (0, kv + p))],
        out_specs=(tile, tile),
        scratch_shapes=[pltpu.VMEM((t, LANE), F32), pltpu.VMEM((2, t, 1), F32)],
        compiler_params=_cparams(("parallel", "arbitrary")),
    )(proj, proj, proj)


def _attn_bwd(proj, do, lsum):
    s = proj.shape[0]
    t = min(ATTN_TILE, s)

    def body(q_ref, k_ref, v_ref, do_ref, l_ref, dq_ref, dk_ref, dv_ref, dqacc_ref, dkacc_ref, dvacc_ref,
             passed_ref, pre_ref):
        i = pl.program_id(1)

        @pl.when(i == 0)
        def _():
            dkacc_ref[...] = jnp.zeros_like(dkacc_ref)
            dvacc_ref[...] = jnp.zeros_like(dvacc_ref)

        lane = lax.broadcasted_iota(jnp.int32, (1, LANE), 1)
        head_mask = (lane < HEAD_DIM, lane >= HEAD_DIM)
        row = lax.broadcasted_iota(jnp.int32, (t, t), 0)
        col = lax.broadcasted_iota(jnp.int32, (t, t), 1)
        causal = col < row
        later = (row > col).astype(BF16)
        earlier = (row < col).astype(BF16)
        q = q_ref[...]
        dov = do_ref[...].astype(BF16)
        qh = [jnp.where(m, q, jnp.zeros_like(q)) for m in head_mask]
        doh = [jnp.where(m, dov, jnp.zeros_like(dov)) for m in head_mask]
        lsum_v = l_ref[...]
        lh = [lsum_v[:, 0:1], lsum_v[:, HEAD_DIM:HEAD_DIM + 1]]
        dqacc_ref[...] = jnp.zeros_like(dqacc_ref)
        passed_ref[...] = jnp.zeros_like(passed_ref)
        pre_ref[...] = jnp.zeros_like(pre_ref)

        def block(j, diag):
            start = pl.multiple_of(j * t, t)
            kb = k_ref[pl.ds(start, t), :]
            vb = v_ref[pl.ds(start, t), :]
            for h in range(2):
                _, lb, l1m = _sb_logits(qh[h], kb)
                if diag:
                    l1m = jnp.where(causal, l1m, 0.0)
                rs = jnp.sum(l1m, axis=1, keepdims=True)
                right = lh[h] - passed_ref[h] - rs
                passed_ref[h] += rs
                tail = jnp.dot(l1m.astype(BF16), later, preferred_element_type=F32) + right
                a = jnp.exp(lb + tail)
                if diag:
                    a = jnp.where(causal, a, 0.0)
                da = lax.dot_general(doh[h], vb, NT, preferred_element_type=F32)
                g = a * da
                dvacc_ref[pl.ds(start, t), :] += lax.dot_general(a.astype(BF16), doh[h], TN,
                                                                preferred_element_type=F32)
                pre = jnp.dot(g.astype(BF16), earlier, preferred_element_type=F32) + pre_ref[h]
                pre_ref[h] += jnp.sum(g, axis=1, keepdims=True)
                dz = (g - jnp.exp(lb) * (g + pre)) * ATTN_SCALE
                if diag:
                    dz = jnp.where(causal, dz, 0.0)
                dzb = dz.astype(BF16)
                kh = jnp.where(head_mask[h], kb, jnp.zeros_like(kb))
                dqacc_ref[...] += jnp.dot(dzb, kh, preferred_element_type=F32)
                dkacc_ref[pl.ds(start, t), :] += lax.dot_general(dzb, qh[h], TN, preferred_element_type=F32)

        def step(j, carry):
            block(j, False)
            return carry

        lax.fori_loop(0, i, step, 0)
        block(i, True)
        dq_ref[...] = dqacc_ref[...].astype(BF16)

        @pl.when(i == pl.num_programs(1) - 1)
        def _():
            dk_ref[...] = dkacc_ref[...].astype(BF16)
            dv_ref[...] = dvacc_ref[...].astype(BF16)

    kq, kk, kv = OFF_Q // LANE, OFF_K // LANE, OFF_V // LANE
    tile = pl.BlockSpec((t, LANE), lambda p, i: (i, p))
    full = pl.BlockSpec((s, LANE), lambda p, i: (0, p))
    shp = jax.ShapeDtypeStruct((s, D_ATTN), BF16)
    return pl.pallas_call(
        body, name="attn_bwd", out_shape=(shp, shp, shp), grid=(N_PAIRS, s // t),
        in_specs=[pl.BlockSpec((t, LANE), lambda p, i: (i, kq + p)),
                  pl.BlockSpec((s, LANE), lambda p, i: (0, kk + p)),
                  pl.BlockSpec((s, LANE), lambda p, i: (0, kv + p)),
                  tile, tile],
        out_specs=(tile, full, full),
        scratch_shapes=[pltpu.VMEM((t, LANE), F32), pltpu.VMEM((s, LANE), F32), pltpu.VMEM((s, LANE), F32),
                        pltpu.VMEM((2, t, 1), F32), pltpu.VMEM((2, t, 1), F32)],
        compiler_params=_cparams(("parallel", "arbitrary")),
    )(proj, proj, proj, do, lsum)


def _shift_down(u, k, rows):
    return jnp.where(rows >= k, pltpu.roll(u, k, 0), 0.0)


def _shift_up(u, k, rows, s):
    return jnp.where(rows < s - k, pltpu.roll(u, s - k, 0), 0.0)


def _conv_fwd(proj, w, b):
    s = proj.shape[0]
    blk0 = OFF_XBC // LANE

    def body(u_ref, w_ref, b_ref, o_ref):
        u = u_ref[...].astype(F32)
        rows = lax.broadcasted_iota(jnp.int32, (s, 1), 0)
        pre = u * w_ref[CONV_K - 1:CONV_K, :] + b_ref[...]
        for k in range(1, CONV_K):
            pre += _shift_down(u, k, rows) * w_ref[CONV_K - 1 - k:CONV_K - k, :]
        o_ref[...] = pre * _sigmoid(pre)

    return pl.pallas_call(
        body, name="conv_fwd", out_shape=jax.ShapeDtypeStruct((s, D_XBC), F32), grid=(D_XBC // LANE,),
        in_specs=[pl.BlockSpec((s, LANE), lambda j: (0, blk0 + j)), pl.BlockSpec((CONV_K, LANE), lambda j: (0, j)),
                  pl.BlockSpec((1, LANE), lambda j: (0, j))],
        out_specs=pl.BlockSpec((s, LANE), lambda j: (0, j)), compiler_params=_cparams(("parallel",)),
    )(proj, w, b)


def _conv_bwd(proj, w, b, dact):
    s = proj.shape[0]
    blk0 = OFF_XBC // LANE

    def body(u_ref, w_ref, b_ref, da_ref, du_ref, dw_ref, db_ref):
        u = u_ref[...].astype(F32)
        rows = lax.broadcasted_iota(jnp.int32, (s, 1), 0)
        shifted = [u] + [_shift_down(u, k, rows) for k in range(1, CONV_K)]
        pre = b_ref[...] + shifted[0] * w_ref[CONV_K - 1:CONV_K, :]
        for k in range(1, CONV_K):
            pre += shifted[k] * w_ref[CONV_K - 1 - k:CONV_K - k, :]
        sg = _sigmoid(pre)
        dpre = da_ref[...] * _silu_grad(pre, sg)
        db_ref[...] = jnp.sum(dpre, axis=0, keepdims=True)
        du = dpre * w_ref[CONV_K - 1:CONV_K, :]
        for k in range(CONV_K):
            dw_ref[CONV_K - 1 - k:CONV_K - k, :] = jnp.sum(dpre * shifted[k], axis=0, keepdims=True)
            if k:
                du += _shift_up(dpre, k, rows, s) * w_ref[CONV_K - 1 - k:CONV_K - k, :]
        du_ref[...] = du.astype(BF16)

    col = pl.BlockSpec((s, LANE), lambda j: (0, j))
    return pl.pallas_call(
        body, name="conv_bwd",
        out_shape=(jax.ShapeDtypeStruct((s, D_XBC), BF16), jax.ShapeDtypeStruct((CONV_K, D_XBC), F32),
                   jax.ShapeDtypeStruct((1, D_XBC), F32)),
        grid=(D_XBC // LANE,),
        in_specs=[pl.BlockSpec((s, LANE), lambda j: (0, blk0 + j)), pl.BlockSpec((CONV_K, LANE), lambda j: (0, j)),
                  pl.BlockSpec((1, LANE), lambda j: (0, j)), col],
        out_specs=(col, pl.BlockSpec((CONV_K, LANE), lambda j: (0, j)), pl.BlockSpec((1, LANE), lambda j: (0, j))),
        compiler_params=_cparams(("parallel",)),
    )(proj, w, b, dact)


def _ssd_decays(dtraw_ref, bias_ref, dtt_ref, biast_ref, arow_ref, acol_ref):
    ln = CHUNK
    dt = _softplus(dtraw_ref[...] + bias_ref[...])
    r = lax.broadcasted_iota(jnp.int32, (ln, ln), 0)
    c = lax.broadcasted_iota(jnp.int32, (ln, ln), 1)
    ac = jnp.dot((r >= c).astype(F32), dt * arow_ref[...], preferred_element_type=F32, precision=HI)
    dtt = _softplus(dtt_ref[...] + biast_ref[...])
    act = jnp.dot(dtt * acol_ref[...], (r <= c).astype(F32), preferred_element_type=F32, precision=HI)
    return dt, ac, act, r >= c


def _pair_cols(m0, v, h0):
    return jnp.where(m0, v[:, h0:h0 + 1], v[:, h0 + 1:h0 + 2])


def _ssd_fwd(act, dtraw, dtt, bias, biast, arow, acol, dskip):
    s = act.shape[0]
    ln = CHUNK
    nc = s // ln

    def body(act_ref, dtraw_ref, dtt_ref, bias_ref, biast_ref, arow_ref, acol_ref, dsk_ref, y_ref, st_ref,
             state_ref):
        @pl.when(pl.program_id(0) == 0)
        def _():
            state_ref[...] = jnp.zeros_like(state_ref)

        dt, ac, act_t, lower = _ssd_decays(dtraw_ref, bias_ref, dtt_ref, biast_ref, arow_ref, acol_ref)
        lane = lax.broadcasted_iota(jnp.int32, (1, LANE), 1)
        m0 = lane < HEAD_DIM
        top = lax.broadcasted_iota(jnp.int32, (LANE, 1), 0) < HEAD_DIM
        for g in range(N_GROUPS):
            bg = act_ref[:, D_SSM + g * D_STATE:D_SSM + (g + 1) * D_STATE].astype(BF16)
            cg = act_ref[:, D_SSM + (N_GROUPS + g) * D_STATE:D_SSM + (N_GROUPS + g + 1) * D_STATE].astype(BF16)
            cb = lax.dot_general(cg, bg, NT, preferred_element_type=F32)
            for p in range(g * 4, g * 4 + 4):
                h0 = 2 * p
                xp = act_ref[:, p * LANE:(p + 1) * LANE]
                xdt = xp * _pair_cols(m0, dt, h0)
                acp = _pair_cols(m0, ac, h0)
                last = acp[ln - 1:ln, :]
                y = xp * dsk_ref[:, p * LANE:(p + 1) * LANE]
                for hh in range(2):
                    h = h0 + hh
                    dm = jnp.exp(jnp.where(lower, ac[:, h:h + 1] - act_t[h:h + 1, :], -jnp.inf))
                    mask = m0 if hh == 0 else jnp.logical_not(m0)
                    y += jnp.dot((cb * dm).astype(BF16), jnp.where(mask, xdt, 0.0).astype(BF16),
                                 preferred_element_type=F32)
                prev = state_ref[p]
                st_ref[0, p] = prev
                y += lax.dot_general(cg, prev.astype(BF16), NT, preferred_element_type=F32) * jnp.exp(acp)
                y_ref[:, p * LANE:(p + 1) * LANE] = y
                cs = lax.dot_general((xdt * jnp.exp(last - acp)).astype(BF16), bg, TN, preferred_element_type=F32)
                dec = jnp.where(top, jnp.exp(ac[ln - 1:ln, h0:h0 + 1]), jnp.exp(ac[ln - 1:ln, h0 + 1:h0 + 2]))
                state_ref[p] = prev * dec + cs

    row = lambda w: pl.BlockSpec((1, w), lambda c: (0, 0))
    return pl.pallas_call(
        body, name="ssd_fwd",
        out_shape=(jax.ShapeDtypeStruct((s, D_SSM), F32),
                   jax.ShapeDtypeStruct((nc, N_PAIRS, LANE, D_STATE), F32)),
        grid=(nc,),
        in_specs=[pl.BlockSpec((ln, D_XBC), lambda c: (c, 0)), pl.BlockSpec((ln, LANE), lambda c: (c, 0)),
                  pl.BlockSpec((N_HEADS, ln), lambda c: (0, c)), row(LANE),
                  pl.BlockSpec((N_HEADS, 1), lambda c: (0, 0)), row(LANE),
                  pl.BlockSpec((N_HEADS, 1), lambda c: (0, 0)), row(D_SSM)],
        out_specs=(pl.BlockSpec((ln, D_SSM), lambda c: (c, 0)),
                   pl.BlockSpec((1, N_PAIRS, LANE, D_STATE), lambda c: (c, 0, 0, 0))),
        scratch_shapes=[pltpu.VMEM((N_PAIRS, LANE, D_STATE), F32)],
        compiler_params=_cparams(("arbitrary",)),
    )(act, dtraw, dtt, bias, biast, arow, acol, dskip)


def _ssd_bwd(act, dtraw, dtt, bias, biast, arow, acol, dskip, states, dy):
    s = act.shape[0]
    ln = CHUNK
    nc = s // ln

    def body(act_ref, dtraw_ref, dtt_ref, bias_ref, biast_ref, arow_ref, acol_ref, dsk_ref, st_ref, dy_ref,
             dact_ref, dldc_ref, dldr_ref, ddt_ref, dd_ref, dstate_ref):
        @pl.when(pl.program_id(0) == 0)
        def _():
            dstate_ref[...] = jnp.zeros_like(dstate_ref)
            dd_ref[...] = jnp.zeros_like(dd_ref)

        dt, ac, act_t, lower = _ssd_decays(dtraw_ref, bias_ref, dtt_ref, biast_ref, arow_ref, acol_ref)
        lane = lax.broadcasted_iota(jnp.int32, (1, LANE), 1)
        m0 = lane < HEAD_DIM
        halves = (m0, jnp.logical_not(m0))
        top = lax.broadcasted_iota(jnp.int32, (LANE, 1), 0) < HEAD_DIM
        is_last = lax.broadcasted_iota(jnp.int32, (ln, 1), 0) == ln - 1
        sub = lax.broadcasted_iota(jnp.int32, (N_HEADS, 1), 0)
        dac_col = jnp.zeros((ln, LANE), F32)
        dac_row = jnp.zeros((N_HEADS, ln), F32)
        ddt_col = jnp.zeros((ln, LANE), F32)

        def half_sum(v, hh):
            return jnp.sum(jnp.where(halves[hh], v, 0.0), axis=1, keepdims=True)

        for g in range(N_GROUPS):
            b_lo, c_lo = D_SSM + g * D_STATE, D_SSM + (N_GROUPS + g) * D_STATE
            bg32 = act_ref[:, b_lo:b_lo + D_STATE]
            cg32 = act_ref[:, c_lo:c_lo + D_STATE]
            bg, cg = bg32.astype(BF16), cg32.astype(BF16)
            cb = lax.dot_general(cg, bg, NT, preferred_element_type=F32)
            dcb = jnp.zeros((ln, ln), F32)
            dbg = jnp.zeros((ln, D_STATE), F32)
            dcg = jnp.zeros((ln, D_STATE), F32)
            for p in range(g * 4, g * 4 + 4):
                h0 = 2 * p
                cols = slice(p * LANE, (p + 1) * LANE)
                xp = act_ref[:, cols]
                dyp = dy_ref[:, cols]
                dtp = _pair_cols(m0, dt, h0)
                acp = _pair_cols(m0, ac, h0)
                last = acp[ln - 1:ln, :]
                xdt = xp * dtp
                eac = jnp.exp(acp)
                dte = jnp.exp(last - acp)
                prev = st_ref[0, p]
                prev_b = prev.astype(BF16)
                ds = dstate_ref[p]
                ds_b = ds.astype(BF16)
                dec_h = [jnp.exp(ac[ln - 1:ln, h0 + hh:h0 + hh + 1]) for hh in range(2)]
                dec = jnp.where(top, dec_h[0], dec_h[1])

                dd_ref[:, cols] += jnp.sum(dyp * xp, axis=0, keepdims=True)
                dx = dyp * dsk_ref[:, cols]
                zoff = lax.dot_general(cg, prev_b, NT, preferred_element_type=F32)
                dz_b = (dyp * eac).astype(BF16)
                dcg += jnp.dot(dz_b, prev_b, preferred_element_type=F32)
                dprev = lax.dot_general(dz_b, cg, TN, preferred_element_type=F32) + ds * dec
                t_off = dyp * zoff * eac
                wmat = lax.dot_general(bg, ds_b, NT, preferred_element_type=F32)
                xdte_b = (xdt * dte).astype(BF16)
                dbg += jnp.dot(xdte_b, ds_b, preferred_element_type=F32)
                dxdt = dte * wmat
                t_dte = xdt * wmat * dte
                sp_rows = jnp.sum(ds * prev, axis=1, keepdims=True)
                for hh in range(2):
                    h = h0 + hh
                    here = lane == h
                    tdh = half_sum(t_dte, hh)
                    dac_col += jnp.where(here, half_sum(t_off, hh) - tdh, 0.0)
                    sp_h = jnp.sum(jnp.where(top if hh == 0 else jnp.logical_not(top), sp_rows, 0.0),
                                   axis=0, keepdims=True) * dec_h[hh]
                    at_last = sp_h + jnp.sum(tdh, axis=0, keepdims=True)
                    dac_col += jnp.where(jnp.logical_and(is_last, here), at_last, 0.0)
                    dm = jnp.exp(jnp.where(lower, ac[:, h:h + 1] - act_t[h:h + 1, :], -jnp.inf))
                    mm = cb * dm
                    dyh = jnp.where(halves[hh], dyp, 0.0).astype(BF16)
                    xdth = jnp.where(halves[hh], xdt, 0.0).astype(BF16)
                    dmm = lax.dot_general(dyh, xdth, NT, preferred_element_type=F32)
                    dxdt += lax.dot_general(mm.astype(BF16), dyh, TN, preferred_element_type=F32)
                    gm = dmm * mm
                    dcb += dmm * dm
                    dac_col += jnp.where(here, jnp.sum(gm, axis=1, keepdims=True), 0.0)
                    dac_row -= jnp.where(sub == h, jnp.sum(gm, axis=0, keepdims=True), 0.0)
                    ddt_col += jnp.where(here, half_sum(dxdt * xp, hh), 0.0)
                dact_ref[:, cols] = dx + dxdt * dtp
                dstate_ref[p] = dprev
            dcb_b = dcb.astype(BF16)
            dact_ref[:, b_lo:b_lo + D_STATE] = dbg + lax.dot_general(dcb_b, cg, TN, preferred_element_type=F32)
            dact_ref[:, c_lo:c_lo + D_STATE] = dcg + jnp.dot(dcb_b, bg, preferred_element_type=F32)

        r = lax.broadcasted_iota(jnp.int32, (ln, ln), 0)
        c = lax.broadcasted_iota(jnp.int32, (ln, ln), 1)
        dldc_ref[...] = jnp.dot((r <= c).astype(F32), dac_col, preferred_element_type=F32, precision=HI)
        dldr_ref[...] = jnp.dot(dac_row, (r >= c).astype(F32), preferred_element_type=F32, precision=HI)
        ddt_ref[...] = ddt_col

    rev = lambda c: nc - 1 - c
    row = lambda w: pl.BlockSpec((1, w), lambda c: (0, 0))
    col16 = pl.BlockSpec((N_HEADS, 1), lambda c: (0, 0))
    chunk128 = pl.BlockSpec((ln, LANE), lambda c: (rev(c), 0))
    return pl.pallas_call(
        body, name="ssd_bwd",
        out_shape=(jax.ShapeDtypeStruct((s, D_XBC), F32), jax.ShapeDtypeStruct((s, LANE), F32),
                   jax.ShapeDtypeStruct((N_HEADS, s), F32), jax.ShapeDtypeStruct((s, LANE), F32),
                   jax.ShapeDtypeStruct((1, D_SSM), F32)),
        grid=(nc,),
        in_specs=[pl.BlockSpec((ln, D_XBC), lambda c: (rev(c), 0)), chunk128,
                  pl.BlockSpec((N_HEADS, ln), lambda c: (0, rev(c))), row(LANE), col16, row(LANE), col16,
                  row(D_SSM), pl.BlockSpec((1, N_PAIRS, LANE, D_STATE), lambda c: (rev(c), 0, 0, 0)),
                  pl.BlockSpec((ln, D_SSM), lambda c: (rev(c), 0))],
        out_specs=(pl.BlockSpec((ln, D_XBC), lambda c: (rev(c), 0)), chunk128,
                   pl.BlockSpec((N_HEADS, ln), lambda c: (0, rev(c))), chunk128, row(D_SSM)),
        scratch_shapes=[pltpu.VMEM((N_PAIRS, LANE, D_STATE), F32)],
        compiler_params=_cparams(("arbitrary",)),
    )(act, dtraw, dtt, bias, biast, arow, acol, dskip, states, dy)


def _dt_bwd(dtraw, bias, arow, dld_col, dld_row_t, ddt_col):
    s = dtraw.shape[0]
    tm = min(512, s)

    def body(raw_ref, bias_ref, a_ref, dc_ref, dr_ref, dd_ref, out_ref, sums_ref):
        @pl.when(pl.program_id(0) == 0)
        def _():
            sums_ref[...] = jnp.zeros_like(sums_ref)

        raw = raw_ref[...] + bias_ref[...]
        dld = dc_ref[...] + dr_ref[...]
        ddt = dld * a_ref[...] + dd_ref[...]
        draw = ddt * _sigmoid(raw)
        out_ref[...] = draw.astype(BF16)
        sums_ref[0:1, :] += jnp.sum(draw, axis=0, keepdims=True)
        sums_ref[1:2, :] += jnp.sum(dld * _softplus(raw), axis=0, keepdims=True)

    tile = pl.BlockSpec((tm, LANE), lambda i: (i, 0))
    row = pl.BlockSpec((1, LANE), lambda i: (0, 0))
    return pl.pallas_call(
        body, name="dt_bwd",
        out_shape=(jax.ShapeDtypeStruct((s, LANE), BF16), jax.ShapeDtypeStruct((2, LANE), F32)), grid=(s // tm,),
        in_specs=[tile, row, row, tile, tile, tile], out_specs=(tile, pl.BlockSpec((2, LANE), lambda i: (0, 0))),
        compiler_params=_cparams(("arbitrary",)),
    )(dtraw, bias, arow, dld_col, dld_row_t, ddt_col)


def _sum8(parts):
    nb, n = parts.shape

    def body(p_ref, o_ref):
        acc = p_ref[0:1, :]
        for b in range(1, nb):
            acc = acc + p_ref[b:b + 1, :]
        o_ref[...] = acc

    return pl.pallas_call(body, name="sum8", out_shape=jax.ShapeDtypeStruct((1, n), F32),
                          compiler_params=_cparams())(parts)


def _outer8(act_t, dmod):
    d, nb = act_t.shape
    n = dmod.shape[1]

    def body(a_ref, m_ref, o_ref):
        acc = a_ref[:, 0:1] * m_ref[0:1, :]
        for b in range(1, nb):
            acc = acc + a_ref[:, b:b + 1] * m_ref[b:b + 1, :]
        o_ref[...] = acc

    return pl.pallas_call(body, name="outer8", out_shape=jax.ShapeDtypeStruct((d, n), F32),
                          compiler_params=_cparams())(act_t, dmod)


def _pad_lanes(v, width=LANE):
    return jnp.pad(v, ((0, 0), (0, width - v.shape[1])))


def kernel(x, c, w_ada, b_ada, norm_in_gain, w_in, conv_w, conv_b, dt_bias, a_log, d_skip, sb_norm_gain, ssm_norm_gain, w_out, norm_f_gain, loss_target, m_w_ada, m_b_ada, m_norm_in_gain, m_w_in, m_conv_w, m_conv_b, m_dt_bias, m_a_log, m_d_skip, m_sb_norm_gain, m_ssm_norm_gain, m_w_out, m_norm_f_gain, v_w_ada, v_b_ada, v_norm_in_gain, v_w_in, v_conv_w, v_conv_b, v_dt_bias, v_a_log, v_d_skip, v_sb_norm_gain, v_ssm_norm_gain, v_w_out, v_norm_f_gain):
    ax, ay, ac_ = _coords()
    chip = 2 * ax + ay
    me = 2 * chip + ac_
    my_c = jnp.reshape(ac_, (1,)).astype(jnp.int32)
    x2d, tgt = x[0], loss_target[0]
    s = x2d.shape[0]
    ada_cols = w_ada.shape[2]
    cw_cols = conv_w.shape[2]
    in_cols = w_in.shape[2]
    out_rows = w_out.shape[1]

    small = jnp.concatenate([c, conv_w[0].reshape(1, CONV_K * cw_cols)], axis=1)
    small_all = _allgather8(small, "gather_cond")[:, 0, :]
    c_all = small_all[:, :D_MODEL]
    conv_w_full = (small_all[0::2, D_MODEL:].reshape(N_CHIPS, CONV_K, cw_cols)
                   .transpose(1, 0, 2).reshape(CONV_K, D_XBC))
    b_ada_shard = lax.dynamic_slice_in_dim(b_ada, chip * ada_cols, ada_cols, axis=1)
    mod_part, c_act_all = _ada_mod(c_all, w_ada[0], b_ada_shard)
    mod_all = _allgather8(mod_part.reshape(1, N_DEV * ada_cols), "gather_mod")[0::2, 0, :]
    mod_all = mod_all.reshape(N_CHIPS, N_DEV, ada_cols)
    mod = lax.dynamic_index_in_dim(mod_all, me, axis=1, keepdims=False).reshape(1, 3 * D_MODEL)
    shift, scale, gate = mod[:, :D_MODEL], mod[:, D_MODEL:2 * D_MODEL], mod[:, 2 * D_MODEL:]

    w_in_all, w_out_all = _gather_shards([w_in[0].astype(BF16), w_out[0].astype(BF16)], "gather_weights")
    w_in_full = w_in_all.transpose(1, 0, 2).reshape(D_MODEL, D_PROJ)
    dt_lo = 4 * D_ATTN + D_XBC
    w_main = jnp.concatenate([w_in_full[:, :4 * D_ATTN], w_in_full[:, dt_lo + N_HEADS:],
                              w_in_full[:, 4 * D_ATTN:dt_lo]], axis=1)
    w_dt = _pad_lanes(w_in_full[:, dt_lo:dt_lo + N_HEADS])
    w_out_full = w_out_all.reshape(N_CHIPS * out_rows, D_MODEL)

    h = _rms_mod_fwd(x2d, norm_in_gain, scale, shift)
    proj = _matmul(h, w_main, BF16, "in_proj", "nn", 1024, 512, 1024)
    dtraw = _matmul(h, w_dt, F32, "in_proj_dt", "nn", 1024, LANE, 1024)
    o_attn, lsum = _attn_fwd(proj)
    y_attn = _gated_norm_fwd(o_attn, proj, OFF_ZA, sb_norm_gain, False, "attn_gate_fwd")
    act = _conv_fwd(proj, conv_w_full, conv_b)
    a_neg = -jnp.exp(a_log)
    arow, acol = _pad_lanes(a_neg), a_neg.reshape(N_HEADS, 1)
    bias_row, bias_col = _pad_lanes(dt_bias), dt_bias.reshape(N_HEADS, 1)
    dtt = dtraw[:, :N_HEADS].T
    dskip_row = jnp.repeat(d_skip, HEAD_DIM, axis=1)
    ssd_args = (act, dtraw, dtt, bias_row, bias_col, arow, acol, dskip_row)
    y_ssd, states = _ssd_fwd(*ssd_args)
    y_ssm = _gated_norm_fwd(y_ssd, proj, OFF_ZS, ssm_norm_gain, True, "ssm_gate_fwd")
    mix_in = jnp.concatenate([y_attn, y_ssm], axis=1)
    mixed = _matmul(mix_in, w_out_full, F32, "out_proj", "nn", 1024, 1024, 2048)

    dx2, dmixed, head_sums = _loss_head(x2d, mixed, gate, norm_f_gain.reshape(1, D_MODEL), tgt)
    g_w_out = _matmul(mix_in, dmixed, F32, "out_proj_dw", "tn", 1024, 1024, 512)
    d_mix_in = _matmul(dmixed, w_out_full, F32, "out_proj_dx", "nt", 1024, 1024, 1024)
    d_o, dz_attn, g_sb = _gated_norm_bwd(d_mix_in, 0, o_attn, proj, OFF_ZA, sb_norm_gain, False, "attn_gate_bwd")
    d_y, dz_ssm, g_ssm = _gated_norm_bwd(d_mix_in, 1, y_ssd, proj, OFF_ZS, ssm_norm_gain, True, "ssm_gate_bwd")
    dq, dk, dv = _attn_bwd(proj, d_o, lsum)
    dact, dld_col, dld_row, ddt_col, dd_cols = _ssd_bwd(*ssd_args, states, d_y)
    dxbc, g_conv_w, g_conv_b = _conv_bwd(proj, conv_w_full, conv_b, dact)
    ddtraw, dt_sums = _dt_bwd(dtraw, bias_row, arow, dld_col, _pad_lanes(dld_row.T), ddt_col)
    dproj = jnp.concatenate([dq, dk, dv, dz_attn, dz_ssm, dxbc], axis=1)
    g_w_main = _matmul(h, dproj, F32, "in_proj_dw", "tn", 1024, 512, 512)
    g_w_dt = _matmul(h, ddtraw, F32, "in_proj_dw_dt", "tn", 1024, LANE, 512)
    dh = _matmul(dproj, w_main, F32, "in_proj_dx", "nt", 1024, 1024, 512, extra=(ddtraw, w_dt))
    grad_x, in_sums = _rms_mod_bwd(x2d, dh, dx2, norm_in_gain, scale)

    g_a_log = dt_sums[1:2, :N_HEADS] * a_neg
    g_d_skip = jnp.sum(dd_cols.reshape(N_HEADS, HEAD_DIM), axis=1).reshape(1, N_HEADS)
    dmod = jnp.concatenate([in_sums[0:1], in_sums[1:2], head_sums[2:3]], axis=1)
    loss_part = 0.5 / D_MODEL * jnp.sum(head_sums[0:1], axis=1, keepdims=True)
    pieces = [dmod, in_sums[2:3], g_conv_w.reshape(1, CONV_K * D_XBC), g_conv_b, _pad_lanes(dt_sums[0:1, :N_HEADS]),
              _pad_lanes(g_a_log), _pad_lanes(g_d_skip), g_sb, g_ssm, head_sums[1:2], _pad_lanes(loss_part)]
    widths = [p.shape[1] for p in pieces]
    parts_all = _allgather8(jnp.concatenate(pieces, axis=1), "gather_small_grads")[:, 0, :]
    total = _sum8(parts_all)
    offs = [0]
    for w_ in widths:
        offs.append(offs[-1] + w_)
    tot = [total[:, offs[i]:offs[i + 1]] for i in range(len(pieces))]
    g_b_ada, g_norm_in, g_conv_w_full = tot[0], tot[1], tot[2].reshape(CONV_K, D_XBC)
    g_conv_b_t, g_dt_bias, g_a_log_t, g_d_skip_t = tot[3], tot[4][:, :N_HEADS], tot[5][:, :N_HEADS], tot[6][:, :N_HEADS]
    g_sb_t, g_ssm_t, g_norm_f, loss = tot[7], tot[8], tot[9], tot[10][0, 0]
    g_conv_w_shard = lax.dynamic_slice_in_dim(g_conv_w_full, chip * cw_cols, cw_cols, axis=1)
    dmod_shard = lax.dynamic_slice_in_dim(parts_all[:, :3 * D_MODEL], chip * ada_cols, ada_cols, axis=1)
    g_w_ada = _outer8(c_act_all.T, dmod_shard)

    g_in_full = jnp.concatenate([g_w_main[:, :4 * D_ATTN], g_w_main[:, OFF_XBC:], g_w_dt[:, :N_HEADS],
                                 g_w_main[:, OFF_ZS:OFF_XBC]], axis=1)
    rh_in, rh_out = D_MODEL // 2, out_rows // 2
    g_in_blocks = g_in_full.reshape(2, rh_in, N_CHIPS, in_cols).transpose(0, 2, 1, 3)
    g_out_blocks = g_w_out.reshape(N_CHIPS, 2, rh_out, D_MODEL).transpose(1, 0, 2, 3)
    land_in, land_out = _send_to_sibling([g_in_blocks, g_out_blocks], "grads_to_sibling")
    chip_in = _add_my_half(g_in_blocks, land_in, my_c, "add_sibling_in")
    chip_out = _add_my_half(g_out_blocks, land_out, my_c, "add_sibling_out")
    slots_in, slots_out = _exchange_chips([chip_in, chip_out], "grads_between_chips")
    half_in, half_out = _sum_slots(slots_in, "sum_chips_in"), _sum_slots(slots_out, "sum_chips_out")
    both_in, both_out = _share_with_sibling([half_in, half_out], "grads_share_sibling")
    g_w_in = both_in.reshape(D_MODEL, in_cols)
    g_w_out_shard = both_out.reshape(out_rows, D_MODEL)

    d_w_ada, nm_w_ada, nv_w_ada = _adamw(w_ada[0], g_w_ada, m_w_ada[0], v_w_ada[0], "adamw_w_ada")
    d_w_in, nm_w_in, nv_w_in = _adamw(w_in[0], g_w_in, m_w_in[0], v_w_in[0], "adamw_w_in")
    d_w_out, nm_w_out, nv_w_out = _adamw(w_out[0], g_w_out_shard, m_w_out[0], v_w_out[0], "adamw_w_out")
    flat = lambda a: a.reshape(1, -1)
    small_w = [b_ada, norm_in_gain, conv_w[0], conv_b, dt_bias, a_log, d_skip, sb_norm_gain, ssm_norm_gain,
               norm_f_gain]
    small_m = [m_b_ada, m_norm_in_gain, m_conv_w[0], m_conv_b, m_dt_bias, m_a_log, m_d_skip, m_sb_norm_gain,
               m_ssm_norm_gain, m_norm_f_gain]
    small_v = [v_b_ada, v_norm_in_gain, v_conv_w[0], v_conv_b, v_dt_bias, v_a_log, v_d_skip, v_sb_norm_gain,
               v_ssm_norm_gain, v_norm_f_gain]
    small_g = [g_b_ada, g_norm_in, g_conv_w_shard, g_conv_b_t, g_dt_bias, g_a_log_t, g_d_skip_t, g_sb_t, g_ssm_t,
               g_norm_f]
    cat = lambda arrs: jnp.concatenate([flat(a) for a in arrs], axis=1)
    d_small, nm_small, nv_small = _adamw(cat(small_w), cat(small_g), cat(small_m), cat(small_v), "adamw_small")
    sizes = [a.size for a in small_w]
    soffs = [0]
    for n_ in sizes:
        soffs.append(soffs[-1] + n_)

    def split(packed):
        return [packed[0, soffs[i]:soffs[i + 1]].reshape(small_w[i].shape) for i in range(len(small_w))]

    def ordered(big_ada, big_in, big_out, smalls):
        (s_b_ada, s_norm_in, s_conv_w, s_conv_b, s_dt_bias, s_a_log, s_d_skip, s_sb, s_ssm, s_norm_f) = smalls
        return [big_ada[None], s_b_ada, s_norm_in, big_in[None], s_conv_w[None], s_conv_b, s_dt_bias, s_a_log,
                s_d_skip, s_sb, s_ssm, big_out[None], s_norm_f]

    grads = ordered(g_w_ada, g_w_in, g_w_out_shard,
                    [g.reshape(w_.shape) for g, w_ in zip(small_g, small_w)])
    deltas = ordered(d_w_ada, d_w_in, d_w_out, split(d_small))
    new_m = ordered(nm_w_ada, nm_w_in, nm_w_out, split(nm_small))
    new_v = ordered(nv_w_ada, nv_w_in, nv_w_out, split(nv_small))
    return (loss, grad_x[None], *grads, *deltas, *new_m, *new_v)
```

```python
import functools

import jax
import jax.numpy as jnp
from jax import lax
from jax.experimental import pallas as pl
from jax.experimental.pallas import tpu as pltpu

F32, BF16 = jnp.float32, jnp.bfloat16
MESH = pl.DeviceIdType.MESH
HI = lax.Precision.HIGHEST
NN = (((1,), (0,)), ((), ()))
NT = (((1,), (1,)), ((), ()))
TN = (((0,), (0,)), ((), ()))

D_MODEL = 1024
D_ATTN = 1024
D_SSM = 1024
HEAD_DIM = 64
N_HEADS = 16
N_PAIRS = 8
N_GROUPS = 2
D_STATE = 128
D_XBC = 1536
D_PROJ = 6672
D_MAIN = 6656
CONV_K = 4
CHUNK = 128
LANE = 128
N_CHIPS = 4
N_DEV = 8
NORM_EPS = 1e-6
ATTN_SCALE = HEAD_DIM ** -0.5
ATTN_TQ = 512
ATTN_TK = 256
LOG_ZERO = -110.0
ADAM_LR, ADAM_B1, ADAM_B2, ADAM_EPS, ADAM_WD, ADAM_STEP = 0.001, 0.9, 0.999, 1e-08, 0.01, 10
VMEM_LIMIT = 56 * 1024 * 1024

OFF_Q, OFF_K, OFF_V, OFF_ZA, OFF_ZS, OFF_XBC = 0, 1024, 2048, 3072, 4096, 5120


def _cparams(sem=None):
    return pltpu.CompilerParams(dimension_semantics=sem, vmem_limit_bytes=VMEM_LIMIT)


def _sigmoid(x):
    return 1.0 / (1.0 + jnp.exp(-x))


def _softplus(x):
    return jnp.maximum(x, 0.0) + jnp.log(1.0 + jnp.exp(-jnp.abs(x)))


def _coords():
    return lax.axis_index("x"), lax.axis_index("y"), lax.axis_index("c")


def _allgather8(v, name):
    n = v.shape[-1]

    def body(v_ref, out_ref, send_sems, recv_sems, local_sem):
        x, y, c = _coords()
        me = 4 * x + 2 * y + c
        mine = pltpu.make_async_copy(v_ref, out_ref.at[me], local_sem)
        mine.start()
        sends, recvs = [], []
        for j in range(1, N_DEV):
            px = 1 - x if (j >> 2) & 1 else x
            py = 1 - y if (j >> 1) & 1 else y
            pc = 1 - c if j & 1 else c
            peer = (px, py, pc)
            sends.append(pltpu.make_async_remote_copy(
                src_ref=v_ref, dst_ref=out_ref.at[me], send_sem=send_sems.at[j - 1],
                recv_sem=recv_sems.at[j - 1], device_id=peer, device_id_type=MESH))
            recvs.append(pltpu.make_async_remote_copy(
                src_ref=v_ref, dst_ref=out_ref.at[4 * px + 2 * py + pc], send_sem=send_sems.at[j - 1],
                recv_sem=recv_sems.at[j - 1], device_id=peer, device_id_type=MESH))
        for s in sends:
            s.start()
        for r in recvs:
            r.wait_recv()
        for s in sends:
            s.wait_send()
        mine.wait()

    vm = pl.BlockSpec(memory_space=pltpu.VMEM)
    return pl.pallas_call(
        body, name=name, out_shape=jax.ShapeDtypeStruct((N_DEV, 1, n), F32),
        in_specs=[vm], out_specs=vm,
        scratch_shapes=[pltpu.SemaphoreType.DMA((N_DEV - 1,)), pltpu.SemaphoreType.DMA((N_DEV - 1,)),
                        pltpu.SemaphoreType.DMA(())],
    )(v)


def _other_chips(x, y):
    chips = [(1 - x, y), (x, 1 - y), (1 - x, 1 - y)]
    return chips, [2 * cx + cy for cx, cy in chips]


def _half_cols(width, which):
    half = width // 2
    return pl.ds(pl.multiple_of(which * half, half), half)


def _gather_shards(arrs, name):
    n = len(arrs)

    def body(*refs):
        ins, outs = refs[:n], refs[n:2 * n]
        send_sems, recv_sems, local_sems = refs[2 * n:]
        x, y, c = _coords()
        k = 2 * x + y
        chips, chip_idx = _other_chips(x, y)
        sibling = (x, y, 1 - c)
        locals_ = []
        for a in range(n):
            cp = pltpu.make_async_copy(ins[a], outs[a].at[k], local_sems.at[a])
            cp.start()
            locals_.append(cp)
        sends = []
        for a in range(n):
            mine = _half_cols(arrs[a].shape[-1], c)
            for j in range(3):
                cp = pltpu.make_async_remote_copy(
                    src_ref=ins[a].at[:, mine], dst_ref=outs[a].at[k, :, mine], send_sem=send_sems.at[6 * a + j],
                    recv_sem=recv_sems.at[6 * a + j], device_id=(*chips[j], c), device_id_type=MESH)
                cp.start()
                sends.append(cp)
        for a in range(n):
            mine = _half_cols(arrs[a].shape[-1], c)
            for j in range(3):
                landed = outs[a].at[chip_idx[j], :, mine]
                pltpu.make_async_remote_copy(
                    src_ref=landed, dst_ref=landed, send_sem=send_sems.at[6 * a + j],
                    recv_sem=recv_sems.at[6 * a + j], device_id=(*chips[j], c), device_id_type=MESH).wait_recv()
                fwd = pltpu.make_async_remote_copy(
                    src_ref=landed, dst_ref=landed, send_sem=send_sems.at[6 * a + 3 + j],
                    recv_sem=recv_sems.at[6 * a + 3 + j], device_id=sibling, device_id_type=MESH)
                fwd.start()
                sends.append(fwd)
        for a in range(n):
            theirs = _half_cols(arrs[a].shape[-1], 1 - c)
            for j in range(3):
                landed = outs[a].at[chip_idx[j], :, theirs]
                pltpu.make_async_remote_copy(
                    src_ref=landed, dst_ref=landed, send_sem=send_sems.at[6 * a + 3 + j],
                    recv_sem=recv_sems.at[6 * a + 3 + j], device_id=sibling, device_id_type=MESH).wait_recv()
        for cp in sends:
            cp.wait_send()
        for cp in locals_:
            cp.wait()

    hbm = pl.BlockSpec(memory_space=pl.ANY)
    return pl.pallas_call(
        body, name=name,
        out_shape=tuple(jax.ShapeDtypeStruct((N_CHIPS,) + a.shape, a.dtype) for a in arrs),
        in_specs=[hbm] * n, out_specs=tuple([hbm] * n),
        scratch_shapes=[pltpu.SemaphoreType.DMA((6 * n,)), pltpu.SemaphoreType.DMA((6 * n,)),
                        pltpu.SemaphoreType.DMA((n,))],
    )(*arrs)


def _send_to_sibling(arrs, name):
    n = len(arrs)

    def body(*refs):
        ins, outs = refs[:n], refs[n:2 * n]
        send_sems, recv_sems = refs[2 * n:]
        x, y, c = _coords()
        cps = []
        for a in range(n):
            cp = pltpu.make_async_remote_copy(
                src_ref=ins[a].at[:, :, _half_cols(arrs[a].shape[-1], 1 - c)], dst_ref=outs[a],
                send_sem=send_sems.at[a], recv_sem=recv_sems.at[a], device_id=(x, y, 1 - c), device_id_type=MESH)
            cp.start()
            cps.append(cp)
        for cp in cps:
            cp.wait()

    hbm = pl.BlockSpec(memory_space=pl.ANY)
    return pl.pallas_call(
        body, name=name,
        out_shape=tuple(jax.ShapeDtypeStruct(a.shape[:-1] + (a.shape[-1] // 2,), a.dtype) for a in arrs),
        in_specs=[hbm] * n, out_specs=tuple([hbm] * n),
        scratch_shapes=[pltpu.SemaphoreType.DMA((n,)), pltpu.SemaphoreType.DMA((n,))],
    )(*arrs)


def _exchange_chips(arrs, name):
    n = len(arrs)

    def body(*refs):
        ins, outs = refs[:n], refs[n:2 * n]
        send_sems, recv_sems, local_sems = refs[2 * n:]
        x, y, c = _coords()
        k = 2 * x + y
        chips, chip_idx = _other_chips(x, y)
        cps = []
        for a in range(n):
            cp = pltpu.make_async_copy(ins[a].at[k], outs[a].at[k], local_sems.at[a])
            cp.start()
            cps.append(cp)
        sends = []
        for a in range(n):
            for j in range(3):
                cp = pltpu.make_async_remote_copy(
                    src_ref=ins[a].at[chip_idx[j]], dst_ref=outs[a].at[k], send_sem=send_sems.at[3 * a + j],
                    recv_sem=recv_sems.at[3 * a + j], device_id=(*chips[j], c), device_id_type=MESH)
                cp.start()
                sends.append(cp)
        for a in range(n):
            for j in range(3):
                landed = outs[a].at[chip_idx[j]]
                pltpu.make_async_remote_copy(
                    src_ref=landed, dst_ref=landed, send_sem=send_sems.at[3 * a + j],
                    recv_sem=recv_sems.at[3 * a + j], device_id=(*chips[j], c), device_id_type=MESH).wait_recv()
        for cp in sends:
            cp.wait_send()
        for cp in cps:
            cp.wait()

    hbm = pl.BlockSpec(memory_space=pl.ANY)
    return pl.pallas_call(
        body, name=name,
        out_shape=tuple(jax.ShapeDtypeStruct(a.shape, a.dtype) for a in arrs),
        in_specs=[hbm] * n, out_specs=tuple([hbm] * n),
        scratch_shapes=[pltpu.SemaphoreType.DMA((3 * n,)), pltpu.SemaphoreType.DMA((3 * n,)),
                        pltpu.SemaphoreType.DMA((n,))],
    )(*arrs)


def _share_with_sibling(arrs, name):
    n = len(arrs)

    def body(*refs):
        ins, outs = refs[:n], refs[n:2 * n]
        send_sems, recv_sems, local_sems = refs[2 * n:]
        x, y, c = _coords()
        cps = []
        for a in range(n):
            lc = pltpu.make_async_copy(ins[a], outs[a].at[:, _half_cols(2 * arrs[a].shape[-1], c)], local_sems.at[a])
            lc.start()
            cps.append(lc)
        sends = []
        for a in range(n):
            cp = pltpu.make_async_remote_copy(
                src_ref=ins[a], dst_ref=outs[a].at[:, _half_cols(2 * arrs[a].shape[-1], c)],
                send_sem=send_sems.at[a], recv_sem=recv_sems.at[a], device_id=(x, y, 1 - c), device_id_type=MESH)
            cp.start()
            sends.append(cp)
        for a in range(n):
            landed = outs[a].at[:, _half_cols(2 * arrs[a].shape[-1], 1 - c)]
            pltpu.make_async_remote_copy(
                src_ref=ins[a], dst_ref=landed, send_sem=send_sems.at[a], recv_sem=recv_sems.at[a],
                device_id=(x, y, 1 - c), device_id_type=MESH).wait_recv()
        for cp in sends:
            cp.wait_send()
        for cp in cps:
            cp.wait()

    hbm = pl.BlockSpec(memory_space=pl.ANY)
    return pl.pallas_call(
        body, name=name,
        out_shape=tuple(jax.ShapeDtypeStruct((a.shape[0], 2 * a.shape[1]), a.dtype) for a in arrs),
        in_specs=[hbm] * n, out_specs=tuple([hbm] * n),
        scratch_shapes=[pltpu.SemaphoreType.DMA((n,)), pltpu.SemaphoreType.DMA((n,)),
                        pltpu.SemaphoreType.DMA((n,))],
    )(*arrs)


def _row_tile(rows, cols, n_arrays):
    budget = VMEM_LIMIT // 2
    t = rows
    while t % 16 == 0 and t * cols * 4 * n_arrays * 2 > budget:
        t //= 2
    return t


def _add_my_half(g, landed, my_c, name):
    nb, r, cdim = g.shape
    half = cdim // 2
    tr = _row_tile(r, half, 3)

    def body(c_ref, g_ref, l_ref, o_ref):
        o_ref[...] = (g_ref[...] + l_ref[...]).astype(BF16)

    spec = pl.BlockSpec((None, tr, half), lambda b, i, c_ref: (b, i, 0))
    return pl.pallas_call(
        body, name=name, out_shape=jax.ShapeDtypeStruct((nb, r, half), BF16),
        grid_spec=pltpu.PrefetchScalarGridSpec(
            num_scalar_prefetch=1, grid=(nb, r // tr),
            in_specs=[pl.BlockSpec((None, tr, half), lambda b, i, c_ref: (b, i, c_ref[0])), spec],
            out_specs=spec),
        compiler_params=_cparams(("parallel", "parallel")),
    )(my_c, g, landed)


def _sum_slots(a, name):
    nb, r, cdim = a.shape
    tr = _row_tile(r, cdim, 4)

    def body(a_ref, o_ref):
        o_ref[...] = ((a_ref[0].astype(F32) + a_ref[1].astype(F32)) + a_ref[2].astype(F32)) + a_ref[3].astype(F32)

    return pl.pallas_call(
        body, name=name, out_shape=jax.ShapeDtypeStruct((r, cdim), F32), grid=(r // tr,),
        in_specs=[pl.BlockSpec((nb, tr, cdim), lambda i: (0, i, 0))],
        out_specs=pl.BlockSpec((tr, cdim), lambda i: (i, 0)),
        compiler_params=_cparams(("parallel",)),
    )(a)


def _adamw(w, g, m, v, name):
    r, cdim = w.shape
    tr = _row_tile(r, cdim, 7)

    def body(w_ref, g_ref, m_ref, v_ref, d_ref, nm_ref, nv_ref):
        gv = g_ref[...]
        nm = ADAM_B1 * m_ref[...] + (1.0 - ADAM_B1) * gv
        nv = ADAM_B2 * v_ref[...] + (1.0 - ADAM_B2) * (gv * gv)
        m_hat = nm / (1.0 - ADAM_B1 ** ADAM_STEP)
        v_hat = nv / (1.0 - ADAM_B2 ** ADAM_STEP)
        d_ref[...] = -ADAM_LR * (m_hat / (jnp.sqrt(v_hat) + ADAM_EPS) + ADAM_WD * w_ref[...])
        nm_ref[...] = nm
        nv_ref[...] = nv

    spec = pl.BlockSpec((tr, cdim), lambda i: (i, 0))
    shp = jax.ShapeDtypeStruct((r, cdim), F32)
    return pl.pallas_call(
        body, name=name, out_shape=(shp, shp, shp), grid=(r // tr,),
        in_specs=[spec] * 4, out_specs=(spec, spec, spec),
        compiler_params=_cparams(("parallel",)),
    )(w, g, m, v)


def _matmul(a, b, out_dtype, name, mode, tm, tn, tk, extra=None):
    dims = {"nn": NN, "nt": NT, "tn": TN}[mode]
    if mode == "tn":
        kdim, m = a.shape
    else:
        m, kdim = a.shape
    n = b.shape[0] if mode == "nt" else b.shape[1]
    tm, tn, tk = min(tm, m), min(tn, n), min(tk, kdim)
    nk = kdim // tk
    a_spec = (pl.BlockSpec((tk, tm), lambda i, j, k: (k, i)) if mode == "tn"
              else pl.BlockSpec((tm, tk), lambda i, j, k: (i, k)))
    b_spec = (pl.BlockSpec((tn, tk), lambda i, j, k: (j, k)) if mode == "nt"
              else pl.BlockSpec((tk, tn), lambda i, j, k: (k, j)))
    in_specs, operands = [a_spec, b_spec], [a, b]
    if extra is not None:
        a2, b2 = extra
        k2 = a2.shape[0] if mode == "tn" else a2.shape[1]
        in_specs.append(pl.BlockSpec((k2, tm), lambda i, j, k: (0, i)) if mode == "tn"
                        else pl.BlockSpec((tm, k2), lambda i, j, k: (i, 0)))
        in_specs.append(pl.BlockSpec((tn, k2), lambda i, j, k: (j, 0)) if mode == "nt"
                        else pl.BlockSpec((k2, tn), lambda i, j, k: (0, j)))
        operands += [a2, b2]

    def body(*refs):
        if extra is not None:
            a_ref, b_ref, a2_ref, b2_ref, o_ref, acc_ref = refs
        else:
            a_ref, b_ref, o_ref, acc_ref = refs
        k = pl.program_id(2)

        @pl.when(k == 0)
        def _():
            if extra is not None:
                acc_ref[...] = lax.dot_general(a2_ref[...], b2_ref[...], dims, preferred_element_type=F32)
            else:
                acc_ref[...] = jnp.zeros_like(acc_ref)

        acc_ref[...] += lax.dot_general(a_ref[...], b_ref[...], dims, preferred_element_type=F32)

        @pl.when(k == nk - 1)
        def _():
            o_ref[...] = acc_ref[...].astype(out_dtype)

    return pl.pallas_call(
        body, name=name, out_shape=jax.ShapeDtypeStruct((m, n), out_dtype), grid=(m // tm, n // tn, nk),
        in_specs=in_specs, out_specs=pl.BlockSpec((tm, tn), lambda i, j, k: (i, j)),
        scratch_shapes=[pltpu.VMEM((tm, tn), F32)],
        compiler_params=_cparams(("parallel", "parallel", "arbitrary")),
    )(*operands)


def _ada_mod(c_all, w_shard, b_shard):
    nb, d = c_all.shape
    cols = w_shard.shape[1]

    def body(c_ref, w_ref, b_ref, mod_ref, act_ref):
        cv = c_ref[...]
        act = cv * _sigmoid(cv)
        act_ref[...] = act
        mod_ref[...] = jnp.dot(act, w_ref[...], preferred_element_type=F32, precision=HI) + b_ref[...]

    return pl.pallas_call(
        body, name="ada_mod",
        out_shape=(jax.ShapeDtypeStruct((nb, cols), F32), jax.ShapeDtypeStruct((nb, d), F32)),
        compiler_params=_cparams(),
    )(c_all, w_shard, b_shard)


def _rms_mod_fwd(x, gain, scale, shift):
    s, d = x.shape
    tm = min(512, s)

    def body(x_ref, g_ref, sc_ref, sh_ref, h_ref):
        xv = x_ref[...]
        r = lax.rsqrt(jnp.mean(xv * xv, axis=-1, keepdims=True) + NORM_EPS)
        h_ref[...] = (xv * r * g_ref[...] * (1.0 + sc_ref[...]) + sh_ref[...]).astype(BF16)

    row = pl.BlockSpec((1, d), lambda i: (0, 0))
    tile = pl.BlockSpec((tm, d), lambda i: (i, 0))
    return pl.pallas_call(
        body, name="rms_mod_fwd", out_shape=jax.ShapeDtypeStruct((s, d), BF16), grid=(s // tm,),
        in_specs=[tile, row, row, row], out_specs=tile, compiler_params=_cparams(("parallel",)),
    )(x, gain, scale, shift)


def _rms_mod_bwd(x, dh, dres, gain, scale):
    s, d = x.shape
    tm = min(512, s)

    def body(x_ref, dh_ref, dres_ref, g_ref, sc_ref, dx_ref, sums_ref):
        @pl.when(pl.program_id(0) == 0)
        def _():
            sums_ref[...] = jnp.zeros_like(sums_ref)

        xv, dhv = x_ref[...], dh_ref[...]
        r = lax.rsqrt(jnp.mean(xv * xv, axis=-1, keepdims=True) + NORM_EPS)
        nrm = xv * r
        g, one_sc = g_ref[...], 1.0 + sc_ref[...]
        dn = dhv * g * one_sc
        dx_ref[...] = r * (dn - nrm * jnp.mean(dn * nrm, axis=-1, keepdims=True)) + dres_ref[...]
        dhn = dhv * nrm
        sums_ref[0:1, :] += jnp.sum(dhv, axis=0, keepdims=True)
        sums_ref[1:2, :] += jnp.sum(dhn * g, axis=0, keepdims=True)
        sums_ref[2:3, :] += jnp.sum(dhn * one_sc, axis=0, keepdims=True)

    row = pl.BlockSpec((1, d), lambda i: (0, 0))
    tile = pl.BlockSpec((tm, d), lambda i: (i, 0))
    return pl.pallas_call(
        body, name="rms_mod_bwd",
        out_shape=(jax.ShapeDtypeStruct((s, d), F32), jax.ShapeDtypeStruct((3, d), F32)), grid=(s // tm,),
        in_specs=[tile, tile, tile, row, row], out_specs=(tile, pl.BlockSpec((3, d), lambda i: (0, 0))),
        compiler_params=_cparams(("arbitrary",)),
    )(x, dh, dres, gain, scale)


def _loss_head(x, mixed, gate, gain_f, target):
    s, d = x.shape
    tm = min(512, s)

    def body(x_ref, mx_ref, gt_ref, gf_ref, t_ref, dx2_ref, dmx_ref, sums_ref):
        @pl.when(pl.program_id(0) == 0)
        def _():
            sums_ref[...] = jnp.zeros_like(sums_ref)

        mx, gt, gf = mx_ref[...], gt_ref[...], gf_ref[...]
        x2 = x_ref[...] + gt * mx
        r = lax.rsqrt(jnp.mean(x2 * x2, axis=-1, keepdims=True) + NORM_EPS)
        nrm = x2 * r
        err = nrm * gf - t_ref[...]
        dyf = err * (1.0 / d)
        dn = dyf * gf
        dx2 = r * (dn - nrm * jnp.mean(dn * nrm, axis=-1, keepdims=True))
        dx2_ref[...] = dx2
        dmx_ref[...] = (dx2 * gt).astype(BF16)
        sums_ref[0:1, :] += jnp.sum(err * err, axis=0, keepdims=True)
        sums_ref[1:2, :] += jnp.sum(dyf * nrm, axis=0, keepdims=True)
        sums_ref[2:3, :] += jnp.sum(dx2 * mx, axis=0, keepdims=True)

    row = pl.BlockSpec((1, d), lambda i: (0, 0))
    tile = pl.BlockSpec((tm, d), lambda i: (i, 0))
    return pl.pallas_call(
        body, name="loss_head",
        out_shape=(jax.ShapeDtypeStruct((s, d), F32), jax.ShapeDtypeStruct((s, d), BF16),
                   jax.ShapeDtypeStruct((3, d), F32)),
        grid=(s // tm,), in_specs=[tile, tile, row, row, tile],
        out_specs=(tile, tile, pl.BlockSpec((3, d), lambda i: (0, 0))),
        compiler_params=_cparams(("arbitrary",)),
    )(x, mixed, gate, gain_f, target)


def _silu_grad(z, sg):
    return sg * (1.0 + z * (1.0 - sg))


def _gated_norm_fwd(o, proj, z_off, gain, gate_inside, name):
    s, d = o.shape
    tm = min(512, s)
    zb = z_off // d

    def body(o_ref, z_ref, g_ref, y_ref):
        z = z_ref[...].astype(F32)
        sz = z * _sigmoid(z)
        u = o_ref[...] * sz if gate_inside else o_ref[...]
        r = lax.rsqrt(jnp.mean(u * u, axis=-1, keepdims=True) + NORM_EPS)
        y = u * r * g_ref[...]
        y_ref[...] = (y if gate_inside else y * sz).astype(BF16)

    tile = pl.BlockSpec((tm, d), lambda i: (i, 0))
    return pl.pallas_call(
        body, name=name, out_shape=jax.ShapeDtypeStruct((s, d), BF16), grid=(s // tm,),
        in_specs=[tile, pl.BlockSpec((tm, d), lambda i: (i, zb)), pl.BlockSpec((1, d), lambda i: (0, 0))],
        out_specs=tile, compiler_params=_cparams(("parallel",)),
    )(o, proj, gain)


def _gated_norm_bwd(dy_all, dy_blk, o, proj, z_off, gain, gate_inside, name):
    s, d = o.shape
    tm = min(512, s)
    zb = z_off // d

    def body(dy_ref, o_ref, z_ref, g_ref, do_ref, dz_ref, dg_ref):
        @pl.when(pl.program_id(0) == 0)
        def _():
            dg_ref[...] = jnp.zeros_like(dg_ref)

        z = z_ref[...].astype(F32)
        sg = _sigmoid(z)
        sz = z * sg
        ov, dy, g = o_ref[...], dy_ref[...], g_ref[...]
        u = ov * sz if gate_inside else ov
        r = lax.rsqrt(jnp.mean(u * u, axis=-1, keepdims=True) + NORM_EPS)
        nrm = u * r
        if gate_inside:
            dg_ref[...] += jnp.sum(dy * nrm, axis=0, keepdims=True)
            dn = dy * g
        else:
            dg_ref[...] += jnp.sum(dy * nrm * sz, axis=0, keepdims=True)
            dn = dy * g * sz
        du = r * (dn - nrm * jnp.mean(dn * nrm, axis=-1, keepdims=True))
        if gate_inside:
            do_ref[...] = du * sz
            dz_ref[...] = (du * ov * _silu_grad(z, sg)).astype(BF16)
        else:
            do_ref[...] = du
            dz_ref[...] = (dy * nrm * g * _silu_grad(z, sg)).astype(BF16)

    tile = pl.BlockSpec((tm, d), lambda i: (i, 0))
    row = pl.BlockSpec((1, d), lambda i: (0, 0))
    return pl.pallas_call(
        body, name=name,
        out_shape=(jax.ShapeDtypeStruct((s, d), F32), jax.ShapeDtypeStruct((s, d), BF16),
                   jax.ShapeDtypeStruct((1, d), F32)),
        grid=(s // tm,),
        in_specs=[pl.BlockSpec((tm, d), lambda i: (i, dy_blk)), tile, pl.BlockSpec((tm, d), lambda i: (i, zb)), row],
        out_specs=(tile, tile, row), compiler_params=_cparams(("arbitrary",)),
    )(dy_all, o, proj, gain)


def _sb_logits(qh, kb):
    z = lax.dot_general(qh, kb, NT, preferred_element_type=F32)
    neg_abs = lax.bitcast_convert_type(lax.bitcast_convert_type(z, jnp.uint32) | jnp.uint32(0x80000000), F32)
    lb = jnp.minimum(z, 0.0) - jnp.log(1.0 + jnp.exp(neg_abs))
    return lb, lb - z


def _attn_consts(tk):
    lane = lax.broadcasted_iota(jnp.int32, (1, LANE), 1)
    row = lax.broadcasted_iota(jnp.int32, (tk, tk), 0)
    col = lax.broadcasted_iota(jnp.int32, (tk, tk), 1)
    return (lane < HEAD_DIM, lane >= HEAD_DIM), row, col


def _band_mask(rows, tk):
    return lax.broadcasted_iota(jnp.int32, (rows, tk), 1) < lax.broadcasted_iota(jnp.int32, (rows, tk), 0)


def _attn_fwd(proj):
    s = proj.shape[0]
    tq, tk = min(ATTN_TQ, s), min(ATTN_TK, s)
    r = tq // tk

    def body(q_ref, k_ref, v_ref, o_ref, l_ref, acc_ref, run_ref):
        i = pl.program_id(1)
        head_mask, row, col = _attn_consts(tk)
        later = (row > col).astype(BF16)
        q = q_ref[...] * ATTN_SCALE
        qh = [jnp.where(m, q, jnp.zeros_like(q)) for m in head_mask]
        acc_ref[...] = jnp.zeros_like(acc_ref)
        run_ref[...] = jnp.zeros_like(run_ref)

        def block(j, lo):
            start = pl.multiple_of(j * tk, tk)
            kb = k_ref[pl.ds(start, tk), :]
            vb = v_ref[pl.ds(start, tk), :]
            rows = slice(0 if lo is None else lo, tq)
            causal = None if lo is None else _band_mask(tq - lo, tk)
            hs = range(2)
            logits = [_sb_logits(qh[h][rows], kb) for h in hs]
            lb = [logits[h][0] for h in hs]
            l1m = [logits[h][1] if causal is None else jnp.where(causal, logits[h][1], 0.0) for h in hs]
            tail = [jnp.dot(l1m[h].astype(BF16), later, preferred_element_type=F32) + run_ref[h, rows] for h in hs]
            w = [jnp.exp(lb[h] + tail[h]) for h in hs]
            if causal is not None:
                w = [jnp.where(causal, w[h], 0.0) for h in hs]
            vh = [jnp.where(head_mask[h], vb, jnp.zeros_like(vb)) for h in hs]
            acc_ref[rows, :] += (jnp.dot(w[0].astype(BF16), vh[0], preferred_element_type=F32)
                                 + jnp.dot(w[1].astype(BF16), vh[1], preferred_element_type=F32))
            for h in hs:
                run_ref[h, rows] += jnp.sum(l1m[h], axis=1, keepdims=True)

        for b in reversed(range(r)):
            block(i * r + b, b * tk)
        n_full = i * r

        def more(c):
            return jnp.logical_and(c[0] < n_full, c[1] > LOG_ZERO)

        def step(c):
            block(n_full - 1 - c[0], None)
            return c[0] + 1, jnp.max(run_ref[...])

        visited, _ = lax.while_loop(more, step, (jnp.int32(0), jnp.float32(0.0)))
        o_ref[...] = acc_ref[...]
        lane = lax.broadcasted_iota(jnp.int32, (1, LANE), 1)
        first = (n_full - visited).astype(F32)
        l_ref[...] = jnp.where(lane < HEAD_DIM // 2, run_ref[0], jnp.where(lane < HEAD_DIM, first, run_ref[1]))

    kq, kk, kv = OFF_Q // LANE, OFF_K // LANE, OFF_V // LANE
    tile = pl.BlockSpec((tq, LANE), lambda p, i: (i, p))
    return pl.pallas_call(
        body, name="attn_fwd",
        out_shape=(jax.ShapeDtypeStruct((s, D_ATTN), F32), jax.ShapeDtypeStruct((s, D_ATTN), F32)),
        grid=(N_PAIRS, s // tq),
        in_specs=[pl.BlockSpec((tq, LANE), lambda p, i: (i, kq + p)),
                  pl.BlockSpec((s, LANE), lambda p, i: (0, kk + p)),
                  pl.BlockSpec((s, LANE), lambda p, i: (0, kv + p))],
        out_specs=(tile, tile),
        scratch_shapes=[pltpu.VMEM((tq, LANE), F32), pltpu.VMEM((2, tq, 1), F32)],
        compiler_params=_cparams(("parallel", "arbitrary")),
    )(proj, proj, proj)


def _attn_bwd(proj, do, lsum):
    s = proj.shape[0]
    tq, tk = min(ATTN_TQ, s), min(ATTN_TK, s)
    r = tq // tk

    def body(q_ref, k_ref, v_ref, do_ref, l_ref, dq_ref, dk_ref, dv_ref, dqacc_ref, dkacc_ref, dvacc_ref,
             passed_ref, pre_ref):
        i = pl.program_id(1)

        @pl.when(i == 0)
        def _():
            dkacc_ref[...] = jnp.zeros_like(dkacc_ref)
            dvacc_ref[...] = jnp.zeros_like(dvacc_ref)

        head_mask, row, col = _attn_consts(tk)
        later = (row > col).astype(BF16)
        earlier = (row < col).astype(BF16)
        q = q_ref[...] * ATTN_SCALE
        dov = do_ref[...].astype(BF16)
        qh = [jnp.where(m, q, jnp.zeros_like(q)) for m in head_mask]
        doh = [jnp.where(m, dov, jnp.zeros_like(dov)) for m in head_mask]
        lsum_v = l_ref[...]
        lh = [lsum_v[:, 0:1], lsum_v[:, HEAD_DIM:HEAD_DIM + 1]]
        n_full = i * r
        first = jnp.clip(jnp.max(lsum_v[0:8, HEAD_DIM // 2:HEAD_DIM]).astype(jnp.int32), 0, n_full)
        dqacc_ref[...] = jnp.zeros_like(dqacc_ref)
        passed_ref[...] = jnp.zeros_like(passed_ref)
        pre_ref[...] = jnp.zeros_like(pre_ref)

        def block(j, lo):
            start = pl.multiple_of(j * tk, tk)
            kb = k_ref[pl.ds(start, tk), :]
            vb = v_ref[pl.ds(start, tk), :]
            rows = slice(0 if lo is None else lo, tq)
            causal = None if lo is None else _band_mask(tq - lo, tk)
            hs = range(2)
            q_rows = [qh[h][rows] for h in hs]
            do_rows = [doh[h][rows] for h in hs]
            logits = [_sb_logits(q_rows[h], kb) for h in hs]
            lb = [logits[h][0] for h in hs]
            l1m = [logits[h][1] if causal is None else jnp.where(causal, logits[h][1], 0.0) for h in hs]
            da = [lax.dot_general(do_rows[h], vb, NT, preferred_element_type=F32) for h in hs]
            rs = [jnp.sum(l1m[h], axis=1, keepdims=True) for h in hs]
            right = [lh[h][rows] - passed_ref[h, rows] - rs[h] for h in hs]
            for h in hs:
                passed_ref[h, rows] += rs[h]
            tail = [jnp.dot(l1m[h].astype(BF16), later, preferred_element_type=F32) + right[h] for h in hs]
            a = [jnp.exp(lb[h] + tail[h]) for h in hs]
            if causal is not None:
                a = [jnp.where(causal, a[h], 0.0) for h in hs]
            g = [a[h] * da[h] for h in hs]
            pre = [jnp.dot(g[h].astype(BF16), earlier, preferred_element_type=F32) + pre_ref[h, rows] for h in hs]
            for h in hs:
                pre_ref[h, rows] += jnp.sum(g[h], axis=1, keepdims=True)
            dz = [g[h] - jnp.exp(lb[h]) * (g[h] + pre[h]) for h in hs]
            if causal is not None:
                dz = [jnp.where(causal, dz[h], 0.0) for h in hs]
            dzb = [dz[h].astype(BF16) for h in hs]
            kh = [jnp.where(head_mask[h], kb, jnp.zeros_like(kb)) * ATTN_SCALE for h in hs]
            dqacc_ref[rows, :] += (jnp.dot(dzb[0], kh[0], preferred_element_type=F32)
                                   + jnp.dot(dzb[1], kh[1], preferred_element_type=F32))
            dvacc_ref[pl.ds(start, tk), :] += (
                lax.dot_general(a[0].astype(BF16), do_rows[0], TN, preferred_element_type=F32)
                + lax.dot_general(a[1].astype(BF16), do_rows[1], TN, preferred_element_type=F32))
            dkacc_ref[pl.ds(start, tk), :] += (
                lax.dot_general(dzb[0], q_rows[0], TN, preferred_element_type=F32)
                + lax.dot_general(dzb[1], q_rows[1], TN, preferred_element_type=F32))

        def step(j, carry):
            block(j, None)
            return carry

        lax.fori_loop(first, n_full, step, 0)
        for b in range(r):
            block(n_full + b, b * tk)
        dq_ref[...] = dqacc_ref[...].astype(BF16)

        @pl.when(i == pl.num_programs(1) - 1)
        def _():
            dk_ref[...] = dkacc_ref[...].astype(BF16)
            dv_ref[...] = dvacc_ref[...].astype(BF16)

    kq, kk, kv = OFF_Q // LANE, OFF_K // LANE, OFF_V // LANE
    tile = pl.BlockSpec((tq, LANE), lambda p, i: (i, p))
    full = pl.BlockSpec((s, LANE), lambda p, i: (0, p))
    shp = jax.ShapeDtypeStruct((s, D_ATTN), BF16)
    return pl.pallas_call(
        body, name="attn_bwd", out_shape=(shp, shp, shp), grid=(N_PAIRS, s // tq),
        in_specs=[pl.BlockSpec((tq, LANE), lambda p, i: (i, kq + p)),
                  pl.BlockSpec((s, LANE), lambda p, i: (0, kk + p)),
                  pl.BlockSpec((s, LANE), lambda p, i: (0, kv + p)),
                  tile, tile],
        out_specs=(tile, full, full),
        scratch_shapes=[pltpu.VMEM((tq, LANE), F32), pltpu.VMEM((s, LANE), F32), pltpu.VMEM((s, LANE), F32),
                        pltpu.VMEM((2, tq, 1), F32), pltpu.VMEM((2, tq, 1), F32)],
        compiler_params=_cparams(("parallel", "arbitrary")),
    )(proj, proj, proj, do, lsum)


def _shift_down(u, k, rows):
    return jnp.where(rows >= k, pltpu.roll(u, k, 0), 0.0)


def _shift_up(u, k, rows, s):
    return jnp.where(rows < s - k, pltpu.roll(u, s - k, 0), 0.0)


def _conv_fwd(proj, w, b):
    s = proj.shape[0]
    blk0 = OFF_XBC // LANE

    def body(u_ref, w_ref, b_ref, o_ref):
        u = u_ref[...].astype(F32)
        rows = lax.broadcasted_iota(jnp.int32, (s, 1), 0)
        pre = u * w_ref[CONV_K - 1:CONV_K, :] + b_ref[...]
        for k in range(1, CONV_K):
            pre += _shift_down(u, k, rows) * w_ref[CONV_K - 1 - k:CONV_K - k, :]
        o_ref[...] = pre * _sigmoid(pre)

    return pl.pallas_call(
        body, name="conv_fwd", out_shape=jax.ShapeDtypeStruct((s, D_XBC), F32), grid=(D_XBC // LANE,),
        in_specs=[pl.BlockSpec((s, LANE), lambda j: (0, blk0 + j)), pl.BlockSpec((CONV_K, LANE), lambda j: (0, j)),
                  pl.BlockSpec((1, LANE), lambda j: (0, j))],
        out_specs=pl.BlockSpec((s, LANE), lambda j: (0, j)), compiler_params=_cparams(("parallel",)),
    )(proj, w, b)


def _conv_bwd(proj, w, b, dact):
    s = proj.shape[0]
    blk0 = OFF_XBC // LANE

    def body(u_ref, w_ref, b_ref, da_ref, du_ref, dw_ref, db_ref):
        u = u_ref[...].astype(F32)
        rows = lax.broadcasted_iota(jnp.int32, (s, 1), 0)
        shifted = [u] + [_shift_down(u, k, rows) for k in range(1, CONV_K)]
        pre = b_ref[...] + shifted[0] * w_ref[CONV_K - 1:CONV_K, :]
        for k in range(1, CONV_K):
            pre += shifted[k] * w_ref[CONV_K - 1 - k:CONV_K - k, :]
        sg = _sigmoid(pre)
        dpre = da_ref[...] * _silu_grad(pre, sg)
        db_ref[...] = jnp.sum(dpre, axis=0, keepdims=True)
        du = dpre * w_ref[CONV_K - 1:CONV_K, :]
        for k in range(CONV_K):
            dw_ref[CONV_K - 1 - k:CONV_K - k, :] = jnp.sum(dpre * shifted[k], axis=0, keepdims=True)
            if k:
                du += _shift_up(dpre, k, rows, s) * w_ref[CONV_K - 1 - k:CONV_K - k, :]
        du_ref[...] = du.astype(BF16)

    col = pl.BlockSpec((s, LANE), lambda j: (0, j))
    return pl.pallas_call(
        body, name="conv_bwd",
        out_shape=(jax.ShapeDtypeStruct((s, D_XBC), BF16), jax.ShapeDtypeStruct((CONV_K, D_XBC), F32),
                   jax.ShapeDtypeStruct((1, D_XBC), F32)),
        grid=(D_XBC // LANE,),
        in_specs=[pl.BlockSpec((s, LANE), lambda j: (0, blk0 + j)), pl.BlockSpec((CONV_K, LANE), lambda j: (0, j)),
                  pl.BlockSpec((1, LANE), lambda j: (0, j)), col],
        out_specs=(col, pl.BlockSpec((CONV_K, LANE), lambda j: (0, j)), pl.BlockSpec((1, LANE), lambda j: (0, j))),
        compiler_params=_cparams(("parallel",)),
    )(proj, w, b, dact)


def _ssd_decays(dtraw_ref, bias_ref, dtt_ref, biast_ref, arow_ref, acol_ref):
    ln = CHUNK
    dt = _softplus(dtraw_ref[...] + bias_ref[...])
    r = lax.broadcasted_iota(jnp.int32, (ln, ln), 0)
    c = lax.broadcasted_iota(jnp.int32, (ln, ln), 1)
    ac = jnp.dot((r >= c).astype(F32), dt * arow_ref[...], preferred_element_type=F32, precision=HI)
    dtt = _softplus(dtt_ref[...] + biast_ref[...])
    act = jnp.dot(dtt * acol_ref[...], (r <= c).astype(F32), preferred_element_type=F32, precision=HI)
    return dt, ac, act, r >= c


def _pair_cols(m0, v, h0):
    return jnp.where(m0, v[:, h0:h0 + 1], v[:, h0 + 1:h0 + 2])


def _ssd_fwd(act, dtraw, dtt, bias, biast, arow, acol, dskip):
    s = act.shape[0]
    ln = CHUNK
    nc = s // ln

    def body(act_ref, dtraw_ref, dtt_ref, bias_ref, biast_ref, arow_ref, acol_ref, dsk_ref, y_ref, st_ref,
             state_ref):
        @pl.when(pl.program_id(0) == 0)
        def _():
            state_ref[...] = jnp.zeros_like(state_ref)

        dt, ac, act_t, lower = _ssd_decays(dtraw_ref, bias_ref, dtt_ref, biast_ref, arow_ref, acol_ref)
        lane = lax.broadcasted_iota(jnp.int32, (1, LANE), 1)
        m0 = lane < HEAD_DIM
        top = lax.broadcasted_iota(jnp.int32, (LANE, 1), 0) < HEAD_DIM
        for g in range(N_GROUPS):
            bg = act_ref[:, D_SSM + g * D_STATE:D_SSM + (g + 1) * D_STATE].astype(BF16)
            cg = act_ref[:, D_SSM + (N_GROUPS + g) * D_STATE:D_SSM + (N_GROUPS + g + 1) * D_STATE].astype(BF16)
            cb = lax.dot_general(cg, bg, NT, preferred_element_type=F32)
            for p in range(g * 4, g * 4 + 4):
                h0 = 2 * p
                xp = act_ref[:, p * LANE:(p + 1) * LANE]
                xdt = xp * _pair_cols(m0, dt, h0)
                acp = _pair_cols(m0, ac, h0)
                last = acp[ln - 1:ln, :]
                y = xp * dsk_ref[:, p * LANE:(p + 1) * LANE]
                for hh in range(2):
                    h = h0 + hh
                    dm = jnp.exp(jnp.where(lower, ac[:, h:h + 1] - act_t[h:h + 1, :], -jnp.inf))
                    mask = m0 if hh == 0 else jnp.logical_not(m0)
                    y += jnp.dot((cb * dm).astype(BF16), jnp.where(mask, xdt, 0.0).astype(BF16),
                                 preferred_element_type=F32)
                prev = state_ref[p]
                st_ref[0, p] = prev
                y += lax.dot_general(cg, prev.astype(BF16), NT, preferred_element_type=F32) * jnp.exp(acp)
                y_ref[:, p * LANE:(p + 1) * LANE] = y
                cs = lax.dot_general((xdt * jnp.exp(last - acp)).astype(BF16), bg, TN, preferred_element_type=F32)
                dec = jnp.where(top, jnp.exp(ac[ln - 1:ln, h0:h0 + 1]), jnp.exp(ac[ln - 1:ln, h0 + 1:h0 + 2]))
                state_ref[p] = prev * dec + cs

    row = lambda w: pl.BlockSpec((1, w), lambda c: (0, 0))
    return pl.pallas_call(
        body, name="ssd_fwd",
        out_shape=(jax.ShapeDtypeStruct((s, D_SSM), F32),
                   jax.ShapeDtypeStruct((nc, N_PAIRS, LANE, D_STATE), F32)),
        grid=(nc,),
        in_specs=[pl.BlockSpec((ln, D_XBC), lambda c: (c, 0)), pl.BlockSpec((ln, LANE), lambda c: (c, 0)),
                  pl.BlockSpec((N_HEADS, ln), lambda c: (0, c)), row(LANE),
                  pl.BlockSpec((N_HEADS, 1), lambda c: (0, 0)), row(LANE),
                  pl.BlockSpec((N_HEADS, 1), lambda c: (0, 0)), row(D_SSM)],
        out_specs=(pl.BlockSpec((ln, D_SSM), lambda c: (c, 0)),
                   pl.BlockSpec((1, N_PAIRS, LANE, D_STATE), lambda c: (c, 0, 0, 0))),
        scratch_shapes=[pltpu.VMEM((N_PAIRS, LANE, D_STATE), F32)],
        compiler_params=_cparams(("arbitrary",)),
    )(act, dtraw, dtt, bias, biast, arow, acol, dskip)


def _ssd_bwd(act, dtraw, dtt, bias, biast, arow, acol, dskip, states, dy):
    s = act.shape[0]
    ln = CHUNK
    nc = s // ln

    def body(act_ref, dtraw_ref, dtt_ref, bias_ref, biast_ref, arow_ref, acol_ref, dsk_ref, st_ref, dy_ref,
             dact_ref, dldc_ref, dldr_ref, ddt_ref, dd_ref, dstate_ref):
        @pl.when(pl.program_id(0) == 0)
        def _():
            dstate_ref[...] = jnp.zeros_like(dstate_ref)
            dd_ref[...] = jnp.zeros_like(dd_ref)

        dt, ac, act_t, lower = _ssd_decays(dtraw_ref, bias_ref, dtt_ref, biast_ref, arow_ref, acol_ref)
        lane = lax.broadcasted_iota(jnp.int32, (1, LANE), 1)
        m0 = lane < HEAD_DIM
        halves = (m0, jnp.logical_not(m0))
        top = lax.broadcasted_iota(jnp.int32, (LANE, 1), 0) < HEAD_DIM
        is_last = lax.broadcasted_iota(jnp.int32, (ln, 1), 0) == ln - 1
        sub = lax.broadcasted_iota(jnp.int32, (N_HEADS, 1), 0)
        dac_col = jnp.zeros((ln, LANE), F32)
        dac_row = jnp.zeros((N_HEADS, ln), F32)
        ddt_col = jnp.zeros((ln, LANE), F32)

        def half_sum(v, hh):
            return jnp.sum(jnp.where(halves[hh], v, 0.0), axis=1, keepdims=True)

        for g in range(N_GROUPS):
            b_lo, c_lo = D_SSM + g * D_STATE, D_SSM + (N_GROUPS + g) * D_STATE
            bg32 = act_ref[:, b_lo:b_lo + D_STATE]
            cg32 = act_ref[:, c_lo:c_lo + D_STATE]
            bg, cg = bg32.astype(BF16), cg32.astype(BF16)
            cb = lax.dot_general(cg, bg, NT, preferred_element_type=F32)
            dcb = jnp.zeros((ln, ln), F32)
            dbg = jnp.zeros((ln, D_STATE), F32)
            dcg = jnp.zeros((ln, D_STATE), F32)
            for p in range(g * 4, g * 4 + 4):
                h0 = 2 * p
                cols = slice(p * LANE, (p + 1) * LANE)
                xp = act_ref[:, cols]
                dyp = dy_ref[:, cols]
                dtp = _pair_cols(m0, dt, h0)
                acp = _pair_cols(m0, ac, h0)
                last = acp[ln - 1:ln, :]
                xdt = xp * dtp
                eac = jnp.exp(acp)
                dte = jnp.exp(last - acp)
                prev = st_ref[0, p]
                prev_b = prev.astype(BF16)
                ds = dstate_ref[p]
                ds_b = ds.astype(BF16)
                dec_h = [jnp.exp(ac[ln - 1:ln, h0 + hh:h0 + hh + 1]) for hh in range(2)]
                dec = jnp.where(top, dec_h[0], dec_h[1])

                dd_ref[:, cols] += jnp.sum(dyp * xp, axis=0, keepdims=True)
                dx = dyp * dsk_ref[:, cols]
                zoff = lax.dot_general(cg, prev_b, NT, preferred_element_type=F32)
                dz_b = (dyp * eac).astype(BF16)
                dcg += jnp.dot(dz_b, prev_b, preferred_element_type=F32)
                dprev = lax.dot_general(dz_b, cg, TN, preferred_element_type=F32) + ds * dec
                t_off = dyp * zoff * eac
                wmat = lax.dot_general(bg, ds_b, NT, preferred_element_type=F32)
                xdte_b = (xdt * dte).astype(BF16)
                dbg += jnp.dot(xdte_b, ds_b, preferred_element_type=F32)
                dxdt = dte * wmat
                t_dte = xdt * wmat * dte
                sp_rows = jnp.sum(ds * prev, axis=1, keepdims=True)
                for hh in range(2):
                    h = h0 + hh
                    here = lane == h
                    tdh = half_sum(t_dte, hh)
                    dac_col += jnp.where(here, half_sum(t_off, hh) - tdh, 0.0)
                    sp_h = jnp.sum(jnp.where(top if hh == 0 else jnp.logical_not(top), sp_rows, 0.0),
                                   axis=0, keepdims=True) * dec_h[hh]
                    at_last = sp_h + jnp.sum(tdh, axis=0, keepdims=True)
                    dac_col += jnp.where(jnp.logical_and(is_last, here), at_last, 0.0)
                    dm = jnp.exp(jnp.where(lower, ac[:, h:h + 1] - act_t[h:h + 1, :], -jnp.inf))
                    mm = cb * dm
                    dyh = jnp.where(halves[hh], dyp, 0.0).astype(BF16)
                    xdth = jnp.where(halves[hh], xdt, 0.0).astype(BF16)
                    dmm = lax.dot_general(dyh, xdth, NT, preferred_element_type=F32)
                    dxdt += lax.dot_general(mm.astype(BF16), dyh, TN, preferred_element_type=F32)
                    gm = dmm * mm
                    dcb += dmm * dm
                    dac_col += jnp.where(here, jnp.sum(gm, axis=1, keepdims=True), 0.0)
                    dac_row -= jnp.where(sub == h, jnp.sum(gm, axis=0, keepdims=True), 0.0)
                    ddt_col += jnp.where(here, half_sum(dxdt * xp, hh), 0.0)
                dact_ref[:, cols] = dx + dxdt * dtp
                dstate_ref[p] = dprev
            dcb_b = dcb.astype(BF16)
            dact_ref[:, b_lo:b_lo + D_STATE] = dbg + lax.dot_general(dcb_b, cg, TN, preferred_element_type=F32)
            dact_ref[:, c_lo:c_lo + D_STATE] = dcg + jnp.dot(dcb_b, bg, preferred_element_type=F32)

        r = lax.broadcasted_iota(jnp.int32, (ln, ln), 0)
        c = lax.broadcasted_iota(jnp.int32, (ln, ln), 1)
        dldc_ref[...] = jnp.dot((r <= c).astype(F32), dac_col, preferred_element_type=F32, precision=HI)
        dldr_ref[...] = jnp.dot(dac_row, (r >= c).astype(F32), preferred_element_type=F32, precision=HI)
        ddt_ref[...] = ddt_col

    rev = lambda c: nc - 1 - c
    row = lambda w: pl.BlockSpec((1, w), lambda c: (0, 0))
    col16 = pl.BlockSpec((N_HEADS, 1), lambda c: (0, 0))
    chunk128 = pl.BlockSpec((ln, LANE), lambda c: (rev(c), 0))
    return pl.pallas_call(
        body, name="ssd_bwd",
        out_shape=(jax.ShapeDtypeStruct((s, D_XBC), F32), jax.ShapeDtypeStruct((s, LANE), F32),
                   jax.ShapeDtypeStruct((N_HEADS, s), F32), jax.ShapeDtypeStruct((s, LANE), F32),
                   jax.ShapeDtypeStruct((1, D_SSM), F32)),
        grid=(nc,),
        in_specs=[pl.BlockSpec((ln, D_XBC), lambda c: (rev(c), 0)), chunk128,
                  pl.BlockSpec((N_HEADS, ln), lambda c: (0, rev(c))), row(LANE), col16, row(LANE), col16,
                  row(D_SSM), pl.BlockSpec((1, N_PAIRS, LANE, D_STATE), lambda c: (rev(c), 0, 0, 0)),
                  pl.BlockSpec((ln, D_SSM), lambda c: (rev(c), 0))],
        out_specs=(pl.BlockSpec((ln, D_XBC), lambda c: (rev(c), 0)), chunk128,
                   pl.BlockSpec((N_HEADS, ln), lambda c: (0, rev(c))), chunk128, row(D_SSM)),
        scratch_shapes=[pltpu.VMEM((N_PAIRS, LANE, D_STATE), F32)],
        compiler_params=_cparams(("arbitrary",)),
    )(act, dtraw, dtt, bias, biast, arow, acol, dskip, states, dy)


def _dt_bwd(dtraw, bias, arow, dld_col, dld_row_t, ddt_col):
    s = dtraw.shape[0]
    tm = min(512, s)

    def body(raw_ref, bias_ref, a_ref, dc_ref, dr_ref, dd_ref, out_ref, sums_ref):
        @pl.when(pl.program_id(0) == 0)
        def _():
            sums_ref[...] = jnp.zeros_like(sums_ref)

        raw = raw_ref[...] + bias_ref[...]
        dld = dc_ref[...] + dr_ref[...]
        ddt = dld * a_ref[...] + dd_ref[...]
        draw = ddt * _sigmoid(raw)
        out_ref[...] = draw.astype(BF16)
        sums_ref[0:1, :] += jnp.sum(draw, axis=0, keepdims=True)
        sums_ref[1:2, :] += jnp.sum(dld * _softplus(raw), axis=0, keepdims=True)

    tile = pl.BlockSpec((tm, LANE), lambda i: (i, 0))
    row = pl.BlockSpec((1, LANE), lambda i: (0, 0))
    return pl.pallas_call(
        body, name="dt_bwd",
        out_shape=(jax.ShapeDtypeStruct((s, LANE), BF16), jax.ShapeDtypeStruct((2, LANE), F32)), grid=(s // tm,),
        in_specs=[tile, row, row, tile, tile, tile], out_specs=(tile, pl.BlockSpec((2, LANE), lambda i: (0, 0))),
        compiler_params=_cparams(("arbitrary",)),
    )(dtraw, bias, arow, dld_col, dld_row_t, ddt_col)


def _sum8(parts):
    nb, n = parts.shape

    def body(p_ref, o_ref):
        acc = p_ref[0:1, :]
        for b in range(1, nb):
            acc = acc + p_ref[b:b + 1, :]
        o_ref[...] = acc

    return pl.pallas_call(body, name="sum8", out_shape=jax.ShapeDtypeStruct((1, n), F32),
                          compiler_params=_cparams())(parts)


def _outer8(act_t, dmod):
    d, nb = act_t.shape
    n = dmod.shape[1]

    def body(a_ref, m_ref, o_ref):
        acc = a_ref[:, 0:1] * m_ref[0:1, :]
        for b in range(1, nb):
            acc = acc + a_ref[:, b:b + 1] * m_ref[b:b + 1, :]
        o_ref[...] = acc

    return pl.pallas_call(body, name="outer8", out_shape=jax.ShapeDtypeStruct((d, n), F32),
                          compiler_params=_cparams())(act_t, dmod)


def _pad_lanes(v, width=LANE):
    return jnp.pad(v, ((0, 0), (0, width - v.shape[1])))


def kernel(x, c, w_ada, b_ada, norm_in_gain, w_in, conv_w, conv_b, dt_bias, a_log, d_skip, sb_norm_gain, ssm_norm_gain, w_out, norm_f_gain, loss_target, m_w_ada, m_b_ada, m_norm_in_gain, m_w_in, m_conv_w, m_conv_b, m_dt_bias, m_a_log, m_d_skip, m_sb_norm_gain, m_ssm_norm_gain, m_w_out, m_norm_f_gain, v_w_ada, v_b_ada, v_norm_in_gain, v_w_in, v_conv_w, v_conv_b, v_dt_bias, v_a_log, v_d_skip, v_sb_norm_gain, v_ssm_norm_gain, v_w_out, v_norm_f_gain):
    ax, ay, ac_ = _coords()
    chip = 2 * ax + ay
    me = 2 * chip + ac_
    my_c = jnp.reshape(ac_, (1,)).astype(jnp.int32)
    x2d, tgt = x[0], loss_target[0]
    s = x2d.shape[0]
    ada_cols = w_ada.shape[2]
    cw_cols = conv_w.shape[2]
    in_cols = w_in.shape[2]
    out_rows = w_out.shape[1]

    small = jnp.concatenate([c, conv_w[0].reshape(1, CONV_K * cw_cols)], axis=1)
    small_all = _allgather8(small, "gather_cond")[:, 0, :]
    c_all = small_all[:, :D_MODEL]
    conv_w_full = (small_all[0::2, D_MODEL:].reshape(N_CHIPS, CONV_K, cw_cols)
                   .transpose(1, 0, 2).reshape(CONV_K, D_XBC))
    b_ada_shard = lax.dynamic_slice_in_dim(b_ada, chip * ada_cols, ada_cols, axis=1)
    mod_part, c_act_all = _ada_mod(c_all, w_ada[0], b_ada_shard)
    mod_all = _allgather8(mod_part.reshape(1, N_DEV * ada_cols), "gather_mod")[0::2, 0, :]
    mod_all = mod_all.reshape(N_CHIPS, N_DEV, ada_cols)
    mod = lax.dynamic_index_in_dim(mod_all, me, axis=1, keepdims=False).reshape(1, 3 * D_MODEL)
    shift, scale, gate = mod[:, :D_MODEL], mod[:, D_MODEL:2 * D_MODEL], mod[:, 2 * D_MODEL:]

    w_in_all, w_out_all = _gather_shards([w_in[0].T.astype(BF16), w_out[0].astype(BF16)], "gather_weights")
    w_in_t = w_in_all.reshape(D_PROJ, D_MODEL)
    dt_lo = 4 * D_ATTN + D_XBC
    w_main_t = jnp.concatenate([w_in_t[:4 * D_ATTN], w_in_t[dt_lo + N_HEADS:], w_in_t[4 * D_ATTN:dt_lo]], axis=0)
    w_dt_t = jnp.pad(w_in_t[dt_lo:dt_lo + N_HEADS], ((0, LANE - N_HEADS), (0, 0)))
    w_out_full = w_out_all.reshape(N_CHIPS * out_rows, D_MODEL)

    h = _rms_mod_fwd(x2d, norm_in_gain, scale, shift)
    proj = _matmul(h, w_main_t, BF16, "in_proj", "nt", 1024, 512, 1024)
    dtraw = _matmul(h, w_dt_t, F32, "in_proj_dt", "nt", 1024, LANE, 1024)
    o_attn, lsum = _attn_fwd(proj)
    y_attn = _gated_norm_fwd(o_attn, proj, OFF_ZA, sb_norm_gain, False, "attn_gate_fwd")
    act = _conv_fwd(proj, conv_w_full, conv_b)
    a_neg = -jnp.exp(a_log)
    arow, acol = _pad_lanes(a_neg), a_neg.reshape(N_HEADS, 1)
    bias_row, bias_col = _pad_lanes(dt_bias), dt_bias.reshape(N_HEADS, 1)
    dtt = dtraw[:, :N_HEADS].T
    dskip_row = jnp.repeat(d_skip, HEAD_DIM, axis=1)
    ssd_args = (act, dtraw, dtt, bias_row, bias_col, arow, acol, dskip_row)
    y_ssd, states = _ssd_fwd(*ssd_args)
    y_ssm = _gated_norm_fwd(y_ssd, proj, OFF_ZS, ssm_norm_gain, True, "ssm_gate_fwd")
    mix_in = jnp.concatenate([y_attn, y_ssm], axis=1)
    mixed = _matmul(mix_in, w_out_full, F32, "out_proj", "nn", 1024, 1024, 2048)

    dx2, dmixed, head_sums = _loss_head(x2d, mixed, gate, norm_f_gain.reshape(1, D_MODEL), tgt)
    g_w_out = _matmul(mix_in, dmixed, F32, "out_proj_dw", "tn", 1024, 1024, 512)
    d_mix_in = _matmul(dmixed, w_out_full, F32, "out_proj_dx", "nt", 1024, 1024, 1024)
    d_o, dz_attn, g_sb = _gated_norm_bwd(d_mix_in, 0, o_attn, proj, OFF_ZA, sb_norm_gain, False, "attn_gate_bwd")
    d_y, dz_ssm, g_ssm = _gated_norm_bwd(d_mix_in, 1, y_ssd, proj, OFF_ZS, ssm_norm_gain, True, "ssm_gate_bwd")
    dq, dk, dv = _attn_bwd(proj, d_o, lsum)
    dact, dld_col, dld_row, ddt_col, dd_cols = _ssd_bwd(*ssd_args, states, d_y)
    dxbc, g_conv_w, g_conv_b = _conv_bwd(proj, conv_w_full, conv_b, dact)
    ddtraw, dt_sums = _dt_bwd(dtraw, bias_row, arow, dld_col, _pad_lanes(dld_row.T), ddt_col)
    dproj = jnp.concatenate([dq, dk, dv, dz_attn, dz_ssm, dxbc], axis=1)
    g_main_t = _matmul(dproj, h, F32, "in_proj_dw", "tn", 512, 1024, 512)
    g_dt_t = _matmul(ddtraw, h, F32, "in_proj_dw_dt", "tn", LANE, 1024, 512)
    dh = _matmul(dproj, w_main_t, F32, "in_proj_dx", "nn", 1024, 1024, 512, extra=(ddtraw, w_dt_t))
    grad_x, in_sums = _rms_mod_bwd(x2d, dh, dx2, norm_in_gain, scale)

    g_a_log = dt_sums[1:2, :N_HEADS] * a_neg
    g_d_skip = jnp.sum(dd_cols.reshape(N_HEADS, HEAD_DIM), axis=1).reshape(1, N_HEADS)
    dmod = jnp.concatenate([in_sums[0:1], in_sums[1:2], head_sums[2:3]], axis=1)
    loss_part = 0.5 / D_MODEL * jnp.sum(head_sums[0:1], axis=1, keepdims=True)
    pieces = [dmod, in_sums[2:3], g_conv_w.reshape(1, CONV_K * D_XBC), g_conv_b, _pad_lanes(dt_sums[0:1, :N_HEADS]),
              _pad_lanes(g_a_log), _pad_lanes(g_d_skip), g_sb, g_ssm, head_sums[1:2], _pad_lanes(loss_part)]
    widths = [p.shape[1] for p in pieces]
    parts_all = _allgather8(jnp.concatenate(pieces, axis=1), "gather_small_grads")[:, 0, :]
    total = _sum8(parts_all)
    offs = [0]
    for w_ in widths:
        offs.append(offs[-1] + w_)
    tot = [total[:, offs[i]:offs[i + 1]] for i in range(len(pieces))]
    g_b_ada, g_norm_in, g_conv_w_full = tot[0], tot[1], tot[2].reshape(CONV_K, D_XBC)
    g_conv_b_t, g_dt_bias, g_a_log_t, g_d_skip_t = tot[3], tot[4][:, :N_HEADS], tot[5][:, :N_HEADS], tot[6][:, :N_HEADS]
    g_sb_t, g_ssm_t, g_norm_f, loss = tot[7], tot[8], tot[9], tot[10][0, 0]
    g_conv_w_shard = lax.dynamic_slice_in_dim(g_conv_w_full, chip * cw_cols, cw_cols, axis=1)
    dmod_shard = lax.dynamic_slice_in_dim(parts_all[:, :3 * D_MODEL], chip * ada_cols, ada_cols, axis=1)
    g_w_ada = _outer8(c_act_all.T, dmod_shard)

    g_in_t = jnp.concatenate([g_main_t[:4 * D_ATTN], g_main_t[OFF_XBC:], g_dt_t[:N_HEADS], g_main_t[OFF_ZS:OFF_XBC]],
                             axis=0)
    g_in_blocks = g_in_t.reshape(N_CHIPS, in_cols, D_MODEL)
    g_out_blocks = g_w_out.reshape(N_CHIPS, out_rows, D_MODEL)
    land_in, land_out = _send_to_sibling([g_in_blocks, g_out_blocks], "grads_to_sibling")
    chip_in = _add_my_half(g_in_blocks, land_in, my_c, "add_sibling_in")
    chip_out = _add_my_half(g_out_blocks, land_out, my_c, "add_sibling_out")
    slots_in, slots_out = _exchange_chips([chip_in, chip_out], "grads_between_chips")
    half_in, half_out = _sum_slots(slots_in, "sum_chips_in"), _sum_slots(slots_out, "sum_chips_out")
    both_in, both_out = _share_with_sibling([half_in, half_out], "grads_share_sibling")
    g_w_in = both_in.T
    g_w_out_shard = both_out

    d_w_ada, nm_w_ada, nv_w_ada = _adamw(w_ada[0], g_w_ada, m_w_ada[0], v_w_ada[0], "adamw_w_ada")
    d_w_in, nm_w_in, nv_w_in = _adamw(w_in[0], g_w_in, m_w_in[0], v_w_in[0], "adamw_w_in")
    d_w_out, nm_w_out, nv_w_out = _adamw(w_out[0], g_w_out_shard, m_w_out[0], v_w_out[0], "adamw_w_out")
    flat = lambda a: a.reshape(1, -1)
    small_w = [b_ada, norm_in_gain, conv_w[0], conv_b, dt_bias, a_log, d_skip, sb_norm_gain, ssm_norm_gain,
               norm_f_gain]
    small_m = [m_b_ada, m_norm_in_gain, m_conv_w[0], m_conv_b, m_dt_bias, m_a_log, m_d_skip, m_sb_norm_gain,
               m_ssm_norm_gain, m_norm_f_gain]
    small_v = [v_b_ada, v_norm_in_gain, v_conv_w[0], v_conv_b, v_dt_bias, v_a_log, v_d_skip, v_sb_norm_gain,
               v_ssm_norm_gain, v_norm_f_gain]
    small_g = [g_b_ada, g_norm_in, g_conv_w_shard, g_conv_b_t, g_dt_bias, g_a_log_t, g_d_skip_t, g_sb_t, g_ssm_t,
               g_norm_f]
    cat = lambda arrs: jnp.concatenate([flat(a) for a in arrs], axis=1)
    d_small, nm_small, nv_small = _adamw(cat(small_w), cat(small_g), cat(small_m), cat(small_v), "adamw_small")
    sizes = [a.size for a in small_w]
    soffs = [0]
    for n_ in sizes:
        soffs.append(soffs[-1] + n_)

    def split(packed):
        return [packed[0, soffs[i]:soffs[i + 1]].reshape(small_w[i].shape) for i in range(len(small_w))]

    def ordered(big_ada, big_in, big_out, smalls):
        (s_b_ada, s_norm_in, s_conv_w, s_conv_b, s_dt_bias, s_a_log, s_d_skip, s_sb, s_ssm, s_norm_f) = smalls
        return [big_ada[None], s_b_ada, s_norm_in, big_in[None], s_conv_w[None], s_conv_b, s_dt_bias, s_a_log,
                s_d_skip, s_sb, s_ssm, big_out[None], s_norm_f]

    grads = ordered(g_w_ada, g_w_in, g_w_out_shard,
                    [g.reshape(w_.shape) for g, w_ in zip(small_g, small_w)])
    deltas = ordered(d_w_ada, d_w_in, d_w_out, split(d_small))
    new_m = ordered(nm_w_ada, nm_w_in, nm_w_out, split(nm_small))
    new_v = ordered(nv_w_ada, nv_w_in, nv_w_out, split(nv_small))
    return (loss, grad_x[None], *grads, *deltas, *new_m, *new_v)
```

```python
import functools

import jax
import jax.numpy as jnp
from jax import lax
from jax.experimental import pallas as pl
from jax.experimental.pallas import tpu as pltpu

F32, BF16 = jnp.float32, jnp.bfloat16
MESH = pl.DeviceIdType.MESH
HI = lax.Precision.HIGHEST
NN = (((1,), (0,)), ((), ()))
NT = (((1,), (1,)), ((), ()))
TN = (((0,), (0,)), ((), ()))

D_MODEL = 1024
D_ATTN = 1024
D_SSM = 1024
HEAD_DIM = 64
N_HEADS = 16
N_PAIRS = 8
N_GROUPS = 2
D_STATE = 128
D_XBC = 1536
D_PROJ = 6672
D_MAIN = 6656
CONV_K = 4
CHUNK = 128
LANE = 128
N_CHIPS = 4
N_DEV = 8
NORM_EPS = 1e-6
ATTN_SCALE = HEAD_DIM ** -0.5
ATTN_TQ = 512
ATTN_TK = 256
LOG_ZERO = -110.0
ADAM_LR, ADAM_B1, ADAM_B2, ADAM_EPS, ADAM_WD, ADAM_STEP = 0.001, 0.9, 0.999, 1e-08, 0.01, 10
VMEM_LIMIT = 56 * 1024 * 1024

OFF_Q, OFF_K, OFF_V, OFF_ZA, OFF_ZS, OFF_XBC = 0, 1024, 2048, 3072, 4096, 5120


def _cparams(sem=None):
    return pltpu.CompilerParams(dimension_semantics=sem, vmem_limit_bytes=VMEM_LIMIT)


def _sigmoid(x):
    return 1.0 / (1.0 + jnp.exp(-x))


def _softplus(x):
    return jnp.maximum(x, 0.0) + jnp.log(1.0 + jnp.exp(-jnp.abs(x)))


def _coords():
    return lax.axis_index("x"), lax.axis_index("y"), lax.axis_index("c")


def _allgather8(v, name):
    n = v.shape[-1]

    def body(v_ref, out_ref, send_sems, recv_sems, local_sem):
        x, y, c = _coords()
        me = 4 * x + 2 * y + c
        mine = pltpu.make_async_copy(v_ref, out_ref.at[me], local_sem)
        mine.start()
        sends, recvs = [], []
        for j in range(1, N_DEV):
            px = 1 - x if (j >> 2) & 1 else x
            py = 1 - y if (j >> 1) & 1 else y
            pc = 1 - c if j & 1 else c
            peer = (px, py, pc)
            sends.append(pltpu.make_async_remote_copy(
                src_ref=v_ref, dst_ref=out_ref.at[me], send_sem=send_sems.at[j - 1],
                recv_sem=recv_sems.at[j - 1], device_id=peer, device_id_type=MESH))
            recvs.append(pltpu.make_async_remote_copy(
                src_ref=v_ref, dst_ref=out_ref.at[4 * px + 2 * py + pc], send_sem=send_sems.at[j - 1],
                recv_sem=recv_sems.at[j - 1], device_id=peer, device_id_type=MESH))
        for s in sends:
            s.start()
        for r in recvs:
            r.wait_recv()
        for s in sends:
            s.wait_send()
        mine.wait()

    vm = pl.BlockSpec(memory_space=pltpu.VMEM)
    return pl.pallas_call(
        body, name=name, out_shape=jax.ShapeDtypeStruct((N_DEV, 1, n), F32),
        in_specs=[vm], out_specs=vm,
        scratch_shapes=[pltpu.SemaphoreType.DMA((N_DEV - 1,)), pltpu.SemaphoreType.DMA((N_DEV - 1,)),
                        pltpu.SemaphoreType.DMA(())],
    )(v)


def _other_chips(x, y):
    chips = [(1 - x, y), (x, 1 - y), (1 - x, 1 - y)]
    return chips, [2 * cx + cy for cx, cy in chips]


def _half_cols(width, which):
    half = width // 2
    return pl.ds(pl.multiple_of(which * half, half), half)


def _gather_shards(arrs, name):
    n = len(arrs)

    def body(*refs):
        ins, outs = refs[:n], refs[n:2 * n]
        send_sems, recv_sems = refs[2 * n:]
        x, y, c = _coords()
        k = 2 * x + y
        chips, chip_idx = _other_chips(x, y)
        sibling = (x, y, 1 - c)
        sends = []
        for a in range(n):
            mine = _half_cols(arrs[a].shape[-1], c)
            for j in range(3):
                cp = pltpu.make_async_remote_copy(
                    src_ref=ins[a].at[:, mine], dst_ref=outs[a].at[k, :, mine], send_sem=send_sems.at[6 * a + j],
                    recv_sem=recv_sems.at[6 * a + j], device_id=(*chips[j], c), device_id_type=MESH)
                cp.start()
                sends.append(cp)
        for a in range(n):
            mine = _half_cols(arrs[a].shape[-1], c)
            for j in range(3):
                landed = outs[a].at[chip_idx[j], :, mine]
                pltpu.make_async_remote_copy(
                    src_ref=landed, dst_ref=landed, send_sem=send_sems.at[6 * a + j],
                    recv_sem=recv_sems.at[6 * a + j], device_id=(*chips[j], c), device_id_type=MESH).wait_recv()
                fwd = pltpu.make_async_remote_copy(
                    src_ref=landed, dst_ref=landed, send_sem=send_sems.at[6 * a + 3 + j],
                    recv_sem=recv_sems.at[6 * a + 3 + j], device_id=sibling, device_id_type=MESH)
                fwd.start()
                sends.append(fwd)
        for a in range(n):
            theirs = _half_cols(arrs[a].shape[-1], 1 - c)
            for j in range(3):
                landed = outs[a].at[chip_idx[j], :, theirs]
                pltpu.make_async_remote_copy(
                    src_ref=landed, dst_ref=landed, send_sem=send_sems.at[6 * a + 3 + j],
                    recv_sem=recv_sems.at[6 * a + 3 + j], device_id=sibling, device_id_type=MESH).wait_recv()
        for cp in sends:
            cp.wait_send()

    hbm = pl.BlockSpec(memory_space=pl.ANY)
    return pl.pallas_call(
        body, name=name,
        out_shape=tuple(jax.ShapeDtypeStruct((N_CHIPS,) + a.shape, a.dtype) for a in arrs),
        in_specs=[hbm] * n, out_specs=tuple([hbm] * n),
        scratch_shapes=[pltpu.SemaphoreType.DMA((6 * n,)), pltpu.SemaphoreType.DMA((6 * n,))],
    )(*arrs)


def _send_to_sibling(arrs, name):
    n = len(arrs)

    def body(*refs):
        ins, outs = refs[:n], refs[n:2 * n]
        send_sems, recv_sems = refs[2 * n:]
        x, y, c = _coords()
        cps = []
        for a in range(n):
            cp = pltpu.make_async_remote_copy(
                src_ref=ins[a].at[:, :, _half_cols(arrs[a].shape[-1], 1 - c)], dst_ref=outs[a],
                send_sem=send_sems.at[a], recv_sem=recv_sems.at[a], device_id=(x, y, 1 - c), device_id_type=MESH)
            cp.start()
            cps.append(cp)
        for cp in cps:
            cp.wait()

    hbm = pl.BlockSpec(memory_space=pl.ANY)
    return pl.pallas_call(
        body, name=name,
        out_shape=tuple(jax.ShapeDtypeStruct(a.shape[:-1] + (a.shape[-1] // 2,), a.dtype) for a in arrs),
        in_specs=[hbm] * n, out_specs=tuple([hbm] * n),
        scratch_shapes=[pltpu.SemaphoreType.DMA((n,)), pltpu.SemaphoreType.DMA((n,))],
    )(*arrs)


def _exchange_chips(arrs, name):
    n = len(arrs)

    def body(*refs):
        ins, outs = refs[:n], refs[n:2 * n]
        send_sems, recv_sems = refs[2 * n:]
        x, y, c = _coords()
        k = 2 * x + y
        chips, chip_idx = _other_chips(x, y)
        sends = []
        for a in range(n):
            for j in range(3):
                cp = pltpu.make_async_remote_copy(
                    src_ref=ins[a].at[chip_idx[j]], dst_ref=outs[a].at[k], send_sem=send_sems.at[3 * a + j],
                    recv_sem=recv_sems.at[3 * a + j], device_id=(*chips[j], c), device_id_type=MESH)
                cp.start()
                sends.append(cp)
        for a in range(n):
            for j in range(3):
                landed = outs[a].at[chip_idx[j]]
                pltpu.make_async_remote_copy(
                    src_ref=landed, dst_ref=landed, send_sem=send_sems.at[3 * a + j],
                    recv_sem=recv_sems.at[3 * a + j], device_id=(*chips[j], c), device_id_type=MESH).wait_recv()
        for cp in sends:
            cp.wait_send()

    hbm = pl.BlockSpec(memory_space=pl.ANY)
    return pl.pallas_call(
        body, name=name,
        out_shape=tuple(jax.ShapeDtypeStruct(a.shape, a.dtype) for a in arrs),
        in_specs=[hbm] * n, out_specs=tuple([hbm] * n),
        scratch_shapes=[pltpu.SemaphoreType.DMA((3 * n,)), pltpu.SemaphoreType.DMA((3 * n,))],
    )(*arrs)


def _swap_with_sibling(arrs, name):
    n = len(arrs)

    def body(*refs):
        ins, outs = refs[:n], refs[n:2 * n]
        send_sems, recv_sems = refs[2 * n:]
        x, y, c = _coords()
        cps = []
        for a in range(n):
            cp = pltpu.make_async_remote_copy(
                src_ref=ins[a], dst_ref=outs[a], send_sem=send_sems.at[a], recv_sem=recv_sems.at[a],
                device_id=(x, y, 1 - c), device_id_type=MESH)
            cp.start()
            cps.append(cp)
        for cp in cps:
            cp.wait()

    hbm = pl.BlockSpec(memory_space=pl.ANY)
    return pl.pallas_call(
        body, name=name,
        out_shape=tuple(jax.ShapeDtypeStruct(a.shape, a.dtype) for a in arrs),
        in_specs=[hbm] * n, out_specs=tuple([hbm] * n),
        scratch_shapes=[pltpu.SemaphoreType.DMA((n,)), pltpu.SemaphoreType.DMA((n,))],
    )(*arrs)


def _row_tile(rows, cols, n_arrays):
    budget = VMEM_LIMIT // 2
    t = rows
    while t % 16 == 0 and t * cols * 4 * n_arrays * 2 > budget:
        t //= 2
    return t


def _add_my_half(g, landed, my_c, name):
    nb, r, cdim = g.shape
    half = cdim // 2
    tr = _row_tile(r, half, 3)

    def body(c_ref, g_ref, l_ref, o_ref):
        o_ref[...] = (g_ref[...] + l_ref[...]).astype(BF16)

    spec = pl.BlockSpec((None, tr, half), lambda b, i, c_ref: (b, i, 0))
    return pl.pallas_call(
        body, name=name, out_shape=jax.ShapeDtypeStruct((nb, r, half), BF16),
        grid_spec=pltpu.PrefetchScalarGridSpec(
            num_scalar_prefetch=1, grid=(nb, r // tr),
            in_specs=[pl.BlockSpec((None, tr, half), lambda b, i, c_ref: (b, i, c_ref[0])), spec],
            out_specs=spec),
        compiler_params=_cparams(("parallel", "parallel")),
    )(my_c, g, landed)


def _sum_slots(a, name):
    nb, r, cdim = a.shape
    tr = _row_tile(r, cdim, 4)

    def body(a_ref, o_ref):
        o_ref[...] = ((a_ref[0].astype(F32) + a_ref[1].astype(F32)) + a_ref[2].astype(F32)) + a_ref[3].astype(F32)

    return pl.pallas_call(
        body, name=name, out_shape=jax.ShapeDtypeStruct((r, cdim), F32), grid=(r // tr,),
        in_specs=[pl.BlockSpec((nb, tr, cdim), lambda i: (0, i, 0))],
        out_specs=pl.BlockSpec((tr, cdim), lambda i: (i, 0)),
        compiler_params=_cparams(("parallel",)),
    )(a)


def _adamw(w, g, m, v, name):
    r, cdim = w.shape
    tr = _row_tile(r, cdim, 7)

    def body(w_ref, g_ref, m_ref, v_ref, d_ref, nm_ref, nv_ref):
        gv = g_ref[...]
        nm = ADAM_B1 * m_ref[...] + (1.0 - ADAM_B1) * gv
        nv = ADAM_B2 * v_ref[...] + (1.0 - ADAM_B2) * (gv * gv)
        m_hat = nm / (1.0 - ADAM_B1 ** ADAM_STEP)
        v_hat = nv / (1.0 - ADAM_B2 ** ADAM_STEP)
        d_ref[...] = -ADAM_LR * (m_hat / (jnp.sqrt(v_hat) + ADAM_EPS) + ADAM_WD * w_ref[...])
        nm_ref[...] = nm
        nv_ref[...] = nv

    spec = pl.BlockSpec((tr, cdim), lambda i: (i, 0))
    shp = jax.ShapeDtypeStruct((r, cdim), F32)
    return pl.pallas_call(
        body, name=name, out_shape=(shp, shp, shp), grid=(r // tr,),
        in_specs=[spec] * 4, out_specs=(spec, spec, spec),
        compiler_params=_cparams(("parallel",)),
    )(w, g, m, v)


def _matmul(a, b, out_dtype, name, mode, tm, tn, tk, extra=None):
    dims = {"nn": NN, "nt": NT, "tn": TN}[mode]
    if mode == "tn":
        kdim, m = a.shape
    else:
        m, kdim = a.shape
    n = b.shape[0] if mode == "nt" else b.shape[1]
    tm, tn, tk = min(tm, m), min(tn, n), min(tk, kdim)
    nk = kdim // tk
    a_spec = (pl.BlockSpec((tk, tm), lambda i, j, k: (k, i)) if mode == "tn"
              else pl.BlockSpec((tm, tk), lambda i, j, k: (i, k)))
    b_spec = (pl.BlockSpec((tn, tk), lambda i, j, k: (j, k)) if mode == "nt"
              else pl.BlockSpec((tk, tn), lambda i, j, k: (k, j)))
    in_specs, operands = [a_spec, b_spec], [a, b]
    if extra is not None:
        a2, b2 = extra
        k2 = a2.shape[0] if mode == "tn" else a2.shape[1]
        in_specs.append(pl.BlockSpec((k2, tm), lambda i, j, k: (0, i)) if mode == "tn"
                        else pl.BlockSpec((tm, k2), lambda i, j, k: (i, 0)))
        in_specs.append(pl.BlockSpec((tn, k2), lambda i, j, k: (j, 0)) if mode == "nt"
                        else pl.BlockSpec((k2, tn), lambda i, j, k: (0, j)))
        operands += [a2, b2]

    def body_one_block(*refs):
        acc = lax.dot_general(refs[0][...], refs[1][...], dims, preferred_element_type=F32)
        if extra is not None:
            acc += lax.dot_general(refs[2][...], refs[3][...], dims, preferred_element_type=F32)
        refs[-1][...] = acc.astype(out_dtype)

    if nk == 1:
        return pl.pallas_call(
            body_one_block, name=name, out_shape=jax.ShapeDtypeStruct((m, n), out_dtype), grid=(m // tm, n // tn, 1),
            in_specs=in_specs, out_specs=pl.BlockSpec((tm, tn), lambda i, j, k: (i, j)),
            compiler_params=_cparams(("parallel", "parallel", "arbitrary")),
        )(*operands)

    def body(*refs):
        if extra is not None:
            a_ref, b_ref, a2_ref, b2_ref, o_ref, acc_ref = refs
        else:
            a_ref, b_ref, o_ref, acc_ref = refs
        k = pl.program_id(2)

        @pl.when(k == 0)
        def _():
            if extra is not None:
                acc_ref[...] = lax.dot_general(a2_ref[...], b2_ref[...], dims, preferred_element_type=F32)
            else:
                acc_ref[...] = jnp.zeros_like(acc_ref)

        acc_ref[...] += lax.dot_general(a_ref[...], b_ref[...], dims, preferred_element_type=F32)

        @pl.when(k == nk - 1)
        def _():
            o_ref[...] = acc_ref[...].astype(out_dtype)

    return pl.pallas_call(
        body, name=name, out_shape=jax.ShapeDtypeStruct((m, n), out_dtype), grid=(m // tm, n // tn, nk),
        in_specs=in_specs, out_specs=pl.BlockSpec((tm, tn), lambda i, j, k: (i, j)),
        scratch_shapes=[pltpu.VMEM((tm, tn), F32)],
        compiler_params=_cparams(("parallel", "parallel", "arbitrary")),
    )(*operands)


def _ada_mod(c_all, w_shard, b_shard):
    nb, d = c_all.shape
    cols = w_shard.shape[1]

    def body(c_ref, w_ref, b_ref, mod_ref, act_ref):
        cv = c_ref[...]
        act = cv * _sigmoid(cv)
        act_ref[...] = act
        mod_ref[...] = jnp.dot(act, w_ref[...], preferred_element_type=F32, precision=HI) + b_ref[...]

    return pl.pallas_call(
        body, name="ada_mod",
        out_shape=(jax.ShapeDtypeStruct((nb, cols), F32), jax.ShapeDtypeStruct((nb, d), F32)),
        compiler_params=_cparams(),
    )(c_all, w_shard, b_shard)


def _rms_mod_fwd(x, gain, scale, shift):
    s, d = x.shape
    tm = min(512, s)

    def body(x_ref, g_ref, sc_ref, sh_ref, h_ref):
        xv = x_ref[...]
        r = lax.rsqrt(jnp.mean(xv * xv, axis=-1, keepdims=True) + NORM_EPS)
        h_ref[...] = (xv * r * g_ref[...] * (1.0 + sc_ref[...]) + sh_ref[...]).astype(BF16)

    row = pl.BlockSpec((1, d), lambda i: (0, 0))
    tile = pl.BlockSpec((tm, d), lambda i: (i, 0))
    return pl.pallas_call(
        body, name="rms_mod_fwd", out_shape=jax.ShapeDtypeStruct((s, d), BF16), grid=(s // tm,),
        in_specs=[tile, row, row, row], out_specs=tile, compiler_params=_cparams(("parallel",)),
    )(x, gain, scale, shift)


def _rms_mod_bwd(x, dh, dres, gain, scale):
    s, d = x.shape
    tm = min(512, s)

    def body(x_ref, dh_ref, dres_ref, g_ref, sc_ref, dx_ref, sums_ref):
        @pl.when(pl.program_id(0) == 0)
        def _():
            sums_ref[...] = jnp.zeros_like(sums_ref)

        xv, dhv = x_ref[...], dh_ref[...]
        r = lax.rsqrt(jnp.mean(xv * xv, axis=-1, keepdims=True) + NORM_EPS)
        nrm = xv * r
        g, one_sc = g_ref[...], 1.0 + sc_ref[...]
        dn = dhv * g * one_sc
        dx_ref[...] = r * (dn - nrm * jnp.mean(dn * nrm, axis=-1, keepdims=True)) + dres_ref[...]
        dhn = dhv * nrm
        sums_ref[0:1, :] += jnp.sum(dhv, axis=0, keepdims=True)
        sums_ref[1:2, :] += jnp.sum(dhn * g, axis=0, keepdims=True)
        sums_ref[2:3, :] += jnp.sum(dhn * one_sc, axis=0, keepdims=True)

    row = pl.BlockSpec((1, d), lambda i: (0, 0))
    tile = pl.BlockSpec((tm, d), lambda i: (i, 0))
    return pl.pallas_call(
        body, name="rms_mod_bwd",
        out_shape=(jax.ShapeDtypeStruct((s, d), F32), jax.ShapeDtypeStruct((3, d), F32)), grid=(s // tm,),
        in_specs=[tile, tile, tile, row, row], out_specs=(tile, pl.BlockSpec((3, d), lambda i: (0, 0))),
        compiler_params=_cparams(("arbitrary",)),
    )(x, dh, dres, gain, scale)


def _loss_head(x, mixed, gate, gain_f, target):
    s, d = x.shape
    tm = min(512, s)

    def body(x_ref, mx_ref, gt_ref, gf_ref, t_ref, dx2_ref, dmx_ref, sums_ref):
        @pl.when(pl.program_id(0) == 0)
        def _():
            sums_ref[...] = jnp.zeros_like(sums_ref)

        mx, gt, gf = mx_ref[...], gt_ref[...], gf_ref[...]
        x2 = x_ref[...] + gt * mx
        r = lax.rsqrt(jnp.mean(x2 * x2, axis=-1, keepdims=True) + NORM_EPS)
        nrm = x2 * r
        err = nrm * gf - t_ref[...]
        dyf = err * (1.0 / d)
        dn = dyf * gf
        dx2 = r * (dn - nrm * jnp.mean(dn * nrm, axis=-1, keepdims=True))
        dx2_ref[...] = dx2
        dmx_ref[...] = (dx2 * gt).astype(BF16)
        sums_ref[0:1, :] += jnp.sum(err * err, axis=0, keepdims=True)
        sums_ref[1:2, :] += jnp.sum(dyf * nrm, axis=0, keepdims=True)
        sums_ref[2:3, :] += jnp.sum(dx2 * mx, axis=0, keepdims=True)

    row = pl.BlockSpec((1, d), lambda i: (0, 0))
    tile = pl.BlockSpec((tm, d), lambda i: (i, 0))
    return pl.pallas_call(
        body, name="loss_head",
        out_shape=(jax.ShapeDtypeStruct((s, d), F32), jax.ShapeDtypeStruct((s, d), BF16),
                   jax.ShapeDtypeStruct((3, d), F32)),
        grid=(s // tm,), in_specs=[tile, tile, row, row, tile],
        out_specs=(tile, tile, pl.BlockSpec((3, d), lambda i: (0, 0))),
        compiler_params=_cparams(("arbitrary",)),
    )(x, mixed, gate, gain_f, target)


def _silu_grad(z, sg):
    return sg * (1.0 + z * (1.0 - sg))


def _gated_norm_fwd(o, proj, z_off, gain, gate_inside, name):
    s, d = o.shape
    tm = min(512, s)
    zb = z_off // d

    def body(o_ref, z_ref, g_ref, y_ref):
        z = z_ref[...].astype(F32)
        sz = z * _sigmoid(z)
        u = o_ref[...] * sz if gate_inside else o_ref[...]
        r = lax.rsqrt(jnp.mean(u * u, axis=-1, keepdims=True) + NORM_EPS)
        y = u * r * g_ref[...]
        y_ref[...] = (y if gate_inside else y * sz).astype(BF16)

    tile = pl.BlockSpec((tm, d), lambda i: (i, 0))
    return pl.pallas_call(
        body, name=name, out_shape=jax.ShapeDtypeStruct((s, d), BF16), grid=(s // tm,),
        in_specs=[tile, pl.BlockSpec((tm, d), lambda i: (i, zb)), pl.BlockSpec((1, d), lambda i: (0, 0))],
        out_specs=tile, compiler_params=_cparams(("parallel",)),
    )(o, proj, gain)


def _gated_norm_bwd(dy_all, dy_blk, o, proj, z_off, gain, gate_inside, name):
    s, d = o.shape
    tm = min(512, s)
    zb = z_off // d

    def body(dy_ref, o_ref, z_ref, g_ref, do_ref, dz_ref, dg_ref):
        @pl.when(pl.program_id(0) == 0)
        def _():
            dg_ref[...] = jnp.zeros_like(dg_ref)

        z = z_ref[...].astype(F32)
        sg = _sigmoid(z)
        sz = z * sg
        ov, dy, g = o_ref[...], dy_ref[...], g_ref[...]
        u = ov * sz if gate_inside else ov
        r = lax.rsqrt(jnp.mean(u * u, axis=-1, keepdims=True) + NORM_EPS)
        nrm = u * r
        if gate_inside:
            dg_ref[...] += jnp.sum(dy * nrm, axis=0, keepdims=True)
            dn = dy * g
        else:
            dg_ref[...] += jnp.sum(dy * nrm * sz, axis=0, keepdims=True)
            dn = dy * g * sz
        du = r * (dn - nrm * jnp.mean(dn * nrm, axis=-1, keepdims=True))
        if gate_inside:
            do_ref[...] = du * sz
            dz_ref[...] = (du * ov * _silu_grad(z, sg)).astype(BF16)
        else:
            do_ref[...] = du
            dz_ref[...] = (dy * nrm * g * _silu_grad(z, sg)).astype(BF16)

    tile = pl.BlockSpec((tm, d), lambda i: (i, 0))
    row = pl.BlockSpec((1, d), lambda i: (0, 0))
    return pl.pallas_call(
        body, name=name,
        out_shape=(jax.ShapeDtypeStruct((s, d), F32), jax.ShapeDtypeStruct((s, d), BF16),
                   jax.ShapeDtypeStruct((1, d), F32)),
        grid=(s // tm,),
        in_specs=[pl.BlockSpec((tm, d), lambda i: (i, dy_blk)), tile, pl.BlockSpec((tm, d), lambda i: (i, zb)), row],
        out_specs=(tile, tile, row), compiler_params=_cparams(("arbitrary",)),
    )(dy_all, o, proj, gain)


def _sb_logits(qh, kb):
    z = lax.dot_general(qh, kb, NT, preferred_element_type=F32)
    neg_abs = lax.bitcast_convert_type(lax.bitcast_convert_type(z, jnp.uint32) | jnp.uint32(0x80000000), F32)
    lb = jnp.minimum(z, 0.0) - jnp.log(1.0 + jnp.exp(neg_abs))
    return lb, lb - z


def _attn_consts(tk):
    lane = lax.broadcasted_iota(jnp.int32, (1, LANE), 1)
    row = lax.broadcasted_iota(jnp.int32, (tk, tk), 0)
    col = lax.broadcasted_iota(jnp.int32, (tk, tk), 1)
    return (lane < HEAD_DIM, lane >= HEAD_DIM), row, col


def _band_mask(rows, tk):
    return lax.broadcasted_iota(jnp.int32, (rows, tk), 1) < lax.broadcasted_iota(jnp.int32, (rows, tk), 0)


def _attn_fwd(proj):
    s = proj.shape[0]
    tq, tk = min(ATTN_TQ, s), min(ATTN_TK, s)
    r = tq // tk

    def body(q_ref, k_ref, v_ref, o_ref, l_ref, acc_ref, run_ref):
        i = pl.program_id(1)
        head_mask, row, col = _attn_consts(tk)
        later = (row > col).astype(BF16)
        q = q_ref[...] * ATTN_SCALE
        qh = [jnp.where(m, q, jnp.zeros_like(q)) for m in head_mask]
        acc_ref[...] = jnp.zeros_like(acc_ref)
        run_ref[...] = jnp.zeros_like(run_ref)

        def block(j, lo):
            start = pl.multiple_of(j * tk, tk)
            kb = k_ref[pl.ds(start, tk), :]
            vb = v_ref[pl.ds(start, tk), :]
            rows = slice(0 if lo is None else lo, tq)
            causal = None if lo is None else _band_mask(tq - lo, tk)
            hs = range(2)
            logits = [_sb_logits(qh[h][rows], kb) for h in hs]
            lb = [logits[h][0] for h in hs]
            l1m = [logits[h][1] if causal is None else jnp.where(causal, logits[h][1], 0.0) for h in hs]
            tail = [jnp.dot(l1m[h].astype(BF16), later, preferred_element_type=F32) + run_ref[h, rows] for h in hs]
            w = [jnp.exp(lb[h] + tail[h]) for h in hs]
            if causal is not None:
                w = [jnp.where(causal, w[h], 0.0) for h in hs]
            vh = [jnp.where(head_mask[h], vb, jnp.zeros_like(vb)) for h in hs]
            acc_ref[rows, :] += (jnp.dot(w[0].astype(BF16), vh[0], preferred_element_type=F32)
                                 + jnp.dot(w[1].astype(BF16), vh[1], preferred_element_type=F32))
            for h in hs:
                run_ref[h, rows] += jnp.sum(l1m[h], axis=1, keepdims=True)

        for b in reversed(range(r)):
            block(i * r + b, b * tk)
        n_full = i * r

        def more(c):
            return jnp.logical_and(c[0] < n_full, c[1] > LOG_ZERO)

        def step(c):
            block(n_full - 1 - c[0], None)
            return c[0] + 1, jnp.max(run_ref[...])

        visited, _ = lax.while_loop(more, step, (jnp.int32(0), jnp.float32(0.0)))
        o_ref[...] = acc_ref[...]
        lane = lax.broadcasted_iota(jnp.int32, (1, LANE), 1)
        first = (n_full - visited).astype(F32)
        l_ref[...] = jnp.where(lane < HEAD_DIM // 2, run_ref[0], jnp.where(lane < HEAD_DIM, first, run_ref[1]))

    kq, kk, kv = OFF_Q // LANE, OFF_K // LANE, OFF_V // LANE
    tile = pl.BlockSpec((tq, LANE), lambda p, i: (i, p))
    return pl.pallas_call(
        body, name="attn_fwd",
        out_shape=(jax.ShapeDtypeStruct((s, D_ATTN), F32), jax.ShapeDtypeStruct((s, D_ATTN), F32)),
        grid=(N_PAIRS, s // tq),
        in_specs=[pl.BlockSpec((tq, LANE), lambda p, i: (i, kq + p)),
                  pl.BlockSpec((s, LANE), lambda p, i: (0, kk + p)),
                  pl.BlockSpec((s, LANE), lambda p, i: (0, kv + p))],
        out_specs=(tile, tile),
        scratch_shapes=[pltpu.VMEM((tq, LANE), F32), pltpu.VMEM((2, tq, 1), F32)],
        compiler_params=_cparams(("parallel", "arbitrary")),
    )(proj, proj, proj)


def _attn_bwd(proj, do, lsum):
    s = proj.shape[0]
    tq, tk = min(ATTN_TQ, s), min(ATTN_TK, s)
    r = tq // tk

    def body(q_ref, k_ref, v_ref, do_ref, l_ref, dq_ref, dk_ref, dv_ref, dqacc_ref, dkacc_ref, dvacc_ref,
             passed_ref, pre_ref):
        i = pl.program_id(1)

        @pl.when(i == 0)
        def _():
            dkacc_ref[...] = jnp.zeros_like(dkacc_ref)
            dvacc_ref[...] = jnp.zeros_like(dvacc_ref)

        head_mask, row, col = _attn_consts(tk)
        later = (row > col).astype(BF16)
        earlier = (row < col).astype(BF16)
        q = q_ref[...] * ATTN_SCALE
        dov = do_ref[...].astype(BF16)
        qh = [jnp.where(m, q, jnp.zeros_like(q)) for m in head_mask]
        doh = [jnp.where(m, dov, jnp.zeros_like(dov)) for m in head_mask]
        lsum_v = l_ref[...]
        lh = [lsum_v[:, 0:1], lsum_v[:, HEAD_DIM:HEAD_DIM + 1]]
        n_full = i * r
        first = jnp.clip(jnp.max(lsum_v[0:8, HEAD_DIM // 2:HEAD_DIM]).astype(jnp.int32), 0, n_full)
        dqacc_ref[...] = jnp.zeros_like(dqacc_ref)
        passed_ref[...] = jnp.zeros_like(passed_ref)
        pre_ref[...] = jnp.zeros_like(pre_ref)

        def block(j, lo):
            start = pl.multiple_of(j * tk, tk)
            kb = k_ref[pl.ds(start, tk), :]
            vb = v_ref[pl.ds(start, tk), :]
            rows = slice(0 if lo is None else lo, tq)
            causal = None if lo is None else _band_mask(tq - lo, tk)
            hs = range(2)
            q_rows = [qh[h][rows] for h in hs]
            do_rows = [doh[h][rows] for h in hs]
            logits = [_sb_logits(q_rows[h], kb) for h in hs]
            lb = [logits[h][0] for h in hs]
            l1m = [logits[h][1] if causal is None else jnp.where(causal, logits[h][1], 0.0) for h in hs]
            da = [lax.dot_general(do_rows[h], vb, NT, preferred_element_type=F32) for h in hs]
            rs = [jnp.sum(l1m[h], axis=1, keepdims=True) for h in hs]
            right = [lh[h][rows] - passed_ref[h, rows] - rs[h] for h in hs]
            for h in hs:
                passed_ref[h, rows] += rs[h]
            tail = [jnp.dot(l1m[h].astype(BF16), later, preferred_element_type=F32) + right[h] for h in hs]
            a = [jnp.exp(lb[h] + tail[h]) for h in hs]
            if causal is not None:
                a = [jnp.where(causal, a[h], 0.0) for h in hs]
            g = [a[h] * da[h] for h in hs]
            pre = [jnp.dot(g[h].astype(BF16), earlier, preferred_element_type=F32) + pre_ref[h, rows] for h in hs]
            for h in hs:
                pre_ref[h, rows] += jnp.sum(g[h], axis=1, keepdims=True)
            dz = [g[h] - jnp.exp(lb[h]) * (g[h] + pre[h]) for h in hs]
            if causal is not None:
                dz = [jnp.where(causal, dz[h], 0.0) for h in hs]
            dzb = [dz[h].astype(BF16) for h in hs]
            kh = [jnp.where(head_mask[h], kb, jnp.zeros_like(kb)) * ATTN_SCALE for h in hs]
            dqacc_ref[rows, :] += (jnp.dot(dzb[0], kh[0], preferred_element_type=F32)
                                   + jnp.dot(dzb[1], kh[1], preferred_element_type=F32))
            dvacc_ref[pl.ds(start, tk), :] += (
                lax.dot_general(a[0].astype(BF16), do_rows[0], TN, preferred_element_type=F32)
                + lax.dot_general(a[1].astype(BF16), do_rows[1], TN, preferred_element_type=F32))
            dkacc_ref[pl.ds(start, tk), :] += (
                lax.dot_general(dzb[0], q_rows[0], TN, preferred_element_type=F32)
                + lax.dot_general(dzb[1], q_rows[1], TN, preferred_element_type=F32))

        def step(j, carry):
            block(j, None)
            return carry

        lax.fori_loop(first, n_full, step, 0)
        for b in range(r):
            block(n_full + b, b * tk)
        dq_ref[...] = dqacc_ref[...].astype(BF16)

        @pl.when(i == pl.num_programs(1) - 1)
        def _():
            dk_ref[...] = dkacc_ref[...].astype(BF16)
            dv_ref[...] = dvacc_ref[...].astype(BF16)

    kq, kk, kv = OFF_Q // LANE, OFF_K // LANE, OFF_V // LANE
    tile = pl.BlockSpec((tq, LANE), lambda p, i: (i, p))
    full = pl.BlockSpec((s, LANE), lambda p, i: (0, p))
    shp = jax.ShapeDtypeStruct((s, D_ATTN), BF16)
    return pl.pallas_call(
        body, name="attn_bwd", out_shape=(shp, shp, shp), grid=(N_PAIRS, s // tq),
        in_specs=[pl.BlockSpec((tq, LANE), lambda p, i: (i, kq + p)),
                  pl.BlockSpec((s, LANE), lambda p, i: (0, kk + p)),
                  pl.BlockSpec((s, LANE), lambda p, i: (0, kv + p)),
                  tile, tile],
        out_specs=(tile, full, full),
        scratch_shapes=[pltpu.VMEM((tq, LANE), F32), pltpu.VMEM((s, LANE), F32), pltpu.VMEM((s, LANE), F32),
                        pltpu.VMEM((2, tq, 1), F32), pltpu.VMEM((2, tq, 1), F32)],
        compiler_params=_cparams(("parallel", "arbitrary")),
    )(proj, proj, proj, do, lsum)


def _shift_down(u, k, rows):
    return jnp.where(rows >= k, pltpu.roll(u, k, 0), 0.0)


def _shift_up(u, k, rows, s):
    return jnp.where(rows < s - k, pltpu.roll(u, s - k, 0), 0.0)


def _conv_fwd(proj, w, b):
    s = proj.shape[0]
    blk0 = OFF_XBC // LANE

    def body(u_ref, w_ref, b_ref, o_ref):
        u = u_ref[...].astype(F32)
        rows = lax.broadcasted_iota(jnp.int32, (s, 1), 0)
        pre = u * w_ref[CONV_K - 1:CONV_K, :] + b_ref[...]
        for k in range(1, CONV_K):
            pre += _shift_down(u, k, rows) * w_ref[CONV_K - 1 - k:CONV_K - k, :]
        o_ref[...] = pre * _sigmoid(pre)

    return pl.pallas_call(
        body, name="conv_fwd", out_shape=jax.ShapeDtypeStruct((s, D_XBC), F32), grid=(D_XBC // LANE,),
        in_specs=[pl.BlockSpec((s, LANE), lambda j: (0, blk0 + j)), pl.BlockSpec((CONV_K, LANE), lambda j: (0, j)),
                  pl.BlockSpec((1, LANE), lambda j: (0, j))],
        out_specs=pl.BlockSpec((s, LANE), lambda j: (0, j)), compiler_params=_cparams(("parallel",)),
    )(proj, w, b)


def _conv_bwd(proj, w, b, dact):
    s = proj.shape[0]
    blk0 = OFF_XBC // LANE

    def body(u_ref, w_ref, b_ref, da_ref, du_ref, dw_ref, db_ref):
        u = u_ref[...].astype(F32)
        rows = lax.broadcasted_iota(jnp.int32, (s, 1), 0)
        shifted = [u] + [_shift_down(u, k, rows) for k in range(1, CONV_K)]
        pre = b_ref[...] + shifted[0] * w_ref[CONV_K - 1:CONV_K, :]
        for k in range(1, CONV_K):
            pre += shifted[k] * w_ref[CONV_K - 1 - k:CONV_K - k, :]
        sg = _sigmoid(pre)
        dpre = da_ref[...] * _silu_grad(pre, sg)
        db_ref[...] = jnp.sum(dpre, axis=0, keepdims=True)
        du = dpre * w_ref[CONV_K - 1:CONV_K, :]
        for k in range(CONV_K):
            dw_ref[CONV_K - 1 - k:CONV_K - k, :] = jnp.sum(dpre * shifted[k], axis=0, keepdims=True)
            if k:
                du += _shift_up(dpre, k, rows, s) * w_ref[CONV_K - 1 - k:CONV_K - k, :]
        du_ref[...] = du.astype(BF16)

    col = pl.BlockSpec((s, LANE), lambda j: (0, j))
    return pl.pallas_call(
        body, name="conv_bwd",
        out_shape=(jax.ShapeDtypeStruct((s, D_XBC), BF16), jax.ShapeDtypeStruct((CONV_K, D_XBC), F32),
                   jax.ShapeDtypeStruct((1, D_XBC), F32)),
        grid=(D_XBC // LANE,),
        in_specs=[pl.BlockSpec((s, LANE), lambda j: (0, blk0 + j)), pl.BlockSpec((CONV_K, LANE), lambda j: (0, j)),
                  pl.BlockSpec((1, LANE), lambda j: (0, j)), col],
        out_specs=(col, pl.BlockSpec((CONV_K, LANE), lambda j: (0, j)), pl.BlockSpec((1, LANE), lambda j: (0, j))),
        compiler_params=_cparams(("parallel",)),
    )(proj, w, b, dact)


def _ssd_decays(dtraw_ref, bias_ref, dtt_ref, biast_ref, arow_ref, acol_ref):
    ln = CHUNK
    dt = _softplus(dtraw_ref[...] + bias_ref[...])
    r = lax.broadcasted_iota(jnp.int32, (ln, ln), 0)
    c = lax.broadcasted_iota(jnp.int32, (ln, ln), 1)
    ac = jnp.dot((r >= c).astype(F32), dt * arow_ref[...], preferred_element_type=F32, precision=HI)
    dtt = _softplus(dtt_ref[...] + biast_ref[...])
    act = jnp.dot(dtt * acol_ref[...], (r <= c).astype(F32), preferred_element_type=F32, precision=HI)
    return dt, ac, act, r >= c


def _pair_cols(m0, v, h0):
    return jnp.where(m0, v[:, h0:h0 + 1], v[:, h0 + 1:h0 + 2])


def _ssd_fwd(act, dtraw, dtt, bias, biast, arow, acol, dskip):
    s = act.shape[0]
    ln = CHUNK
    nc = s // ln

    def body(act_ref, dtraw_ref, dtt_ref, bias_ref, biast_ref, arow_ref, acol_ref, dsk_ref, y_ref, st_ref,
             state_ref):
        @pl.when(pl.program_id(0) == 0)
        def _():
            state_ref[...] = jnp.zeros_like(state_ref)

        dt, ac, act_t, lower = _ssd_decays(dtraw_ref, bias_ref, dtt_ref, biast_ref, arow_ref, acol_ref)
        lane = lax.broadcasted_iota(jnp.int32, (1, LANE), 1)
        m0 = lane < HEAD_DIM
        top = lax.broadcasted_iota(jnp.int32, (LANE, 1), 0) < HEAD_DIM
        for g in range(N_GROUPS):
            bg = act_ref[:, D_SSM + g * D_STATE:D_SSM + (g + 1) * D_STATE].astype(BF16)
            cg = act_ref[:, D_SSM + (N_GROUPS + g) * D_STATE:D_SSM + (N_GROUPS + g + 1) * D_STATE].astype(BF16)
            cb = lax.dot_general(cg, bg, NT, preferred_element_type=F32)
            for p in range(g * 4, g * 4 + 4):
                h0 = 2 * p
                xp = act_ref[:, p * LANE:(p + 1) * LANE]
                xdt = xp * _pair_cols(m0, dt, h0)
                acp = _pair_cols(m0, ac, h0)
                last = acp[ln - 1:ln, :]
                y = xp * dsk_ref[:, p * LANE:(p + 1) * LANE]
                for hh in range(2):
                    h = h0 + hh
                    dm = jnp.exp(jnp.where(lower, ac[:, h:h + 1] - act_t[h:h + 1, :], -jnp.inf))
                    mask = m0 if hh == 0 else jnp.logical_not(m0)
                    y += jnp.dot((cb * dm).astype(BF16), jnp.where(mask, xdt, 0.0).astype(BF16),
                                 preferred_element_type=F32)
                prev = state_ref[p]
                st_ref[0, p] = prev
                y += lax.dot_general(cg, prev.astype(BF16), NT, preferred_element_type=F32) * jnp.exp(acp)
                y_ref[:, p * LANE:(p + 1) * LANE] = y
                cs = lax.dot_general((xdt * jnp.exp(last - acp)).astype(BF16), bg, TN, preferred_element_type=F32)
                dec = jnp.where(top, jnp.exp(ac[ln - 1:ln, h0:h0 + 1]), jnp.exp(ac[ln - 1:ln, h0 + 1:h0 + 2]))
                state_ref[p] = prev * dec + cs

    row = lambda w: pl.BlockSpec((1, w), lambda c: (0, 0))
    return pl.pallas_call(
        body, name="ssd_fwd",
        out_shape=(jax.ShapeDtypeStruct((s, D_SSM), F32),
                   jax.ShapeDtypeStruct((nc, N_PAIRS, LANE, D_STATE), F32)),
        grid=(nc,),
        in_specs=[pl.BlockSpec((ln, D_XBC), lambda c: (c, 0)), pl.BlockSpec((ln, LANE), lambda c: (c, 0)),
                  pl.BlockSpec((N_HEADS, ln), lambda c: (0, c)), row(LANE),
                  pl.BlockSpec((N_HEADS, 1), lambda c: (0, 0)), row(LANE),
                  pl.BlockSpec((N_HEADS, 1), lambda c: (0, 0)), row(D_SSM)],
        out_specs=(pl.BlockSpec((ln, D_SSM), lambda c: (c, 0)),
                   pl.BlockSpec((1, N_PAIRS, LANE, D_STATE), lambda c: (c, 0, 0, 0))),
        scratch_shapes=[pltpu.VMEM((N_PAIRS, LANE, D_STATE), F32)],
        compiler_params=_cparams(("arbitrary",)),
    )(act, dtraw, dtt, bias, biast, arow, acol, dskip)


def _ssd_bwd(act, dtraw, dtt, bias, biast, arow, acol, dskip, states, dy):
    s = act.shape[0]
    ln = CHUNK
    nc = s // ln

    def body(act_ref, dtraw_ref, dtt_ref, bias_ref, biast_ref, arow_ref, acol_ref, dsk_ref, st_ref, dy_ref,
             dact_ref, dldc_ref, dldr_ref, ddt_ref, dd_ref, dstate_ref):
        @pl.when(pl.program_id(0) == 0)
        def _():
            dstate_ref[...] = jnp.zeros_like(dstate_ref)
            dd_ref[...] = jnp.zeros_like(dd_ref)

        dt, ac, act_t, lower = _ssd_decays(dtraw_ref, bias_ref, dtt_ref, biast_ref, arow_ref, acol_ref)
        lane = lax.broadcasted_iota(jnp.int32, (1, LANE), 1)
        m0 = lane < HEAD_DIM
        halves = (m0, jnp.logical_not(m0))
        top = lax.broadcasted_iota(jnp.int32, (LANE, 1), 0) < HEAD_DIM
        is_last = lax.broadcasted_iota(jnp.int32, (ln, 1), 0) == ln - 1
        sub = lax.broadcasted_iota(jnp.int32, (N_HEADS, 1), 0)
        dac_col = jnp.zeros((ln, LANE), F32)
        dac_row = jnp.zeros((N_HEADS, ln), F32)
        ddt_col = jnp.zeros((ln, LANE), F32)

        def half_sum(v, hh):
            return jnp.sum(jnp.where(halves[hh], v, 0.0), axis=1, keepdims=True)

        for g in range(N_GROUPS):
            b_lo, c_lo = D_SSM + g * D_STATE, D_SSM + (N_GROUPS + g) * D_STATE
            bg32 = act_ref[:, b_lo:b_lo + D_STATE]
            cg32 = act_ref[:, c_lo:c_lo + D_STATE]
            bg, cg = bg32.astype(BF16), cg32.astype(BF16)
            cb = lax.dot_general(cg, bg, NT, preferred_element_type=F32)
            dcb = jnp.zeros((ln, ln), F32)
            dbg = jnp.zeros((ln, D_STATE), F32)
            dcg = jnp.zeros((ln, D_STATE), F32)
            for p in range(g * 4, g * 4 + 4):
                h0 = 2 * p
                cols = slice(p * LANE, (p + 1) * LANE)
                xp = act_ref[:, cols]
                dyp = dy_ref[:, cols]
                dtp = _pair_cols(m0, dt, h0)
                acp = _pair_cols(m0, ac, h0)
                last = acp[ln - 1:ln, :]
                xdt = xp * dtp
                eac = jnp.exp(acp)
                dte = jnp.exp(last - acp)
                prev = st_ref[0, p]
                prev_b = prev.astype(BF16)
                ds = dstate_ref[p]
                ds_b = ds.astype(BF16)
                dec_h = [jnp.exp(ac[ln - 1:ln, h0 + hh:h0 + hh + 1]) for hh in range(2)]
                dec = jnp.where(top, dec_h[0], dec_h[1])

                dd_ref[:, cols] += jnp.sum(dyp * xp, axis=0, keepdims=True)
                dx = dyp * dsk_ref[:, cols]
                zoff = lax.dot_general(cg, prev_b, NT, preferred_element_type=F32)
                dz_b = (dyp * eac).astype(BF16)
                dcg += jnp.dot(dz_b, prev_b, preferred_element_type=F32)
                dprev = lax.dot_general(dz_b, cg, TN, preferred_element_type=F32) + ds * dec
                t_off = dyp * zoff * eac
                wmat = lax.dot_general(bg, ds_b, NT, preferred_element_type=F32)
                xdte_b = (xdt * dte).astype(BF16)
                dbg += jnp.dot(xdte_b, ds_b, preferred_element_type=F32)
                dxdt = dte * wmat
                t_dte = xdt * wmat * dte
                sp_rows = jnp.sum(ds * prev, axis=1, keepdims=True)
                for hh in range(2):
                    h = h0 + hh
                    here = lane == h
                    tdh = half_sum(t_dte, hh)
                    dac_col += jnp.where(here, half_sum(t_off, hh) - tdh, 0.0)
                    sp_h = jnp.sum(jnp.where(top if hh == 0 else jnp.logical_not(top), sp_rows, 0.0),
                                   axis=0, keepdims=True) * dec_h[hh]
                    at_last = sp_h + jnp.sum(tdh, axis=0, keepdims=True)
                    dac_col += jnp.where(jnp.logical_and(is_last, here), at_last, 0.0)
                    dm = jnp.exp(jnp.where(lower, ac[:, h:h + 1] - act_t[h:h + 1, :], -jnp.inf))
                    mm = cb * dm
                    dyh = jnp.where(halves[hh], dyp, 0.0).astype(BF16)
                    xdth = jnp.where(halves[hh], xdt, 0.0).astype(BF16)
                    dmm = lax.dot_general(dyh, xdth, NT, preferred_element_type=F32)
                    dxdt += lax.dot_general(mm.astype(BF16), dyh, TN, preferred_element_type=F32)
                    gm = dmm * mm
                    dcb += dmm * dm
                    dac_col += jnp.where(here, jnp.sum(gm, axis=1, keepdims=True), 0.0)
                    dac_row -= jnp.where(sub == h, jnp.sum(gm, axis=0, keepdims=True), 0.0)
                    ddt_col += jnp.where(here, half_sum(dxdt * xp, hh), 0.0)
                dact_ref[:, cols] = dx + dxdt * dtp
                dstate_ref[p] = dprev
            dcb_b = dcb.astype(BF16)
            dact_ref[:, b_lo:b_lo + D_STATE] = dbg + lax.dot_general(dcb_b, cg, TN, preferred_element_type=F32)
            dact_ref[:, c_lo:c_lo + D_STATE] = dcg + jnp.dot(dcb_b, bg, preferred_element_type=F32)

        r = lax.broadcasted_iota(jnp.int32, (ln, ln), 0)
        c = lax.broadcasted_iota(jnp.int32, (ln, ln), 1)
        dldc_ref[...] = jnp.dot((r <= c).astype(F32), dac_col, preferred_element_type=F32, precision=HI)
        dldr_ref[...] = jnp.dot(dac_row, (r >= c).astype(F32), preferred_element_type=F32, precision=HI)
        ddt_ref[...] = ddt_col

    rev = lambda c: nc - 1 - c
    row = lambda w: pl.BlockSpec((1, w), lambda c: (0, 0))
    col16 = pl.BlockSpec((N_HEADS, 1), lambda c: (0, 0))
    chunk128 = pl.BlockSpec((ln, LANE), lambda c: (rev(c), 0))
    return pl.pallas_call(
        body, name="ssd_bwd",
        out_shape=(jax.ShapeDtypeStruct((s, D_XBC), F32), jax.ShapeDtypeStruct((s, LANE), F32),
                   jax.ShapeDtypeStruct((N_HEADS, s), F32), jax.ShapeDtypeStruct((s, LANE), F32),
                   jax.ShapeDtypeStruct((1, D_SSM), F32)),
        grid=(nc,),
        in_specs=[pl.BlockSpec((ln, D_XBC), lambda c: (rev(c), 0)), chunk128,
                  pl.BlockSpec((N_HEADS, ln), lambda c: (0, rev(c))), row(LANE), col16, row(LANE), col16,
                  row(D_SSM), pl.BlockSpec((1, N_PAIRS, LANE, D_STATE), lambda c: (rev(c), 0, 0, 0)),
                  pl.BlockSpec((ln, D_SSM), lambda c: (rev(c), 0))],
        out_specs=(pl.BlockSpec((ln, D_XBC), lambda c: (rev(c), 0)), chunk128,
                   pl.BlockSpec((N_HEADS, ln), lambda c: (0, rev(c))), chunk128, row(D_SSM)),
        scratch_shapes=[pltpu.VMEM((N_PAIRS, LANE, D_STATE), F32)],
        compiler_params=_cparams(("arbitrary",)),
    )(act, dtraw, dtt, bias, biast, arow, acol, dskip, states, dy)


def _dt_bwd(dtraw, bias, arow, dld_col, dld_row_t, ddt_col):
    s = dtraw.shape[0]
    tm = min(512, s)

    def body(raw_ref, bias_ref, a_ref, dc_ref, dr_ref, dd_ref, out_ref, sums_ref):
        @pl.when(pl.program_id(0) == 0)
        def _():
            sums_ref[...] = jnp.zeros_like(sums_ref)

        raw = raw_ref[...] + bias_ref[...]
        dld = dc_ref[...] + dr_ref[...]
        ddt = dld * a_ref[...] + dd_ref[...]
        draw = ddt * _sigmoid(raw)
        out_ref[...] = draw.astype(BF16)
        sums_ref[0:1, :] += jnp.sum(draw, axis=0, keepdims=True)
        sums_ref[1:2, :] += jnp.sum(dld * _softplus(raw), axis=0, keepdims=True)

    tile = pl.BlockSpec((tm, LANE), lambda i: (i, 0))
    row = pl.BlockSpec((1, LANE), lambda i: (0, 0))
    return pl.pallas_call(
        body, name="dt_bwd",
        out_shape=(jax.ShapeDtypeStruct((s, LANE), BF16), jax.ShapeDtypeStruct((2, LANE), F32)), grid=(s // tm,),
        in_specs=[tile, row, row, tile, tile, tile], out_specs=(tile, pl.BlockSpec((2, LANE), lambda i: (0, 0))),
        compiler_params=_cparams(("arbitrary",)),
    )(dtraw, bias, arow, dld_col, dld_row_t, ddt_col)


def _sum8(parts):
    nb, n = parts.shape

    def body(p_ref, o_ref):
        acc = p_ref[0:1, :]
        for b in range(1, nb):
            acc = acc + p_ref[b:b + 1, :]
        o_ref[...] = acc

    return pl.pallas_call(body, name="sum8", out_shape=jax.ShapeDtypeStruct((1, n), F32),
                          compiler_params=_cparams())(parts)


def _outer8(act_t, dmod):
    d, nb = act_t.shape
    n = dmod.shape[1]

    def body(a_ref, m_ref, o_ref):
        acc = a_ref[:, 0:1] * m_ref[0:1, :]
        for b in range(1, nb):
            acc = acc + a_ref[:, b:b + 1] * m_ref[b:b + 1, :]
        o_ref[...] = acc

    return pl.pallas_call(body, name="outer8", out_shape=jax.ShapeDtypeStruct((d, n), F32),
                          compiler_params=_cparams())(act_t, dmod)


def _pad_lanes(v, width=LANE):
    return jnp.pad(v, ((0, 0), (0, width - v.shape[1])))


def kernel(x, c, w_ada, b_ada, norm_in_gain, w_in, conv_w, conv_b, dt_bias, a_log, d_skip, sb_norm_gain, ssm_norm_gain, w_out, norm_f_gain, loss_target, m_w_ada, m_b_ada, m_norm_in_gain, m_w_in, m_conv_w, m_conv_b, m_dt_bias, m_a_log, m_d_skip, m_sb_norm_gain, m_ssm_norm_gain, m_w_out, m_norm_f_gain, v_w_ada, v_b_ada, v_norm_in_gain, v_w_in, v_conv_w, v_conv_b, v_dt_bias, v_a_log, v_d_skip, v_sb_norm_gain, v_ssm_norm_gain, v_w_out, v_norm_f_gain):
    ax, ay, ac_ = _coords()
    chip = 2 * ax + ay
    me = 2 * chip + ac_
    my_c = jnp.reshape(ac_, (1,)).astype(jnp.int32)
    x2d, tgt = x[0], loss_target[0]
    s = x2d.shape[0]
    ada_cols = w_ada.shape[2]
    cw_cols = conv_w.shape[2]
    in_cols = w_in.shape[2]
    out_rows = w_out.shape[1]

    small = jnp.concatenate([c, conv_w[0].reshape(1, CONV_K * cw_cols)], axis=1)
    small_all = _allgather8(small, "gather_cond")[:, 0, :]
    c_all = small_all[:, :D_MODEL]
    conv_w_full = (small_all[0::2, D_MODEL:].reshape(N_CHIPS, CONV_K, cw_cols)
                   .transpose(1, 0, 2).reshape(CONV_K, D_XBC))
    b_ada_shard = lax.dynamic_slice_in_dim(b_ada, chip * ada_cols, ada_cols, axis=1)
    mod_part, c_act_all = _ada_mod(c_all, w_ada[0], b_ada_shard)
    mod_all = _allgather8(mod_part.reshape(1, N_DEV * ada_cols), "gather_mod")[0::2, 0, :]
    mod_all = mod_all.reshape(N_CHIPS, N_DEV, ada_cols)
    mod = lax.dynamic_index_in_dim(mod_all, me, axis=1, keepdims=False).reshape(1, 3 * D_MODEL)
    shift, scale, gate = mod[:, :D_MODEL], mod[:, D_MODEL:2 * D_MODEL], mod[:, 2 * D_MODEL:]

    w_in_mine, w_out_mine = w_in[0].T.astype(BF16), w_out[0].astype(BF16)
    w_in_all, w_out_all = _gather_shards([w_in_mine, w_out_mine], "gather_weights")
    w_in_all = lax.dynamic_update_slice(w_in_all, w_in_mine[None], (chip, 0, 0))
    w_out_all = lax.dynamic_update_slice(w_out_all, w_out_mine[None], (chip, 0, 0))
    w_in_t = w_in_all.reshape(D_PROJ, D_MODEL)
    dt_lo = 4 * D_ATTN + D_XBC
    w_main_t = jnp.concatenate([w_in_t[:4 * D_ATTN], w_in_t[dt_lo + N_HEADS:], w_in_t[4 * D_ATTN:dt_lo]], axis=0)
    w_dt_t = jnp.pad(w_in_t[dt_lo:dt_lo + N_HEADS], ((0, LANE - N_HEADS), (0, 0)))
    w_out_full = w_out_all.reshape(N_CHIPS * out_rows, D_MODEL)

    h = _rms_mod_fwd(x2d, norm_in_gain, scale, shift)
    proj = _matmul(h, w_main_t, BF16, "in_proj", "nt", 1024, 512, 1024)
    dtraw = _matmul(h, w_dt_t, F32, "in_proj_dt", "nt", 1024, LANE, 1024)
    o_attn, lsum = _attn_fwd(proj)
    y_attn = _gated_norm_fwd(o_attn, proj, OFF_ZA, sb_norm_gain, False, "attn_gate_fwd")
    act = _conv_fwd(proj, conv_w_full, conv_b)
    a_neg = -jnp.exp(a_log)
    arow, acol = _pad_lanes(a_neg), a_neg.reshape(N_HEADS, 1)
    bias_row, bias_col = _pad_lanes(dt_bias), dt_bias.reshape(N_HEADS, 1)
    dtt = dtraw[:, :N_HEADS].T
    dskip_row = jnp.repeat(d_skip, HEAD_DIM, axis=1)
    ssd_args = (act, dtraw, dtt, bias_row, bias_col, arow, acol, dskip_row)
    y_ssd, states = _ssd_fwd(*ssd_args)
    y_ssm = _gated_norm_fwd(y_ssd, proj, OFF_ZS, ssm_norm_gain, True, "ssm_gate_fwd")
    mix_in = jnp.concatenate([y_attn, y_ssm], axis=1)
    mixed = _matmul(mix_in, w_out_full, F32, "out_proj", "nn", 1024, 1024, 2048)

    dx2, dmixed, head_sums = _loss_head(x2d, mixed, gate, norm_f_gain.reshape(1, D_MODEL), tgt)
    g_w_out = _matmul(mix_in, dmixed, F32, "out_proj_dw", "tn", 512, 1024, 4096)
    d_mix_in = _matmul(dmixed, w_out_full, F32, "out_proj_dx", "nt", 1024, 1024, 1024)
    d_o, dz_attn, g_sb = _gated_norm_bwd(d_mix_in, 0, o_attn, proj, OFF_ZA, sb_norm_gain, False, "attn_gate_bwd")
    d_y, dz_ssm, g_ssm = _gated_norm_bwd(d_mix_in, 1, y_ssd, proj, OFF_ZS, ssm_norm_gain, True, "ssm_gate_bwd")
    dq, dk, dv = _attn_bwd(proj, d_o, lsum)
    dact, dld_col, dld_row, ddt_col, dd_cols = _ssd_bwd(*ssd_args, states, d_y)
    dxbc, g_conv_w, g_conv_b = _conv_bwd(proj, conv_w_full, conv_b, dact)
    ddtraw, dt_sums = _dt_bwd(dtraw, bias_row, arow, dld_col, _pad_lanes(dld_row.T), ddt_col)
    dproj = jnp.concatenate([dq, dk, dv, dz_attn, dz_ssm, dxbc], axis=1)
    g_main_t = _matmul(dproj, h, F32, "in_proj_dw", "tn", 512, 1024, 4096)
    g_dt_t = _matmul(ddtraw, h, F32, "in_proj_dw_dt", "tn", LANE, 1024, 4096)
    dh = _matmul(dproj, w_main_t, F32, "in_proj_dx", "nn", 1024, 1024, 3328, extra=(ddtraw, w_dt_t))
    grad_x, in_sums = _rms_mod_bwd(x2d, dh, dx2, norm_in_gain, scale)

    g_a_log = dt_sums[1:2, :N_HEADS] * a_neg
    g_d_skip = jnp.sum(dd_cols.reshape(N_HEADS, HEAD_DIM), axis=1).reshape(1, N_HEADS)
    dmod = jnp.concatenate([in_sums[0:1], in_sums[1:2], head_sums[2:3]], axis=1)
    loss_part = 0.5 / D_MODEL * jnp.sum(head_sums[0:1], axis=1, keepdims=True)
    pieces = [dmod, in_sums[2:3], g_conv_w.reshape(1, CONV_K * D_XBC), g_conv_b, _pad_lanes(dt_sums[0:1, :N_HEADS]),
              _pad_lanes(g_a_log), _pad_lanes(g_d_skip), g_sb, g_ssm, head_sums[1:2], _pad_lanes(loss_part)]
    widths = [p.shape[1] for p in pieces]
    parts_all = _allgather8(jnp.concatenate(pieces, axis=1), "gather_small_grads")[:, 0, :]
    total = _sum8(parts_all)
    offs = [0]
    for w_ in widths:
        offs.append(offs[-1] + w_)
    tot = [total[:, offs[i]:offs[i + 1]] for i in range(len(pieces))]
    g_b_ada, g_norm_in, g_conv_w_full = tot[0], tot[1], tot[2].reshape(CONV_K, D_XBC)
    g_conv_b_t, g_dt_bias, g_a_log_t, g_d_skip_t = tot[3], tot[4][:, :N_HEADS], tot[5][:, :N_HEADS], tot[6][:, :N_HEADS]
    g_sb_t, g_ssm_t, g_norm_f, loss = tot[7], tot[8], tot[9], tot[10][0, 0]
    g_conv_w_shard = lax.dynamic_slice_in_dim(g_conv_w_full, chip * cw_cols, cw_cols, axis=1)
    dmod_shard = lax.dynamic_slice_in_dim(parts_all[:, :3 * D_MODEL], chip * ada_cols, ada_cols, axis=1)
    g_w_ada = _outer8(c_act_all.T, dmod_shard)

    g_in_t = jnp.concatenate([g_main_t[:4 * D_ATTN], g_main_t[OFF_XBC:], g_dt_t[:N_HEADS], g_main_t[OFF_ZS:OFF_XBC]],
                             axis=0)
    g_in_blocks = g_in_t.reshape(N_CHIPS, in_cols, D_MODEL)
    g_out_blocks = g_w_out.reshape(N_CHIPS, out_rows, D_MODEL)
    land_in, land_out = _send_to_sibling([g_in_blocks, g_out_blocks], "grads_to_sibling")
    chip_in = _add_my_half(g_in_blocks, land_in, my_c, "add_sibling_in")
    chip_out = _add_my_half(g_out_blocks, land_out, my_c, "add_sibling_out")
    slots_in, slots_out = _exchange_chips([chip_in, chip_out], "grads_between_chips")
    own = lambda blocks: lax.dynamic_slice_in_dim(blocks, chip, 1, axis=0)
    slots_in = lax.dynamic_update_slice(slots_in, own(chip_in), (chip, 0, 0))
    slots_out = lax.dynamic_update_slice(slots_out, own(chip_out), (chip, 0, 0))
    half_in, half_out = _sum_slots(slots_in, "sum_chips_in"), _sum_slots(slots_out, "sum_chips_out")
    their_in, their_out = _swap_with_sibling([half_in, half_out], "grads_swap_sibling")
    south = ac_ == 0
    g_w_in = jnp.concatenate([jnp.where(south, half_in, their_in).T, jnp.where(south, their_in, half_in).T], axis=0)
    g_w_out_shard = jnp.concatenate([jnp.where(south, half_out, their_out), jnp.where(south, their_out, half_out)],
                                    axis=1)

    d_w_ada, nm_w_ada, nv_w_ada = _adamw(w_ada[0], g_w_ada, m_w_ada[0], v_w_ada[0], "adamw_w_ada")
    d_w_in, nm_w_in, nv_w_in = _adamw(w_in[0], g_w_in, m_w_in[0], v_w_in[0], "adamw_w_in")
    d_w_out, nm_w_out, nv_w_out = _adamw(w_out[0], g_w_out_shard, m_w_out[0], v_w_out[0], "adamw_w_out")
    flat = lambda a: a.reshape(1, -1)
    small_w = [b_ada, norm_in_gain, conv_w[0], conv_b, dt_bias, a_log, d_skip, sb_norm_gain, ssm_norm_gain,
               norm_f_gain]
    small_m = [m_b_ada, m_norm_in_gain, m_conv_w[0], m_conv_b, m_dt_bias, m_a_log, m_d_skip, m_sb_norm_gain,
               m_ssm_norm_gain, m_norm_f_gain]
    small_v = [v_b_ada, v_norm_in_gain, v_conv_w[0], v_conv_b, v_dt_bias, v_a_log, v_d_skip, v_sb_norm_gain,
               v_ssm_norm_gain, v_norm_f_gain]
    small_g = [g_b_ada, g_norm_in, g_conv_w_shard, g_conv_b_t, g_dt_bias, g_a_log_t, g_d_skip_t, g_sb_t, g_ssm_t,
               g_norm_f]
    cat = lambda arrs: jnp.concatenate([flat(a) for a in arrs], axis=1)
    d_small, nm_small, nv_small = _adamw(cat(small_w), cat(small_g), cat(small_m), cat(small_v), "adamw_small")
    sizes = [a.size for a in small_w]
    soffs = [0]
    for n_ in sizes:
        soffs.append(soffs[-1] + n_)

    def split(packed):
        return [packed[0, soffs[i]:soffs[i + 1]].reshape(small_w[i].shape) for i in range(len(small_w))]

    def ordered(big_ada, big_in, big_out, smalls):
        (s_b_ada, s_norm_in, s_conv_w, s_conv_b, s_dt_bias, s_a_log, s_d_skip, s_sb, s_ssm, s_norm_f) = smalls
        return [big_ada[None], s_b_ada, s_norm_in, big_in[None], s_conv_w[None], s_conv_b, s_dt_bias, s_a_log,
                s_d_skip, s_sb, s_ssm, big_out[None], s_norm_f]

    grads = ordered(g_w_ada, g_w_in, g_w_out_shard,
                    [g.reshape(w_.shape) for g, w_ in zip(small_g, small_w)])
    deltas = ordered(d_w_ada, d_w_in, d_w_out, split(d_small))
    new_m = ordered(nm_w_ada, nm_w_in, nm_w_out, split(nm_small))
    new_v = ordered(nv_w_ada, nv_w_in, nv_w_out, split(nv_small))
    return (loss, grad_x[None], *grads, *deltas, *new_m, *new_v)
```

```python
import functools

import jax
import jax.numpy as jnp
from jax import lax
from jax.experimental import pallas as pl
from jax.experimental.pallas import tpu as pltpu

F32, BF16 = jnp.float32, jnp.bfloat16
MESH = pl.DeviceIdType.MESH
HI = lax.Precision.HIGHEST
NN = (((1,), (0,)), ((), ()))
NT = (((1,), (1,)), ((), ()))
TN = (((0,), (0,)), ((), ()))

D_MODEL = 1024
D_ATTN = 1024
D_SSM = 1024
HEAD_DIM = 64
N_HEADS = 16
N_PAIRS = 8
N_GROUPS = 2
D_STATE = 128
D_XBC = 1536
D_PROJ = 6672
D_MAIN = 5632
CONV_K = 4
CHUNK = 128
LANE = 128
N_CHIPS = 4
N_DEV = 8
NORM_EPS = 1e-6
ATTN_SCALE = HEAD_DIM ** -0.5
ATTN_TQ = 512
ATTN_TK = 256
LOG_ZERO = -110.0
ADAM_LR, ADAM_B1, ADAM_B2, ADAM_EPS, ADAM_WD, ADAM_STEP = 0.001, 0.9, 0.999, 1e-08, 0.01, 10
VMEM_LIMIT = 56 * 1024 * 1024

OFF_Q, OFF_K, OFF_V, OFF_ZA, OFF_XBC = 0, 1024, 2048, 3072, 4096
DT_LO = D_MAIN
ZS_LO = DT_LO + N_HEADS


def _cparams(sem=None):
    return pltpu.CompilerParams(dimension_semantics=sem, vmem_limit_bytes=VMEM_LIMIT)


def _sigmoid(x):
    return 1.0 / (1.0 + jnp.exp(-x))


def _softplus(x):
    return jnp.maximum(x, 0.0) + jnp.log(1.0 + jnp.exp(-jnp.abs(x)))


def _coords():
    return lax.axis_index("x"), lax.axis_index("y"), lax.axis_index("c")


def _allgather8(v, name):
    n = v.shape[-1]

    def body(v_ref, out_ref, send_sems, recv_sems, local_sem):
        x, y, c = _coords()
        me = 4 * x + 2 * y + c
        mine = pltpu.make_async_copy(v_ref, out_ref.at[me], local_sem)
        mine.start()
        sends, recvs = [], []
        for j in range(1, N_DEV):
            px = 1 - x if (j >> 2) & 1 else x
            py = 1 - y if (j >> 1) & 1 else y
            pc = 1 - c if j & 1 else c
            peer = (px, py, pc)
            sends.append(pltpu.make_async_remote_copy(
                src_ref=v_ref, dst_ref=out_ref.at[me], send_sem=send_sems.at[j - 1],
                recv_sem=recv_sems.at[j - 1], device_id=peer, device_id_type=MESH))
            recvs.append(pltpu.make_async_remote_copy(
                src_ref=v_ref, dst_ref=out_ref.at[4 * px + 2 * py + pc], send_sem=send_sems.at[j - 1],
                recv_sem=recv_sems.at[j - 1], device_id=peer, device_id_type=MESH))
        for s in sends:
            s.start()
        for r in recvs:
            r.wait_recv()
        for s in sends:
            s.wait_send()
        mine.wait()

    vm = pl.BlockSpec(memory_space=pltpu.VMEM)
    return pl.pallas_call(
        body, name=name, out_shape=jax.ShapeDtypeStruct((N_DEV, 1, n), F32),
        in_specs=[vm], out_specs=vm,
        scratch_shapes=[pltpu.SemaphoreType.DMA((N_DEV - 1,)), pltpu.SemaphoreType.DMA((N_DEV - 1,)),
                        pltpu.SemaphoreType.DMA(())],
    )(v)


def _other_chips(x, y):
    chips = [(1 - x, y), (x, 1 - y), (1 - x, 1 - y)]
    return chips, [2 * cx + cy for cx, cy in chips]


def _half_cols(width, which):
    half = width // 2
    return pl.ds(pl.multiple_of(which * half, half), half)


def _gather_shards(arrs, name):
    n = len(arrs)

    def body(*refs):
        ins, outs = refs[:n], refs[n:2 * n]
        send_sems, recv_sems = refs[2 * n:]
        x, y, c = _coords()
        k = 2 * x + y
        chips, chip_idx = _other_chips(x, y)
        sibling = (x, y, 1 - c)
        sends = []
        for a in range(n):
            mine = _half_cols(arrs[a].shape[-1], c)
            for j in range(3):
                cp = pltpu.make_async_remote_copy(
                    src_ref=ins[a].at[:, mine], dst_ref=outs[a].at[k, :, mine], send_sem=send_sems.at[6 * a + j],
                    recv_sem=recv_sems.at[6 * a + j], device_id=(*chips[j], c), device_id_type=MESH)
                cp.start()
                sends.append(cp)
        for a in range(n):
            mine = _half_cols(arrs[a].shape[-1], c)
            for j in range(3):
                landed = outs[a].at[chip_idx[j], :, mine]
                pltpu.make_async_remote_copy(
                    src_ref=landed, dst_ref=landed, send_sem=send_sems.at[6 * a + j],
                    recv_sem=recv_sems.at[6 * a + j], device_id=(*chips[j], c), device_id_type=MESH).wait_recv()
                fwd = pltpu.make_async_remote_copy(
                    src_ref=landed, dst_ref=landed, send_sem=send_sems.at[6 * a + 3 + j],
                    recv_sem=recv_sems.at[6 * a + 3 + j], device_id=sibling, device_id_type=MESH)
                fwd.start()
                sends.append(fwd)
        for a in range(n):
            theirs = _half_cols(arrs[a].shape[-1], 1 - c)
            for j in range(3):
                landed = outs[a].at[chip_idx[j], :, theirs]
                pltpu.make_async_remote_copy(
                    src_ref=landed, dst_ref=landed, send_sem=send_sems.at[6 * a + 3 + j],
                    recv_sem=recv_sems.at[6 * a + 3 + j], device_id=sibling, device_id_type=MESH).wait_recv()
        for cp in sends:
            cp.wait_send()

    hbm = pl.BlockSpec(memory_space=pl.ANY)
    return pl.pallas_call(
        body, name=name,
        out_shape=tuple(jax.ShapeDtypeStruct((N_CHIPS,) + a.shape, a.dtype) for a in arrs),
        in_specs=[hbm] * n, out_specs=tuple([hbm] * n),
        scratch_shapes=[pltpu.SemaphoreType.DMA((6 * n,)), pltpu.SemaphoreType.DMA((6 * n,))],
    )(*arrs)


def _send_to_sibling(arrs, name):
    n = len(arrs)

    def body(*refs):
        ins, outs = refs[:n], refs[n:2 * n]
        send_sems, recv_sems = refs[2 * n:]
        x, y, c = _coords()
        cps = []
        for a in range(n):
            cp = pltpu.make_async_remote_copy(
                src_ref=ins[a].at[:, :, _half_cols(arrs[a].shape[-1], 1 - c)], dst_ref=outs[a],
                send_sem=send_sems.at[a], recv_sem=recv_sems.at[a], device_id=(x, y, 1 - c), device_id_type=MESH)
            cp.start()
            cps.append(cp)
        for cp in cps:
            cp.wait()

    hbm = pl.BlockSpec(memory_space=pl.ANY)
    return pl.pallas_call(
        body, name=name,
        out_shape=tuple(jax.ShapeDtypeStruct(a.shape[:-1] + (a.shape[-1] // 2,), a.dtype) for a in arrs),
        in_specs=[hbm] * n, out_specs=tuple([hbm] * n),
        scratch_shapes=[pltpu.SemaphoreType.DMA((n,)), pltpu.SemaphoreType.DMA((n,))],
    )(*arrs)


def _exchange_chips(arrs, name):
    n = len(arrs)

    def body(*refs):
        ins, outs = refs[:n], refs[n:2 * n]
        send_sems, recv_sems = refs[2 * n:]
        x, y, c = _coords()
        k = 2 * x + y
        chips, chip_idx = _other_chips(x, y)
        sends = []
        for a in range(n):
            for j in range(3):
                cp = pltpu.make_async_remote_copy(
                    src_ref=ins[a].at[chip_idx[j]], dst_ref=outs[a].at[k], send_sem=send_sems.at[3 * a + j],
                    recv_sem=recv_sems.at[3 * a + j], device_id=(*chips[j], c), device_id_type=MESH)
                cp.start()
                sends.append(cp)
        for a in range(n):
            for j in range(3):
                landed = outs[a].at[chip_idx[j]]
                pltpu.make_async_remote_copy(
                    src_ref=landed, dst_ref=landed, send_sem=send_sems.at[3 * a + j],
                    recv_sem=recv_sems.at[3 * a + j], device_id=(*chips[j], c), device_id_type=MESH).wait_recv()
        for cp in sends:
            cp.wait_send()

    hbm = pl.BlockSpec(memory_space=pl.ANY)
    return pl.pallas_call(
        body, name=name,
        out_shape=tuple(jax.ShapeDtypeStruct(a.shape, a.dtype) for a in arrs),
        in_specs=[hbm] * n, out_specs=tuple([hbm] * n),
        scratch_shapes=[pltpu.SemaphoreType.DMA((3 * n,)), pltpu.SemaphoreType.DMA((3 * n,))],
    )(*arrs)


def _swap_with_sibling(arrs, name):
    n = len(arrs)

    def body(*refs):
        ins, outs = refs[:n], refs[n:2 * n]
        send_sems, recv_sems = refs[2 * n:]
        x, y, c = _coords()
        cps = []
        for a in range(n):
            cp = pltpu.make_async_remote_copy(
                src_ref=ins[a], dst_ref=outs[a], send_sem=send_sems.at[a], recv_sem=recv_sems.at[a],
                device_id=(x, y, 1 - c), device_id_type=MESH)
            cp.start()
            cps.append(cp)
        for cp in cps:
            cp.wait()

    hbm = pl.BlockSpec(memory_space=pl.ANY)
    return pl.pallas_call(
        body, name=name,
        out_shape=tuple(jax.ShapeDtypeStruct(a.shape, a.dtype) for a in arrs),
        in_specs=[hbm] * n, out_specs=tuple([hbm] * n),
        scratch_shapes=[pltpu.SemaphoreType.DMA((n,)), pltpu.SemaphoreType.DMA((n,))],
    )(*arrs)


def _row_tile(rows, cols, n_arrays):
    budget = VMEM_LIMIT // 2
    t = rows
    while t % 16 == 0 and t * cols * 4 * n_arrays * 2 > budget:
        t //= 2
    return t


def _add_my_half(g, landed, my_c, name):
    nb, r, cdim = g.shape
    half = cdim // 2
    tr = _row_tile(r, half, 3)

    def body(c_ref, g_ref, l_ref, o_ref):
        o_ref[...] = (g_ref[...] + l_ref[...]).astype(BF16)

    spec = pl.BlockSpec((None, tr, half), lambda b, i, c_ref: (b, i, 0))
    return pl.pallas_call(
        body, name=name, out_shape=jax.ShapeDtypeStruct((nb, r, half), BF16),
        grid_spec=pltpu.PrefetchScalarGridSpec(
            num_scalar_prefetch=1, grid=(nb, r // tr),
            in_specs=[pl.BlockSpec((None, tr, half), lambda b, i, c_ref: (b, i, c_ref[0])), spec],
            out_specs=spec),
        compiler_params=_cparams(("parallel", "parallel")),
    )(my_c, g, landed)


def _sum_slots(a, name):
    nb, r, cdim = a.shape
    tr = _row_tile(r, cdim, 4)

    def body(a_ref, o_ref):
        o_ref[...] = ((a_ref[0].astype(F32) + a_ref[1].astype(F32)) + a_ref[2].astype(F32)) + a_ref[3].astype(F32)

    return pl.pallas_call(
        body, name=name, out_shape=jax.ShapeDtypeStruct((r, cdim), F32), grid=(r // tr,),
        in_specs=[pl.BlockSpec((nb, tr, cdim), lambda i: (0, i, 0))],
        out_specs=pl.BlockSpec((tr, cdim), lambda i: (i, 0)),
        compiler_params=_cparams(("parallel",)),
    )(a)


def _adamw(w, g, m, v, name):
    r, cdim = w.shape
    tr = _row_tile(r, cdim, 7)

    def body(w_ref, g_ref, m_ref, v_ref, d_ref, nm_ref, nv_ref):
        gv = g_ref[...]
        nm = ADAM_B1 * m_ref[...] + (1.0 - ADAM_B1) * gv
        nv = ADAM_B2 * v_ref[...] + (1.0 - ADAM_B2) * (gv * gv)
        m_hat = nm / (1.0 - ADAM_B1 ** ADAM_STEP)
        v_hat = nv / (1.0 - ADAM_B2 ** ADAM_STEP)
        d_ref[...] = -ADAM_LR * (m_hat / (jnp.sqrt(v_hat) + ADAM_EPS) + ADAM_WD * w_ref[...])
        nm_ref[...] = nm
        nv_ref[...] = nv

    spec = pl.BlockSpec((tr, cdim), lambda i: (i, 0))
    shp = jax.ShapeDtypeStruct((r, cdim), F32)
    return pl.pallas_call(
        body, name=name, out_shape=(shp, shp, shp), grid=(r // tr,),
        in_specs=[spec] * 4, out_specs=(spec, spec, spec),
        compiler_params=_cparams(("parallel",)),
    )(w, g, m, v)


def _matmul(a, b, out_dtype, name, mode, tm, tn, tk, extra=None, n_out=None):
    dims = {"nn": NN, "nt": NT, "tn": TN}[mode]
    if mode == "tn":
        kdim, m = a.shape
    else:
        m, kdim = a.shape
    n = n_out if n_out is not None else (b.shape[0] if mode == "nt" else b.shape[1])
    tm, tn, tk = min(tm, m), min(tn, n), min(tk, kdim)
    nk = kdim // tk
    a_spec = (pl.BlockSpec((tk, tm), lambda i, j, k: (k, i)) if mode == "tn"
              else pl.BlockSpec((tm, tk), lambda i, j, k: (i, k)))
    b_spec = (pl.BlockSpec((tn, tk), lambda i, j, k: (j, k)) if mode == "nt"
              else pl.BlockSpec((tk, tn), lambda i, j, k: (k, j)))
    in_specs, operands = [a_spec, b_spec], [a, b]
    if extra is not None:
        a2, b2 = extra
        k2 = a2.shape[0] if mode == "tn" else a2.shape[1]
        in_specs.append(pl.BlockSpec((k2, tm), lambda i, j, k: (0, i)) if mode == "tn"
                        else pl.BlockSpec((tm, k2), lambda i, j, k: (i, 0)))
        in_specs.append(pl.BlockSpec((tn, k2), lambda i, j, k: (j, 0)) if mode == "nt"
                        else pl.BlockSpec((k2, tn), lambda i, j, k: (0, j)))
        operands += [a2, b2]

    def body_one_block(*refs):
        acc = lax.dot_general(refs[0][...], refs[1][...], dims, preferred_element_type=F32)
        if extra is not None:
            acc += lax.dot_general(refs[2][...], refs[3][...], dims, preferred_element_type=F32)
        refs[-1][...] = acc.astype(out_dtype)

    if nk == 1:
        return pl.pallas_call(
            body_one_block, name=name, out_shape=jax.ShapeDtypeStruct((m, n), out_dtype), grid=(m // tm, n // tn, 1),
            in_specs=in_specs, out_specs=pl.BlockSpec((tm, tn), lambda i, j, k: (i, j)),
            compiler_params=_cparams(("parallel", "parallel", "arbitrary")),
        )(*operands)

    def body(*refs):
        if extra is not None:
            a_ref, b_ref, a2_ref, b2_ref, o_ref, acc_ref = refs
        else:
            a_ref, b_ref, o_ref, acc_ref = refs
        k = pl.program_id(2)

        @pl.when(k == 0)
        def _():
            if extra is not None:
                acc_ref[...] = lax.dot_general(a2_ref[...], b2_ref[...], dims, preferred_element_type=F32)
            else:
                acc_ref[...] = jnp.zeros_like(acc_ref)

        acc_ref[...] += lax.dot_general(a_ref[...], b_ref[...], dims, preferred_element_type=F32)

        @pl.when(k == nk - 1)
        def _():
            o_ref[...] = acc_ref[...].astype(out_dtype)

    return pl.pallas_call(
        body, name=name, out_shape=jax.ShapeDtypeStruct((m, n), out_dtype), grid=(m // tm, n // tn, nk),
        in_specs=in_specs, out_specs=pl.BlockSpec((tm, tn), lambda i, j, k: (i, j)),
        scratch_shapes=[pltpu.VMEM((tm, tn), F32)],
        compiler_params=_cparams(("parallel", "parallel", "arbitrary")),
    )(*operands)


def _matmul_sum(terms, out_dtype, name, tm, tn):
    m, n = terms[0][0].shape[0], terms[0][3].shape[1]
    tm, tn = min(tm, m), min(tn, n)
    in_specs, operands = [], []
    for a, ka, ia, b, ib in terms:
        in_specs.append(pl.BlockSpec((tm, ka), functools.partial(lambda i, j, ia: (i, ia), ia=ia)))
        in_specs.append(pl.BlockSpec((ka, tn), functools.partial(lambda i, j, ib: (ib, j), ib=ib)))
        operands += [a, b]

    def body(*refs):
        acc = jnp.dot(refs[0][...], refs[1][...], preferred_element_type=F32)
        for t in range(1, len(terms)):
            acc += jnp.dot(refs[2 * t][...], refs[2 * t + 1][...], preferred_element_type=F32)
        refs[-1][...] = acc.astype(out_dtype)

    return pl.pallas_call(
        body, name=name, out_shape=jax.ShapeDtypeStruct((m, n), out_dtype), grid=(m // tm, n // tn),
        in_specs=in_specs, out_specs=pl.BlockSpec((tm, tn), lambda i, j: (i, j)),
        compiler_params=_cparams(("parallel", "parallel")),
    )(*operands)


def _matmul_tn_rows(buf, rows, a, b, row_blk, name, tm):
    kdim, m = a.shape
    n = b.shape[1]
    tm = min(tm, m)

    def body(*refs):
        a_ref, b_ref, o_ref = refs[-3:]
        o_ref[...] = lax.dot_general(a_ref[...], b_ref[...], TN, preferred_element_type=F32)

    in_specs = [pl.BlockSpec((kdim, tm), lambda i: (0, i)), pl.BlockSpec((kdim, n), lambda i: (0, 0))]
    operands = [a, b]
    if buf is not None:
        in_specs.insert(0, pl.BlockSpec(memory_space=pl.ANY))
        operands.insert(0, buf)
    return pl.pallas_call(
        body, name=name, out_shape=jax.ShapeDtypeStruct((rows, n), F32), grid=(m // tm,),
        in_specs=in_specs, out_specs=pl.BlockSpec((tm, n), lambda i: (row_blk + i, 0)),
        input_output_aliases={} if buf is None else {0: 0},
        compiler_params=_cparams(("parallel",)),
    )(*operands)


def _ada_mod(c_all, w_shard, b_shard):
    nb, d = c_all.shape
    cols = w_shard.shape[1]

    def body(c_ref, w_ref, b_ref, mod_ref, act_ref):
        cv = c_ref[...]
        act = cv * _sigmoid(cv)
        act_ref[...] = act
        mod_ref[...] = jnp.dot(act, w_ref[...], preferred_element_type=F32, precision=HI) + b_ref[...]

    return pl.pallas_call(
        body, name="ada_mod",
        out_shape=(jax.ShapeDtypeStruct((nb, cols), F32), jax.ShapeDtypeStruct((nb, d), F32)),
        compiler_params=_cparams(),
    )(c_all, w_shard, b_shard)


def _rms_mod_fwd(x, gain, scale, shift):
    s, d = x.shape
    tm = min(512, s)

    def body(x_ref, g_ref, sc_ref, sh_ref, h_ref):
        xv = x_ref[...]
        r = lax.rsqrt(jnp.mean(xv * xv, axis=-1, keepdims=True) + NORM_EPS)
        h_ref[...] = (xv * r * g_ref[...] * (1.0 + sc_ref[...]) + sh_ref[...]).astype(BF16)

    row = pl.BlockSpec((1, d), lambda i: (0, 0))
    tile = pl.BlockSpec((tm, d), lambda i: (i, 0))
    return pl.pallas_call(
        body, name="rms_mod_fwd", out_shape=jax.ShapeDtypeStruct((s, d), BF16), grid=(s // tm,),
        in_specs=[tile, row, row, row], out_specs=tile, compiler_params=_cparams(("parallel",)),
    )(x, gain, scale, shift)


def _rms_mod_bwd(x, dh, dres, gain, scale):
    s, d = x.shape
    tm = min(512, s)

    def body(x_ref, dh_ref, dres_ref, g_ref, sc_ref, dx_ref, sums_ref):
        @pl.when(pl.program_id(0) == 0)
        def _():
            sums_ref[...] = jnp.zeros_like(sums_ref)

        xv, dhv = x_ref[...], dh_ref[...]
        r = lax.rsqrt(jnp.mean(xv * xv, axis=-1, keepdims=True) + NORM_EPS)
        nrm = xv * r
        g, one_sc = g_ref[...], 1.0 + sc_ref[...]
        dn = dhv * g * one_sc
        dx_ref[...] = r * (dn - nrm * jnp.mean(dn * nrm, axis=-1, keepdims=True)) + dres_ref[...]
        dhn = dhv * nrm
        sums_ref[0:1, :] += jnp.sum(dhv, axis=0, keepdims=True)
        sums_ref[1:2, :] += jnp.sum(dhn * g, axis=0, keepdims=True)
        sums_ref[2:3, :] += jnp.sum(dhn * one_sc, axis=0, keepdims=True)

    row = pl.BlockSpec((1, d), lambda i: (0, 0))
    tile = pl.BlockSpec((tm, d), lambda i: (i, 0))
    return pl.pallas_call(
        body, name="rms_mod_bwd",
        out_shape=(jax.ShapeDtypeStruct((s, d), F32), jax.ShapeDtypeStruct((3, d), F32)), grid=(s // tm,),
        in_specs=[tile, tile, tile, row, row], out_specs=(tile, pl.BlockSpec((3, d), lambda i: (0, 0))),
        compiler_params=_cparams(("arbitrary",)),
    )(x, dh, dres, gain, scale)


def _loss_head(x, mixed, gate, gain_f, target):
    s, d = x.shape
    tm = min(512, s)

    def body(x_ref, mx_ref, gt_ref, gf_ref, t_ref, dx2_ref, dmx_ref, sums_ref):
        @pl.when(pl.program_id(0) == 0)
        def _():
            sums_ref[...] = jnp.zeros_like(sums_ref)

        mx, gt, gf = mx_ref[...], gt_ref[...], gf_ref[...]
        x2 = x_ref[...] + gt * mx
        r = lax.rsqrt(jnp.mean(x2 * x2, axis=-1, keepdims=True) + NORM_EPS)
        nrm = x2 * r
        err = nrm * gf - t_ref[...]
        dyf = err * (1.0 / d)
        dn = dyf * gf
        dx2 = r * (dn - nrm * jnp.mean(dn * nrm, axis=-1, keepdims=True))
        dx2_ref[...] = dx2
        dmx_ref[...] = (dx2 * gt).astype(BF16)
        sums_ref[0:1, :] += jnp.sum(err * err, axis=0, keepdims=True)
        sums_ref[1:2, :] += jnp.sum(dyf * nrm, axis=0, keepdims=True)
        sums_ref[2:3, :] += jnp.sum(dx2 * mx, axis=0, keepdims=True)

    row = pl.BlockSpec((1, d), lambda i: (0, 0))
    tile = pl.BlockSpec((tm, d), lambda i: (i, 0))
    return pl.pallas_call(
        body, name="loss_head",
        out_shape=(jax.ShapeDtypeStruct((s, d), F32), jax.ShapeDtypeStruct((s, d), BF16),
                   jax.ShapeDtypeStruct((3, d), F32)),
        grid=(s // tm,), in_specs=[tile, tile, row, row, tile],
        out_specs=(tile, tile, pl.BlockSpec((3, d), lambda i: (0, 0))),
        compiler_params=_cparams(("arbitrary",)),
    )(x, mixed, gate, gain_f, target)


def _silu_grad(z, sg):
    return sg * (1.0 + z * (1.0 - sg))


def _gated_norm_fwd(o, proj, z_off, gain, gate_inside, name):
    s, d = o.shape
    tm = min(512, s)
    zb = z_off // d

    def body(o_ref, z_ref, g_ref, y_ref):
        z = z_ref[...].astype(F32)
        sz = z * _sigmoid(z)
        u = o_ref[...] * sz if gate_inside else o_ref[...]
        r = lax.rsqrt(jnp.mean(u * u, axis=-1, keepdims=True) + NORM_EPS)
        y = u * r * g_ref[...]
        y_ref[...] = (y if gate_inside else y * sz).astype(BF16)

    tile = pl.BlockSpec((tm, d), lambda i: (i, 0))
    return pl.pallas_call(
        body, name=name, out_shape=jax.ShapeDtypeStruct((s, d), BF16), grid=(s // tm,),
        in_specs=[tile, pl.BlockSpec((tm, d), lambda i: (i, zb)), pl.BlockSpec((1, d), lambda i: (0, 0))],
        out_specs=tile, compiler_params=_cparams(("parallel",)),
    )(o, proj, gain)


def _gated_norm_bwd(dy_all, dy_blk, o, proj, z_off, gain, gate_inside, name):
    s, d = o.shape
    tm = min(512, s)
    zb = z_off // d

    def body(dy_ref, o_ref, z_ref, g_ref, do_ref, dz_ref, dg_ref):
        @pl.when(pl.program_id(0) == 0)
        def _():
            dg_ref[...] = jnp.zeros_like(dg_ref)

        z = z_ref[...].astype(F32)
        sg = _sigmoid(z)
        sz = z * sg
        ov, dy, g = o_ref[...], dy_ref[...], g_ref[...]
        u = ov * sz if gate_inside else ov
        r = lax.rsqrt(jnp.mean(u * u, axis=-1, keepdims=True) + NORM_EPS)
        nrm = u * r
        if gate_inside:
            dg_ref[...] += jnp.sum(dy * nrm, axis=0, keepdims=True)
            dn = dy * g
        else:
            dg_ref[...] += jnp.sum(dy * nrm * sz, axis=0, keepdims=True)
            dn = dy * g * sz
        du = r * (dn - nrm * jnp.mean(dn * nrm, axis=-1, keepdims=True))
        if gate_inside:
            do_ref[...] = du * sz
            dz_ref[...] = (du * ov * _silu_grad(z, sg)).astype(BF16)
        else:
            do_ref[...] = du
            dz_ref[...] = (dy * nrm * g * _silu_grad(z, sg)).astype(BF16)

    tile = pl.BlockSpec((tm, d), lambda i: (i, 0))
    row = pl.BlockSpec((1, d), lambda i: (0, 0))
    return pl.pallas_call(
        body, name=name,
        out_shape=(jax.ShapeDtypeStruct((s, d), F32), jax.ShapeDtypeStruct((s, d), BF16),
                   jax.ShapeDtypeStruct((1, d), F32)),
        grid=(s // tm,),
        in_specs=[pl.BlockSpec((tm, d), lambda i: (i, dy_blk)), tile, pl.BlockSpec((tm, d), lambda i: (i, zb)), row],
        out_specs=(tile, tile, row), compiler_params=_cparams(("arbitrary",)),
    )(dy_all, o, proj, gain)


def _sb_logits(qh, kb):
    z = lax.dot_general(qh, kb, NT, preferred_element_type=F32)
    neg_abs = lax.bitcast_convert_type(lax.bitcast_convert_type(z, jnp.uint32) | jnp.uint32(0x80000000), F32)
    lb = jnp.minimum(z, 0.0) - jnp.log(1.0 + jnp.exp(neg_abs))
    return lb, lb - z


def _attn_consts(tk):
    lane = lax.broadcasted_iota(jnp.int32, (1, LANE), 1)
    row = lax.broadcasted_iota(jnp.int32, (tk, tk), 0)
    col = lax.broadcasted_iota(jnp.int32, (tk, tk), 1)
    return (lane < HEAD_DIM, lane >= HEAD_DIM), row, col


def _band_mask(rows, tk):
    return lax.broadcasted_iota(jnp.int32, (rows, tk), 1) < lax.broadcasted_iota(jnp.int32, (rows, tk), 0)


def _attn_fwd(proj):
    s = proj.shape[0]
    tq, tk = min(ATTN_TQ, s), min(ATTN_TK, s)
    r = tq // tk

    def body(q_ref, k_ref, v_ref, o_ref, l_ref, acc_ref, run_ref):
        i = pl.program_id(1)
        head_mask, row, col = _attn_consts(tk)
        later = (row > col).astype(BF16)
        q = q_ref[...] * ATTN_SCALE
        qh = [jnp.where(m, q, jnp.zeros_like(q)) for m in head_mask]
        acc_ref[...] = jnp.zeros_like(acc_ref)
        run_ref[...] = jnp.zeros_like(run_ref)

        def block(j, lo, hi, band):
            start = pl.multiple_of(j * tk, tk)
            kb = k_ref[pl.ds(start, tk), :]
            vb = v_ref[pl.ds(start, tk), :]
            rows = slice(lo, hi)
            causal = _band_mask(hi - lo, tk) if band else None
            hs = range(2)
            logits = [_sb_logits(qh[h][rows], kb) for h in hs]
            lb = [logits[h][0] for h in hs]
            l1m = [logits[h][1] if causal is None else jnp.where(causal, logits[h][1], 0.0) for h in hs]
            tail = [jnp.dot(l1m[h].astype(BF16), later, preferred_element_type=F32) + run_ref[h, rows] for h in hs]
            w = [jnp.exp(lb[h] + tail[h]) for h in hs]
            if causal is not None:
                w = [jnp.where(causal, w[h], 0.0) for h in hs]
            vh = [jnp.where(head_mask[h], vb, jnp.zeros_like(vb)) for h in hs]
            acc_ref[rows, :] += (jnp.dot(w[0].astype(BF16), vh[0], preferred_element_type=F32)
                                 + jnp.dot(w[1].astype(BF16), vh[1], preferred_element_type=F32))
            for h in hs:
                run_ref[h, rows] += jnp.sum(l1m[h], axis=1, keepdims=True)

        for b in reversed(range(r)):
            block(i * r + b, b * tk, tq, True)
        n_full = i * r
        half = tq // 2

        def more(c):
            return jnp.logical_and(c[0] < n_full, c[1] > LOG_ZERO)

        def step_all(c):
            block(n_full - 1 - c[0], 0, tq, False)
            return c[0] + 1, jnp.max(run_ref[:, half:, :])

        def step_upper(c):
            block(n_full - 1 - c[0], 0, half, False)
            return c[0] + 1, jnp.max(run_ref[:, :half, :])

        seen_all, _ = lax.while_loop(more, step_all, (jnp.int32(0), jnp.max(run_ref[:, half:, :])))
        seen, _ = lax.while_loop(more, step_upper, (seen_all, jnp.max(run_ref[:, :half, :])))
        o_ref[...] = acc_ref[...]
        lane = lax.broadcasted_iota(jnp.int32, (1, LANE), 1)
        first = jnp.where(lane < 3 * HEAD_DIM // 4, n_full - seen, n_full - seen_all).astype(F32)
        l_ref[...] = jnp.where(lane < HEAD_DIM // 2, run_ref[0], jnp.where(lane < HEAD_DIM, first, run_ref[1]))

    kq, kk, kv = OFF_Q // LANE, OFF_K // LANE, OFF_V // LANE
    tile = pl.BlockSpec((tq, LANE), lambda p, i: (i, p))
    return pl.pallas_call(
        body, name="attn_fwd",
        out_shape=(jax.ShapeDtypeStruct((s, D_ATTN), F32), jax.ShapeDtypeStruct((s, D_ATTN), F32)),
        grid=(N_PAIRS, s // tq),
        in_specs=[pl.BlockSpec((tq, LANE), lambda p, i: (i, kq + p)),
                  pl.BlockSpec((s, LANE), lambda p, i: (0, kk + p)),
                  pl.BlockSpec((s, LANE), lambda p, i: (0, kv + p))],
        out_specs=(tile, tile),
        scratch_shapes=[pltpu.VMEM((tq, LANE), F32), pltpu.VMEM((2, tq, 1), F32)],
        compiler_params=_cparams(("parallel", "arbitrary")),
    )(proj, proj, proj)


def _attn_bwd(proj, do, lsum):
    s = proj.shape[0]
    tq, tk = min(ATTN_TQ, s), min(ATTN_TK, s)
    r = tq // tk

    def body(q_ref, k_ref, v_ref, do_ref, l_ref, dq_ref, dk_ref, dv_ref, dqacc_ref, dkacc_ref, dvacc_ref,
             passed_ref, pre_ref):
        i = pl.program_id(1)

        @pl.when(i == 0)
        def _():
            dkacc_ref[...] = jnp.zeros_like(dkacc_ref)
            dvacc_ref[...] = jnp.zeros_like(dvacc_ref)

        head_mask, row, col = _attn_consts(tk)
        later = (row > col).astype(BF16)
        earlier = (row < col).astype(BF16)
        q = q_ref[...] * ATTN_SCALE
        dov = do_ref[...].astype(BF16)
        qh = [jnp.where(m, q, jnp.zeros_like(q)) for m in head_mask]
        doh = [jnp.where(m, dov, jnp.zeros_like(dov)) for m in head_mask]
        lsum_v = l_ref[...]
        lh = [lsum_v[:, 0:1], lsum_v[:, HEAD_DIM:HEAD_DIM + 1]]
        n_full = i * r
        half = tq // 2
        quarter = HEAD_DIM // 4
        first_all = jnp.clip(jnp.max(lsum_v[0:8, 3 * quarter:HEAD_DIM]).astype(jnp.int32), 0, n_full)
        first = jnp.clip(jnp.max(lsum_v[0:8, 2 * quarter:3 * quarter]).astype(jnp.int32), 0, first_all)
        dqacc_ref[...] = jnp.zeros_like(dqacc_ref)
        passed_ref[...] = jnp.zeros_like(passed_ref)
        pre_ref[...] = jnp.zeros_like(pre_ref)

        def block(j, lo, hi, band):
            start = pl.multiple_of(j * tk, tk)
            kb = k_ref[pl.ds(start, tk), :]
            vb = v_ref[pl.ds(start, tk), :]
            rows = slice(lo, hi)
            causal = _band_mask(hi - lo, tk) if band else None
            hs = range(2)
            q_rows = [qh[h][rows] for h in hs]
            do_rows = [doh[h][rows] for h in hs]
            logits = [_sb_logits(q_rows[h], kb) for h in hs]
            lb = [logits[h][0] for h in hs]
            l1m = [logits[h][1] if causal is None else jnp.where(causal, logits[h][1], 0.0) for h in hs]
            da = [lax.dot_general(do_rows[h], vb, NT, preferred_element_type=F32) for h in hs]
            rs = [jnp.sum(l1m[h], axis=1, keepdims=True) for h in hs]
            right = [lh[h][rows] - passed_ref[h, rows] - rs[h] for h in hs]
            for h in hs:
                passed_ref[h, rows] += rs[h]
            tail = [jnp.dot(l1m[h].astype(BF16), later, preferred_element_type=F32) + right[h] for h in hs]
            a = [jnp.exp(lb[h] + tail[h]) for h in hs]
            if causal is not None:
                a = [jnp.where(causal, a[h], 0.0) for h in hs]
            g = [a[h] * da[h] for h in hs]
            pre = [jnp.dot(g[h].astype(BF16), earlier, preferred_element_type=F32) + pre_ref[h, rows] for h in hs]
            for h in hs:
                pre_ref[h, rows] += jnp.sum(g[h], axis=1, keepdims=True)
            dz = [g[h] - jnp.exp(lb[h]) * (g[h] + pre[h]) for h in hs]
            if causal is not None:
                dz = [jnp.where(causal, dz[h], 0.0) for h in hs]
            dzb = [dz[h].astype(BF16) for h in hs]
            kh = [jnp.where(head_mask[h], kb, jnp.zeros_like(kb)) * ATTN_SCALE for h in hs]
            dqacc_ref[rows, :] += (jnp.dot(dzb[0], kh[0], preferred_element_type=F32)
                                   + jnp.dot(dzb[1], kh[1], preferred_element_type=F32))
            dvacc_ref[pl.ds(start, tk), :] += (
                lax.dot_general(a[0].astype(BF16), do_rows[0], TN, preferred_element_type=F32)
                + lax.dot_general(a[1].astype(BF16), do_rows[1], TN, preferred_element_type=F32))
            dkacc_ref[pl.ds(start, tk), :] += (
                lax.dot_general(dzb[0], q_rows[0], TN, preferred_element_type=F32)
                + lax.dot_general(dzb[1], q_rows[1], TN, preferred_element_type=F32))

        def step_upper(j, carry):
            block(j, 0, half, False)
            return carry

        def step_all(j, carry):
            block(j, 0, tq, False)
            return carry

        lax.fori_loop(first, first_all, step_upper, 0)
        lax.fori_loop(first_all, n_full, step_all, 0)
        for b in range(r):
            block(n_full + b, b * tk, tq, True)
        dq_ref[...] = dqacc_ref[...].astype(BF16)

        @pl.when(i == pl.num_programs(1) - 1)
        def _():
            dk_ref[...] = dkacc_ref[...].astype(BF16)
            dv_ref[...] = dvacc_ref[...].astype(BF16)

    kq, kk, kv = OFF_Q // LANE, OFF_K // LANE, OFF_V // LANE
    tile = pl.BlockSpec((tq, LANE), lambda p, i: (i, p))
    full = pl.BlockSpec((s, LANE), lambda p, i: (0, p))
    shp = jax.ShapeDtypeStruct((s, D_ATTN), BF16)
    return pl.pallas_call(
        body, name="attn_bwd", out_shape=(shp, shp, shp), grid=(N_PAIRS, s // tq),
        in_specs=[pl.BlockSpec((tq, LANE), lambda p, i: (i, kq + p)),
                  pl.BlockSpec((s, LANE), lambda p, i: (0, kk + p)),
                  pl.BlockSpec((s, LANE), lambda p, i: (0, kv + p)),
                  tile, tile],
        out_specs=(tile, full, full),
        scratch_shapes=[pltpu.VMEM((tq, LANE), F32), pltpu.VMEM((s, LANE), F32), pltpu.VMEM((s, LANE), F32),
                        pltpu.VMEM((2, tq, 1), F32), pltpu.VMEM((2, tq, 1), F32)],
        compiler_params=_cparams(("parallel", "arbitrary")),
    )(proj, proj, proj, do, lsum)


def _shift_down(u, k, rows):
    return jnp.where(rows >= k, pltpu.roll(u, k, 0), 0.0)


def _shift_up(u, k, rows, s):
    return jnp.where(rows < s - k, pltpu.roll(u, s - k, 0), 0.0)


def _conv_fwd(proj, w, b):
    s = proj.shape[0]
    blk0 = OFF_XBC // LANE

    def body(u_ref, w_ref, b_ref, o_ref):
        u = u_ref[...].astype(F32)
        rows = lax.broadcasted_iota(jnp.int32, (s, 1), 0)
        pre = u * w_ref[CONV_K - 1:CONV_K, :] + b_ref[...]
        for k in range(1, CONV_K):
            pre += _shift_down(u, k, rows) * w_ref[CONV_K - 1 - k:CONV_K - k, :]
        o_ref[...] = pre * _sigmoid(pre)

    return pl.pallas_call(
        body, name="conv_fwd", out_shape=jax.ShapeDtypeStruct((s, D_XBC), F32), grid=(D_XBC // LANE,),
        in_specs=[pl.BlockSpec((s, LANE), lambda j: (0, blk0 + j)), pl.BlockSpec((CONV_K, LANE), lambda j: (0, j)),
                  pl.BlockSpec((1, LANE), lambda j: (0, j))],
        out_specs=pl.BlockSpec((s, LANE), lambda j: (0, j)), compiler_params=_cparams(("parallel",)),
    )(proj, w, b)


def _conv_bwd(proj, w, b, dact):
    s = proj.shape[0]
    blk0 = OFF_XBC // LANE

    def body(u_ref, w_ref, b_ref, da_ref, du_ref, dw_ref, db_ref):
        u = u_ref[...].astype(F32)
        rows = lax.broadcasted_iota(jnp.int32, (s, 1), 0)
        shifted = [u] + [_shift_down(u, k, rows) for k in range(1, CONV_K)]
        pre = b_ref[...] + shifted[0] * w_ref[CONV_K - 1:CONV_K, :]
        for k in range(1, CONV_K):
            pre += shifted[k] * w_ref[CONV_K - 1 - k:CONV_K - k, :]
        sg = _sigmoid(pre)
        dpre = da_ref[...] * _silu_grad(pre, sg)
        db_ref[...] = jnp.sum(dpre, axis=0, keepdims=True)
        du = dpre * w_ref[CONV_K - 1:CONV_K, :]
        for k in range(CONV_K):
            dw_ref[CONV_K - 1 - k:CONV_K - k, :] = jnp.sum(dpre * shifted[k], axis=0, keepdims=True)
            if k:
                du += _shift_up(dpre, k, rows, s) * w_ref[CONV_K - 1 - k:CONV_K - k, :]
        du_ref[...] = du.astype(BF16)

    col = pl.BlockSpec((s, LANE), lambda j: (0, j))
    return pl.pallas_call(
        body, name="conv_bwd",
        out_shape=(jax.ShapeDtypeStruct((s, D_XBC), BF16), jax.ShapeDtypeStruct((CONV_K, D_XBC), F32),
                   jax.ShapeDtypeStruct((1, D_XBC), F32)),
        grid=(D_XBC // LANE,),
        in_specs=[pl.BlockSpec((s, LANE), lambda j: (0, blk0 + j)), pl.BlockSpec((CONV_K, LANE), lambda j: (0, j)),
                  pl.BlockSpec((1, LANE), lambda j: (0, j)), col],
        out_specs=(col, pl.BlockSpec((CONV_K, LANE), lambda j: (0, j)), pl.BlockSpec((1, LANE), lambda j: (0, j))),
        compiler_params=_cparams(("parallel",)),
    )(proj, w, b, dact)


def _ssd_decays(dtraw_ref, bias_ref, dtt_ref, biast_ref, arow_ref, acol_ref):
    ln = CHUNK
    dt = _softplus(dtraw_ref[...] + bias_ref[...])
    r = lax.broadcasted_iota(jnp.int32, (ln, ln), 0)
    c = lax.broadcasted_iota(jnp.int32, (ln, ln), 1)
    ac = jnp.dot((r >= c).astype(F32), dt * arow_ref[...], preferred_element_type=F32, precision=HI)
    dtt = _softplus(dtt_ref[...] + biast_ref[...])
    act = jnp.dot(dtt * acol_ref[...], (r <= c).astype(F32), preferred_element_type=F32, precision=HI)
    return dt, ac, act, r >= c


def _pair_cols(m0, v, h0):
    return jnp.where(m0, v[:, h0:h0 + 1], v[:, h0 + 1:h0 + 2])


def _ssd_fwd(act, dtraw, dtt, bias, biast, arow, acol, dskip):
    s = act.shape[0]
    ln = CHUNK
    nc = s // ln

    def body(act_ref, dtraw_ref, dtt_ref, bias_ref, biast_ref, arow_ref, acol_ref, dsk_ref, y_ref, st_ref,
             state_ref):
        @pl.when(pl.program_id(0) == 0)
        def _():
            state_ref[...] = jnp.zeros_like(state_ref)

        dt, ac, act_t, lower = _ssd_decays(dtraw_ref, bias_ref, dtt_ref, biast_ref, arow_ref, acol_ref)
        lane = lax.broadcasted_iota(jnp.int32, (1, LANE), 1)
        m0 = lane < HEAD_DIM
        for g in range(N_GROUPS):
            bg32 = act_ref[:, D_SSM + g * D_STATE:D_SSM + (g + 1) * D_STATE]
            bg, bg_t = bg32.astype(BF16), bg32.T.astype(BF16)
            cg = act_ref[:, D_SSM + (N_GROUPS + g) * D_STATE:D_SSM + (N_GROUPS + g + 1) * D_STATE].astype(BF16)
            cb = lax.dot_general(cg, bg, NT, preferred_element_type=F32)
            for p in range(g * 4, g * 4 + 4):
                h0 = 2 * p
                xp = act_ref[:, p * LANE:(p + 1) * LANE]
                xdt = xp * _pair_cols(m0, dt, h0)
                acp = _pair_cols(m0, ac, h0)
                last = acp[ln - 1:ln, :]
                y = xp * dsk_ref[:, p * LANE:(p + 1) * LANE]
                for hh in range(2):
                    h = h0 + hh
                    dm = jnp.exp(jnp.where(lower, ac[:, h:h + 1] - act_t[h:h + 1, :], -jnp.inf))
                    mask = m0 if hh == 0 else jnp.logical_not(m0)
                    y += jnp.dot((cb * dm).astype(BF16), jnp.where(mask, xdt, 0.0).astype(BF16),
                                 preferred_element_type=F32)
                prev = state_ref[p]
                st_ref[0, p] = prev
                y += jnp.dot(cg, prev.astype(BF16), preferred_element_type=F32) * jnp.exp(acp)
                y_ref[:, p * LANE:(p + 1) * LANE] = y
                cs = jnp.dot(bg_t, (xdt * jnp.exp(last - acp)).astype(BF16), preferred_element_type=F32)
                state_ref[p] = prev * jnp.exp(last) + cs

    row = lambda w: pl.BlockSpec((1, w), lambda c: (0, 0))
    return pl.pallas_call(
        body, name="ssd_fwd",
        out_shape=(jax.ShapeDtypeStruct((s, D_SSM), F32),
                   jax.ShapeDtypeStruct((nc, N_PAIRS, LANE, D_STATE), F32)),
        grid=(nc,),
        in_specs=[pl.BlockSpec((ln, D_XBC), lambda c: (c, 0)), pl.BlockSpec((ln, LANE), lambda c: (c, 0)),
                  pl.BlockSpec((N_HEADS, ln), lambda c: (0, c)), row(LANE),
                  pl.BlockSpec((N_HEADS, 1), lambda c: (0, 0)), row(LANE),
                  pl.BlockSpec((N_HEADS, 1), lambda c: (0, 0)), row(D_SSM)],
        out_specs=(pl.BlockSpec((ln, D_SSM), lambda c: (c, 0)),
                   pl.BlockSpec((1, N_PAIRS, LANE, D_STATE), lambda c: (c, 0, 0, 0))),
        scratch_shapes=[pltpu.VMEM((N_PAIRS, LANE, D_STATE), F32)],
        compiler_params=_cparams(("arbitrary",)),
    )(act, dtraw, dtt, bias, biast, arow, acol, dskip)


def _ssd_bwd(act, dtraw, dtt, bias, biast, arow, acol, dskip, states, dy):
    s = act.shape[0]
    ln = CHUNK
    nc = s // ln

    def body(act_ref, dtraw_ref, dtt_ref, bias_ref, biast_ref, arow_ref, acol_ref, dsk_ref, st_ref, dy_ref,
             dact_ref, dldc_ref, dldr_ref, ddt_ref, dd_ref, dstate_ref):
        @pl.when(pl.program_id(0) == 0)
        def _():
            dstate_ref[...] = jnp.zeros_like(dstate_ref)
            dd_ref[...] = jnp.zeros_like(dd_ref)

        dt, ac, act_t, lower = _ssd_decays(dtraw_ref, bias_ref, dtt_ref, biast_ref, arow_ref, acol_ref)
        lane = lax.broadcasted_iota(jnp.int32, (1, LANE), 1)
        m0 = lane < HEAD_DIM
        halves = (m0, jnp.logical_not(m0))
        is_last = lax.broadcasted_iota(jnp.int32, (ln, 1), 0) == ln - 1
        sub = lax.broadcasted_iota(jnp.int32, (N_HEADS, 1), 0)
        earlier_eq = jnp.logical_not(lower) | (lax.broadcasted_iota(jnp.int32, (ln, ln), 0)
                                               == lax.broadcasted_iota(jnp.int32, (ln, ln), 1))
        dac_col = jnp.zeros((ln, LANE), F32)
        dac_row = jnp.zeros((N_HEADS, ln), F32)
        ddt_col = jnp.zeros((ln, LANE), F32)

        def half_sum(v, hh):
            return jnp.sum(jnp.where(halves[hh], v, 0.0), axis=1, keepdims=True)

        for g in range(N_GROUPS):
            b_lo, c_lo = D_SSM + g * D_STATE, D_SSM + (N_GROUPS + g) * D_STATE
            bg32 = act_ref[:, b_lo:b_lo + D_STATE]
            cg32 = act_ref[:, c_lo:c_lo + D_STATE]
            bg, cg = bg32.astype(BF16), cg32.astype(BF16)
            cg_t = cg32.T.astype(BF16)
            cb_t = lax.dot_general(bg, cg, NT, preferred_element_type=F32)
            dcb_t = jnp.zeros((ln, ln), F32)
            dbg = jnp.zeros((ln, D_STATE), F32)
            dcg = jnp.zeros((ln, D_STATE), F32)
            for p in range(g * 4, g * 4 + 4):
                h0 = 2 * p
                cols = slice(p * LANE, (p + 1) * LANE)
                xp = act_ref[:, cols]
                dyp = dy_ref[:, cols]
                dtp = _pair_cols(m0, dt, h0)
                acp = _pair_cols(m0, ac, h0)
                last = acp[ln - 1:ln, :]
                xdt = xp * dtp
                eac = jnp.exp(acp)
                dte = jnp.exp(last - acp)
                dec = jnp.exp(last)
                prev = st_ref[0, p]
                prev_b = prev.astype(BF16)
                ds = dstate_ref[p]
                ds_b = ds.astype(BF16)

                dd_ref[:, cols] += jnp.sum(dyp * xp, axis=0, keepdims=True)
                dx = dyp * dsk_ref[:, cols]
                zoff = jnp.dot(cg, prev_b, preferred_element_type=F32)
                dz_b = (dyp * eac).astype(BF16)
                dcg += lax.dot_general(dz_b, prev_b, NT, preferred_element_type=F32)
                dprev = jnp.dot(cg_t, dz_b, preferred_element_type=F32) + ds * dec
                wmat = jnp.dot(bg, ds_b, preferred_element_type=F32)
                xdte_b = (xdt * dte).astype(BF16)
                dbg += lax.dot_general(xdte_b, ds_b, NT, preferred_element_type=F32)
                dxdt = dte * wmat
                t_dte = xdt * wmat * dte
                t_ac = dyp * zoff * eac - t_dte
                at_last = jnp.sum(ds * prev, axis=0, keepdims=True) * dec + jnp.sum(t_dte, axis=0, keepdims=True)
                for hh in range(2):
                    h = h0 + hh
                    here = lane == h
                    dm_t = jnp.exp(jnp.where(earlier_eq, act_t[h:h + 1, :] - ac[:, h:h + 1], -jnp.inf))
                    mm_t = cb_t * dm_t
                    dyh = jnp.where(halves[hh], dyp, 0.0).astype(BF16)
                    xdth = jnp.where(halves[hh], xdt, 0.0).astype(BF16)
                    dmm_t = lax.dot_general(xdth, dyh, NT, preferred_element_type=F32)
                    dxdt += jnp.dot(mm_t.astype(BF16), dyh, preferred_element_type=F32)
                    gm_t = dmm_t * mm_t
                    dcb_t += dmm_t * dm_t
                    dac_col += jnp.where(here, half_sum(t_ac, hh) - jnp.sum(gm_t, axis=1, keepdims=True), 0.0)
                    dac_col += jnp.where(jnp.logical_and(is_last, here), half_sum(at_last, hh), 0.0)
                    dac_row += jnp.where(sub == h, jnp.sum(gm_t, axis=0, keepdims=True), 0.0)
                    ddt_col += jnp.where(here, half_sum(dxdt * xp, hh), 0.0)
                dact_ref[:, cols] = dx + dxdt * dtp
                dstate_ref[p] = dprev
            dcb_tb = dcb_t.astype(BF16)
            dact_ref[:, b_lo:b_lo + D_STATE] = dbg + jnp.dot(dcb_tb, cg, preferred_element_type=F32)
            dact_ref[:, c_lo:c_lo + D_STATE] = dcg + lax.dot_general(dcb_tb, bg, TN, preferred_element_type=F32)

        r = lax.broadcasted_iota(jnp.int32, (ln, ln), 0)
        c = lax.broadcasted_iota(jnp.int32, (ln, ln), 1)
        dldc_ref[...] = jnp.dot((r <= c).astype(F32), dac_col, preferred_element_type=F32, precision=HI)
        dldr_ref[...] = jnp.dot(dac_row, (r >= c).astype(F32), preferred_element_type=F32, precision=HI)
        ddt_ref[...] = ddt_col

    rev = lambda c: nc - 1 - c
    row = lambda w: pl.BlockSpec((1, w), lambda c: (0, 0))
    col16 = pl.BlockSpec((N_HEADS, 1), lambda c: (0, 0))
    chunk128 = pl.BlockSpec((ln, LANE), lambda c: (rev(c), 0))
    return pl.pallas_call(
        body, name="ssd_bwd",
        out_shape=(jax.ShapeDtypeStruct((s, D_XBC), F32), jax.ShapeDtypeStruct((s, LANE), F32),
                   jax.ShapeDtypeStruct((N_HEADS, s), F32), jax.ShapeDtypeStruct((s, LANE), F32),
                   jax.ShapeDtypeStruct((1, D_SSM), F32)),
        grid=(nc,),
        in_specs=[pl.BlockSpec((ln, D_XBC), lambda c: (rev(c), 0)), chunk128,
                  pl.BlockSpec((N_HEADS, ln), lambda c: (0, rev(c))), row(LANE), col16, row(LANE), col16,
                  row(D_SSM), pl.BlockSpec((1, N_PAIRS, LANE, D_STATE), lambda c: (rev(c), 0, 0, 0)),
                  pl.BlockSpec((ln, D_SSM), lambda c: (rev(c), 0))],
        out_specs=(pl.BlockSpec((ln, D_XBC), lambda c: (rev(c), 0)), chunk128,
                   pl.BlockSpec((N_HEADS, ln), lambda c: (0, rev(c))), chunk128, row(D_SSM)),
        scratch_shapes=[pltpu.VMEM((N_PAIRS, LANE, D_STATE), F32)],
        compiler_params=_cparams(("arbitrary",)),
    )(act, dtraw, dtt, bias, biast, arow, acol, dskip, states, dy)


def _dt_bwd(dtraw, bias, arow, dld_col, dld_row_t, ddt_col):
    s = dtraw.shape[0]
    tm = min(512, s)

    def body(raw_ref, bias_ref, a_ref, dc_ref, dr_ref, dd_ref, out_ref, sums_ref):
        @pl.when(pl.program_id(0) == 0)
        def _():
            sums_ref[...] = jnp.zeros_like(sums_ref)

        raw = raw_ref[...] + bias_ref[...]
        dld = dc_ref[...] + dr_ref[...]
        ddt = dld * a_ref[...] + dd_ref[...]
        draw = ddt * _sigmoid(raw)
        out_ref[...] = draw.astype(BF16)
        sums_ref[0:1, :] += jnp.sum(draw, axis=0, keepdims=True)
        sums_ref[1:2, :] += jnp.sum(dld * _softplus(raw), axis=0, keepdims=True)

    tile = pl.BlockSpec((tm, LANE), lambda i: (i, 0))
    row = pl.BlockSpec((1, LANE), lambda i: (0, 0))
    return pl.pallas_call(
        body, name="dt_bwd",
        out_shape=(jax.ShapeDtypeStruct((s, LANE), BF16), jax.ShapeDtypeStruct((2, LANE), F32)), grid=(s // tm,),
        in_specs=[tile, row, row, tile, tile, tile], out_specs=(tile, pl.BlockSpec((2, LANE), lambda i: (0, 0))),
        compiler_params=_cparams(("arbitrary",)),
    )(dtraw, bias, arow, dld_col, dld_row_t, ddt_col)


def _sum8(parts):
    nb, n = parts.shape

    def body(p_ref, o_ref):
        acc = p_ref[0:1, :]
        for b in range(1, nb):
            acc = acc + p_ref[b:b + 1, :]
        o_ref[...] = acc

    return pl.pallas_call(body, name="sum8", out_shape=jax.ShapeDtypeStruct((1, n), F32),
                          compiler_params=_cparams())(parts)


def _outer8(act_t, dmod):
    d, nb = act_t.shape
    n = dmod.shape[1]

    def body(a_ref, m_ref, o_ref):
        acc = a_ref[:, 0:1] * m_ref[0:1, :]
        for b in range(1, nb):
            acc = acc + a_ref[:, b:b + 1] * m_ref[b:b + 1, :]
        o_ref[...] = acc

    return pl.pallas_call(body, name="outer8", out_shape=jax.ShapeDtypeStruct((d, n), F32),
                          compiler_params=_cparams())(act_t, dmod)


def _pad_lanes(v, width=LANE):
    return jnp.pad(v, ((0, 0), (0, width - v.shape[1])))


def kernel(x, c, w_ada, b_ada, norm_in_gain, w_in, conv_w, conv_b, dt_bias, a_log, d_skip, sb_norm_gain, ssm_norm_gain, w_out, norm_f_gain, loss_target, m_w_ada, m_b_ada, m_norm_in_gain, m_w_in, m_conv_w, m_conv_b, m_dt_bias, m_a_log, m_d_skip, m_sb_norm_gain, m_ssm_norm_gain, m_w_out, m_norm_f_gain, v_w_ada, v_b_ada, v_norm_in_gain, v_w_in, v_conv_w, v_conv_b, v_dt_bias, v_a_log, v_d_skip, v_sb_norm_gain, v_ssm_norm_gain, v_w_out, v_norm_f_gain):
    ax, ay, ac_ = _coords()
    chip = 2 * ax + ay
    me = 2 * chip + ac_
    my_c = jnp.reshape(ac_, (1,)).astype(jnp.int32)
    x2d, tgt = x[0], loss_target[0]
    s = x2d.shape[0]
    ada_cols = w_ada.shape[2]
    cw_cols = conv_w.shape[2]
    in_cols = w_in.shape[2]
    out_rows = w_out.shape[1]

    small = jnp.concatenate([c, conv_w[0].reshape(1, CONV_K * cw_cols)], axis=1)
    small_all = _allgather8(small, "gather_cond")[:, 0, :]
    c_all = small_all[:, :D_MODEL]
    conv_w_full = (small_all[0::2, D_MODEL:].reshape(N_CHIPS, CONV_K, cw_cols)
                   .transpose(1, 0, 2).reshape(CONV_K, D_XBC))
    b_ada_shard = lax.dynamic_slice_in_dim(b_ada, chip * ada_cols, ada_cols, axis=1)
    mod_part, c_act_all = _ada_mod(c_all, w_ada[0], b_ada_shard)
    mod_all = _allgather8(mod_part.reshape(1, N_DEV * ada_cols), "gather_mod")[0::2, 0, :]
    mod_all = mod_all.reshape(N_CHIPS, N_DEV, ada_cols)
    mod = lax.dynamic_index_in_dim(mod_all, me, axis=1, keepdims=False).reshape(1, 3 * D_MODEL)
    shift, scale, gate = mod[:, :D_MODEL], mod[:, D_MODEL:2 * D_MODEL], mod[:, 2 * D_MODEL:]

    w_in_mine, w_out_mine = w_in[0].T.astype(BF16), w_out[0].astype(BF16)
    w_in_all, w_out_all = _gather_shards([w_in_mine, w_out_mine], "gather_weights")
    w_in_all = lax.dynamic_update_slice(w_in_all, w_in_mine[None], (chip, 0, 0))
    w_out_all = lax.dynamic_update_slice(w_out_all, w_out_mine[None], (chip, 0, 0))
    w_in_t = w_in_all.reshape(D_PROJ, D_MODEL)
    w_zs_t = w_in_t[ZS_LO:]
    w_dt_t = jnp.pad(w_in_t[DT_LO:ZS_LO], ((0, LANE - N_HEADS), (0, 0)))
    w_out_full = w_out_all.reshape(N_CHIPS * out_rows, D_MODEL)

    h = _rms_mod_fwd(x2d, norm_in_gain, scale, shift)
    proj = _matmul(h, w_in_t, BF16, "in_proj", "nt", 1024, 512, 1024, n_out=D_MAIN)
    proj_zs = _matmul(h, w_zs_t, BF16, "in_proj_zs", "nt", 1024, 512, 1024)
    dtraw = _matmul(h, w_dt_t, F32, "in_proj_dt", "nt", 1024, LANE, 1024)
    o_attn, lsum = _attn_fwd(proj)
    y_attn = _gated_norm_fwd(o_attn, proj, OFF_ZA, sb_norm_gain, False, "attn_gate_fwd")
    act = _conv_fwd(proj, conv_w_full, conv_b)
    a_neg = -jnp.exp(a_log)
    arow, acol = _pad_lanes(a_neg), a_neg.reshape(N_HEADS, 1)
    bias_row, bias_col = _pad_lanes(dt_bias), dt_bias.reshape(N_HEADS, 1)
    dtt = dtraw[:, :N_HEADS].T
    dskip_row = jnp.repeat(d_skip, HEAD_DIM, axis=1)
    ssd_args = (act, dtraw, dtt, bias_row, bias_col, arow, acol, dskip_row)
    y_ssd, states = _ssd_fwd(*ssd_args)
    y_ssm = _gated_norm_fwd(y_ssd, proj_zs, 0, ssm_norm_gain, True, "ssm_gate_fwd")
    mixed = _matmul_sum([(y_attn, D_ATTN, 0, w_out_full, 0), (y_ssm, D_SSM, 0, w_out_full, 1)], F32, "out_proj",
                        1024, 1024)

    dx2, dmixed, head_sums = _loss_head(x2d, mixed, gate, norm_f_gain.reshape(1, D_MODEL), tgt)
    out_all = N_CHIPS * out_rows
    g_w_out = _matmul_tn_rows(None, out_all, y_attn, dmixed, 0, "out_proj_dw_attn", 512)
    g_w_out = _matmul_tn_rows(g_w_out, out_all, y_ssm, dmixed, D_ATTN // 512, "out_proj_dw_ssm", 512)
    d_mix_in = _matmul(dmixed, w_out_full, F32, "out_proj_dx", "nt", 1024, 1024, 1024)
    d_o, dz_attn, g_sb = _gated_norm_bwd(d_mix_in, 0, o_attn, proj, OFF_ZA, sb_norm_gain, False, "attn_gate_bwd")
    d_y, dz_ssm, g_ssm = _gated_norm_bwd(d_mix_in, 1, y_ssd, proj_zs, 0, ssm_norm_gain, True, "ssm_gate_bwd")
    dq, dk, dv = _attn_bwd(proj, d_o, lsum)
    dact, dld_col, dld_row, ddt_col, dd_cols = _ssd_bwd(*ssd_args, states, d_y)
    dxbc, g_conv_w, g_conv_b = _conv_bwd(proj, conv_w_full, conv_b, dact)
    ddtraw, dt_sums = _dt_bwd(dtraw, bias_row, arow, dld_col, _pad_lanes(dld_row.T), ddt_col)
    g_in_t = None
    for piece, lo, label in [(dq, OFF_Q, "q"), (dk, OFF_K, "k"), (dv, OFF_V, "v"), (dz_attn, OFF_ZA, "za"),
                             (dxbc, OFF_XBC, "xbc")]:
        g_in_t = _matmul_tn_rows(g_in_t, D_PROJ, piece, h, lo // 512, "in_proj_dw_" + label, 512)
    g_in_t = _matmul_tn_rows(g_in_t, D_PROJ, ddtraw, h, DT_LO // LANE, "in_proj_dw_dt", LANE)
    g_zs_t = _matmul(dz_ssm, h, F32, "in_proj_dw_zs", "tn", 512, 1024, 4096)
    g_in_t = lax.dynamic_update_slice(g_in_t, g_zs_t, (ZS_LO, 0))
    dh_terms = [(dq, D_ATTN, 0, w_in_t, 0), (dk, D_ATTN, 0, w_in_t, 1), (dv, D_ATTN, 0, w_in_t, 2),
                (dz_attn, D_ATTN, 0, w_in_t, 3)]
    dh_terms += [(dxbc, 512, j, w_in_t, OFF_XBC // 512 + j) for j in range(D_XBC // 512)]
    dh_terms += [(dz_ssm, D_SSM, 0, w_zs_t, 0), (ddtraw, LANE, 0, w_dt_t, 0)]
    dh = _matmul_sum(dh_terms, F32, "in_proj_dx", 512, 512)
    grad_x, in_sums = _rms_mod_bwd(x2d, dh, dx2, norm_in_gain, scale)

    g_a_log = dt_sums[1:2, :N_HEADS] * a_neg
    g_d_skip = jnp.sum(dd_cols.reshape(N_HEADS, HEAD_DIM), axis=1).reshape(1, N_HEADS)
    dmod = jnp.concatenate([in_sums[0:1], in_sums[1:2], head_sums[2:3]], axis=1)
    loss_part = 0.5 / D_MODEL * jnp.sum(head_sums[0:1], axis=1, keepdims=True)
    pieces = [dmod, in_sums[2:3], g_conv_w.reshape(1, CONV_K * D_XBC), g_conv_b, _pad_lanes(dt_sums[0:1, :N_HEADS]),
              _pad_lanes(g_a_log), _pad_lanes(g_d_skip), g_sb, g_ssm, head_sums[1:2], _pad_lanes(loss_part)]
    widths = [p.shape[1] for p in pieces]
    parts_all = _allgather8(jnp.concatenate(pieces, axis=1), "gather_small_grads")[:, 0, :]
    total = _sum8(parts_all)
    offs = [0]
    for w_ in widths:
        offs.append(offs[-1] + w_)
    tot = [total[:, offs[i]:offs[i + 1]] for i in range(len(pieces))]
    g_b_ada, g_norm_in, g_conv_w_full = tot[0], tot[1], tot[2].reshape(CONV_K, D_XBC)
    g_conv_b_t, g_dt_bias, g_a_log_t, g_d_skip_t = tot[3], tot[4][:, :N_HEADS], tot[5][:, :N_HEADS], tot[6][:, :N_HEADS]
    g_sb_t, g_ssm_t, g_norm_f, loss = tot[7], tot[8], tot[9], tot[10][0, 0]
    g_conv_w_shard = lax.dynamic_slice_in_dim(g_conv_w_full, chip * cw_cols, cw_cols, axis=1)
    dmod_shard = lax.dynamic_slice_in_dim(parts_all[:, :3 * D_MODEL], chip * ada_cols, ada_cols, axis=1)
    g_w_ada = _outer8(c_act_all.T, dmod_shard)

    g_in_blocks = g_in_t.reshape(N_CHIPS, in_cols, D_MODEL)
    g_out_blocks = g_w_out.reshape(N_CHIPS, out_rows, D_MODEL)
    land_in, land_out = _send_to_sibling([g_in_blocks, g_out_blocks], "grads_to_sibling")
    chip_in = _add_my_half(g_in_blocks, land_in, my_c, "add_sibling_in")
    chip_out = _add_my_half(g_out_blocks, land_out, my_c, "add_sibling_out")
    slots_in, slots_out = _exchange_chips([chip_in, chip_out], "grads_between_chips")
    own = lambda blocks: lax.dynamic_slice_in_dim(blocks, chip, 1, axis=0)
    slots_in = lax.dynamic_update_slice(slots_in, own(chip_in), (chip, 0, 0))
    slots_out = lax.dynamic_update_slice(slots_out, own(chip_out), (chip, 0, 0))
    half_in, half_out = _sum_slots(slots_in, "sum_chips_in"), _sum_slots(slots_out, "sum_chips_out")
    their_in, their_out = _swap_with_sibling([half_in, half_out], "grads_swap_sibling")
    south = ac_ == 0
    g_w_in = jnp.concatenate([jnp.where(south, half_in, their_in).T, jnp.where(south, their_in, half_in).T], axis=0)
    g_w_out_shard = jnp.concatenate([jnp.where(south, half_out, their_out), jnp.where(south, their_out, half_out)],
                                    axis=1)

    d_w_ada, nm_w_ada, nv_w_ada = _adamw(w_ada[0], g_w_ada, m_w_ada[0], v_w_ada[0], "adamw_w_ada")
    d_w_in, nm_w_in, nv_w_in = _adamw(w_in[0], g_w_in, m_w_in[0], v_w_in[0], "adamw_w_in")
    d_w_out, nm_w_out, nv_w_out = _adamw(w_out[0], g_w_out_shard, m_w_out[0], v_w_out[0], "adamw_w_out")
    flat = lambda a: a.reshape(1, -1)
    small_w = [b_ada, norm_in_gain, conv_w[0], conv_b, dt_bias, a_log, d_skip, sb_norm_gain, ssm_norm_gain,
               norm_f_gain]
    small_m = [m_b_ada, m_norm_in_gain, m_conv_w[0], m_conv_b, m_dt_bias, m_a_log, m_d_skip, m_sb_norm_gain,
               m_ssm_norm_gain, m_norm_f_gain]
    small_v = [v_b_ada, v_norm_in_gain, v_conv_w[0], v_conv_b, v_dt_bias, v_a_log, v_d_skip, v_sb_norm_gain,
               v_ssm_norm_gain, v_norm_f_gain]
    small_g = [g_b_ada, g_norm_in, g_conv_w_shard, g_conv_b_t, g_dt_bias, g_a_log_t, g_d_skip_t, g_sb_t, g_ssm_t,
               g_norm_f]
    cat = lambda arrs: jnp.concatenate([flat(a) for a in arrs], axis=1)
    d_small, nm_small, nv_small = _adamw(cat(small_w), cat(small_g), cat(small_m), cat(small_v), "adamw_small")
    sizes = [a.size for a in small_w]
    soffs = [0]
    for n_ in sizes:
        soffs.append(soffs[-1] + n_)

    def split(packed):
        return [packed[0, soffs[i]:soffs[i + 1]].reshape(small_w[i].shape) for i in range(len(small_w))]

    def ordered(big_ada, big_in, big_out, smalls):
        (s_b_ada, s_norm_in, s_conv_w, s_conv_b, s_dt_bias, s_a_log, s_d_skip, s_sb, s_ssm, s_norm_f) = smalls
        return [big_ada[None], s_b_ada, s_norm_in, big_in[None], s_conv_w[None], s_conv_b, s_dt_bias, s_a_log,
                s_d_skip, s_sb, s_ssm, big_out[None], s_norm_f]

    grads = ordered(g_w_ada, g_w_in, g_w_out_shard,
                    [g.reshape(w_.shape) for g, w_ in zip(small_g, small_w)])
    deltas = ordered(d_w_ada, d_w_in, d_w_out, split(d_small))
    new_m = ordered(nm_w_ada, nm_w_in, nm_w_out, split(nm_small))
    new_v = ordered(nv_w_ada, nv_w_in, nv_w_out, split(nv_small))
    return (loss, grad_x[None], *grads, *deltas, *new_m, *new_v)
```

```python
import functools

import jax
import jax.numpy as jnp
from jax import lax
from jax.experimental import pallas as pl
from jax.experimental.pallas import tpu as pltpu

F32, BF16 = jnp.float32, jnp.bfloat16
MESH = pl.DeviceIdType.MESH
HI = lax.Precision.HIGHEST
NN = (((1,), (0,)), ((), ()))
NT = (((1,), (1,)), ((), ()))
TN = (((0,), (0,)), ((), ()))

D_MODEL = 1024
D_ATTN = 1024
D_SSM = 1024
HEAD_DIM = 64
N_HEADS = 16
N_PAIRS = 8
N_GROUPS = 2
D_STATE = 128
D_XBC = 1536
D_PROJ = 6672
D_MAIN = 5632
CONV_K = 4
CHUNK = 128
LANE = 128
N_CHIPS = 4
N_DEV = 8
NORM_EPS = 1e-6
ATTN_SCALE = HEAD_DIM ** -0.5
ATTN_TQ = 512
ATTN_TK = 256
LOG_ZERO = -110.0
ADAM_LR, ADAM_B1, ADAM_B2, ADAM_EPS, ADAM_WD, ADAM_STEP = 0.001, 0.9, 0.999, 1e-08, 0.01, 10
VMEM_LIMIT = 56 * 1024 * 1024

OFF_Q, OFF_K, OFF_V, OFF_ZA, OFF_XBC = 0, 1024, 2048, 3072, 4096
DT_LO = D_MAIN
ZS_LO = DT_LO + N_HEADS


def _cparams(sem=None):
    return pltpu.CompilerParams(dimension_semantics=sem, vmem_limit_bytes=VMEM_LIMIT)


def _sigmoid(x):
    return 1.0 / (1.0 + jnp.exp(-x))


def _softplus(x):
    return jnp.maximum(x, 0.0) + jnp.log(1.0 + jnp.exp(-jnp.abs(x)))


def _coords():
    return lax.axis_index("x"), lax.axis_index("y"), lax.axis_index("c")


def _allgather8(v, name):
    n = v.shape[-1]

    def body(v_ref, out_ref, send_sems, recv_sems, local_sem):
        x, y, c = _coords()
        me = 4 * x + 2 * y + c
        mine = pltpu.make_async_copy(v_ref, out_ref.at[me], local_sem)
        mine.start()
        sends, recvs = [], []
        for j in range(1, N_DEV):
            px = 1 - x if (j >> 2) & 1 else x
            py = 1 - y if (j >> 1) & 1 else y
            pc = 1 - c if j & 1 else c
            peer = (px, py, pc)
            sends.append(pltpu.make_async_remote_copy(
                src_ref=v_ref, dst_ref=out_ref.at[me], send_sem=send_sems.at[j - 1],
                recv_sem=recv_sems.at[j - 1], device_id=peer, device_id_type=MESH))
            recvs.append(pltpu.make_async_remote_copy(
                src_ref=v_ref, dst_ref=out_ref.at[4 * px + 2 * py + pc], send_sem=send_sems.at[j - 1],
                recv_sem=recv_sems.at[j - 1], device_id=peer, device_id_type=MESH))
        for s in sends:
            s.start()
        for r in recvs:
            r.wait_recv()
        for s in sends:
            s.wait_send()
        mine.wait()

    vm = pl.BlockSpec(memory_space=pltpu.VMEM)
    return pl.pallas_call(
        body, name=name, out_shape=jax.ShapeDtypeStruct((N_DEV, 1, n), F32),
        in_specs=[vm], out_specs=vm,
        scratch_shapes=[pltpu.SemaphoreType.DMA((N_DEV - 1,)), pltpu.SemaphoreType.DMA((N_DEV - 1,)),
                        pltpu.SemaphoreType.DMA(())],
    )(v)


def _other_chips(x, y):
    chips = [(1 - x, y), (x, 1 - y), (1 - x, 1 - y)]
    return chips, [2 * cx + cy for cx, cy in chips]


def _half_cols(width, which):
    half = width // 2
    return pl.ds(pl.multiple_of(which * half, half), half)


def _gather_shards(arrs, name):
    n = len(arrs)

    def body(*refs):
        ins, outs = refs[:n], refs[n:2 * n]
        send_sems, recv_sems = refs[2 * n:]
        x, y, c = _coords()
        k = 2 * x + y
        chips, chip_idx = _other_chips(x, y)
        sibling = (x, y, 1 - c)
        sends = []
        for a in range(n):
            mine = _half_cols(arrs[a].shape[-1], c)
            for j in range(3):
                cp = pltpu.make_async_remote_copy(
                    src_ref=ins[a].at[:, mine], dst_ref=outs[a].at[k, :, mine], send_sem=send_sems.at[6 * a + j],
                    recv_sem=recv_sems.at[6 * a + j], device_id=(*chips[j], c), device_id_type=MESH)
                cp.start()
                sends.append(cp)
        for a in range(n):
            mine = _half_cols(arrs[a].shape[-1], c)
            for j in range(3):
                landed = outs[a].at[chip_idx[j], :, mine]
                pltpu.make_async_remote_copy(
                    src_ref=landed, dst_ref=landed, send_sem=send_sems.at[6 * a + j],
                    recv_sem=recv_sems.at[6 * a + j], device_id=(*chips[j], c), device_id_type=MESH).wait_recv()
                fwd = pltpu.make_async_remote_copy(
                    src_ref=landed, dst_ref=landed, send_sem=send_sems.at[6 * a + 3 + j],
                    recv_sem=recv_sems.at[6 * a + 3 + j], device_id=sibling, device_id_type=MESH)
                fwd.start()
                sends.append(fwd)
        for a in range(n):
            theirs = _half_cols(arrs[a].shape[-1], 1 - c)
            for j in range(3):
                landed = outs[a].at[chip_idx[j], :, theirs]
                pltpu.make_async_remote_copy(
                    src_ref=landed, dst_ref=landed, send_sem=send_sems.at[6 * a + 3 + j],
                    recv_sem=recv_sems.at[6 * a + 3 + j], device_id=sibling, device_id_type=MESH).wait_recv()
        for cp in sends:
            cp.wait_send()

    hbm = pl.BlockSpec(memory_space=pl.ANY)
    return pl.pallas_call(
        body, name=name,
        out_shape=tuple(jax.ShapeDtypeStruct((N_CHIPS,) + a.shape, a.dtype) for a in arrs),
        in_specs=[hbm] * n, out_specs=tuple([hbm] * n),
        scratch_shapes=[pltpu.SemaphoreType.DMA((6 * n,)), pltpu.SemaphoreType.DMA((6 * n,))],
    )(*arrs)


def _send_to_sibling(arrs, name):
    n = len(arrs)

    def body(*refs):
        ins, outs = refs[:n], refs[n:2 * n]
        send_sems, recv_sems = refs[2 * n:]
        x, y, c = _coords()
        cps = []
        for a in range(n):
            cp = pltpu.make_async_remote_copy(
                src_ref=ins[a].at[:, :, _half_cols(arrs[a].shape[-1], 1 - c)], dst_ref=outs[a],
                send_sem=send_sems.at[a], recv_sem=recv_sems.at[a], device_id=(x, y, 1 - c), device_id_type=MESH)
            cp.start()
            cps.append(cp)
        for cp in cps:
            cp.wait()

    hbm = pl.BlockSpec(memory_space=pl.ANY)
    return pl.pallas_call(
        body, name=name,
        out_shape=tuple(jax.ShapeDtypeStruct(a.shape[:-1] + (a.shape[-1] // 2,), a.dtype) for a in arrs),
        in_specs=[hbm] * n, out_specs=tuple([hbm] * n),
        scratch_shapes=[pltpu.SemaphoreType.DMA((n,)), pltpu.SemaphoreType.DMA((n,))],
    )(*arrs)


def _swap_with_sibling(arrs, name):
    n = len(arrs)

    def body(*refs):
        ins, outs = refs[:n], refs[n:2 * n]
        send_sems, recv_sems = refs[2 * n:]
        x, y, c = _coords()
        cps = []
        for a in range(n):
            cp = pltpu.make_async_remote_copy(
                src_ref=ins[a], dst_ref=outs[a], send_sem=send_sems.at[a], recv_sem=recv_sems.at[a],
                device_id=(x, y, 1 - c), device_id_type=MESH)
            cp.start()
            cps.append(cp)
        for cp in cps:
            cp.wait()

    hbm = pl.BlockSpec(memory_space=pl.ANY)
    return pl.pallas_call(
        body, name=name,
        out_shape=tuple(jax.ShapeDtypeStruct(a.shape, a.dtype) for a in arrs),
        in_specs=[hbm] * n, out_specs=tuple([hbm] * n),
        scratch_shapes=[pltpu.SemaphoreType.DMA((n,)), pltpu.SemaphoreType.DMA((n,))],
    )(*arrs)


def _row_tile(rows, cols, n_arrays):
    budget = VMEM_LIMIT // 2
    t = rows
    while t % 16 == 0 and t * cols * 4 * n_arrays * 2 > budget:
        t //= 2
    return t


def _add_my_half(g, landed, my_c, name):
    nb, r, cdim = g.shape
    half = cdim // 2
    tr = _row_tile(r, half, 3)

    def body(c_ref, g_ref, l_ref, o_ref):
        o_ref[...] = (g_ref[...] + l_ref[...]).astype(BF16)

    spec = pl.BlockSpec((None, tr, half), lambda b, i, c_ref: (b, i, 0))
    return pl.pallas_call(
        body, name=name, out_shape=jax.ShapeDtypeStruct((nb, r, half), BF16),
        grid_spec=pltpu.PrefetchScalarGridSpec(
            num_scalar_prefetch=1, grid=(nb, r // tr),
            in_specs=[pl.BlockSpec((None, tr, half), lambda b, i, c_ref: (b, i, c_ref[0])), spec],
            out_specs=spec),
        compiler_params=_cparams(("parallel", "parallel")),
    )(my_c, g, landed)


def _sum_slots(a, name):
    nb, r, cdim = a.shape
    tr = _row_tile(r, cdim, 4)

    def body(a_ref, o_ref):
        o_ref[...] = ((a_ref[0].astype(F32) + a_ref[1].astype(F32)) + a_ref[2].astype(F32)) + a_ref[3].astype(F32)

    return pl.pallas_call(
        body, name=name, out_shape=jax.ShapeDtypeStruct((r, cdim), F32), grid=(r // tr,),
        in_specs=[pl.BlockSpec((nb, tr, cdim), lambda i: (0, i, 0))],
        out_specs=pl.BlockSpec((tr, cdim), lambda i: (i, 0)),
        compiler_params=_cparams(("parallel",)),
    )(a)


def _adamw(w, g, m, v, name):
    r, cdim = w.shape
    tr = _row_tile(r, cdim, 7)
    tc = cdim
    if tr == r and r > 8:
        while tc % (2 * LANE) == 0 and r * tc * 4 * 7 * 2 > VMEM_LIMIT // 2:
            tc //= 2

    def body(w_ref, g_ref, m_ref, v_ref, d_ref, nm_ref, nv_ref):
        gv = g_ref[...]
        nm = ADAM_B1 * m_ref[...] + (1.0 - ADAM_B1) * gv
        nv = ADAM_B2 * v_ref[...] + (1.0 - ADAM_B2) * (gv * gv)
        m_hat = nm / (1.0 - ADAM_B1 ** ADAM_STEP)
        v_hat = nv / (1.0 - ADAM_B2 ** ADAM_STEP)
        d_ref[...] = -ADAM_LR * (m_hat / (jnp.sqrt(v_hat) + ADAM_EPS) + ADAM_WD * w_ref[...])
        nm_ref[...] = nm
        nv_ref[...] = nv

    spec = pl.BlockSpec((tr, tc), lambda i, j: (i, j))
    shp = jax.ShapeDtypeStruct((r, cdim), F32)
    return pl.pallas_call(
        body, name=name, out_shape=(shp, shp, shp), grid=(r // tr, cdim // tc),
        in_specs=[spec] * 4, out_specs=(spec, spec, spec),
        compiler_params=_cparams(("parallel", "parallel")),
    )(w, g, m, v)


def _matmul(a, b, out_dtype, name, mode, tm, tn, tk, extra=None, n_out=None):
    dims = {"nn": NN, "nt": NT, "tn": TN}[mode]
    if mode == "tn":
        kdim, m = a.shape
    else:
        m, kdim = a.shape
    n = n_out if n_out is not None else (b.shape[0] if mode == "nt" else b.shape[1])
    tm, tn, tk = min(tm, m), min(tn, n), min(tk, kdim)
    nk = kdim // tk
    a_spec = (pl.BlockSpec((tk, tm), lambda i, j, k: (k, i)) if mode == "tn"
              else pl.BlockSpec((tm, tk), lambda i, j, k: (i, k)))
    b_spec = (pl.BlockSpec((tn, tk), lambda i, j, k: (j, k)) if mode == "nt"
              else pl.BlockSpec((tk, tn), lambda i, j, k: (k, j)))
    in_specs, operands = [a_spec, b_spec], [a, b]
    if extra is not None:
        a2, b2 = extra
        k2 = a2.shape[0] if mode == "tn" else a2.shape[1]
        in_specs.append(pl.BlockSpec((k2, tm), lambda i, j, k: (0, i)) if mode == "tn"
                        else pl.BlockSpec((tm, k2), lambda i, j, k: (i, 0)))
        in_specs.append(pl.BlockSpec((tn, k2), lambda i, j, k: (j, 0)) if mode == "nt"
                        else pl.BlockSpec((k2, tn), lambda i, j, k: (0, j)))
        operands += [a2, b2]

    def body_one_block(*refs):
        acc = lax.dot_general(refs[0][...], refs[1][...], dims, preferred_element_type=F32)
        if extra is not None:
            acc += lax.dot_general(refs[2][...], refs[3][...], dims, preferred_element_type=F32)
        refs[-1][...] = acc.astype(out_dtype)

    if nk == 1:
        return pl.pallas_call(
            body_one_block, name=name, out_shape=jax.ShapeDtypeStruct((m, n), out_dtype), grid=(m // tm, n // tn, 1),
            in_specs=in_specs, out_specs=pl.BlockSpec((tm, tn), lambda i, j, k: (i, j)),
            compiler_params=_cparams(("parallel", "parallel", "arbitrary")),
        )(*operands)

    def body(*refs):
        if extra is not None:
            a_ref, b_ref, a2_ref, b2_ref, o_ref, acc_ref = refs
        else:
            a_ref, b_ref, o_ref, acc_ref = refs
        k = pl.program_id(2)

        @pl.when(k == 0)
        def _():
            if extra is not None:
                acc_ref[...] = lax.dot_general(a2_ref[...], b2_ref[...], dims, preferred_element_type=F32)
            else:
                acc_ref[...] = jnp.zeros_like(acc_ref)

        acc_ref[...] += lax.dot_general(a_ref[...], b_ref[...], dims, preferred_element_type=F32)

        @pl.when(k == nk - 1)
        def _():
            o_ref[...] = acc_ref[...].astype(out_dtype)

    return pl.pallas_call(
        body, name=name, out_shape=jax.ShapeDtypeStruct((m, n), out_dtype), grid=(m // tm, n // tn, nk),
        in_specs=in_specs, out_specs=pl.BlockSpec((tm, tn), lambda i, j, k: (i, j)),
        scratch_shapes=[pltpu.VMEM((tm, tn), F32)],
        compiler_params=_cparams(("parallel", "parallel", "arbitrary")),
    )(*operands)


def _matmul_sum(terms, out_dtype, name, tm, tn, exchange=None):
    m, n = terms[0][0].shape[0], terms[0][3].shape[1]
    tm, tn = min(tm, m), min(tn, n)
    gm, gn = m // tm, n // tn
    nt = len(terms)
    in_specs, operands = [], []
    for a, ka, ia, b, ib in terms:
        in_specs.append(pl.BlockSpec((tm, ka), functools.partial(lambda i, j, ia: (i, ia), ia=ia)))
        in_specs.append(pl.BlockSpec((ka, tn), functools.partial(lambda i, j, ib: (ib, j), ib=ib)))
        operands += [a, b]
    sent = [] if exchange is None else list(exchange)
    ns = len(sent)
    hbm = pl.BlockSpec(memory_space=pl.ANY)

    def body(*refs):
        o_ref = refs[2 * nt + ns]
        if ns:
            ins, outs = refs[2 * nt:2 * nt + ns], refs[2 * nt + ns + 1:2 * nt + 2 * ns + 1]
            send_sems, recv_sems = refs[2 * nt + 2 * ns + 1:]
            x, y, c = _coords()
            k = 2 * x + y
            chips, chip_idx = _other_chips(x, y)
            step = pl.program_id(0) * gn + pl.program_id(1)

            def copies(a, j, landed):
                dst = outs[a].at[chip_idx[j]] if landed else outs[a].at[k]
                src = dst if landed else ins[a].at[chip_idx[j]]
                return pltpu.make_async_remote_copy(
                    src_ref=src, dst_ref=dst, send_sem=send_sems.at[3 * a + j], recv_sem=recv_sems.at[3 * a + j],
                    device_id=(*chips[j], c), device_id_type=MESH)

            @pl.when(step == 0)
            def _():
                for a in range(ns):
                    for j in range(3):
                        copies(a, j, False).start()

        acc = jnp.dot(refs[0][...], refs[1][...], preferred_element_type=F32)
        for t in range(1, nt):
            acc += jnp.dot(refs[2 * t][...], refs[2 * t + 1][...], preferred_element_type=F32)
        o_ref[...] = acc.astype(out_dtype)

        if ns:
            @pl.when(step == gm * gn - 1)
            def _():
                for a in range(ns):
                    for j in range(3):
                        copies(a, j, True).wait_recv()
                for a in range(ns):
                    for j in range(3):
                        copies(a, j, False).wait_send()

    main = jax.ShapeDtypeStruct((m, n), out_dtype)
    tile = pl.BlockSpec((tm, tn), lambda i, j: (i, j))
    if not ns:
        return pl.pallas_call(
            body, name=name, out_shape=main, grid=(gm, gn), in_specs=in_specs, out_specs=tile,
            compiler_params=_cparams(("parallel", "parallel")),
        )(*operands)
    return pl.pallas_call(
        body, name=name, out_shape=(main, *[jax.ShapeDtypeStruct(a.shape, a.dtype) for a in sent]), grid=(gm, gn),
        in_specs=in_specs + [hbm] * ns, out_specs=(tile, *[hbm] * ns),
        scratch_shapes=[pltpu.SemaphoreType.DMA((3 * ns,)), pltpu.SemaphoreType.DMA((3 * ns,))],
        compiler_params=_cparams(("arbitrary", "arbitrary")),
    )(*operands, *sent)


def _matmul_tn_rows(buf, rows, a, b, row_blk, name, tm):
    kdim, m = a.shape
    n = b.shape[1]
    tm = min(tm, m)

    def body(*refs):
        a_ref, b_ref, o_ref = refs[-3:]
        o_ref[...] = lax.dot_general(a_ref[...], b_ref[...], TN, preferred_element_type=F32)

    in_specs = [pl.BlockSpec((kdim, tm), lambda i: (0, i)), pl.BlockSpec((kdim, n), lambda i: (0, 0))]
    operands = [a, b]
    if buf is not None:
        in_specs.insert(0, pl.BlockSpec(memory_space=pl.ANY))
        operands.insert(0, buf)
    return pl.pallas_call(
        body, name=name, out_shape=jax.ShapeDtypeStruct((rows, n), F32), grid=(m // tm,),
        in_specs=in_specs, out_specs=pl.BlockSpec((tm, n), lambda i: (row_blk + i, 0)),
        input_output_aliases={} if buf is None else {0: 0},
        compiler_params=_cparams(("parallel",)),
    )(*operands)


def _ada_mod(c_all, w_shard, b_shard):
    nb, d = c_all.shape
    cols = w_shard.shape[1]

    def body(c_ref, w_ref, b_ref, mod_ref, act_ref):
        cv = c_ref[...]
        act = cv * _sigmoid(cv)
        act_ref[...] = act
        mod_ref[...] = jnp.dot(act, w_ref[...], preferred_element_type=F32, precision=HI) + b_ref[...]

    return pl.pallas_call(
        body, name="ada_mod",
        out_shape=(jax.ShapeDtypeStruct((nb, cols), F32), jax.ShapeDtypeStruct((nb, d), F32)),
        compiler_params=_cparams(),
    )(c_all, w_shard, b_shard)


def _rms_mod_fwd(x, gain, scale, shift):
    s, d = x.shape
    tm = min(512, s)

    def body(x_ref, g_ref, sc_ref, sh_ref, h_ref):
        xv = x_ref[...]
        r = lax.rsqrt(jnp.mean(xv * xv, axis=-1, keepdims=True) + NORM_EPS)
        h_ref[...] = (xv * r * g_ref[...] * (1.0 + sc_ref[...]) + sh_ref[...]).astype(BF16)

    row = pl.BlockSpec((1, d), lambda i: (0, 0))
    tile = pl.BlockSpec((tm, d), lambda i: (i, 0))
    return pl.pallas_call(
        body, name="rms_mod_fwd", out_shape=jax.ShapeDtypeStruct((s, d), BF16), grid=(s // tm,),
        in_specs=[tile, row, row, row], out_specs=tile, compiler_params=_cparams(("parallel",)),
    )(x, gain, scale, shift)


def _rms_mod_bwd(x, dh, dres, gain, scale):
    s, d = x.shape
    tm = min(512, s)

    def body(x_ref, dh_ref, dres_ref, g_ref, sc_ref, dx_ref, sums_ref):
        @pl.when(pl.program_id(0) == 0)
        def _():
            sums_ref[...] = jnp.zeros_like(sums_ref)

        xv, dhv = x_ref[...], dh_ref[...]
        r = lax.rsqrt(jnp.mean(xv * xv, axis=-1, keepdims=True) + NORM_EPS)
        nrm = xv * r
        g, one_sc = g_ref[...], 1.0 + sc_ref[...]
        dn = dhv * g * one_sc
        dx_ref[...] = r * (dn - nrm * jnp.mean(dn * nrm, axis=-1, keepdims=True)) + dres_ref[...]
        dhn = dhv * nrm
        sums_ref[0:1, :] += jnp.sum(dhv, axis=0, keepdims=True)
        sums_ref[1:2, :] += jnp.sum(dhn * g, axis=0, keepdims=True)
        sums_ref[2:3, :] += jnp.sum(dhn * one_sc, axis=0, keepdims=True)

    row = pl.BlockSpec((1, d), lambda i: (0, 0))
    tile = pl.BlockSpec((tm, d), lambda i: (i, 0))
    return pl.pallas_call(
        body, name="rms_mod_bwd",
        out_shape=(jax.ShapeDtypeStruct((s, d), F32), jax.ShapeDtypeStruct((3, d), F32)), grid=(s // tm,),
        in_specs=[tile, tile, tile, row, row], out_specs=(tile, pl.BlockSpec((3, d), lambda i: (0, 0))),
        compiler_params=_cparams(("arbitrary",)),
    )(x, dh, dres, gain, scale)


def _loss_head(x, mixed, gate, gain_f, target):
    s, d = x.shape
    tm = min(512, s)

    def body(x_ref, mx_ref, gt_ref, gf_ref, t_ref, dx2_ref, dmx_ref, sums_ref):
        @pl.when(pl.program_id(0) == 0)
        def _():
            sums_ref[...] = jnp.zeros_like(sums_ref)

        mx, gt, gf = mx_ref[...], gt_ref[...], gf_ref[...]
        x2 = x_ref[...] + gt * mx
        r = lax.rsqrt(jnp.mean(x2 * x2, axis=-1, keepdims=True) + NORM_EPS)
        nrm = x2 * r
        err = nrm * gf - t_ref[...]
        dyf = err * (1.0 / d)
        dn = dyf * gf
        dx2 = r * (dn - nrm * jnp.mean(dn * nrm, axis=-1, keepdims=True))
        dx2_ref[...] = dx2
        dmx_ref[...] = (dx2 * gt).astype(BF16)
        sums_ref[0:1, :] += jnp.sum(err * err, axis=0, keepdims=True)
        sums_ref[1:2, :] += jnp.sum(dyf * nrm, axis=0, keepdims=True)
        sums_ref[2:3, :] += jnp.sum(dx2 * mx, axis=0, keepdims=True)

    row = pl.BlockSpec((1, d), lambda i: (0, 0))
    tile = pl.BlockSpec((tm, d), lambda i: (i, 0))
    return pl.pallas_call(
        body, name="loss_head",
        out_shape=(jax.ShapeDtypeStruct((s, d), F32), jax.ShapeDtypeStruct((s, d), BF16),
                   jax.ShapeDtypeStruct((3, d), F32)),
        grid=(s // tm,), in_specs=[tile, tile, row, row, tile],
        out_specs=(tile, tile, pl.BlockSpec((3, d), lambda i: (0, 0))),
        compiler_params=_cparams(("arbitrary",)),
    )(x, mixed, gate, gain_f, target)


def _silu_grad(z, sg):
    return sg * (1.0 + z * (1.0 - sg))


def _gated_norm_fwd(o, proj, z_off, gain, gate_inside, name):
    s, d = o.shape
    tm = min(512, s)
    zb = z_off // d

    def body(o_ref, z_ref, g_ref, y_ref):
        z = z_ref[...].astype(F32)
        sz = z * _sigmoid(z)
        u = o_ref[...] * sz if gate_inside else o_ref[...]
        r = lax.rsqrt(jnp.mean(u * u, axis=-1, keepdims=True) + NORM_EPS)
        y = u * r * g_ref[...]
        y_ref[...] = (y if gate_inside else y * sz).astype(BF16)

    tile = pl.BlockSpec((tm, d), lambda i: (i, 0))
    return pl.pallas_call(
        body, name=name, out_shape=jax.ShapeDtypeStruct((s, d), BF16), grid=(s // tm,),
        in_specs=[tile, pl.BlockSpec((tm, d), lambda i: (i, zb)), pl.BlockSpec((1, d), lambda i: (0, 0))],
        out_specs=tile, compiler_params=_cparams(("parallel",)),
    )(o, proj, gain)


def _gated_norm_bwd(dy_all, dy_blk, o, proj, z_off, gain, gate_inside, name):
    s, d = o.shape
    tm = min(512, s)
    zb = z_off // d

    def body(dy_ref, o_ref, z_ref, g_ref, do_ref, dz_ref, dg_ref):
        @pl.when(pl.program_id(0) == 0)
        def _():
            dg_ref[...] = jnp.zeros_like(dg_ref)

        z = z_ref[...].astype(F32)
        sg = _sigmoid(z)
        sz = z * sg
        ov, dy, g = o_ref[...], dy_ref[...], g_ref[...]
        u = ov * sz if gate_inside else ov
        r = lax.rsqrt(jnp.mean(u * u, axis=-1, keepdims=True) + NORM_EPS)
        nrm = u * r
        if gate_inside:
            dg_ref[...] += jnp.sum(dy * nrm, axis=0, keepdims=True)
            dn = dy * g
        else:
            dg_ref[...] += jnp.sum(dy * nrm * sz, axis=0, keepdims=True)
            dn = dy * g * sz
        du = r * (dn - nrm * jnp.mean(dn * nrm, axis=-1, keepdims=True))
        if gate_inside:
            do_ref[...] = du * sz
            dz_ref[...] = (du * ov * _silu_grad(z, sg)).astype(BF16)
        else:
            do_ref[...] = du
            dz_ref[...] = (dy * nrm * g * _silu_grad(z, sg)).astype(BF16)

    tile = pl.BlockSpec((tm, d), lambda i: (i, 0))
    row = pl.BlockSpec((1, d), lambda i: (0, 0))
    return pl.pallas_call(
        body, name=name,
        out_shape=(jax.ShapeDtypeStruct((s, d), F32), jax.ShapeDtypeStruct((s, d), BF16),
                   jax.ShapeDtypeStruct((1, d), F32)),
        grid=(s // tm,),
        in_specs=[pl.BlockSpec((tm, d), lambda i: (i, dy_blk)), tile, pl.BlockSpec((tm, d), lambda i: (i, zb)), row],
        out_specs=(tile, tile, row), compiler_params=_cparams(("arbitrary",)),
    )(dy_all, o, proj, gain)


def _sb_logits(qh, kb):
    z = lax.dot_general(qh, kb, NT, preferred_element_type=F32)
    neg_abs = lax.bitcast_convert_type(lax.bitcast_convert_type(z, jnp.uint32) | jnp.uint32(0x80000000), F32)
    lb = jnp.minimum(z, 0.0) - jnp.log(1.0 + jnp.exp(neg_abs))
    return lb, lb - z


def _attn_consts(tk):
    lane = lax.broadcasted_iota(jnp.int32, (1, LANE), 1)
    row = lax.broadcasted_iota(jnp.int32, (tk, tk), 0)
    col = lax.broadcasted_iota(jnp.int32, (tk, tk), 1)
    return (lane < HEAD_DIM, lane >= HEAD_DIM), row, col


def _band_mask(rows, tk):
    return lax.broadcasted_iota(jnp.int32, (rows, tk), 1) < lax.broadcasted_iota(jnp.int32, (rows, tk), 0)


def _attn_fwd(proj):
    s = proj.shape[0]
    tq, tk = min(ATTN_TQ, s), min(ATTN_TK, s)
    r = tq // tk

    def body(q_ref, k_ref, v_ref, o_ref, l_ref, acc_ref, run_ref):
        i = pl.program_id(1)
        head_mask, row, col = _attn_consts(tk)
        later = (row > col).astype(BF16)
        q = q_ref[...] * ATTN_SCALE
        qh = [jnp.where(m, q, jnp.zeros_like(q)) for m in head_mask]
        acc_ref[...] = jnp.zeros_like(acc_ref)
        run_ref[...] = jnp.zeros_like(run_ref)

        def block(j, lo, hi, band):
            start = pl.multiple_of(j * tk, tk)
            kb = k_ref[pl.ds(start, tk), :]
            vb = v_ref[pl.ds(start, tk), :]
            rows = slice(lo, hi)
            causal = _band_mask(hi - lo, tk) if band else None
            hs = range(2)
            logits = [_sb_logits(qh[h][rows], kb) for h in hs]
            lb = [logits[h][0] for h in hs]
            l1m = [logits[h][1] if causal is None else jnp.where(causal, logits[h][1], 0.0) for h in hs]
            tail = [jnp.dot(l1m[h].astype(BF16), later, preferred_element_type=F32) + run_ref[h, rows] for h in hs]
            w = [jnp.exp(lb[h] + tail[h]) for h in hs]
            if causal is not None:
                w = [jnp.where(causal, w[h], 0.0) for h in hs]
            vh = [jnp.where(head_mask[h], vb, jnp.zeros_like(vb)) for h in hs]
            acc_ref[rows, :] += (jnp.dot(w[0].astype(BF16), vh[0], preferred_element_type=F32)
                                 + jnp.dot(w[1].astype(BF16), vh[1], preferred_element_type=F32))
            for h in hs:
                run_ref[h, rows] += jnp.sum(l1m[h], axis=1, keepdims=True)

        for b in reversed(range(r)):
            block(i * r + b, b * tk, tq, True)
        n_full = i * r
        half = tq // 2

        def more(c):
            return jnp.logical_and(c[0] < n_full, c[1] > LOG_ZERO)

        def step_all(c):
            block(n_full - 1 - c[0], 0, tq, False)
            return c[0] + 1, jnp.max(run_ref[:, half:, :])

        def step_upper(c):
            block(n_full - 1 - c[0], 0, half, False)
            return c[0] + 1, jnp.max(run_ref[:, :half, :])

        seen_all, _ = lax.while_loop(more, step_all, (jnp.int32(0), jnp.max(run_ref[:, half:, :])))
        seen, _ = lax.while_loop(more, step_upper, (seen_all, jnp.max(run_ref[:, :half, :])))
        o_ref[...] = acc_ref[...]
        lane = lax.broadcasted_iota(jnp.int32, (1, LANE), 1)
        first = jnp.where(lane < 3 * HEAD_DIM // 4, n_full - seen, n_full - seen_all).astype(F32)
        l_ref[...] = jnp.where(lane < HEAD_DIM // 2, run_ref[0], jnp.where(lane < HEAD_DIM, first, run_ref[1]))

    kq, kk, kv = OFF_Q // LANE, OFF_K // LANE, OFF_V // LANE
    tile = pl.BlockSpec((tq, LANE), lambda p, i: (i, p))
    return pl.pallas_call(
        body, name="attn_fwd",
        out_shape=(jax.ShapeDtypeStruct((s, D_ATTN), F32), jax.ShapeDtypeStruct((s, D_ATTN), F32)),
        grid=(N_PAIRS, s // tq),
        in_specs=[pl.BlockSpec((tq, LANE), lambda p, i: (i, kq + p)),
                  pl.BlockSpec((s, LANE), lambda p, i: (0, kk + p)),
                  pl.BlockSpec((s, LANE), lambda p, i: (0, kv + p))],
        out_specs=(tile, tile),
        scratch_shapes=[pltpu.VMEM((tq, LANE), F32), pltpu.VMEM((2, tq, 1), F32)],
        compiler_params=_cparams(("parallel", "arbitrary")),
    )(proj, proj, proj)


def _attn_bwd(proj, do, lsum):
    s = proj.shape[0]
    tq, tk = min(ATTN_TQ, s), min(ATTN_TK, s)
    r = tq // tk

    def body(q_ref, k_ref, v_ref, do_ref, l_ref, dq_ref, dk_ref, dv_ref, dqacc_ref, dkacc_ref, dvacc_ref,
             passed_ref, pre_ref):
        i = pl.program_id(1)

        @pl.when(i == 0)
        def _():
            dkacc_ref[...] = jnp.zeros_like(dkacc_ref)
            dvacc_ref[...] = jnp.zeros_like(dvacc_ref)

        head_mask, row, col = _attn_consts(tk)
        later = (row > col).astype(BF16)
        earlier = (row < col).astype(BF16)
        q = q_ref[...] * ATTN_SCALE
        dov = do_ref[...].astype(BF16)
        qh = [jnp.where(m, q, jnp.zeros_like(q)) for m in head_mask]
        doh = [jnp.where(m, dov, jnp.zeros_like(dov)) for m in head_mask]
        lsum_v = l_ref[...]
        lh = [lsum_v[:, 0:1], lsum_v[:, HEAD_DIM:HEAD_DIM + 1]]
        n_full = i * r
        half = tq // 2
        quarter = HEAD_DIM // 4
        first_all = jnp.clip(jnp.max(lsum_v[0:8, 3 * quarter:HEAD_DIM]).astype(jnp.int32), 0, n_full)
        first = jnp.clip(jnp.max(lsum_v[0:8, 2 * quarter:3 * quarter]).astype(jnp.int32), 0, first_all)
        dqacc_ref[...] = jnp.zeros_like(dqacc_ref)
        passed_ref[...] = jnp.zeros_like(passed_ref)
        pre_ref[...] = jnp.zeros_like(pre_ref)

        def block(j, lo, hi, band):
            start = pl.multiple_of(j * tk, tk)
            kb = k_ref[pl.ds(start, tk), :]
            vb = v_ref[pl.ds(start, tk), :]
            rows = slice(lo, hi)
            causal = _band_mask(hi - lo, tk) if band else None
            hs = range(2)
            q_rows = [qh[h][rows] for h in hs]
            do_rows = [doh[h][rows] for h in hs]
            logits = [_sb_logits(q_rows[h], kb) for h in hs]
            lb = [logits[h][0] for h in hs]
            l1m = [logits[h][1] if causal is None else jnp.where(causal, logits[h][1], 0.0) for h in hs]
            da = [lax.dot_general(do_rows[h], vb, NT, preferred_element_type=F32) for h in hs]
            rs = [jnp.sum(l1m[h], axis=1, keepdims=True) for h in hs]
            right = [lh[h][rows] - passed_ref[h, rows] - rs[h] for h in hs]
            for h in hs:
                passed_ref[h, rows] += rs[h]
            tail = [jnp.dot(l1m[h].astype(BF16), later, preferred_element_type=F32) + right[h] for h in hs]
            a = [jnp.exp(lb[h] + tail[h]) for h in hs]
            if causal is not None:
                a = [jnp.where(causal, a[h], 0.0) for h in hs]
            g = [a[h] * da[h] for h in hs]
            pre = [jnp.dot(g[h].astype(BF16), earlier, preferred_element_type=F32) + pre_ref[h, rows] for h in hs]
            for h in hs:
                pre_ref[h, rows] += jnp.sum(g[h], axis=1, keepdims=True)
            dz = [g[h] - jnp.exp(lb[h]) * (g[h] + pre[h]) for h in hs]
            if causal is not None:
                dz = [jnp.where(causal, dz[h], 0.0) for h in hs]
            dzb = [dz[h].astype(BF16) for h in hs]
            kh = [jnp.where(head_mask[h], kb, jnp.zeros_like(kb)) * ATTN_SCALE for h in hs]
            dqacc_ref[rows, :] += (jnp.dot(dzb[0], kh[0], preferred_element_type=F32)
                                   + jnp.dot(dzb[1], kh[1], preferred_element_type=F32))
            dvacc_ref[pl.ds(start, tk), :] += (
                lax.dot_general(a[0].astype(BF16), do_rows[0], TN, preferred_element_type=F32)
                + lax.dot_general(a[1].astype(BF16), do_rows[1], TN, preferred_element_type=F32))
            dkacc_ref[pl.ds(start, tk), :] += (
                lax.dot_general(dzb[0], q_rows[0], TN, preferred_element_type=F32)
                + lax.dot_general(dzb[1], q_rows[1], TN, preferred_element_type=F32))

        def step_upper(j, carry):
            block(j, 0, half, False)
            return carry

        def step_all(j, carry):
            block(j, 0, tq, False)
            return carry

        lax.fori_loop(first, first_all, step_upper, 0)
        lax.fori_loop(first_all, n_full, step_all, 0)
        for b in range(r):
            block(n_full + b, b * tk, tq, True)
        dq_ref[...] = dqacc_ref[...].astype(BF16)

        @pl.when(i == pl.num_programs(1) - 1)
        def _():
            dk_ref[...] = dkacc_ref[...].astype(BF16)
            dv_ref[...] = dvacc_ref[...].astype(BF16)

    kq, kk, kv = OFF_Q // LANE, OFF_K // LANE, OFF_V // LANE
    tile = pl.BlockSpec((tq, LANE), lambda p, i: (i, p))
    full = pl.BlockSpec((s, LANE), lambda p, i: (0, p))
    shp = jax.ShapeDtypeStruct((s, D_ATTN), BF16)
    return pl.pallas_call(
        body, name="attn_bwd", out_shape=(shp, shp, shp), grid=(N_PAIRS, s // tq),
        in_specs=[pl.BlockSpec((tq, LANE), lambda p, i: (i, kq + p)),
                  pl.BlockSpec((s, LANE), lambda p, i: (0, kk + p)),
                  pl.BlockSpec((s, LANE), lambda p, i: (0, kv + p)),
                  tile, tile],
        out_specs=(tile, full, full),
        scratch_shapes=[pltpu.VMEM((tq, LANE), F32), pltpu.VMEM((s, LANE), F32), pltpu.VMEM((s, LANE), F32),
                        pltpu.VMEM((2, tq, 1), F32), pltpu.VMEM((2, tq, 1), F32)],
        compiler_params=_cparams(("parallel", "arbitrary")),
    )(proj, proj, proj, do, lsum)


def _shift_down(u, k, rows):
    return jnp.where(rows >= k, pltpu.roll(u, k, 0), 0.0)


def _shift_up(u, k, rows, s):
    return jnp.where(rows < s - k, pltpu.roll(u, s - k, 0), 0.0)


def _conv_fwd(proj, w, b):
    s = proj.shape[0]
    blk0 = OFF_XBC // LANE

    def body(u_ref, w_ref, b_ref, o_ref):
        u = u_ref[...].astype(F32)
        rows = lax.broadcasted_iota(jnp.int32, (s, 1), 0)
        pre = u * w_ref[CONV_K - 1:CONV_K, :] + b_ref[...]
        for k in range(1, CONV_K):
            pre += _shift_down(u, k, rows) * w_ref[CONV_K - 1 - k:CONV_K - k, :]
        o_ref[...] = pre * _sigmoid(pre)

    return pl.pallas_call(
        body, name="conv_fwd", out_shape=jax.ShapeDtypeStruct((s, D_XBC), F32), grid=(D_XBC // LANE,),
        in_specs=[pl.BlockSpec((s, LANE), lambda j: (0, blk0 + j)), pl.BlockSpec((CONV_K, LANE), lambda j: (0, j)),
                  pl.BlockSpec((1, LANE), lambda j: (0, j))],
        out_specs=pl.BlockSpec((s, LANE), lambda j: (0, j)), compiler_params=_cparams(("parallel",)),
    )(proj, w, b)


def _conv_bwd(proj, w, b, dact):
    s = proj.shape[0]
    blk0 = OFF_XBC // LANE

    def body(u_ref, w_ref, b_ref, da_ref, du_ref, dw_ref, db_ref):
        u = u_ref[...].astype(F32)
        rows = lax.broadcasted_iota(jnp.int32, (s, 1), 0)
        shifted = [u] + [_shift_down(u, k, rows) for k in range(1, CONV_K)]
        pre = b_ref[...] + shifted[0] * w_ref[CONV_K - 1:CONV_K, :]
        for k in range(1, CONV_K):
            pre += shifted[k] * w_ref[CONV_K - 1 - k:CONV_K - k, :]
        sg = _sigmoid(pre)
        dpre = da_ref[...] * _silu_grad(pre, sg)
        db_ref[...] = jnp.sum(dpre, axis=0, keepdims=True)
        du = dpre * w_ref[CONV_K - 1:CONV_K, :]
        for k in range(CONV_K):
            dw_ref[CONV_K - 1 - k:CONV_K - k, :] = jnp.sum(dpre * shifted[k], axis=0, keepdims=True)
            if k:
                du += _shift_up(dpre, k, rows, s) * w_ref[CONV_K - 1 - k:CONV_K - k, :]
        du_ref[...] = du.astype(BF16)

    col = pl.BlockSpec((s, LANE), lambda j: (0, j))
    return pl.pallas_call(
        body, name="conv_bwd",
        out_shape=(jax.ShapeDtypeStruct((s, D_XBC), BF16), jax.ShapeDtypeStruct((CONV_K, D_XBC), F32),
                   jax.ShapeDtypeStruct((1, D_XBC), F32)),
        grid=(D_XBC // LANE,),
        in_specs=[pl.BlockSpec((s, LANE), lambda j: (0, blk0 + j)), pl.BlockSpec((CONV_K, LANE), lambda j: (0, j)),
                  pl.BlockSpec((1, LANE), lambda j: (0, j)), col],
        out_specs=(col, pl.BlockSpec((CONV_K, LANE), lambda j: (0, j)), pl.BlockSpec((1, LANE), lambda j: (0, j))),
        compiler_params=_cparams(("parallel",)),
    )(proj, w, b, dact)


def _ssd_decays(dtraw_ref, bias_ref, dtt_ref, biast_ref, arow_ref, acol_ref):
    ln = CHUNK
    dt = _softplus(dtraw_ref[...] + bias_ref[...])
    r = lax.broadcasted_iota(jnp.int32, (ln, ln), 0)
    c = lax.broadcasted_iota(jnp.int32, (ln, ln), 1)
    ac = jnp.dot((r >= c).astype(F32), dt * arow_ref[...], preferred_element_type=F32, precision=HI)
    dtt = _softplus(dtt_ref[...] + biast_ref[...])
    act = jnp.dot(dtt * acol_ref[...], (r <= c).astype(F32), preferred_element_type=F32, precision=HI)
    return dt, ac, act, r >= c


def _pair_cols(m0, v, h0):
    return jnp.where(m0, v[:, h0:h0 + 1], v[:, h0 + 1:h0 + 2])


def _ssd_fwd(act, dtraw, dtt, bias, biast, arow, acol, dskip):
    s = act.shape[0]
    ln = CHUNK
    nc = s // ln

    def body(act_ref, dtraw_ref, dtt_ref, bias_ref, biast_ref, arow_ref, acol_ref, dsk_ref, y_ref, st_ref,
             state_ref):
        @pl.when(pl.program_id(0) == 0)
        def _():
            state_ref[...] = jnp.zeros_like(state_ref)

        dt, ac, act_t, lower = _ssd_decays(dtraw_ref, bias_ref, dtt_ref, biast_ref, arow_ref, acol_ref)
        lane = lax.broadcasted_iota(jnp.int32, (1, LANE), 1)
        m0 = lane < HEAD_DIM
        for g in range(N_GROUPS):
            bg32 = act_ref[:, D_SSM + g * D_STATE:D_SSM + (g + 1) * D_STATE]
            bg, bg_t = bg32.astype(BF16), bg32.T.astype(BF16)
            cg = act_ref[:, D_SSM + (N_GROUPS + g) * D_STATE:D_SSM + (N_GROUPS + g + 1) * D_STATE].astype(BF16)
            cb = lax.dot_general(cg, bg, NT, preferred_element_type=F32)
            for p in range(g * 4, g * 4 + 4):
                h0 = 2 * p
                xp = act_ref[:, p * LANE:(p + 1) * LANE]
                xdt = xp * _pair_cols(m0, dt, h0)
                acp = _pair_cols(m0, ac, h0)
                last = acp[ln - 1:ln, :]
                y = xp * dsk_ref[:, p * LANE:(p + 1) * LANE]
                for hh in range(2):
                    h = h0 + hh
                    dm = jnp.exp(jnp.where(lower, ac[:, h:h + 1] - act_t[h:h + 1, :], -jnp.inf))
                    mask = m0 if hh == 0 else jnp.logical_not(m0)
                    y += jnp.dot((cb * dm).astype(BF16), jnp.where(mask, xdt, 0.0).astype(BF16),
                                 preferred_element_type=F32)
                prev = state_ref[p]
                st_ref[0, p] = prev
                y += jnp.dot(cg, prev.astype(BF16), preferred_element_type=F32) * jnp.exp(acp)
                y_ref[:, p * LANE:(p + 1) * LANE] = y
                cs = jnp.dot(bg_t, (xdt * jnp.exp(last - acp)).astype(BF16), preferred_element_type=F32)
                state_ref[p] = prev * jnp.exp(last) + cs

    row = lambda w: pl.BlockSpec((1, w), lambda c: (0, 0))
    return pl.pallas_call(
        body, name="ssd_fwd",
        out_shape=(jax.ShapeDtypeStruct((s, D_SSM), F32),
                   jax.ShapeDtypeStruct((nc, N_PAIRS, LANE, D_STATE), F32)),
        grid=(nc,),
        in_specs=[pl.BlockSpec((ln, D_XBC), lambda c: (c, 0)), pl.BlockSpec((ln, LANE), lambda c: (c, 0)),
                  pl.BlockSpec((N_HEADS, ln), lambda c: (0, c)), row(LANE),
                  pl.BlockSpec((N_HEADS, 1), lambda c: (0, 0)), row(LANE),
                  pl.BlockSpec((N_HEADS, 1), lambda c: (0, 0)), row(D_SSM)],
        out_specs=(pl.BlockSpec((ln, D_SSM), lambda c: (c, 0)),
                   pl.BlockSpec((1, N_PAIRS, LANE, D_STATE), lambda c: (c, 0, 0, 0))),
        scratch_shapes=[pltpu.VMEM((N_PAIRS, LANE, D_STATE), F32)],
        compiler_params=_cparams(("arbitrary",)),
    )(act, dtraw, dtt, bias, biast, arow, acol, dskip)


def _ssd_bwd(act, dtraw, dtt, bias, biast, arow, acol, dskip, states, dy):
    s = act.shape[0]
    ln = CHUNK
    nc = s // ln

    def body(act_ref, dtraw_ref, dtt_ref, bias_ref, biast_ref, arow_ref, acol_ref, dsk_ref, st_ref, dy_ref,
             dact_ref, dldc_ref, dldr_ref, ddt_ref, dd_ref, dstate_ref):
        @pl.when(pl.program_id(0) == 0)
        def _():
            dstate_ref[...] = jnp.zeros_like(dstate_ref)
            dd_ref[...] = jnp.zeros_like(dd_ref)

        dt, ac, act_t, lower = _ssd_decays(dtraw_ref, bias_ref, dtt_ref, biast_ref, arow_ref, acol_ref)
        lane = lax.broadcasted_iota(jnp.int32, (1, LANE), 1)
        m0 = lane < HEAD_DIM
        halves = (m0, jnp.logical_not(m0))
        is_last = lax.broadcasted_iota(jnp.int32, (ln, 1), 0) == ln - 1
        sub = lax.broadcasted_iota(jnp.int32, (N_HEADS, 1), 0)
        earlier_eq = jnp.logical_not(lower) | (lax.broadcasted_iota(jnp.int32, (ln, ln), 0)
                                               == lax.broadcasted_iota(jnp.int32, (ln, ln), 1))
        dac_col = jnp.zeros((ln, LANE), F32)
        dac_row = jnp.zeros((N_HEADS, ln), F32)
        ddt_col = jnp.zeros((ln, LANE), F32)

        def half_sum(v, hh):
            return jnp.sum(jnp.where(halves[hh], v, 0.0), axis=1, keepdims=True)

        for g in range(N_GROUPS):
            b_lo, c_lo = D_SSM + g * D_STATE, D_SSM + (N_GROUPS + g) * D_STATE
            bg32 = act_ref[:, b_lo:b_lo + D_STATE]
            cg32 = act_ref[:, c_lo:c_lo + D_STATE]
            bg, cg = bg32.astype(BF16), cg32.astype(BF16)
            cg_t = cg32.T.astype(BF16)
            cb_t = lax.dot_general(bg, cg, NT, preferred_element_type=F32)
            dcb_t = jnp.zeros((ln, ln), F32)
            dbg = jnp.zeros((ln, D_STATE), F32)
            dcg = jnp.zeros((ln, D_STATE), F32)
            for p in range(g * 4, g * 4 + 4):
                h0 = 2 * p
                cols = slice(p * LANE, (p + 1) * LANE)
                xp = act_ref[:, cols]
                dyp = dy_ref[:, cols]
                dtp = _pair_cols(m0, dt, h0)
                acp = _pair_cols(m0, ac, h0)
                last = acp[ln - 1:ln, :]
                xdt = xp * dtp
                eac = jnp.exp(acp)
                dte = jnp.exp(last - acp)
                dec = jnp.exp(last)
                prev = st_ref[0, p]
                prev_b = prev.astype(BF16)
                ds = dstate_ref[p]
                ds_b = ds.astype(BF16)

                dd_ref[:, cols] += jnp.sum(dyp * xp, axis=0, keepdims=True)
                dx = dyp * dsk_ref[:, cols]
                zoff = jnp.dot(cg, prev_b, preferred_element_type=F32)
                dz_b = (dyp * eac).astype(BF16)
                dcg += lax.dot_general(dz_b, prev_b, NT, preferred_element_type=F32)
                dprev = jnp.dot(cg_t, dz_b, preferred_element_type=F32) + ds * dec
                wmat = jnp.dot(bg, ds_b, preferred_element_type=F32)
                xdte_b = (xdt * dte).astype(BF16)
                dbg += lax.dot_general(xdte_b, ds_b, NT, preferred_element_type=F32)
                dxdt = dte * wmat
                t_dte = xdt * wmat * dte
                t_ac = dyp * zoff * eac - t_dte
                at_last = jnp.sum(ds * prev, axis=0, keepdims=True) * dec + jnp.sum(t_dte, axis=0, keepdims=True)
                for hh in range(2):
                    h = h0 + hh
                    here = lane == h
                    dm_t = jnp.exp(jnp.where(earlier_eq, act_t[h:h + 1, :] - ac[:, h:h + 1], -jnp.inf))
                    mm_t = cb_t * dm_t
                    dyh = jnp.where(halves[hh], dyp, 0.0).astype(BF16)
                    xdth = jnp.where(halves[hh], xdt, 0.0).astype(BF16)
                    dmm_t = lax.dot_general(xdth, dyh, NT, preferred_element_type=F32)
                    dxdt += jnp.dot(mm_t.astype(BF16), dyh, preferred_element_type=F32)
                    gm_t = dmm_t * mm_t
                    dcb_t += dmm_t * dm_t
                    dac_col += jnp.where(here, half_sum(t_ac, hh) - jnp.sum(gm_t, axis=1, keepdims=True), 0.0)
                    dac_col += jnp.where(jnp.logical_and(is_last, here), half_sum(at_last, hh), 0.0)
                    dac_row += jnp.where(sub == h, jnp.sum(gm_t, axis=0, keepdims=True), 0.0)
                    ddt_col += jnp.where(here, half_sum(dxdt * xp, hh), 0.0)
                dact_ref[:, cols] = dx + dxdt * dtp
                dstate_ref[p] = dprev
            dcb_tb = dcb_t.astype(BF16)
            dact_ref[:, b_lo:b_lo + D_STATE] = dbg + jnp.dot(dcb_tb, cg, preferred_element_type=F32)
            dact_ref[:, c_lo:c_lo + D_STATE] = dcg + lax.dot_general(dcb_tb, bg, TN, preferred_element_type=F32)

        r = lax.broadcasted_iota(jnp.int32, (ln, ln), 0)
        c = lax.broadcasted_iota(jnp.int32, (ln, ln), 1)
        dldc_ref[...] = jnp.dot((r <= c).astype(F32), dac_col, preferred_element_type=F32, precision=HI)
        dldr_ref[...] = jnp.dot(dac_row, (r >= c).astype(F32), preferred_element_type=F32, precision=HI)
        ddt_ref[...] = ddt_col

    rev = lambda c: nc - 1 - c
    row = lambda w: pl.BlockSpec((1, w), lambda c: (0, 0))
    col16 = pl.BlockSpec((N_HEADS, 1), lambda c: (0, 0))
    chunk128 = pl.BlockSpec((ln, LANE), lambda c: (rev(c), 0))
    return pl.pallas_call(
        body, name="ssd_bwd",
        out_shape=(jax.ShapeDtypeStruct((s, D_XBC), F32), jax.ShapeDtypeStruct((s, LANE), F32),
                   jax.ShapeDtypeStruct((N_HEADS, s), F32), jax.ShapeDtypeStruct((s, LANE), F32),
                   jax.ShapeDtypeStruct((1, D_SSM), F32)),
        grid=(nc,),
        in_specs=[pl.BlockSpec((ln, D_XBC), lambda c: (rev(c), 0)), chunk128,
                  pl.BlockSpec((N_HEADS, ln), lambda c: (0, rev(c))), row(LANE), col16, row(LANE), col16,
                  row(D_SSM), pl.BlockSpec((1, N_PAIRS, LANE, D_STATE), lambda c: (rev(c), 0, 0, 0)),
                  pl.BlockSpec((ln, D_SSM), lambda c: (rev(c), 0))],
        out_specs=(pl.BlockSpec((ln, D_XBC), lambda c: (rev(c), 0)), chunk128,
                   pl.BlockSpec((N_HEADS, ln), lambda c: (0, rev(c))), chunk128, row(D_SSM)),
        scratch_shapes=[pltpu.VMEM((N_PAIRS, LANE, D_STATE), F32)],
        compiler_params=_cparams(("arbitrary",)),
    )(act, dtraw, dtt, bias, biast, arow, acol, dskip, states, dy)


def _dt_bwd(dtraw, bias, arow, dld_col, dld_row_t, ddt_col):
    s = dtraw.shape[0]
    tm = min(512, s)

    def body(raw_ref, bias_ref, a_ref, dc_ref, dr_ref, dd_ref, out_ref, sums_ref):
        @pl.when(pl.program_id(0) == 0)
        def _():
            sums_ref[...] = jnp.zeros_like(sums_ref)

        raw = raw_ref[...] + bias_ref[...]
        dld = dc_ref[...] + dr_ref[...]
        ddt = dld * a_ref[...] + dd_ref[...]
        draw = ddt * _sigmoid(raw)
        out_ref[...] = draw.astype(BF16)
        sums_ref[0:1, :] += jnp.sum(draw, axis=0, keepdims=True)
        sums_ref[1:2, :] += jnp.sum(dld * _softplus(raw), axis=0, keepdims=True)

    tile = pl.BlockSpec((tm, LANE), lambda i: (i, 0))
    row = pl.BlockSpec((1, LANE), lambda i: (0, 0))
    return pl.pallas_call(
        body, name="dt_bwd",
        out_shape=(jax.ShapeDtypeStruct((s, LANE), BF16), jax.ShapeDtypeStruct((2, LANE), F32)), grid=(s // tm,),
        in_specs=[tile, row, row, tile, tile, tile], out_specs=(tile, pl.BlockSpec((2, LANE), lambda i: (0, 0))),
        compiler_params=_cparams(("arbitrary",)),
    )(dtraw, bias, arow, dld_col, dld_row_t, ddt_col)


def _sum8(parts):
    nb, n = parts.shape

    def body(p_ref, o_ref):
        acc = p_ref[0:1, :]
        for b in range(1, nb):
            acc = acc + p_ref[b:b + 1, :]
        o_ref[...] = acc

    return pl.pallas_call(body, name="sum8", out_shape=jax.ShapeDtypeStruct((1, n), F32),
                          compiler_params=_cparams())(parts)


def _outer8(act_t, dmod):
    d, nb = act_t.shape
    n = dmod.shape[1]

    def body(a_ref, m_ref, o_ref):
        acc = a_ref[:, 0:1] * m_ref[0:1, :]
        for b in range(1, nb):
            acc = acc + a_ref[:, b:b + 1] * m_ref[b:b + 1, :]
        o_ref[...] = acc

    return pl.pallas_call(body, name="outer8", out_shape=jax.ShapeDtypeStruct((d, n), F32),
                          compiler_params=_cparams())(act_t, dmod)


def _pad_lanes(v, width=LANE):
    return jnp.pad(v, ((0, 0), (0, width - v.shape[1])))


def kernel(x, c, w_ada, b_ada, norm_in_gain, w_in, conv_w, conv_b, dt_bias, a_log, d_skip, sb_norm_gain, ssm_norm_gain, w_out, norm_f_gain, loss_target, m_w_ada, m_b_ada, m_norm_in_gain, m_w_in, m_conv_w, m_conv_b, m_dt_bias, m_a_log, m_d_skip, m_sb_norm_gain, m_ssm_norm_gain, m_w_out, m_norm_f_gain, v_w_ada, v_b_ada, v_norm_in_gain, v_w_in, v_conv_w, v_conv_b, v_dt_bias, v_a_log, v_d_skip, v_sb_norm_gain, v_ssm_norm_gain, v_w_out, v_norm_f_gain):
    ax, ay, ac_ = _coords()
    chip = 2 * ax + ay
    me = 2 * chip + ac_
    my_c = jnp.reshape(ac_, (1,)).astype(jnp.int32)
    x2d, tgt = x[0], loss_target[0]
    s = x2d.shape[0]
    ada_cols = w_ada.shape[2]
    cw_cols = conv_w.shape[2]
    in_cols = w_in.shape[2]
    out_rows = w_out.shape[1]

    small = jnp.concatenate([c, conv_w[0].reshape(1, CONV_K * cw_cols)], axis=1)
    small_all = _allgather8(small, "gather_cond")[:, 0, :]
    c_all = small_all[:, :D_MODEL]
    conv_w_full = (small_all[0::2, D_MODEL:].reshape(N_CHIPS, CONV_K, cw_cols)
                   .transpose(1, 0, 2).reshape(CONV_K, D_XBC))
    b_ada_shard = lax.dynamic_slice_in_dim(b_ada, chip * ada_cols, ada_cols, axis=1)
    mod_part, c_act_all = _ada_mod(c_all, w_ada[0], b_ada_shard)
    mod_all = _allgather8(mod_part.reshape(1, N_DEV * ada_cols), "gather_mod")[0::2, 0, :]
    mod_all = mod_all.reshape(N_CHIPS, N_DEV, ada_cols)
    mod = lax.dynamic_index_in_dim(mod_all, me, axis=1, keepdims=False).reshape(1, 3 * D_MODEL)
    shift, scale, gate = mod[:, :D_MODEL], mod[:, D_MODEL:2 * D_MODEL], mod[:, 2 * D_MODEL:]

    w_in_mine, w_out_mine = w_in[0].T.astype(BF16), w_out[0].astype(BF16)
    w_in_all, w_out_all = _gather_shards([w_in_mine, w_out_mine], "gather_weights")
    w_in_all = lax.dynamic_update_slice(w_in_all, w_in_mine[None], (chip, 0, 0))
    w_out_all = lax.dynamic_update_slice(w_out_all, w_out_mine[None], (chip, 0, 0))
    w_in_t = w_in_all.reshape(D_PROJ, D_MODEL)
    w_zs_t = w_in_t[ZS_LO:]
    w_dt_t = jnp.pad(w_in_t[DT_LO:ZS_LO], ((0, LANE - N_HEADS), (0, 0)))
    w_out_full = w_out_all.reshape(N_CHIPS * out_rows, D_MODEL)

    h = _rms_mod_fwd(x2d, norm_in_gain, scale, shift)
    proj = _matmul(h, w_in_t, BF16, "in_proj", "nt", 1024, 512, 1024, n_out=D_MAIN)
    proj_zs = _matmul(h, w_zs_t, BF16, "in_proj_zs", "nt", 1024, 512, 1024)
    dtraw = _matmul(h, w_dt_t, F32, "in_proj_dt", "nt", 1024, LANE, 1024)
    o_attn, lsum = _attn_fwd(proj)
    y_attn = _gated_norm_fwd(o_attn, proj, OFF_ZA, sb_norm_gain, False, "attn_gate_fwd")
    act = _conv_fwd(proj, conv_w_full, conv_b)
    a_neg = -jnp.exp(a_log)
    arow, acol = _pad_lanes(a_neg), a_neg.reshape(N_HEADS, 1)
    bias_row, bias_col = _pad_lanes(dt_bias), dt_bias.reshape(N_HEADS, 1)
    dtt = dtraw[:, :N_HEADS].T
    dskip_row = jnp.repeat(d_skip, HEAD_DIM, axis=1)
    ssd_args = (act, dtraw, dtt, bias_row, bias_col, arow, acol, dskip_row)
    y_ssd, states = _ssd_fwd(*ssd_args)
    y_ssm = _gated_norm_fwd(y_ssd, proj_zs, 0, ssm_norm_gain, True, "ssm_gate_fwd")
    mixed = _matmul_sum([(y_attn, D_ATTN, 0, w_out_full, 0), (y_ssm, D_SSM, 0, w_out_full, 1)], F32, "out_proj",
                        1024, 1024)

    dx2, dmixed, head_sums = _loss_head(x2d, mixed, gate, norm_f_gain.reshape(1, D_MODEL), tgt)
    out_all = N_CHIPS * out_rows
    g_w_out = _matmul_tn_rows(None, out_all, y_attn, dmixed, 0, "out_proj_dw_attn", 512)
    g_w_out = _matmul_tn_rows(g_w_out, out_all, y_ssm, dmixed, D_ATTN // 512, "out_proj_dw_ssm", 512)
    d_mix_in = _matmul(dmixed, w_out_full, F32, "out_proj_dx", "nt", 1024, 1024, 1024)
    d_o, dz_attn, g_sb = _gated_norm_bwd(d_mix_in, 0, o_attn, proj, OFF_ZA, sb_norm_gain, False, "attn_gate_bwd")
    d_y, dz_ssm, g_ssm = _gated_norm_bwd(d_mix_in, 1, y_ssd, proj_zs, 0, ssm_norm_gain, True, "ssm_gate_bwd")
    dq, dk, dv = _attn_bwd(proj, d_o, lsum)
    dact, dld_col, dld_row, ddt_col, dd_cols = _ssd_bwd(*ssd_args, states, d_y)
    dxbc, g_conv_w, g_conv_b = _conv_bwd(proj, conv_w_full, conv_b, dact)
    ddtraw, dt_sums = _dt_bwd(dtraw, bias_row, arow, dld_col, _pad_lanes(dld_row.T), ddt_col)
    g_in_t = None
    for piece, lo, label in [(dq, OFF_Q, "q"), (dk, OFF_K, "k"), (dv, OFF_V, "v"), (dz_attn, OFF_ZA, "za"),
                             (dxbc, OFF_XBC, "xbc")]:
        g_in_t = _matmul_tn_rows(g_in_t, D_PROJ, piece, h, lo // 512, "in_proj_dw_" + label, 512)
    g_in_t = _matmul_tn_rows(g_in_t, D_PROJ, ddtraw, h, DT_LO // LANE, "in_proj_dw_dt", LANE)
    g_zs_t = _matmul(dz_ssm, h, F32, "in_proj_dw_zs", "tn", 512, 1024, 4096)
    g_in_t = lax.dynamic_update_slice(g_in_t, g_zs_t, (ZS_LO, 0))
    dh_terms = [(dq, D_ATTN, 0, w_in_t, 0), (dk, D_ATTN, 0, w_in_t, 1), (dv, D_ATTN, 0, w_in_t, 2),
                (dz_attn, D_ATTN, 0, w_in_t, 3)]
    dh_terms += [(dxbc, 512, j, w_in_t, OFF_XBC // 512 + j) for j in range(D_XBC // 512)]
    dh_terms += [(dz_ssm, D_SSM, 0, w_zs_t, 0), (ddtraw, LANE, 0, w_dt_t, 0)]
    g_in_blocks = g_in_t.reshape(N_CHIPS, in_cols, D_MODEL)
    g_out_blocks = g_w_out.reshape(N_CHIPS, out_rows, D_MODEL)
    land_in, land_out = _send_to_sibling([g_in_blocks, g_out_blocks], "grads_to_sibling")
    chip_in = _add_my_half(g_in_blocks, land_in, my_c, "add_sibling_in")
    chip_out = _add_my_half(g_out_blocks, land_out, my_c, "add_sibling_out")
    dh, slots_in, slots_out = _matmul_sum(dh_terms, F32, "in_proj_dx_and_grads_between_chips", 512, 1024,
                                          exchange=[chip_in, chip_out])
    grad_x, in_sums = _rms_mod_bwd(x2d, dh, dx2, norm_in_gain, scale)

    g_a_log = dt_sums[1:2, :N_HEADS] * a_neg
    g_d_skip = jnp.sum(dd_cols.reshape(N_HEADS, HEAD_DIM), axis=1).reshape(1, N_HEADS)
    dmod = jnp.concatenate([in_sums[0:1], in_sums[1:2], head_sums[2:3]], axis=1)
    loss_part = 0.5 / D_MODEL * jnp.sum(head_sums[0:1], axis=1, keepdims=True)
    pieces = [dmod, in_sums[2:3], g_conv_w.reshape(1, CONV_K * D_XBC), g_conv_b, _pad_lanes(dt_sums[0:1, :N_HEADS]),
              _pad_lanes(g_a_log), _pad_lanes(g_d_skip), g_sb, g_ssm, head_sums[1:2], _pad_lanes(loss_part)]
    widths = [p.shape[1] for p in pieces]
    parts_all = _allgather8(jnp.concatenate(pieces, axis=1), "gather_small_grads")[:, 0, :]
    total = _sum8(parts_all)
    offs = [0]
    for w_ in widths:
        offs.append(offs[-1] + w_)
    tot = [total[:, offs[i]:offs[i + 1]] for i in range(len(pieces))]
    g_b_ada, g_norm_in, g_conv_w_full = tot[0], tot[1], tot[2].reshape(CONV_K, D_XBC)
    g_conv_b_t, g_dt_bias, g_a_log_t, g_d_skip_t = tot[3], tot[4][:, :N_HEADS], tot[5][:, :N_HEADS], tot[6][:, :N_HEADS]
    g_sb_t, g_ssm_t, g_norm_f, loss = tot[7], tot[8], tot[9], tot[10][0, 0]
    g_conv_w_shard = lax.dynamic_slice_in_dim(g_conv_w_full, chip * cw_cols, cw_cols, axis=1)
    dmod_shard = lax.dynamic_slice_in_dim(parts_all[:, :3 * D_MODEL], chip * ada_cols, ada_cols, axis=1)
    g_w_ada = _outer8(c_act_all.T, dmod_shard)

    own = lambda blocks: lax.dynamic_slice_in_dim(blocks, chip, 1, axis=0)
    slots_in = lax.dynamic_update_slice(slots_in, own(chip_in), (chip, 0, 0))
    slots_out = lax.dynamic_update_slice(slots_out, own(chip_out), (chip, 0, 0))
    half_in, half_out = _sum_slots(slots_in, "sum_chips_in"), _sum_slots(slots_out, "sum_chips_out")
    their_in, their_out = _swap_with_sibling([half_in, half_out], "grads_swap_sibling")
    south = ac_ == 0
    both = lambda mine, theirs: jnp.concatenate([jnp.where(south, mine, theirs), jnp.where(south, theirs, mine)],
                                                axis=1)
    g_w_in_t, g_w_out_shard = both(half_in, their_in), both(half_out, their_out)

    d_w_ada, nm_w_ada, nv_w_ada = _adamw(w_ada[0], g_w_ada, m_w_ada[0], v_w_ada[0], "adamw_w_ada")
    d_w_in, nm_w_in, nv_w_in = [r.T for r in _adamw(w_in[0].T, g_w_in_t, m_w_in[0].T, v_w_in[0].T, "adamw_w_in")]
    g_w_in = g_w_in_t.T
    d_w_out, nm_w_out, nv_w_out = _adamw(w_out[0], g_w_out_shard, m_w_out[0], v_w_out[0], "adamw_w_out")
    flat = lambda a: a.reshape(1, -1)
    small_w = [b_ada, norm_in_gain, conv_w[0], conv_b, dt_bias, a_log, d_skip, sb_norm_gain, ssm_norm_gain,
               norm_f_gain]
    small_m = [m_b_ada, m_norm_in_gain, m_conv_w[0], m_conv_b, m_dt_bias, m_a_log, m_d_skip, m_sb_norm_gain,
               m_ssm_norm_gain, m_norm_f_gain]
    small_v = [v_b_ada, v_norm_in_gain, v_conv_w[0], v_conv_b, v_dt_bias, v_a_log, v_d_skip, v_sb_norm_gain,
               v_ssm_norm_gain, v_norm_f_gain]
    small_g = [g_b_ada, g_norm_in, g_conv_w_shard, g_conv_b_t, g_dt_bias, g_a_log_t, g_d_skip_t, g_sb_t, g_ssm_t,
               g_norm_f]
    cat = lambda arrs: jnp.concatenate([flat(a) for a in arrs], axis=1)
    d_small, nm_small, nv_small = _adamw(cat(small_w), cat(small_g), cat(small_m), cat(small_v), "adamw_small")
    sizes = [a.size for a in small_w]
    soffs = [0]
    for n_ in sizes:
        soffs.append(soffs[-1] + n_)

    def split(packed):
        return [packed[0, soffs[i]:soffs[i + 1]].reshape(small_w[i].shape) for i in range(len(small_w))]

    def ordered(big_ada, big_in, big_out, smalls):
        (s_b_ada, s_norm_in, s_conv_w, s_conv_b, s_dt_bias, s_a_log, s_d_skip, s_sb, s_ssm, s_norm_f) = smalls
        return [big_ada[None], s_b_ada, s_norm_in, big_in[None], s_conv_w[None], s_conv_b, s_dt_bias, s_a_log,
                s_d_skip, s_sb, s_ssm, big_out[None], s_norm_f]

    grads = ordered(g_w_ada, g_w_in, g_w_out_shard,
                    [g.reshape(w_.shape) for g, w_ in zip(small_g, small_w)])
    deltas = ordered(d_w_ada, d_w_in, d_w_out, split(d_small))
    new_m = ordered(nm_w_ada, nm_w_in, nm_w_out, split(nm_small))
    new_v = ordered(nv_w_ada, nv_w_in, nv_w_out, split(nv_small))
    return (loss, grad_x[None], *grads, *deltas, *new_m, *new_v)
```

```python
import functools

import jax
import jax.numpy as jnp
from jax import lax
from jax.experimental import pallas as pl
from jax.experimental.pallas import tpu as pltpu

F32, BF16 = jnp.float32, jnp.bfloat16
MESH = pl.DeviceIdType.MESH
HI = lax.Precision.HIGHEST
NN = (((1,), (0,)), ((), ()))
NT = (((1,), (1,)), ((), ()))
TN = (((0,), (0,)), ((), ()))

D_MODEL = 1024
D_ATTN = 1024
D_SSM = 1024
HEAD_DIM = 64
N_HEADS = 16
N_PAIRS = 8
N_GROUPS = 2
D_STATE = 128
D_XBC = 1536
D_PROJ = 6672
D_MAIN = 5632
CONV_K = 4
CHUNK = 128
LANE = 128
N_CHIPS = 4
N_DEV = 8
NORM_EPS = 1e-6
ATTN_SCALE = HEAD_DIM ** -0.5
ATTN_TQ = 512
ATTN_TK = 256
LOG_ZERO = -110.0
ADAM_LR, ADAM_B1, ADAM_B2, ADAM_EPS, ADAM_WD, ADAM_STEP = 0.001, 0.9, 0.999, 1e-08, 0.01, 10
VMEM_LIMIT = 56 * 1024 * 1024

OFF_Q, OFF_K, OFF_V, OFF_ZA, OFF_XBC = 0, 1024, 2048, 3072, 4096
DT_LO = D_MAIN
ZS_LO = DT_LO + N_HEADS


def _cparams(sem=None):
    return pltpu.CompilerParams(dimension_semantics=sem, vmem_limit_bytes=VMEM_LIMIT)


def _sigmoid(x):
    return 1.0 / (1.0 + jnp.exp(-x))


def _softplus(x):
    return jnp.maximum(x, 0.0) + jnp.log(1.0 + jnp.exp(-jnp.abs(x)))


def _coords():
    return lax.axis_index("x"), lax.axis_index("y"), lax.axis_index("c")


def _allgather8(v, name):
    n = v.shape[-1]

    def body(v_ref, out_ref, send_sems, recv_sems, local_sem):
        x, y, c = _coords()
        me = 4 * x + 2 * y + c
        mine = pltpu.make_async_copy(v_ref, out_ref.at[me], local_sem)
        mine.start()
        sends, recvs = [], []
        for j in range(1, N_DEV):
            px = 1 - x if (j >> 2) & 1 else x
            py = 1 - y if (j >> 1) & 1 else y
            pc = 1 - c if j & 1 else c
            peer = (px, py, pc)
            sends.append(pltpu.make_async_remote_copy(
                src_ref=v_ref, dst_ref=out_ref.at[me], send_sem=send_sems.at[j - 1],
                recv_sem=recv_sems.at[j - 1], device_id=peer, device_id_type=MESH))
            recvs.append(pltpu.make_async_remote_copy(
                src_ref=v_ref, dst_ref=out_ref.at[4 * px + 2 * py + pc], send_sem=send_sems.at[j - 1],
                recv_sem=recv_sems.at[j - 1], device_id=peer, device_id_type=MESH))
        for s in sends:
            s.start()
        for r in recvs:
            r.wait_recv()
        for s in sends:
            s.wait_send()
        mine.wait()

    vm = pl.BlockSpec(memory_space=pltpu.VMEM)
    return pl.pallas_call(
        body, name=name, out_shape=jax.ShapeDtypeStruct((N_DEV, 1, n), F32),
        in_specs=[vm], out_specs=vm,
        scratch_shapes=[pltpu.SemaphoreType.DMA((N_DEV - 1,)), pltpu.SemaphoreType.DMA((N_DEV - 1,)),
                        pltpu.SemaphoreType.DMA(())],
    )(v)


def _other_chips(x, y):
    chips = [(1 - x, y), (x, 1 - y), (1 - x, 1 - y)]
    return chips, [2 * cx + cy for cx, cy in chips]


def _half_cols(width, which):
    half = width // 2
    return pl.ds(pl.multiple_of(which * half, half), half)


def _gather_shards(arrs, name):
    n = len(arrs)

    def body(*refs):
        ins, outs = refs[:n], refs[n:2 * n]
        send_sems, recv_sems = refs[2 * n:]
        x, y, c = _coords()
        k = 2 * x + y
        chips, chip_idx = _other_chips(x, y)
        sibling = (x, y, 1 - c)
        sends = []
        for a in range(n):
            mine = _half_cols(arrs[a].shape[-1], c)
            for j in range(3):
                cp = pltpu.make_async_remote_copy(
                    src_ref=ins[a].at[:, mine], dst_ref=outs[a].at[k, :, mine], send_sem=send_sems.at[6 * a + j],
                    recv_sem=recv_sems.at[6 * a + j], device_id=(*chips[j], c), device_id_type=MESH)
                cp.start()
                sends.append(cp)
        for a in range(n):
            mine = _half_cols(arrs[a].shape[-1], c)
            for j in range(3):
                landed = outs[a].at[chip_idx[j], :, mine]
                pltpu.make_async_remote_copy(
                    src_ref=landed, dst_ref=landed, send_sem=send_sems.at[6 * a + j],
                    recv_sem=recv_sems.at[6 * a + j], device_id=(*chips[j], c), device_id_type=MESH).wait_recv()
                fwd = pltpu.make_async_remote_copy(
                    src_ref=landed, dst_ref=landed, send_sem=send_sems.at[6 * a + 3 + j],
                    recv_sem=recv_sems.at[6 * a + 3 + j], device_id=sibling, device_id_type=MESH)
                fwd.start()
                sends.append(fwd)
        for a in range(n):
            theirs = _half_cols(arrs[a].shape[-1], 1 - c)
            for j in range(3):
                landed = outs[a].at[chip_idx[j], :, theirs]
                pltpu.make_async_remote_copy(
                    src_ref=landed, dst_ref=landed, send_sem=send_sems.at[6 * a + 3 + j],
                    recv_sem=recv_sems.at[6 * a + 3 + j], device_id=sibling, device_id_type=MESH).wait_recv()
        for cp in sends:
            cp.wait_send()

    hbm = pl.BlockSpec(memory_space=pl.ANY)
    return pl.pallas_call(
        body, name=name,
        out_shape=tuple(jax.ShapeDtypeStruct((N_CHIPS,) + a.shape, a.dtype) for a in arrs),
        in_specs=[hbm] * n, out_specs=tuple([hbm] * n),
        scratch_shapes=[pltpu.SemaphoreType.DMA((6 * n,)), pltpu.SemaphoreType.DMA((6 * n,))],
    )(*arrs)


def _send_to_sibling(arrs, name):
    n = len(arrs)

    def body(*refs):
        ins, outs = refs[:n], refs[n:2 * n]
        send_sems, recv_sems = refs[2 * n:]
        x, y, c = _coords()
        cps = []
        for a in range(n):
            cp = pltpu.make_async_remote_copy(
                src_ref=ins[a].at[:, :, _half_cols(arrs[a].shape[-1], 1 - c)], dst_ref=outs[a],
                send_sem=send_sems.at[a], recv_sem=recv_sems.at[a], device_id=(x, y, 1 - c), device_id_type=MESH)
            cp.start()
            cps.append(cp)
        for cp in cps:
            cp.wait()

    hbm = pl.BlockSpec(memory_space=pl.ANY)
    return pl.pallas_call(
        body, name=name,
        out_shape=tuple(jax.ShapeDtypeStruct(a.shape[:-1] + (a.shape[-1] // 2,), a.dtype) for a in arrs),
        in_specs=[hbm] * n, out_specs=tuple([hbm] * n),
        scratch_shapes=[pltpu.SemaphoreType.DMA((n,)), pltpu.SemaphoreType.DMA((n,))],
    )(*arrs)


def _swap_with_sibling(arrs, name):
    n = len(arrs)

    def body(*refs):
        ins, outs = refs[:n], refs[n:2 * n]
        send_sems, recv_sems = refs[2 * n:]
        x, y, c = _coords()
        cps = []
        for a in range(n):
            cp = pltpu.make_async_remote_copy(
                src_ref=ins[a], dst_ref=outs[a], send_sem=send_sems.at[a], recv_sem=recv_sems.at[a],
                device_id=(x, y, 1 - c), device_id_type=MESH)
            cp.start()
            cps.append(cp)
        for cp in cps:
            cp.wait()

    hbm = pl.BlockSpec(memory_space=pl.ANY)
    return pl.pallas_call(
        body, name=name,
        out_shape=tuple(jax.ShapeDtypeStruct(a.shape, a.dtype) for a in arrs),
        in_specs=[hbm] * n, out_specs=tuple([hbm] * n),
        scratch_shapes=[pltpu.SemaphoreType.DMA((n,)), pltpu.SemaphoreType.DMA((n,))],
    )(*arrs)


def _row_tile(rows, cols, n_arrays):
    budget = VMEM_LIMIT // 2
    t = rows
    while t % 16 == 0 and t * cols * 4 * n_arrays * 2 > budget:
        t //= 2
    return t


def _add_my_half(g, landed, my_c, name):
    nb, r, cdim = g.shape
    half = cdim // 2
    tr = _row_tile(r, half, 3)

    def body(c_ref, g_ref, l_ref, o_ref):
        o_ref[...] = (g_ref[...] + l_ref[...]).astype(BF16)

    spec = pl.BlockSpec((None, tr, half), lambda b, i, c_ref: (b, i, 0))
    return pl.pallas_call(
        body, name=name, out_shape=jax.ShapeDtypeStruct((nb, r, half), BF16),
        grid_spec=pltpu.PrefetchScalarGridSpec(
            num_scalar_prefetch=1, grid=(nb, r // tr),
            in_specs=[pl.BlockSpec((None, tr, half), lambda b, i, c_ref: (b, i, c_ref[0])), spec],
            out_specs=spec),
        compiler_params=_cparams(("parallel", "parallel")),
    )(my_c, g, landed)


def _sum_slots(a, name):
    nb, r, cdim = a.shape
    tr = _row_tile(r, cdim, 4)

    def body(a_ref, o_ref):
        o_ref[...] = ((a_ref[0].astype(F32) + a_ref[1].astype(F32)) + a_ref[2].astype(F32)) + a_ref[3].astype(F32)

    return pl.pallas_call(
        body, name=name, out_shape=jax.ShapeDtypeStruct((r, cdim), F32), grid=(r // tr,),
        in_specs=[pl.BlockSpec((nb, tr, cdim), lambda i: (0, i, 0))],
        out_specs=pl.BlockSpec((tr, cdim), lambda i: (i, 0)),
        compiler_params=_cparams(("parallel",)),
    )(a)


def _adamw(w, g, m, v, name):
    r, cdim = w.shape
    tr = _row_tile(r, cdim, 7)
    tc = cdim
    if tr == r and r > 8:
        while tc % (2 * LANE) == 0 and r * tc * 4 * 7 * 2 > VMEM_LIMIT // 2:
            tc //= 2

    def body(w_ref, g_ref, m_ref, v_ref, d_ref, nm_ref, nv_ref):
        gv = g_ref[...]
        nm = ADAM_B1 * m_ref[...] + (1.0 - ADAM_B1) * gv
        nv = ADAM_B2 * v_ref[...] + (1.0 - ADAM_B2) * (gv * gv)
        m_hat = nm / (1.0 - ADAM_B1 ** ADAM_STEP)
        v_hat = nv / (1.0 - ADAM_B2 ** ADAM_STEP)
        d_ref[...] = -ADAM_LR * (m_hat / (jnp.sqrt(v_hat) + ADAM_EPS) + ADAM_WD * w_ref[...])
        nm_ref[...] = nm
        nv_ref[...] = nv

    spec = pl.BlockSpec((tr, tc), lambda i, j: (i, j))
    shp = jax.ShapeDtypeStruct((r, cdim), F32)
    return pl.pallas_call(
        body, name=name, out_shape=(shp, shp, shp), grid=(r // tr, cdim // tc),
        in_specs=[spec] * 4, out_specs=(spec, spec, spec),
        compiler_params=_cparams(("parallel", "parallel")),
    )(w, g, m, v)


def _matmul(a, b, out_dtype, name, mode, tm, tn, tk, extra=None, n_out=None):
    dims = {"nn": NN, "nt": NT, "tn": TN}[mode]
    if mode == "tn":
        kdim, m = a.shape
    else:
        m, kdim = a.shape
    n = n_out if n_out is not None else (b.shape[0] if mode == "nt" else b.shape[1])
    tm, tn, tk = min(tm, m), min(tn, n), min(tk, kdim)
    nk = kdim // tk
    a_spec = (pl.BlockSpec((tk, tm), lambda i, j, k: (k, i)) if mode == "tn"
              else pl.BlockSpec((tm, tk), lambda i, j, k: (i, k)))
    b_spec = (pl.BlockSpec((tn, tk), lambda i, j, k: (j, k)) if mode == "nt"
              else pl.BlockSpec((tk, tn), lambda i, j, k: (k, j)))
    in_specs, operands = [a_spec, b_spec], [a, b]
    if extra is not None:
        a2, b2 = extra
        k2 = a2.shape[0] if mode == "tn" else a2.shape[1]
        in_specs.append(pl.BlockSpec((k2, tm), lambda i, j, k: (0, i)) if mode == "tn"
                        else pl.BlockSpec((tm, k2), lambda i, j, k: (i, 0)))
        in_specs.append(pl.BlockSpec((tn, k2), lambda i, j, k: (j, 0)) if mode == "nt"
                        else pl.BlockSpec((k2, tn), lambda i, j, k: (0, j)))
        operands += [a2, b2]

    def body_one_block(*refs):
        acc = lax.dot_general(refs[0][...], refs[1][...], dims, preferred_element_type=F32)
        if extra is not None:
            acc += lax.dot_general(refs[2][...], refs[3][...], dims, preferred_element_type=F32)
        refs[-1][...] = acc.astype(out_dtype)

    if nk == 1:
        return pl.pallas_call(
            body_one_block, name=name, out_shape=jax.ShapeDtypeStruct((m, n), out_dtype), grid=(m // tm, n // tn, 1),
            in_specs=in_specs, out_specs=pl.BlockSpec((tm, tn), lambda i, j, k: (i, j)),
            compiler_params=_cparams(("parallel", "parallel", "arbitrary")),
        )(*operands)

    def body(*refs):
        if extra is not None:
            a_ref, b_ref, a2_ref, b2_ref, o_ref, acc_ref = refs
        else:
            a_ref, b_ref, o_ref, acc_ref = refs
        k = pl.program_id(2)

        @pl.when(k == 0)
        def _():
            if extra is not None:
                acc_ref[...] = lax.dot_general(a2_ref[...], b2_ref[...], dims, preferred_element_type=F32)
            else:
                acc_ref[...] = jnp.zeros_like(acc_ref)

        acc_ref[...] += lax.dot_general(a_ref[...], b_ref[...], dims, preferred_element_type=F32)

        @pl.when(k == nk - 1)
        def _():
            o_ref[...] = acc_ref[...].astype(out_dtype)

    return pl.pallas_call(
        body, name=name, out_shape=jax.ShapeDtypeStruct((m, n), out_dtype), grid=(m // tm, n // tn, nk),
        in_specs=in_specs, out_specs=pl.BlockSpec((tm, tn), lambda i, j, k: (i, j)),
        scratch_shapes=[pltpu.VMEM((tm, tn), F32)],
        compiler_params=_cparams(("parallel", "parallel", "arbitrary")),
    )(*operands)


def _matmul_sum(terms, out_dtype, name, tm, tn, exchange=None):
    m, n = terms[0][0].shape[0], terms[0][3].shape[1]
    tm, tn = min(tm, m), min(tn, n)
    gm, gn = m // tm, n // tn
    nt = len(terms)
    in_specs, operands = [], []
    for a, ka, ia, b, ib in terms:
        in_specs.append(pl.BlockSpec((tm, ka), functools.partial(lambda i, j, ia: (i, ia), ia=ia)))
        in_specs.append(pl.BlockSpec((ka, tn), functools.partial(lambda i, j, ib: (ib, j), ib=ib)))
        operands += [a, b]
    sent = [] if exchange is None else list(exchange)
    ns = len(sent)
    hbm = pl.BlockSpec(memory_space=pl.ANY)

    def body(*refs):
        o_ref = refs[2 * nt + ns]
        if ns:
            ins, outs = refs[2 * nt:2 * nt + ns], refs[2 * nt + ns + 1:2 * nt + 2 * ns + 1]
            send_sems, recv_sems = refs[2 * nt + 2 * ns + 1:]
            x, y, c = _coords()
            k = 2 * x + y
            chips, chip_idx = _other_chips(x, y)
            step = pl.program_id(0) * gn + pl.program_id(1)

            def copies(a, j, landed):
                dst = outs[a].at[chip_idx[j]] if landed else outs[a].at[k]
                src = dst if landed else ins[a].at[chip_idx[j]]
                return pltpu.make_async_remote_copy(
                    src_ref=src, dst_ref=dst, send_sem=send_sems.at[3 * a + j], recv_sem=recv_sems.at[3 * a + j],
                    device_id=(*chips[j], c), device_id_type=MESH)

            @pl.when(step == 0)
            def _():
                for a in range(ns):
                    for j in range(3):
                        copies(a, j, False).start()

        acc = jnp.dot(refs[0][...], refs[1][...], preferred_element_type=F32)
        for t in range(1, nt):
            acc += jnp.dot(refs[2 * t][...], refs[2 * t + 1][...], preferred_element_type=F32)
        o_ref[...] = acc.astype(out_dtype)

        if ns:
            @pl.when(step == gm * gn - 1)
            def _():
                for a in range(ns):
                    for j in range(3):
                        copies(a, j, True).wait_recv()
                for a in range(ns):
                    for j in range(3):
                        copies(a, j, False).wait_send()

    main = jax.ShapeDtypeStruct((m, n), out_dtype)
    tile = pl.BlockSpec((tm, tn), lambda i, j: (i, j))
    if not ns:
        return pl.pallas_call(
            body, name=name, out_shape=main, grid=(gm, gn), in_specs=in_specs, out_specs=tile,
            compiler_params=_cparams(("parallel", "parallel")),
        )(*operands)
    return pl.pallas_call(
        body, name=name, out_shape=(main, *[jax.ShapeDtypeStruct(a.shape, a.dtype) for a in sent]), grid=(gm, gn),
        in_specs=in_specs + [hbm] * ns, out_specs=(tile, *[hbm] * ns),
        scratch_shapes=[pltpu.SemaphoreType.DMA((3 * ns,)), pltpu.SemaphoreType.DMA((3 * ns,))],
        compiler_params=_cparams(("arbitrary", "arbitrary")),
    )(*operands, *sent)


def _matmul_tn_pieces(pieces, rows, b, name, tm):
    kdim, n = b.shape
    tm = min(tm, min(p.shape[1] for p in pieces))
    tiles = [p.shape[1] // tm for p in pieces]
    first = [sum(tiles[:i]) for i in range(len(pieces))]

    def body(*refs):
        b_ref, o_ref = refs[-2:]
        i = pl.program_id(0)
        for p in range(len(pieces)):
            @pl.when(jnp.logical_and(i >= first[p], i < first[p] + tiles[p]))
            def _(p=p):
                o_ref[...] = lax.dot_general(refs[p][...], b_ref[...], TN, preferred_element_type=F32)

    in_specs = [pl.BlockSpec((kdim, tm), functools.partial(lambda i, lo, cnt: (0, jnp.clip(i - lo, 0, cnt - 1)),
                                                           lo=first[p], cnt=tiles[p])) for p in range(len(pieces))]
    return pl.pallas_call(
        body, name=name, out_shape=jax.ShapeDtypeStruct((rows, n), F32), grid=(sum(tiles),),
        in_specs=in_specs + [pl.BlockSpec((kdim, n), lambda i: (0, 0))],
        out_specs=pl.BlockSpec((tm, n), lambda i: (i, 0)),
        compiler_params=_cparams(("arbitrary",)),
    )(*pieces, b)


def _matmul_tn_rows(buf, rows, a, b, row_blk, name, tm):
    kdim, m = a.shape
    n = b.shape[1]
    tm = min(tm, m)

    def body(*refs):
        a_ref, b_ref, o_ref = refs[-3:]
        o_ref[...] = lax.dot_general(a_ref[...], b_ref[...], TN, preferred_element_type=F32)

    in_specs = [pl.BlockSpec((kdim, tm), lambda i: (0, i)), pl.BlockSpec((kdim, n), lambda i: (0, 0))]
    operands = [a, b]
    if buf is not None:
        in_specs.insert(0, pl.BlockSpec(memory_space=pl.ANY))
        operands.insert(0, buf)
    return pl.pallas_call(
        body, name=name, out_shape=jax.ShapeDtypeStruct((rows, n), F32), grid=(m // tm,),
        in_specs=in_specs, out_specs=pl.BlockSpec((tm, n), lambda i: (row_blk + i, 0)),
        input_output_aliases={} if buf is None else {0: 0},
        compiler_params=_cparams(("parallel",)),
    )(*operands)


def _ada_mod(c_all, w_shard, b_shard):
    nb, d = c_all.shape
    cols = w_shard.shape[1]

    def body(c_ref, w_ref, b_ref, mod_ref, act_ref):
        cv = c_ref[...]
        act = cv * _sigmoid(cv)
        act_ref[...] = act
        mod_ref[...] = jnp.dot(act, w_ref[...], preferred_element_type=F32, precision=HI) + b_ref[...]

    return pl.pallas_call(
        body, name="ada_mod",
        out_shape=(jax.ShapeDtypeStruct((nb, cols), F32), jax.ShapeDtypeStruct((nb, d), F32)),
        compiler_params=_cparams(),
    )(c_all, w_shard, b_shard)


def _rms_mod_fwd(x, gain, scale, shift):
    s, d = x.shape
    tm = min(512, s)

    def body(x_ref, g_ref, sc_ref, sh_ref, h_ref):
        xv = x_ref[...]
        r = lax.rsqrt(jnp.mean(xv * xv, axis=-1, keepdims=True) + NORM_EPS)
        h_ref[...] = (xv * r * g_ref[...] * (1.0 + sc_ref[...]) + sh_ref[...]).astype(BF16)

    row = pl.BlockSpec((1, d), lambda i: (0, 0))
    tile = pl.BlockSpec((tm, d), lambda i: (i, 0))
    return pl.pallas_call(
        body, name="rms_mod_fwd", out_shape=jax.ShapeDtypeStruct((s, d), BF16), grid=(s // tm,),
        in_specs=[tile, row, row, row], out_specs=tile, compiler_params=_cparams(("parallel",)),
    )(x, gain, scale, shift)


def _rms_mod_bwd(x, dh, dres, gain, scale):
    s, d = x.shape
    tm = min(512, s)

    def body(x_ref, dh_ref, dres_ref, g_ref, sc_ref, dx_ref, sums_ref):
        @pl.when(pl.program_id(0) == 0)
        def _():
            sums_ref[...] = jnp.zeros_like(sums_ref)

        xv, dhv = x_ref[...], dh_ref[...]
        r = lax.rsqrt(jnp.mean(xv * xv, axis=-1, keepdims=True) + NORM_EPS)
        nrm = xv * r
        g, one_sc = g_ref[...], 1.0 + sc_ref[...]
        dn = dhv * g * one_sc
        dx_ref[...] = r * (dn - nrm * jnp.mean(dn * nrm, axis=-1, keepdims=True)) + dres_ref[...]
        dhn = dhv * nrm
        sums_ref[0:1, :] += jnp.sum(dhv, axis=0, keepdims=True)
        sums_ref[1:2, :] += jnp.sum(dhn * g, axis=0, keepdims=True)
        sums_ref[2:3, :] += jnp.sum(dhn * one_sc, axis=0, keepdims=True)

    row = pl.BlockSpec((1, d), lambda i: (0, 0))
    tile = pl.BlockSpec((tm, d), lambda i: (i, 0))
    return pl.pallas_call(
        body, name="rms_mod_bwd",
        out_shape=(jax.ShapeDtypeStruct((s, d), F32), jax.ShapeDtypeStruct((3, d), F32)), grid=(s // tm,),
        in_specs=[tile, tile, tile, row, row], out_specs=(tile, pl.BlockSpec((3, d), lambda i: (0, 0))),
        compiler_params=_cparams(("arbitrary",)),
    )(x, dh, dres, gain, scale)


def _loss_head(x, mixed, gate, gain_f, target):
    s, d = x.shape
    tm = min(512, s)

    def body(x_ref, mx_ref, gt_ref, gf_ref, t_ref, dx2_ref, dmx_ref, sums_ref):
        @pl.when(pl.program_id(0) == 0)
        def _():
            sums_ref[...] = jnp.zeros_like(sums_ref)

        mx, gt, gf = mx_ref[...], gt_ref[...], gf_ref[...]
        x2 = x_ref[...] + gt * mx
        r = lax.rsqrt(jnp.mean(x2 * x2, axis=-1, keepdims=True) + NORM_EPS)
        nrm = x2 * r
        err = nrm * gf - t_ref[...]
        dyf = err * (1.0 / d)
        dn = dyf * gf
        dx2 = r * (dn - nrm * jnp.mean(dn * nrm, axis=-1, keepdims=True))
        dx2_ref[...] = dx2
        dmx_ref[...] = (dx2 * gt).astype(BF16)
        sums_ref[0:1, :] += jnp.sum(err * err, axis=0, keepdims=True)
        sums_ref[1:2, :] += jnp.sum(dyf * nrm, axis=0, keepdims=True)
        sums_ref[2:3, :] += jnp.sum(dx2 * mx, axis=0, keepdims=True)

    row = pl.BlockSpec((1, d), lambda i: (0, 0))
    tile = pl.BlockSpec((tm, d), lambda i: (i, 0))
    return pl.pallas_call(
        body, name="loss_head",
        out_shape=(jax.ShapeDtypeStruct((s, d), F32), jax.ShapeDtypeStruct((s, d), BF16),
                   jax.ShapeDtypeStruct((3, d), F32)),
        grid=(s // tm,), in_specs=[tile, tile, row, row, tile],
        out_specs=(tile, tile, pl.BlockSpec((3, d), lambda i: (0, 0))),
        compiler_params=_cparams(("arbitrary",)),
    )(x, mixed, gate, gain_f, target)


def _silu_grad(z, sg):
    return sg * (1.0 + z * (1.0 - sg))


def _gated_norm_fwd(o, proj, z_off, gain, gate_inside, name):
    s, d = o.shape
    tm = min(512, s)
    zb = z_off // d

    def body(o_ref, z_ref, g_ref, y_ref):
        z = z_ref[...].astype(F32)
        sz = z * _sigmoid(z)
        u = o_ref[...] * sz if gate_inside else o_ref[...]
        r = lax.rsqrt(jnp.mean(u * u, axis=-1, keepdims=True) + NORM_EPS)
        y = u * r * g_ref[...]
        y_ref[...] = (y if gate_inside else y * sz).astype(BF16)

    tile = pl.BlockSpec((tm, d), lambda i: (i, 0))
    return pl.pallas_call(
        body, name=name, out_shape=jax.ShapeDtypeStruct((s, d), BF16), grid=(s // tm,),
        in_specs=[tile, pl.BlockSpec((tm, d), lambda i: (i, zb)), pl.BlockSpec((1, d), lambda i: (0, 0))],
        out_specs=tile, compiler_params=_cparams(("parallel",)),
    )(o, proj, gain)


def _gated_norm_bwd(dy_all, dy_blk, o, proj, z_off, gain, gate_inside, name):
    s, d = o.shape
    tm = min(512, s)
    zb = z_off // d

    def body(dy_ref, o_ref, z_ref, g_ref, do_ref, dz_ref, dg_ref):
        @pl.when(pl.program_id(0) == 0)
        def _():
            dg_ref[...] = jnp.zeros_like(dg_ref)

        z = z_ref[...].astype(F32)
        sg = _sigmoid(z)
        sz = z * sg
        ov, dy, g = o_ref[...], dy_ref[...].astype(F32), g_ref[...]
        u = ov * sz if gate_inside else ov
        r = lax.rsqrt(jnp.mean(u * u, axis=-1, keepdims=True) + NORM_EPS)
        nrm = u * r
        if gate_inside:
            dg_ref[...] += jnp.sum(dy * nrm, axis=0, keepdims=True)
            dn = dy * g
        else:
            dg_ref[...] += jnp.sum(dy * nrm * sz, axis=0, keepdims=True)
            dn = dy * g * sz
        du = r * (dn - nrm * jnp.mean(dn * nrm, axis=-1, keepdims=True))
        if gate_inside:
            do_ref[...] = du * sz
            dz_ref[...] = (du * ov * _silu_grad(z, sg)).astype(BF16)
        else:
            do_ref[...] = du
            dz_ref[...] = (dy * nrm * g * _silu_grad(z, sg)).astype(BF16)

    tile = pl.BlockSpec((tm, d), lambda i: (i, 0))
    row = pl.BlockSpec((1, d), lambda i: (0, 0))
    return pl.pallas_call(
        body, name=name,
        out_shape=(jax.ShapeDtypeStruct((s, d), F32), jax.ShapeDtypeStruct((s, d), BF16),
                   jax.ShapeDtypeStruct((1, d), F32)),
        grid=(s // tm,),
        in_specs=[pl.BlockSpec((tm, d), lambda i: (i, dy_blk)), tile, pl.BlockSpec((tm, d), lambda i: (i, zb)), row],
        out_specs=(tile, tile, row), compiler_params=_cparams(("arbitrary",)),
    )(dy_all, o, proj, gain)


def _sb_logits(qh, kb):
    z = lax.dot_general(qh, kb, NT, preferred_element_type=F32)
    neg_abs = lax.bitcast_convert_type(lax.bitcast_convert_type(z, jnp.uint32) | jnp.uint32(0x80000000), F32)
    lb = jnp.minimum(z, 0.0) - jnp.log(1.0 + jnp.exp(neg_abs))
    return lb, lb - z


def _attn_consts(tk):
    lane = lax.broadcasted_iota(jnp.int32, (1, LANE), 1)
    row = lax.broadcasted_iota(jnp.int32, (tk, tk), 0)
    col = lax.broadcasted_iota(jnp.int32, (tk, tk), 1)
    return (lane < HEAD_DIM, lane >= HEAD_DIM), row, col


def _band_mask(rows, tk):
    return lax.broadcasted_iota(jnp.int32, (rows, tk), 1) < lax.broadcasted_iota(jnp.int32, (rows, tk), 0)


def _attn_fwd(proj):
    s = proj.shape[0]
    tq, tk = min(ATTN_TQ, s), min(ATTN_TK, s)
    r = tq // tk

    def body(q_ref, k_ref, v_ref, o_ref, l_ref, acc_ref, run_ref):
        i = pl.program_id(1)
        head_mask, row, col = _attn_consts(tk)
        later = (row > col).astype(BF16)
        q = q_ref[...] * ATTN_SCALE
        qh = [jnp.where(m, q, jnp.zeros_like(q)) for m in head_mask]
        acc_ref[...] = jnp.zeros_like(acc_ref)
        run_ref[...] = jnp.zeros_like(run_ref)

        def block(j, lo, hi, band):
            start = pl.multiple_of(j * tk, tk)
            kb = k_ref[pl.ds(start, tk), :]
            vb = v_ref[pl.ds(start, tk), :]
            rows = slice(lo, hi)
            causal = _band_mask(hi - lo, tk) if band else None
            hs = range(2)
            logits = [_sb_logits(qh[h][rows], kb) for h in hs]
            lb = [logits[h][0] for h in hs]
            l1m = [logits[h][1] if causal is None else jnp.where(causal, logits[h][1], 0.0) for h in hs]
            tail = [jnp.dot(l1m[h].astype(BF16), later, preferred_element_type=F32) + run_ref[h, rows] for h in hs]
            w = [jnp.exp(lb[h] + tail[h]) for h in hs]
            if causal is not None:
                w = [jnp.where(causal, w[h], 0.0) for h in hs]
            vh = [jnp.where(head_mask[h], vb, jnp.zeros_like(vb)) for h in hs]
            acc_ref[rows, :] += (jnp.dot(w[0].astype(BF16), vh[0], preferred_element_type=F32)
                                 + jnp.dot(w[1].astype(BF16), vh[1], preferred_element_type=F32))
            for h in hs:
                run_ref[h, rows] += jnp.sum(l1m[h], axis=1, keepdims=True)

        for b in reversed(range(r)):
            block(i * r + b, b * tk, tq, True)
        n_full = i * r
        half = tq // 2

        def more(c):
            return jnp.logical_and(c[0] < n_full, c[1] > LOG_ZERO)

        def step_all(c):
            block(n_full - 1 - c[0], 0, tq, False)
            return c[0] + 1, jnp.max(run_ref[:, half:, :])

        def step_upper(c):
            block(n_full - 1 - c[0], 0, half, False)
            return c[0] + 1, jnp.max(run_ref[:, :half, :])

        seen_all, _ = lax.while_loop(more, step_all, (jnp.int32(0), jnp.max(run_ref[:, half:, :])))
        seen, _ = lax.while_loop(more, step_upper, (seen_all, jnp.max(run_ref[:, :half, :])))
        o_ref[...] = acc_ref[...]
        lane = lax.broadcasted_iota(jnp.int32, (1, LANE), 1)
        first = jnp.where(lane < 3 * HEAD_DIM // 4, n_full - seen, n_full - seen_all).astype(F32)
        l_ref[...] = jnp.where(lane < HEAD_DIM // 2, run_ref[0], jnp.where(lane < HEAD_DIM, first, run_ref[1]))

    kq, kk, kv = OFF_Q // LANE, OFF_K // LANE, OFF_V // LANE
    tile = pl.BlockSpec((tq, LANE), lambda p, i: (i, p))
    return pl.pallas_call(
        body, name="attn_fwd",
        out_shape=(jax.ShapeDtypeStruct((s, D_ATTN), F32), jax.ShapeDtypeStruct((s, D_ATTN), F32)),
        grid=(N_PAIRS, s // tq),
        in_specs=[pl.BlockSpec((tq, LANE), lambda p, i: (i, kq + p)),
                  pl.BlockSpec((s, LANE), lambda p, i: (0, kk + p)),
                  pl.BlockSpec((s, LANE), lambda p, i: (0, kv + p))],
        out_specs=(tile, tile),
        scratch_shapes=[pltpu.VMEM((tq, LANE), F32), pltpu.VMEM((2, tq, 1), F32)],
        compiler_params=_cparams(("parallel", "arbitrary")),
    )(proj, proj, proj)


def _attn_bwd(proj, do, lsum):
    s = proj.shape[0]
    tq, tk = min(ATTN_TQ, s), min(ATTN_TK, s)
    r = tq // tk

    def body(q_ref, k_ref, v_ref, do_ref, l_ref, dq_ref, dk_ref, dv_ref, dqacc_ref, dkacc_ref, dvacc_ref,
             passed_ref, pre_ref):
        i = pl.program_id(1)

        @pl.when(i == 0)
        def _():
            dkacc_ref[...] = jnp.zeros_like(dkacc_ref)
            dvacc_ref[...] = jnp.zeros_like(dvacc_ref)

        head_mask, row, col = _attn_consts(tk)
        later = (row > col).astype(BF16)
        earlier = (row < col).astype(BF16)
        q = q_ref[...] * ATTN_SCALE
        dov = do_ref[...].astype(BF16)
        qh = [jnp.where(m, q, jnp.zeros_like(q)) for m in head_mask]
        doh = [jnp.where(m, dov, jnp.zeros_like(dov)) for m in head_mask]
        lsum_v = l_ref[...]
        lh = [lsum_v[:, 0:1], lsum_v[:, HEAD_DIM:HEAD_DIM + 1]]
        n_full = i * r
        half = tq // 2
        quarter = HEAD_DIM // 4
        first_all = jnp.clip(jnp.max(lsum_v[0:8, 3 * quarter:HEAD_DIM]).astype(jnp.int32), 0, n_full)
        first = jnp.clip(jnp.max(lsum_v[0:8, 2 * quarter:3 * quarter]).astype(jnp.int32), 0, first_all)
        dqacc_ref[...] = jnp.zeros_like(dqacc_ref)
        passed_ref[...] = jnp.zeros_like(passed_ref)
        pre_ref[...] = jnp.zeros_like(pre_ref)

        def block(j, lo, hi, band):
            start = pl.multiple_of(j * tk, tk)
            kb = k_ref[pl.ds(start, tk), :]
            vb = v_ref[pl.ds(start, tk), :]
            rows = slice(lo, hi)
            causal = _band_mask(hi - lo, tk) if band else None
            hs = range(2)
            q_rows = [qh[h][rows] for h in hs]
            do_rows = [doh[h][rows] for h in hs]
            logits = [_sb_logits(q_rows[h], kb) for h in hs]
            lb = [logits[h][0] for h in hs]
            l1m = [logits[h][1] if causal is None else jnp.where(causal, logits[h][1], 0.0) for h in hs]
            da = [lax.dot_general(do_rows[h], vb, NT, preferred_element_type=F32) for h in hs]
            rs = [jnp.sum(l1m[h], axis=1, keepdims=True) for h in hs]
            right = [lh[h][rows] - passed_ref[h, rows] - rs[h] for h in hs]
            for h in hs:
                passed_ref[h, rows] += rs[h]
            tail = [jnp.dot(l1m[h].astype(BF16), later, preferred_element_type=F32) + right[h] for h in hs]
            a = [jnp.exp(lb[h] + tail[h]) for h in hs]
            if causal is not None:
                a = [jnp.where(causal, a[h], 0.0) for h in hs]
            g = [a[h] * da[h] for h in hs]
            pre = [jnp.dot(g[h].astype(BF16), earlier, preferred_element_type=F32) + pre_ref[h, rows] for h in hs]
            for h in hs:
                pre_ref[h, rows] += jnp.sum(g[h], axis=1, keepdims=True)
            dz = [g[h] - jnp.exp(lb[h]) * (g[h] + pre[h]) for h in hs]
            if causal is not None:
                dz = [jnp.where(causal, dz[h], 0.0) for h in hs]
            dzb = [dz[h].astype(BF16) for h in hs]
            kh = [jnp.where(head_mask[h], kb, jnp.zeros_like(kb)) * ATTN_SCALE for h in hs]
            dqacc_ref[rows, :] += (jnp.dot(dzb[0], kh[0], preferred_element_type=F32)
                                   + jnp.dot(dzb[1], kh[1], preferred_element_type=F32))
            dvacc_ref[pl.ds(start, tk), :] += (
                lax.dot_general(a[0].astype(BF16), do_rows[0], TN, preferred_element_type=F32)
                + lax.dot_general(a[1].astype(BF16), do_rows[1], TN, preferred_element_type=F32))
            dkacc_ref[pl.ds(start, tk), :] += (
                lax.dot_general(dzb[0], q_rows[0], TN, preferred_element_type=F32)
                + lax.dot_general(dzb[1], q_rows[1], TN, preferred_element_type=F32))

        def step_upper(j, carry):
            block(j, 0, half, False)
            return carry

        def step_all(j, carry):
            block(j, 0, tq, False)
            return carry

        lax.fori_loop(first, first_all, step_upper, 0)
        lax.fori_loop(first_all, n_full, step_all, 0)
        for b in range(r):
            block(n_full + b, b * tk, tq, True)
        dq_ref[...] = dqacc_ref[...].astype(BF16)

        @pl.when(i == pl.num_programs(1) - 1)
        def _():
            dk_ref[...] = dkacc_ref[...].astype(BF16)
            dv_ref[...] = dvacc_ref[...].astype(BF16)

    kq, kk, kv = OFF_Q // LANE, OFF_K // LANE, OFF_V // LANE
    tile = pl.BlockSpec((tq, LANE), lambda p, i: (i, p))
    full = pl.BlockSpec((s, LANE), lambda p, i: (0, p))
    shp = jax.ShapeDtypeStruct((s, D_ATTN), BF16)
    return pl.pallas_call(
        body, name="attn_bwd", out_shape=(shp, shp, shp), grid=(N_PAIRS, s // tq),
        in_specs=[pl.BlockSpec((tq, LANE), lambda p, i: (i, kq + p)),
                  pl.BlockSpec((s, LANE), lambda p, i: (0, kk + p)),
                  pl.BlockSpec((s, LANE), lambda p, i: (0, kv + p)),
                  tile, tile],
        out_specs=(tile, full, full),
        scratch_shapes=[pltpu.VMEM((tq, LANE), F32), pltpu.VMEM((s, LANE), F32), pltpu.VMEM((s, LANE), F32),
                        pltpu.VMEM((2, tq, 1), F32), pltpu.VMEM((2, tq, 1), F32)],
        compiler_params=_cparams(("parallel", "arbitrary")),
    )(proj, proj, proj, do, lsum)


def _shift_down(u, k, rows):
    return jnp.where(rows >= k, pltpu.roll(u, k, 0), 0.0)


def _shift_up(u, k, rows, s):
    return jnp.where(rows < s - k, pltpu.roll(u, s - k, 0), 0.0)


def _conv_fwd(proj, w, b):
    s = proj.shape[0]
    blk0 = OFF_XBC // LANE

    def body(u_ref, w_ref, b_ref, o_ref):
        u = u_ref[...].astype(F32)
        rows = lax.broadcasted_iota(jnp.int32, (s, 1), 0)
        pre = u * w_ref[CONV_K - 1:CONV_K, :] + b_ref[...]
        for k in range(1, CONV_K):
            pre += _shift_down(u, k, rows) * w_ref[CONV_K - 1 - k:CONV_K - k, :]
        o_ref[...] = pre * _sigmoid(pre)

    return pl.pallas_call(
        body, name="conv_fwd", out_shape=jax.ShapeDtypeStruct((s, D_XBC), F32), grid=(D_XBC // LANE,),
        in_specs=[pl.BlockSpec((s, LANE), lambda j: (0, blk0 + j)), pl.BlockSpec((CONV_K, LANE), lambda j: (0, j)),
                  pl.BlockSpec((1, LANE), lambda j: (0, j))],
        out_specs=pl.BlockSpec((s, LANE), lambda j: (0, j)), compiler_params=_cparams(("parallel",)),
    )(proj, w, b)


def _conv_bwd(proj, w, b, dact):
    s = proj.shape[0]
    blk0 = OFF_XBC // LANE

    def body(u_ref, w_ref, b_ref, da_ref, du_ref, dw_ref, db_ref):
        u = u_ref[...].astype(F32)
        rows = lax.broadcasted_iota(jnp.int32, (s, 1), 0)
        shifted = [u] + [_shift_down(u, k, rows) for k in range(1, CONV_K)]
        pre = b_ref[...] + shifted[0] * w_ref[CONV_K - 1:CONV_K, :]
        for k in range(1, CONV_K):
            pre += shifted[k] * w_ref[CONV_K - 1 - k:CONV_K - k, :]
        sg = _sigmoid(pre)
        dpre = da_ref[...] * _silu_grad(pre, sg)
        db_ref[...] = jnp.sum(dpre, axis=0, keepdims=True)
        du = dpre * w_ref[CONV_K - 1:CONV_K, :]
        for k in range(CONV_K):
            dw_ref[CONV_K - 1 - k:CONV_K - k, :] = jnp.sum(dpre * shifted[k], axis=0, keepdims=True)
            if k:
                du += _shift_up(dpre, k, rows, s) * w_ref[CONV_K - 1 - k:CONV_K - k, :]
        du_ref[...] = du.astype(BF16)

    col = pl.BlockSpec((s, LANE), lambda j: (0, j))
    return pl.pallas_call(
        body, name="conv_bwd",
        out_shape=(jax.ShapeDtypeStruct((s, D_XBC), BF16), jax.ShapeDtypeStruct((CONV_K, D_XBC), F32),
                   jax.ShapeDtypeStruct((1, D_XBC), F32)),
        grid=(D_XBC // LANE,),
        in_specs=[pl.BlockSpec((s, LANE), lambda j: (0, blk0 + j)), pl.BlockSpec((CONV_K, LANE), lambda j: (0, j)),
                  pl.BlockSpec((1, LANE), lambda j: (0, j)), col],
        out_specs=(col, pl.BlockSpec((CONV_K, LANE), lambda j: (0, j)), pl.BlockSpec((1, LANE), lambda j: (0, j))),
        compiler_params=_cparams(("parallel",)),
    )(proj, w, b, dact)


def _ssd_decays(dtraw_ref, bias_ref, dtt_ref, biast_ref, arow_ref, acol_ref):
    ln = CHUNK
    dt = _softplus(dtraw_ref[...] + bias_ref[...])
    r = lax.broadcasted_iota(jnp.int32, (ln, ln), 0)
    c = lax.broadcasted_iota(jnp.int32, (ln, ln), 1)
    ac = jnp.dot((r >= c).astype(F32), dt * arow_ref[...], preferred_element_type=F32, precision=HI)
    dtt = _softplus(dtt_ref[...] + biast_ref[...])
    act = jnp.dot(dtt * acol_ref[...], (r <= c).astype(F32), preferred_element_type=F32, precision=HI)
    return dt, ac, act, r >= c


def _pair_cols(m0, v, h0):
    return jnp.where(m0, v[:, h0:h0 + 1], v[:, h0 + 1:h0 + 2])


def _ssd_fwd(act, dtraw, dtt, bias, biast, arow, acol, dskip):
    s = act.shape[0]
    ln = CHUNK
    nc = s // ln

    def body(act_ref, dtraw_ref, dtt_ref, bias_ref, biast_ref, arow_ref, acol_ref, dsk_ref, y_ref, st_ref,
             state_ref):
        @pl.when(pl.program_id(0) == 0)
        def _():
            state_ref[...] = jnp.zeros_like(state_ref)

        dt, ac, act_t, lower = _ssd_decays(dtraw_ref, bias_ref, dtt_ref, biast_ref, arow_ref, acol_ref)
        lane = lax.broadcasted_iota(jnp.int32, (1, LANE), 1)
        m0 = lane < HEAD_DIM
        for g in range(N_GROUPS):
            bg32 = act_ref[:, D_SSM + g * D_STATE:D_SSM + (g + 1) * D_STATE]
            bg, bg_t = bg32.astype(BF16), bg32.T.astype(BF16)
            cg = act_ref[:, D_SSM + (N_GROUPS + g) * D_STATE:D_SSM + (N_GROUPS + g + 1) * D_STATE].astype(BF16)
            cb = lax.dot_general(cg, bg, NT, preferred_element_type=F32)
            for p in range(g * 4, g * 4 + 4):
                h0 = 2 * p
                xp = act_ref[:, p * LANE:(p + 1) * LANE]
                xdt = xp * _pair_cols(m0, dt, h0)
                acp = _pair_cols(m0, ac, h0)
                last = acp[ln - 1:ln, :]
                y = xp * dsk_ref[:, p * LANE:(p + 1) * LANE]
                for hh in range(2):
                    h = h0 + hh
                    dm = jnp.exp(jnp.where(lower, ac[:, h:h + 1] - act_t[h:h + 1, :], -jnp.inf))
                    mask = m0 if hh == 0 else jnp.logical_not(m0)
                    y += jnp.dot((cb * dm).astype(BF16), jnp.where(mask, xdt, 0.0).astype(BF16),
                                 preferred_element_type=F32)
                prev = state_ref[p]
                st_ref[0, p] = prev
                y += jnp.dot(cg, prev.astype(BF16), preferred_element_type=F32) * jnp.exp(acp)
                y_ref[:, p * LANE:(p + 1) * LANE] = y
                cs = jnp.dot(bg_t, (xdt * jnp.exp(last - acp)).astype(BF16), preferred_element_type=F32)
                state_ref[p] = prev * jnp.exp(last) + cs

    row = lambda w: pl.BlockSpec((1, w), lambda c: (0, 0))
    return pl.pallas_call(
        body, name="ssd_fwd",
        out_shape=(jax.ShapeDtypeStruct((s, D_SSM), F32),
                   jax.ShapeDtypeStruct((nc, N_PAIRS, LANE, D_STATE), F32)),
        grid=(nc,),
        in_specs=[pl.BlockSpec((ln, D_XBC), lambda c: (c, 0)), pl.BlockSpec((ln, LANE), lambda c: (c, 0)),
                  pl.BlockSpec((N_HEADS, ln), lambda c: (0, c)), row(LANE),
                  pl.BlockSpec((N_HEADS, 1), lambda c: (0, 0)), row(LANE),
                  pl.BlockSpec((N_HEADS, 1), lambda c: (0, 0)), row(D_SSM)],
        out_specs=(pl.BlockSpec((ln, D_SSM), lambda c: (c, 0)),
                   pl.BlockSpec((1, N_PAIRS, LANE, D_STATE), lambda c: (c, 0, 0, 0))),
        scratch_shapes=[pltpu.VMEM((N_PAIRS, LANE, D_STATE), F32)],
        compiler_params=_cparams(("arbitrary",)),
    )(act, dtraw, dtt, bias, biast, arow, acol, dskip)


def _ssd_bwd(act, dtraw, dtt, bias, biast, arow, acol, dskip, states, dy):
    s = act.shape[0]
    ln = CHUNK
    nc = s // ln

    def body(act_ref, dtraw_ref, dtt_ref, bias_ref, biast_ref, arow_ref, acol_ref, dsk_ref, st_ref, dy_ref,
             dact_ref, dldc_ref, dldr_ref, ddt_ref, dd_ref, dstate_ref):
        @pl.when(pl.program_id(0) == 0)
        def _():
            dstate_ref[...] = jnp.zeros_like(dstate_ref)
            dd_ref[...] = jnp.zeros_like(dd_ref)

        dt, ac, act_t, lower = _ssd_decays(dtraw_ref, bias_ref, dtt_ref, biast_ref, arow_ref, acol_ref)
        lane = lax.broadcasted_iota(jnp.int32, (1, LANE), 1)
        m0 = lane < HEAD_DIM
        halves = (m0, jnp.logical_not(m0))
        is_last = lax.broadcasted_iota(jnp.int32, (ln, 1), 0) == ln - 1
        sub = lax.broadcasted_iota(jnp.int32, (N_HEADS, 1), 0)
        earlier_eq = jnp.logical_not(lower) | (lax.broadcasted_iota(jnp.int32, (ln, ln), 0)
                                               == lax.broadcasted_iota(jnp.int32, (ln, ln), 1))
        dac_col = jnp.zeros((ln, LANE), F32)
        dac_row = jnp.zeros((N_HEADS, ln), F32)
        ddt_col = jnp.zeros((ln, LANE), F32)

        def half_sum(v, hh):
            return jnp.sum(jnp.where(halves[hh], v, 0.0), axis=1, keepdims=True)

        for g in range(N_GROUPS):
            b_lo, c_lo = D_SSM + g * D_STATE, D_SSM + (N_GROUPS + g) * D_STATE
            bg32 = act_ref[:, b_lo:b_lo + D_STATE]
            cg32 = act_ref[:, c_lo:c_lo + D_STATE]
            bg, cg = bg32.astype(BF16), cg32.astype(BF16)
            cg_t = cg32.T.astype(BF16)
            cb_t = lax.dot_general(bg, cg, NT, preferred_element_type=F32)
            dcb_t = jnp.zeros((ln, ln), F32)
            dbg = jnp.zeros((ln, D_STATE), F32)
            dcg = jnp.zeros((ln, D_STATE), F32)
            for p in range(g * 4, g * 4 + 4):
                h0 = 2 * p
                cols = slice(p * LANE, (p + 1) * LANE)
                xp = act_ref[:, cols]
                dyp = dy_ref[:, cols]
                dtp = _pair_cols(m0, dt, h0)
                acp = _pair_cols(m0, ac, h0)
                last = acp[ln - 1:ln, :]
                xdt = xp * dtp
                eac = jnp.exp(acp)
                dte = jnp.exp(last - acp)
                dec = jnp.exp(last)
                prev = st_ref[0, p]
                prev_b = prev.astype(BF16)
                ds = dstate_ref[p]
                ds_b = ds.astype(BF16)

                dd_ref[:, cols] += jnp.sum(dyp * xp, axis=0, keepdims=True)
                dx = dyp * dsk_ref[:, cols]
                zoff = jnp.dot(cg, prev_b, preferred_element_type=F32)
                dz_b = (dyp * eac).astype(BF16)
                dcg += lax.dot_general(dz_b, prev_b, NT, preferred_element_type=F32)
                dprev = jnp.dot(cg_t, dz_b, preferred_element_type=F32) + ds * dec
                wmat = jnp.dot(bg, ds_b, preferred_element_type=F32)
                xdte_b = (xdt * dte).astype(BF16)
                dbg += lax.dot_general(xdte_b, ds_b, NT, preferred_element_type=F32)
                dxdt = dte * wmat
                t_dte = xdt * wmat * dte
                t_ac = dyp * zoff * eac - t_dte
                at_last = jnp.sum(ds * prev, axis=0, keepdims=True) * dec + jnp.sum(t_dte, axis=0, keepdims=True)
                for hh in range(2):
                    h = h0 + hh
                    here = lane == h
                    dm_t = jnp.exp(jnp.where(earlier_eq, act_t[h:h + 1, :] - ac[:, h:h + 1], -jnp.inf))
                    mm_t = cb_t * dm_t
                    dyh = jnp.where(halves[hh], dyp, 0.0).astype(BF16)
                    xdth = jnp.where(halves[hh], xdt, 0.0).astype(BF16)
                    dmm_t = lax.dot_general(xdth, dyh, NT, preferred_element_type=F32)
                    dxdt += jnp.dot(mm_t.astype(BF16), dyh, preferred_element_type=F32)
                    gm_t = dmm_t * mm_t
                    dcb_t += dmm_t * dm_t
                    dac_col += jnp.where(here, half_sum(t_ac, hh) - jnp.sum(gm_t, axis=1, keepdims=True), 0.0)
                    dac_col += jnp.where(jnp.logical_and(is_last, here), half_sum(at_last, hh), 0.0)
                    dac_row += jnp.where(sub == h, jnp.sum(gm_t, axis=0, keepdims=True), 0.0)
                    ddt_col += jnp.where(here, half_sum(dxdt * xp, hh), 0.0)
                dact_ref[:, cols] = dx + dxdt * dtp
                dstate_ref[p] = dprev
            dcb_tb = dcb_t.astype(BF16)
            dact_ref[:, b_lo:b_lo + D_STATE] = dbg + jnp.dot(dcb_tb, cg, preferred_element_type=F32)
            dact_ref[:, c_lo:c_lo + D_STATE] = dcg + lax.dot_general(dcb_tb, bg, TN, preferred_element_type=F32)

        r = lax.broadcasted_iota(jnp.int32, (ln, ln), 0)
        c = lax.broadcasted_iota(jnp.int32, (ln, ln), 1)
        dldc_ref[...] = jnp.dot((r <= c).astype(F32), dac_col, preferred_element_type=F32, precision=HI)
        dldr_ref[...] = jnp.dot(dac_row, (r >= c).astype(F32), preferred_element_type=F32, precision=HI)
        ddt_ref[...] = ddt_col

    rev = lambda c: nc - 1 - c
    row = lambda w: pl.BlockSpec((1, w), lambda c: (0, 0))
    col16 = pl.BlockSpec((N_HEADS, 1), lambda c: (0, 0))
    chunk128 = pl.BlockSpec((ln, LANE), lambda c: (rev(c), 0))
    return pl.pallas_call(
        body, name="ssd_bwd",
        out_shape=(jax.ShapeDtypeStruct((s, D_XBC), F32), jax.ShapeDtypeStruct((s, LANE), F32),
                   jax.ShapeDtypeStruct((N_HEADS, s), F32), jax.ShapeDtypeStruct((s, LANE), F32),
                   jax.ShapeDtypeStruct((1, D_SSM), F32)),
        grid=(nc,),
        in_specs=[pl.BlockSpec((ln, D_XBC), lambda c: (rev(c), 0)), chunk128,
                  pl.BlockSpec((N_HEADS, ln), lambda c: (0, rev(c))), row(LANE), col16, row(LANE), col16,
                  row(D_SSM), pl.BlockSpec((1, N_PAIRS, LANE, D_STATE), lambda c: (rev(c), 0, 0, 0)),
                  pl.BlockSpec((ln, D_SSM), lambda c: (rev(c), 0))],
        out_specs=(pl.BlockSpec((ln, D_XBC), lambda c: (rev(c), 0)), chunk128,
                   pl.BlockSpec((N_HEADS, ln), lambda c: (0, rev(c))), chunk128, row(D_SSM)),
        scratch_shapes=[pltpu.VMEM((N_PAIRS, LANE, D_STATE), F32)],
        compiler_params=_cparams(("arbitrary",)),
    )(act, dtraw, dtt, bias, biast, arow, acol, dskip, states, dy)


def _dt_bwd(dtraw, bias, arow, dld_col, dld_row_t, ddt_col):
    s = dtraw.shape[0]
    tm = min(512, s)

    def body(raw_ref, bias_ref, a_ref, dc_ref, dr_ref, dd_ref, out_ref, sums_ref):
        @pl.when(pl.program_id(0) == 0)
        def _():
            sums_ref[...] = jnp.zeros_like(sums_ref)

        raw = raw_ref[...] + bias_ref[...]
        dld = dc_ref[...] + dr_ref[...]
        ddt = dld * a_ref[...] + dd_ref[...]
        draw = ddt * _sigmoid(raw)
        out_ref[...] = draw.astype(BF16)
        sums_ref[0:1, :] += jnp.sum(draw, axis=0, keepdims=True)
        sums_ref[1:2, :] += jnp.sum(dld * _softplus(raw), axis=0, keepdims=True)

    tile = pl.BlockSpec((tm, LANE), lambda i: (i, 0))
    row = pl.BlockSpec((1, LANE), lambda i: (0, 0))
    return pl.pallas_call(
        body, name="dt_bwd",
        out_shape=(jax.ShapeDtypeStruct((s, LANE), BF16), jax.ShapeDtypeStruct((2, LANE), F32)), grid=(s // tm,),
        in_specs=[tile, row, row, tile, tile, tile], out_specs=(tile, pl.BlockSpec((2, LANE), lambda i: (0, 0))),
        compiler_params=_cparams(("arbitrary",)),
    )(dtraw, bias, arow, dld_col, dld_row_t, ddt_col)


def _sum8(parts):
    nb, n = parts.shape

    def body(p_ref, o_ref):
        acc = p_ref[0:1, :]
        for b in range(1, nb):
            acc = acc + p_ref[b:b + 1, :]
        o_ref[...] = acc

    return pl.pallas_call(body, name="sum8", out_shape=jax.ShapeDtypeStruct((1, n), F32),
                          compiler_params=_cparams())(parts)


def _outer8(act_t, dmod):
    d, nb = act_t.shape
    n = dmod.shape[1]

    def body(a_ref, m_ref, o_ref):
        acc = a_ref[:, 0:1] * m_ref[0:1, :]
        for b in range(1, nb):
            acc = acc + a_ref[:, b:b + 1] * m_ref[b:b + 1, :]
        o_ref[...] = acc

    return pl.pallas_call(body, name="outer8", out_shape=jax.ShapeDtypeStruct((d, n), F32),
                          compiler_params=_cparams())(act_t, dmod)


def _pad_lanes(v, width=LANE):
    return jnp.pad(v, ((0, 0), (0, width - v.shape[1])))


def kernel(x, c, w_ada, b_ada, norm_in_gain, w_in, conv_w, conv_b, dt_bias, a_log, d_skip, sb_norm_gain, ssm_norm_gain, w_out, norm_f_gain, loss_target, m_w_ada, m_b_ada, m_norm_in_gain, m_w_in, m_conv_w, m_conv_b, m_dt_bias, m_a_log, m_d_skip, m_sb_norm_gain, m_ssm_norm_gain, m_w_out, m_norm_f_gain, v_w_ada, v_b_ada, v_norm_in_gain, v_w_in, v_conv_w, v_conv_b, v_dt_bias, v_a_log, v_d_skip, v_sb_norm_gain, v_ssm_norm_gain, v_w_out, v_norm_f_gain):
    ax, ay, ac_ = _coords()
    chip = 2 * ax + ay
    me = 2 * chip + ac_
    my_c = jnp.reshape(ac_, (1,)).astype(jnp.int32)
    x2d, tgt = x[0], loss_target[0]
    s = x2d.shape[0]
    ada_cols = w_ada.shape[2]
    cw_cols = conv_w.shape[2]
    in_cols = w_in.shape[2]
    out_rows = w_out.shape[1]

    small = jnp.concatenate([c, conv_w[0].reshape(1, CONV_K * cw_cols)], axis=1)
    small_all = _allgather8(small, "gather_cond")[:, 0, :]
    c_all = small_all[:, :D_MODEL]
    conv_w_full = (small_all[0::2, D_MODEL:].reshape(N_CHIPS, CONV_K, cw_cols)
                   .transpose(1, 0, 2).reshape(CONV_K, D_XBC))
    b_ada_shard = lax.dynamic_slice_in_dim(b_ada, chip * ada_cols, ada_cols, axis=1)
    mod_part, c_act_all = _ada_mod(c_all, w_ada[0], b_ada_shard)
    mod_all = _allgather8(mod_part.reshape(1, N_DEV * ada_cols), "gather_mod")[0::2, 0, :]
    mod_all = mod_all.reshape(N_CHIPS, N_DEV, ada_cols)
    mod = lax.dynamic_index_in_dim(mod_all, me, axis=1, keepdims=False).reshape(1, 3 * D_MODEL)
    shift, scale, gate = mod[:, :D_MODEL], mod[:, D_MODEL:2 * D_MODEL], mod[:, 2 * D_MODEL:]

    w_in_mine, w_out_mine = w_in[0].T.astype(BF16), w_out[0].astype(BF16)
    w_in_all, w_out_all = _gather_shards([w_in_mine, w_out_mine], "gather_weights")
    w_in_all = lax.dynamic_update_slice(w_in_all, w_in_mine[None], (chip, 0, 0))
    w_out_all = lax.dynamic_update_slice(w_out_all, w_out_mine[None], (chip, 0, 0))
    w_in_t = w_in_all.reshape(D_PROJ, D_MODEL)
    w_zs_t = w_in_t[ZS_LO:]
    w_dt_t = jnp.pad(w_in_t[DT_LO:ZS_LO], ((0, LANE - N_HEADS), (0, 0)))
    w_out_full = w_out_all.reshape(N_CHIPS * out_rows, D_MODEL)

    h = _rms_mod_fwd(x2d, norm_in_gain, scale, shift)
    proj = _matmul(h, w_in_t, BF16, "in_proj", "nt", 1024, 512, 1024, n_out=D_MAIN)
    proj_zs = _matmul(h, w_zs_t, BF16, "in_proj_zs", "nt", 1024, 512, 1024)
    dtraw = _matmul(h, w_dt_t, F32, "in_proj_dt", "nt", 1024, LANE, 1024)
    o_attn, lsum = _attn_fwd(proj)
    y_attn = _gated_norm_fwd(o_attn, proj, OFF_ZA, sb_norm_gain, False, "attn_gate_fwd")
    act = _conv_fwd(proj, conv_w_full, conv_b)
    a_neg = -jnp.exp(a_log)
    arow, acol = _pad_lanes(a_neg), a_neg.reshape(N_HEADS, 1)
    bias_row, bias_col = _pad_lanes(dt_bias), dt_bias.reshape(N_HEADS, 1)
    dtt = dtraw[:, :N_HEADS].T
    dskip_row = jnp.repeat(d_skip, HEAD_DIM, axis=1)
    ssd_args = (act, dtraw, dtt, bias_row, bias_col, arow, acol, dskip_row)
    y_ssd, states = _ssd_fwd(*ssd_args)
    y_ssm = _gated_norm_fwd(y_ssd, proj_zs, 0, ssm_norm_gain, True, "ssm_gate_fwd")
    mixed = _matmul_sum([(y_attn, D_ATTN, 0, w_out_full, 0), (y_ssm, D_SSM, 0, w_out_full, 1)], F32, "out_proj",
                        1024, 1024)

    dx2, dmixed, head_sums = _loss_head(x2d, mixed, gate, norm_f_gain.reshape(1, D_MODEL), tgt)
    g_w_out = _matmul_tn_pieces([y_attn, y_ssm], N_CHIPS * out_rows, dmixed, "out_proj_dw", 512)
    d_mix_in = _matmul(dmixed, w_out_full, BF16, "out_proj_dx", "nt", 1024, 1024, 1024)
    d_o, dz_attn, g_sb = _gated_norm_bwd(d_mix_in, 0, o_attn, proj, OFF_ZA, sb_norm_gain, False, "attn_gate_bwd")
    d_y, dz_ssm, g_ssm = _gated_norm_bwd(d_mix_in, 1, y_ssd, proj_zs, 0, ssm_norm_gain, True, "ssm_gate_bwd")
    dq, dk, dv = _attn_bwd(proj, d_o, lsum)
    dact, dld_col, dld_row, ddt_col, dd_cols = _ssd_bwd(*ssd_args, states, d_y)
    dxbc, g_conv_w, g_conv_b = _conv_bwd(proj, conv_w_full, conv_b, dact)
    ddtraw, dt_sums = _dt_bwd(dtraw, bias_row, arow, dld_col, _pad_lanes(dld_row.T), ddt_col)
    g_in_t = _matmul_tn_pieces([dq, dk, dv, dz_attn, dxbc], D_PROJ, h, "in_proj_dw", 256)
    g_in_t = _matmul_tn_rows(g_in_t, D_PROJ, ddtraw, h, DT_LO // LANE, "in_proj_dw_dt", LANE)
    g_zs_t = _matmul(dz_ssm, h, F32, "in_proj_dw_zs", "tn", 512, 1024, 4096)
    g_in_t = lax.dynamic_update_slice(g_in_t, g_zs_t, (ZS_LO, 0))
    dh_terms = [(dq, D_ATTN, 0, w_in_t, 0), (dk, D_ATTN, 0, w_in_t, 1), (dv, D_ATTN, 0, w_in_t, 2),
                (dz_attn, D_ATTN, 0, w_in_t, 3)]
    dh_terms += [(dxbc, 512, j, w_in_t, OFF_XBC // 512 + j) for j in range(D_XBC // 512)]
    dh_terms += [(dz_ssm, D_SSM, 0, w_zs_t, 0), (ddtraw, LANE, 0, w_dt_t, 0)]
    g_in_blocks = g_in_t.reshape(N_CHIPS, in_cols, D_MODEL)
    g_out_blocks = g_w_out.reshape(N_CHIPS, out_rows, D_MODEL)
    land_in, land_out = _send_to_sibling([g_in_blocks, g_out_blocks], "grads_to_sibling")
    chip_in = _add_my_half(g_in_blocks, land_in, my_c, "add_sibling_in")
    chip_out = _add_my_half(g_out_blocks, land_out, my_c, "add_sibling_out")
    dh, slots_in, slots_out = _matmul_sum(dh_terms, F32, "in_proj_dx_and_grads_between_chips", 512, 1024,
                                          exchange=[chip_in, chip_out])
    grad_x, in_sums = _rms_mod_bwd(x2d, dh, dx2, norm_in_gain, scale)

    g_a_log = dt_sums[1:2, :N_HEADS] * a_neg
    g_d_skip = jnp.sum(dd_cols.reshape(N_HEADS, HEAD_DIM), axis=1).reshape(1, N_HEADS)
    dmod = jnp.concatenate([in_sums[0:1], in_sums[1:2], head_sums[2:3]], axis=1)
    loss_part = 0.5 / D_MODEL * jnp.sum(head_sums[0:1], axis=1, keepdims=True)
    pieces = [dmod, in_sums[2:3], g_conv_w.reshape(1, CONV_K * D_XBC), g_conv_b, _pad_lanes(dt_sums[0:1, :N_HEADS]),
              _pad_lanes(g_a_log), _pad_lanes(g_d_skip), g_sb, g_ssm, head_sums[1:2], _pad_lanes(loss_part)]
    widths = [p.shape[1] for p in pieces]
    parts_all = _allgather8(jnp.concatenate(pieces, axis=1), "gather_small_grads")[:, 0, :]
    total = _sum8(parts_all)
    offs = [0]
    for w_ in widths:
        offs.append(offs[-1] + w_)
    tot = [total[:, offs[i]:offs[i + 1]] for i in range(len(pieces))]
    g_b_ada, g_norm_in, g_conv_w_full = tot[0], tot[1], tot[2].reshape(CONV_K, D_XBC)
    g_conv_b_t, g_dt_bias, g_a_log_t, g_d_skip_t = tot[3], tot[4][:, :N_HEADS], tot[5][:, :N_HEADS], tot[6][:, :N_HEADS]
    g_sb_t, g_ssm_t, g_norm_f, loss = tot[7], tot[8], tot[9], tot[10][0, 0]
    g_conv_w_shard = lax.dynamic_slice_in_dim(g_conv_w_full, chip * cw_cols, cw_cols, axis=1)
    dmod_shard = lax.dynamic_slice_in_dim(parts_all[:, :3 * D_MODEL], chip * ada_cols, ada_cols, axis=1)
    g_w_ada = _outer8(c_act_all.T, dmod_shard)

    own = lambda blocks: lax.dynamic_slice_in_dim(blocks, chip, 1, axis=0)
    slots_in = lax.dynamic_update_slice(slots_in, own(chip_in), (chip, 0, 0))
    slots_out = lax.dynamic_update_slice(slots_out, own(chip_out), (chip, 0, 0))
    half_in, half_out = _sum_slots(slots_in, "sum_chips_in"), _sum_slots(slots_out, "sum_chips_out")
    their_in, their_out = _swap_with_sibling([half_in, half_out], "grads_swap_sibling")
    south = ac_ == 0
    both = lambda mine, theirs: jnp.concatenate([jnp.where(south, mine, theirs), jnp.where(south, theirs, mine)],
                                                axis=1)
    g_w_in_t, g_w_out_shard = both(half_in, their_in), both(half_out, their_out)

    d_w_ada, nm_w_ada, nv_w_ada = _adamw(w_ada[0], g_w_ada, m_w_ada[0], v_w_ada[0], "adamw_w_ada")
    d_w_in, nm_w_in, nv_w_in = [r.T for r in _adamw(w_in[0].T, g_w_in_t, m_w_in[0].T, v_w_in[0].T, "adamw_w_in")]
    g_w_in = g_w_in_t.T
    d_w_out, nm_w_out, nv_w_out = _adamw(w_out[0], g_w_out_shard, m_w_out[0], v_w_out[0], "adamw_w_out")
    flat = lambda a: a.reshape(1, -1)
    small_w = [b_ada, norm_in_gain, conv_w[0], conv_b, dt_bias, a_log, d_skip, sb_norm_gain, ssm_norm_gain,
               norm_f_gain]
    small_m = [m_b_ada, m_norm_in_gain, m_conv_w[0], m_conv_b, m_dt_bias, m_a_log, m_d_skip, m_sb_norm_gain,
               m_ssm_norm_gain, m_norm_f_gain]
    small_v = [v_b_ada, v_norm_in_gain, v_conv_w[0], v_conv_b, v_dt_bias, v_a_log, v_d_skip, v_sb_norm_gain,
               v_ssm_norm_gain, v_norm_f_gain]
    small_g = [g_b_ada, g_norm_in, g_conv_w_shard, g_conv_b_t, g_dt_bias, g_a_log_t, g_d_skip_t, g_sb_t, g_ssm_t,
               g_norm_f]
    cat = lambda arrs: jnp.concatenate([flat(a) for a in arrs], axis=1)
    d_small, nm_small, nv_small = _adamw(cat(small_w), cat(small_g), cat(small_m), cat(small_v), "adamw_small")
    sizes = [a.size for a in small_w]
    soffs = [0]
    for n_ in sizes:
        soffs.append(soffs[-1] + n_)

    def split(packed):
        return [packed[0, soffs[i]:soffs[i + 1]].reshape(small_w[i].shape) for i in range(len(small_w))]

    def ordered(big_ada, big_in, big_out, smalls):
        (s_b_ada, s_norm_in, s_conv_w, s_conv_b, s_dt_bias, s_a_log, s_d_skip, s_sb, s_ssm, s_norm_f) = smalls
        return [big_ada[None], s_b_ada, s_norm_in, big_in[None], s_conv_w[None], s_conv_b, s_dt_bias, s_a_log,
                s_d_skip, s_sb, s_ssm, big_out[None], s_norm_f]

    grads = ordered(g_w_ada, g_w_in, g_w_out_shard,
                    [g.reshape(w_.shape) for g, w_ in zip(small_g, small_w)])
    deltas = ordered(d_w_ada, d_w_in, d_w_out, split(d_small))
    new_m = ordered(nm_w_ada, nm_w_in, nm_w_out, split(nm_small))
    new_v = ordered(nv_w_ada, nv_w_in, nv_w_out, split(nv_small))
    return (loss, grad_x[None], *grads, *deltas, *new_m, *new_v)
```

```python
import functools

import jax
import jax.numpy as jnp
from jax import lax
from jax.experimental import pallas as pl
from jax.experimental.pallas import tpu as pltpu

F32, BF16 = jnp.float32, jnp.bfloat16
MESH = pl.DeviceIdType.MESH
HI = lax.Precision.HIGHEST
NN = (((1,), (0,)), ((), ()))
NT = (((1,), (1,)), ((), ()))
TN = (((0,), (0,)), ((), ()))

D_MODEL = 1024
D_ATTN = 1024
D_SSM = 1024
HEAD_DIM = 64
N_HEADS = 16
N_PAIRS = 8
N_GROUPS = 2
D_STATE = 128
D_XBC = 1536
D_PROJ = 6672
D_MAIN = 5632
CONV_K = 4
CHUNK = 128
LANE = 128
N_CHIPS = 4
N_DEV = 8
NORM_EPS = 1e-6
ATTN_SCALE = HEAD_DIM ** -0.5
ATTN_TQ = 512
ATTN_TK = 256
LOG_ZERO = -110.0
ADAM_LR, ADAM_B1, ADAM_B2, ADAM_EPS, ADAM_WD, ADAM_STEP = 0.001, 0.9, 0.999, 1e-08, 0.01, 10
VMEM_LIMIT = 56 * 1024 * 1024

OFF_Q, OFF_K, OFF_V, OFF_ZA, OFF_XBC = 0, 1024, 2048, 3072, 4096
DT_LO = D_MAIN
ZS_LO = DT_LO + N_HEADS


def _cparams(sem=None):
    return pltpu.CompilerParams(dimension_semantics=sem, vmem_limit_bytes=VMEM_LIMIT)


def _sigmoid(x):
    return 1.0 / (1.0 + jnp.exp(-x))


def _softplus(x):
    return jnp.maximum(x, 0.0) + jnp.log(1.0 + jnp.exp(-jnp.abs(x)))


def _coords():
    return lax.axis_index("x"), lax.axis_index("y"), lax.axis_index("c")


def _allgather8(v, name):
    n = v.shape[-1]

    def body(v_ref, out_ref, send_sems, recv_sems, local_sem):
        x, y, c = _coords()
        me = 4 * x + 2 * y + c
        mine = pltpu.make_async_copy(v_ref, out_ref.at[me], local_sem)
        mine.start()
        sends, recvs = [], []
        for j in range(1, N_DEV):
            px = 1 - x if (j >> 2) & 1 else x
            py = 1 - y if (j >> 1) & 1 else y
            pc = 1 - c if j & 1 else c
            peer = (px, py, pc)
            sends.append(pltpu.make_async_remote_copy(
                src_ref=v_ref, dst_ref=out_ref.at[me], send_sem=send_sems.at[j - 1],
                recv_sem=recv_sems.at[j - 1], device_id=peer, device_id_type=MESH))
            recvs.append(pltpu.make_async_remote_copy(
                src_ref=v_ref, dst_ref=out_ref.at[4 * px + 2 * py + pc], send_sem=send_sems.at[j - 1],
                recv_sem=recv_sems.at[j - 1], device_id=peer, device_id_type=MESH))
        for s in sends:
            s.start()
        for r in recvs:
            r.wait_recv()
        for s in sends:
            s.wait_send()
        mine.wait()

    vm = pl.BlockSpec(memory_space=pltpu.VMEM)
    return pl.pallas_call(
        body, name=name, out_shape=jax.ShapeDtypeStruct((N_DEV, 1, n), F32),
        in_specs=[vm], out_specs=vm,
        scratch_shapes=[pltpu.SemaphoreType.DMA((N_DEV - 1,)), pltpu.SemaphoreType.DMA((N_DEV - 1,)),
                        pltpu.SemaphoreType.DMA(())],
    )(v)


def _other_chips(x, y):
    chips = [(1 - x, y), (x, 1 - y), (1 - x, 1 - y)]
    return chips, [2 * cx + cy for cx, cy in chips]


def _half_cols(width, which):
    half = width // 2
    return pl.ds(pl.multiple_of(which * half, half), half)


def _gather_shards(arrs, name):
    n = len(arrs)

    def body(*refs):
        ins, outs = refs[:n], refs[n:2 * n]
        send_sems, recv_sems = refs[2 * n:]
        x, y, c = _coords()
        k = 2 * x + y
        chips, chip_idx = _other_chips(x, y)
        sibling = (x, y, 1 - c)
        sends = []
        for a in range(n):
            mine = _half_cols(arrs[a].shape[-1], c)
            for j in range(3):
                cp = pltpu.make_async_remote_copy(
                    src_ref=ins[a].at[:, mine], dst_ref=outs[a].at[k, :, mine], send_sem=send_sems.at[6 * a + j],
                    recv_sem=recv_sems.at[6 * a + j], device_id=(*chips[j], c), device_id_type=MESH)
                cp.start()
                sends.append(cp)
        for a in range(n):
            mine = _half_cols(arrs[a].shape[-1], c)
            for j in range(3):
                landed = outs[a].at[chip_idx[j], :, mine]
                pltpu.make_async_remote_copy(
                    src_ref=landed, dst_ref=landed, send_sem=send_sems.at[6 * a + j],
                    recv_sem=recv_sems.at[6 * a + j], device_id=(*chips[j], c), device_id_type=MESH).wait_recv()
                fwd = pltpu.make_async_remote_copy(
                    src_ref=landed, dst_ref=landed, send_sem=send_sems.at[6 * a + 3 + j],
                    recv_sem=recv_sems.at[6 * a + 3 + j], device_id=sibling, device_id_type=MESH)
                fwd.start()
                sends.append(fwd)
        for a in range(n):
            theirs = _half_cols(arrs[a].shape[-1], 1 - c)
            for j in range(3):
                landed = outs[a].at[chip_idx[j], :, theirs]
                pltpu.make_async_remote_copy(
                    src_ref=landed, dst_ref=landed, send_sem=send_sems.at[6 * a + 3 + j],
                    recv_sem=recv_sems.at[6 * a + 3 + j], device_id=sibling, device_id_type=MESH).wait_recv()
        for cp in sends:
            cp.wait_send()

    hbm = pl.BlockSpec(memory_space=pl.ANY)
    return pl.pallas_call(
        body, name=name,
        out_shape=tuple(jax.ShapeDtypeStruct((N_CHIPS,) + a.shape, a.dtype) for a in arrs),
        in_specs=[hbm] * n, out_specs=tuple([hbm] * n),
        scratch_shapes=[pltpu.SemaphoreType.DMA((6 * n,)), pltpu.SemaphoreType.DMA((6 * n,))],
    )(*arrs)


def _send_to_sibling(arrs, name):
    n = len(arrs)

    def body(*refs):
        ins, outs = refs[:n], refs[n:2 * n]
        send_sems, recv_sems = refs[2 * n:]
        x, y, c = _coords()
        cps = []
        for a in range(n):
            cp = pltpu.make_async_remote_copy(
                src_ref=ins[a].at[:, :, _half_cols(arrs[a].shape[-1], 1 - c)], dst_ref=outs[a],
                send_sem=send_sems.at[a], recv_sem=recv_sems.at[a], device_id=(x, y, 1 - c), device_id_type=MESH)
            cp.start()
            cps.append(cp)
        for cp in cps:
            cp.wait()

    hbm = pl.BlockSpec(memory_space=pl.ANY)
    return pl.pallas_call(
        body, name=name,
        out_shape=tuple(jax.ShapeDtypeStruct(a.shape[:-1] + (a.shape[-1] // 2,), a.dtype) for a in arrs),
        in_specs=[hbm] * n, out_specs=tuple([hbm] * n),
        scratch_shapes=[pltpu.SemaphoreType.DMA((n,)), pltpu.SemaphoreType.DMA((n,))],
    )(*arrs)


def _swap_with_sibling(arrs, name):
    n = len(arrs)

    def body(*refs):
        ins, outs = refs[:n], refs[n:2 * n]
        send_sems, recv_sems = refs[2 * n:]
        x, y, c = _coords()
        cps = []
        for a in range(n):
            cp = pltpu.make_async_remote_copy(
                src_ref=ins[a], dst_ref=outs[a], send_sem=send_sems.at[a], recv_sem=recv_sems.at[a],
                device_id=(x, y, 1 - c), device_id_type=MESH)
            cp.start()
            cps.append(cp)
        for cp in cps:
            cp.wait()

    hbm = pl.BlockSpec(memory_space=pl.ANY)
    return pl.pallas_call(
        body, name=name,
        out_shape=tuple(jax.ShapeDtypeStruct(a.shape, a.dtype) for a in arrs),
        in_specs=[hbm] * n, out_specs=tuple([hbm] * n),
        scratch_shapes=[pltpu.SemaphoreType.DMA((n,)), pltpu.SemaphoreType.DMA((n,))],
    )(*arrs)


def _row_tile(rows, cols, n_arrays):
    budget = VMEM_LIMIT // 2
    t = rows
    while t % 16 == 0 and t * cols * 4 * n_arrays * 2 > budget:
        t //= 2
    return t


def _add_my_half(g, landed, my_c, name):
    nb, r, cdim = g.shape
    half = cdim // 2
    tr = _row_tile(r, half, 3)

    def body(c_ref, g_ref, l_ref, o_ref):
        o_ref[...] = (g_ref[...] + l_ref[...]).astype(BF16)

    spec = pl.BlockSpec((None, tr, half), lambda b, i, c_ref: (b, i, 0))
    return pl.pallas_call(
        body, name=name, out_shape=jax.ShapeDtypeStruct((nb, r, half), BF16),
        grid_spec=pltpu.PrefetchScalarGridSpec(
            num_scalar_prefetch=1, grid=(nb, r // tr),
            in_specs=[pl.BlockSpec((None, tr, half), lambda b, i, c_ref: (b, i, c_ref[0])), spec],
            out_specs=spec),
        compiler_params=_cparams(("parallel", "parallel")),
    )(my_c, g, landed)


def _sum_slots(a, name):
    nb, r, cdim = a.shape
    tr = _row_tile(r, cdim, 4)

    def body(a_ref, o_ref):
        o_ref[...] = ((a_ref[0].astype(F32) + a_ref[1].astype(F32)) + a_ref[2].astype(F32)) + a_ref[3].astype(F32)

    return pl.pallas_call(
        body, name=name, out_shape=jax.ShapeDtypeStruct((r, cdim), F32), grid=(r // tr,),
        in_specs=[pl.BlockSpec((nb, tr, cdim), lambda i: (0, i, 0))],
        out_specs=pl.BlockSpec((tr, cdim), lambda i: (i, 0)),
        compiler_params=_cparams(("parallel",)),
    )(a)


def _adamw(w, g, m, v, name):
    r, cdim = w.shape
    tr = _row_tile(r, cdim, 7)
    tc = cdim
    if tr == r and r > 8:
        while tc % (2 * LANE) == 0 and r * tc * 4 * 7 * 2 > VMEM_LIMIT // 2:
            tc //= 2

    def body(w_ref, g_ref, m_ref, v_ref, d_ref, nm_ref, nv_ref):
        gv = g_ref[...]
        nm = ADAM_B1 * m_ref[...] + (1.0 - ADAM_B1) * gv
        nv = ADAM_B2 * v_ref[...] + (1.0 - ADAM_B2) * (gv * gv)
        m_hat = nm / (1.0 - ADAM_B1 ** ADAM_STEP)
        v_hat = nv / (1.0 - ADAM_B2 ** ADAM_STEP)
        d_ref[...] = -ADAM_LR * (m_hat / (jnp.sqrt(v_hat) + ADAM_EPS) + ADAM_WD * w_ref[...])
        nm_ref[...] = nm
        nv_ref[...] = nv

    spec = pl.BlockSpec((tr, tc), lambda i, j: (i, j))
    shp = jax.ShapeDtypeStruct((r, cdim), F32)
    return pl.pallas_call(
        body, name=name, out_shape=(shp, shp, shp), grid=(r // tr, cdim // tc),
        in_specs=[spec] * 4, out_specs=(spec, spec, spec),
        compiler_params=_cparams(("parallel", "parallel")),
    )(w, g, m, v)


def _matmul(a, b, out_dtype, name, mode, tm, tn, tk, extra=None, n_out=None):
    dims = {"nn": NN, "nt": NT, "tn": TN}[mode]
    if mode == "tn":
        kdim, m = a.shape
    else:
        m, kdim = a.shape
    n = n_out if n_out is not None else (b.shape[0] if mode == "nt" else b.shape[1])
    tm, tn, tk = min(tm, m), min(tn, n), min(tk, kdim)
    nk = kdim // tk
    a_spec = (pl.BlockSpec((tk, tm), lambda i, j, k: (k, i)) if mode == "tn"
              else pl.BlockSpec((tm, tk), lambda i, j, k: (i, k)))
    b_spec = (pl.BlockSpec((tn, tk), lambda i, j, k: (j, k)) if mode == "nt"
              else pl.BlockSpec((tk, tn), lambda i, j, k: (k, j)))
    in_specs, operands = [a_spec, b_spec], [a, b]
    if extra is not None:
        a2, b2 = extra
        k2 = a2.shape[0] if mode == "tn" else a2.shape[1]
        in_specs.append(pl.BlockSpec((k2, tm), lambda i, j, k: (0, i)) if mode == "tn"
                        else pl.BlockSpec((tm, k2), lambda i, j, k: (i, 0)))
        in_specs.append(pl.BlockSpec((tn, k2), lambda i, j, k: (j, 0)) if mode == "nt"
                        else pl.BlockSpec((k2, tn), lambda i, j, k: (0, j)))
        operands += [a2, b2]

    def body_one_block(*refs):
        acc = lax.dot_general(refs[0][...], refs[1][...], dims, preferred_element_type=F32)
        if extra is not None:
            acc += lax.dot_general(refs[2][...], refs[3][...], dims, preferred_element_type=F32)
        refs[-1][...] = acc.astype(out_dtype)

    if nk == 1:
        return pl.pallas_call(
            body_one_block, name=name, out_shape=jax.ShapeDtypeStruct((m, n), out_dtype), grid=(m // tm, n // tn, 1),
            in_specs=in_specs, out_specs=pl.BlockSpec((tm, tn), lambda i, j, k: (i, j)),
            compiler_params=_cparams(("parallel", "parallel", "arbitrary")),
        )(*operands)

    def body(*refs):
        if extra is not None:
            a_ref, b_ref, a2_ref, b2_ref, o_ref, acc_ref = refs
        else:
            a_ref, b_ref, o_ref, acc_ref = refs
        k = pl.program_id(2)

        @pl.when(k == 0)
        def _():
            if extra is not None:
                acc_ref[...] = lax.dot_general(a2_ref[...], b2_ref[...], dims, preferred_element_type=F32)
            else:
                acc_ref[...] = jnp.zeros_like(acc_ref)

        acc_ref[...] += lax.dot_general(a_ref[...], b_ref[...], dims, preferred_element_type=F32)

        @pl.when(k == nk - 1)
        def _():
            o_ref[...] = acc_ref[...].astype(out_dtype)

    return pl.pallas_call(
        body, name=name, out_shape=jax.ShapeDtypeStruct((m, n), out_dtype), grid=(m // tm, n // tn, nk),
        in_specs=in_specs, out_specs=pl.BlockSpec((tm, tn), lambda i, j, k: (i, j)),
        scratch_shapes=[pltpu.VMEM((tm, tn), F32)],
        compiler_params=_cparams(("parallel", "parallel", "arbitrary")),
    )(*operands)


def _matmul_sum(terms, out_dtype, name, tm, tn, exchange=None, trans_b=False, n_out=None):
    m = terms[0][0].shape[0]
    n = n_out if n_out is not None else terms[0][3].shape[0 if trans_b else 1]
    tm, tn = min(tm, m), min(tn, n)
    gm, gn = m // tm, n // tn
    nt = len(terms)
    dims = NT if trans_b else NN
    in_specs, operands = [], []
    for a, ka, ia, b, ib in terms:
        in_specs.append(pl.BlockSpec((tm, ka), functools.partial(lambda i, j, ia: (i, ia), ia=ia)))
        if trans_b:
            in_specs.append(pl.BlockSpec((tn, ka), functools.partial(lambda i, j, ib: (j, ib), ib=ib)))
        else:
            in_specs.append(pl.BlockSpec((ka, tn), functools.partial(lambda i, j, ib: (ib, j), ib=ib)))
        operands += [a, b]
    sent = [] if exchange is None else list(exchange)
    ns = len(sent)
    hbm = pl.BlockSpec(memory_space=pl.ANY)

    def body(*refs):
        o_ref = refs[2 * nt + ns]
        if ns:
            ins, outs = refs[2 * nt:2 * nt + ns], refs[2 * nt + ns + 1:2 * nt + 2 * ns + 1]
            send_sems, recv_sems = refs[2 * nt + 2 * ns + 1:]
            x, y, c = _coords()
            k = 2 * x + y
            chips, chip_idx = _other_chips(x, y)
            step = pl.program_id(0) * gn + pl.program_id(1)

            def copies(a, j, landed):
                dst = outs[a].at[chip_idx[j]] if landed else outs[a].at[k]
                src = dst if landed else ins[a].at[chip_idx[j]]
                return pltpu.make_async_remote_copy(
                    src_ref=src, dst_ref=dst, send_sem=send_sems.at[3 * a + j], recv_sem=recv_sems.at[3 * a + j],
                    device_id=(*chips[j], c), device_id_type=MESH)

            @pl.when(step == 0)
            def _():
                for a in range(ns):
                    for j in range(3):
                        copies(a, j, False).start()

        acc = lax.dot_general(refs[0][...], refs[1][...], dims, preferred_element_type=F32)
        for t in range(1, nt):
            acc += lax.dot_general(refs[2 * t][...], refs[2 * t + 1][...], dims, preferred_element_type=F32)
        o_ref[...] = acc.astype(out_dtype)

        if ns:
            @pl.when(step == gm * gn - 1)
            def _():
                for a in range(ns):
                    for j in range(3):
                        copies(a, j, True).wait_recv()
                for a in range(ns):
                    for j in range(3):
                        copies(a, j, False).wait_send()

    main = jax.ShapeDtypeStruct((m, n), out_dtype)
    tile = pl.BlockSpec((tm, tn), lambda i, j: (i, j))
    if not ns:
        return pl.pallas_call(
            body, name=name, out_shape=main, grid=(gm, gn), in_specs=in_specs, out_specs=tile,
            compiler_params=_cparams(("parallel", "parallel")),
        )(*operands)
    return pl.pallas_call(
        body, name=name, out_shape=(main, *[jax.ShapeDtypeStruct(a.shape, a.dtype) for a in sent]), grid=(gm, gn),
        in_specs=in_specs + [hbm] * ns, out_specs=(tile, *[hbm] * ns),
        scratch_shapes=[pltpu.SemaphoreType.DMA((3 * ns,)), pltpu.SemaphoreType.DMA((3 * ns,))],
        compiler_params=_cparams(("arbitrary", "arbitrary")),
    )(*operands, *sent)


def _matmul_tn_pieces(pieces, rows, b, name, tm):
    kdim, n = b.shape
    tm = min(tm, min(p.shape[1] for p in pieces))
    tiles = [p.shape[1] // tm for p in pieces]
    first = [sum(tiles[:i]) for i in range(len(pieces))]

    def body(*refs):
        b_ref, o_ref = refs[-2:]
        i = pl.program_id(0)
        for p in range(len(pieces)):
            @pl.when(jnp.logical_and(i >= first[p], i < first[p] + tiles[p]))
            def _(p=p):
                o_ref[...] = lax.dot_general(refs[p][...], b_ref[...], TN, preferred_element_type=F32)

    in_specs = [pl.BlockSpec((kdim, tm), functools.partial(lambda i, lo, cnt: (0, jnp.clip(i - lo, 0, cnt - 1)),
                                                           lo=first[p], cnt=tiles[p])) for p in range(len(pieces))]
    return pl.pallas_call(
        body, name=name, out_shape=jax.ShapeDtypeStruct((rows, n), F32), grid=(sum(tiles),),
        in_specs=in_specs + [pl.BlockSpec((kdim, n), lambda i: (0, 0))],
        out_specs=pl.BlockSpec((tm, n), lambda i: (i, 0)),
        compiler_params=_cparams(("arbitrary",)),
    )(*pieces, b)


def _matmul_tn_rows(buf, rows, a, b, row_blk, name, tm):
    kdim, m = a.shape
    n = b.shape[1]
    tm = min(tm, m)

    def body(*refs):
        a_ref, b_ref, o_ref = refs[-3:]
        o_ref[...] = lax.dot_general(a_ref[...], b_ref[...], TN, preferred_element_type=F32)

    in_specs = [pl.BlockSpec((kdim, tm), lambda i: (0, i)), pl.BlockSpec((kdim, n), lambda i: (0, 0))]
    operands = [a, b]
    if buf is not None:
        in_specs.insert(0, pl.BlockSpec(memory_space=pl.ANY))
        operands.insert(0, buf)
    return pl.pallas_call(
        body, name=name, out_shape=jax.ShapeDtypeStruct((rows, n), F32), grid=(m // tm,),
        in_specs=in_specs, out_specs=pl.BlockSpec((tm, n), lambda i: (row_blk + i, 0)),
        input_output_aliases={} if buf is None else {0: 0},
        compiler_params=_cparams(("parallel",)),
    )(*operands)


def _ada_mod(c_all, w_shard, b_shard):
    nb, d = c_all.shape
    cols = w_shard.shape[1]

    def body(c_ref, w_ref, b_ref, mod_ref, act_ref):
        cv = c_ref[...]
        act = cv * _sigmoid(cv)
        act_ref[...] = act
        mod_ref[...] = jnp.dot(act, w_ref[...], preferred_element_type=F32, precision=HI) + b_ref[...]

    return pl.pallas_call(
        body, name="ada_mod",
        out_shape=(jax.ShapeDtypeStruct((nb, cols), F32), jax.ShapeDtypeStruct((nb, d), F32)),
        compiler_params=_cparams(),
    )(c_all, w_shard, b_shard)


def _rms_mod_fwd(x, gain, scale, shift):
    s, d = x.shape
    tm = min(512, s)

    def body(x_ref, g_ref, sc_ref, sh_ref, h_ref):
        xv = x_ref[...]
        r = lax.rsqrt(jnp.mean(xv * xv, axis=-1, keepdims=True) + NORM_EPS)
        h_ref[...] = (xv * r * g_ref[...] * (1.0 + sc_ref[...]) + sh_ref[...]).astype(BF16)

    row = pl.BlockSpec((1, d), lambda i: (0, 0))
    tile = pl.BlockSpec((tm, d), lambda i: (i, 0))
    return pl.pallas_call(
        body, name="rms_mod_fwd", out_shape=jax.ShapeDtypeStruct((s, d), BF16), grid=(s // tm,),
        in_specs=[tile, row, row, row], out_specs=tile, compiler_params=_cparams(("parallel",)),
    )(x, gain, scale, shift)


def _rms_mod_bwd(x, dh, dres, gain, scale):
    s, d = x.shape
    tm = min(512, s)

    def body(x_ref, dh_ref, dres_ref, g_ref, sc_ref, dx_ref, sums_ref):
        @pl.when(pl.program_id(0) == 0)
        def _():
            sums_ref[...] = jnp.zeros_like(sums_ref)

        xv, dhv = x_ref[...], dh_ref[...]
        r = lax.rsqrt(jnp.mean(xv * xv, axis=-1, keepdims=True) + NORM_EPS)
        nrm = xv * r
        g, one_sc = g_ref[...], 1.0 + sc_ref[...]
        dn = dhv * g * one_sc
        dx_ref[...] = r * (dn - nrm * jnp.mean(dn * nrm, axis=-1, keepdims=True)) + dres_ref[...]
        dhn = dhv * nrm
        sums_ref[0:1, :] += jnp.sum(dhv, axis=0, keepdims=True)
        sums_ref[1:2, :] += jnp.sum(dhn * g, axis=0, keepdims=True)
        sums_ref[2:3, :] += jnp.sum(dhn * one_sc, axis=0, keepdims=True)

    row = pl.BlockSpec((1, d), lambda i: (0, 0))
    tile = pl.BlockSpec((tm, d), lambda i: (i, 0))
    return pl.pallas_call(
        body, name="rms_mod_bwd",
        out_shape=(jax.ShapeDtypeStruct((s, d), F32), jax.ShapeDtypeStruct((3, d), F32)), grid=(s // tm,),
        in_specs=[tile, tile, tile, row, row], out_specs=(tile, pl.BlockSpec((3, d), lambda i: (0, 0))),
        compiler_params=_cparams(("arbitrary",)),
    )(x, dh, dres, gain, scale)


def _loss_head(x, mixed, gate, gain_f, target):
    s, d = x.shape
    tm = min(512, s)

    def body(x_ref, mx_ref, gt_ref, gf_ref, t_ref, dx2_ref, dmx_ref, sums_ref):
        @pl.when(pl.program_id(0) == 0)
        def _():
            sums_ref[...] = jnp.zeros_like(sums_ref)

        mx, gt, gf = mx_ref[...], gt_ref[...], gf_ref[...]
        x2 = x_ref[...] + gt * mx
        r = lax.rsqrt(jnp.mean(x2 * x2, axis=-1, keepdims=True) + NORM_EPS)
        nrm = x2 * r
        err = nrm * gf - t_ref[...]
        dyf = err * (1.0 / d)
        dn = dyf * gf
        dx2 = r * (dn - nrm * jnp.mean(dn * nrm, axis=-1, keepdims=True))
        dx2_ref[...] = dx2
        dmx_ref[...] = (dx2 * gt).astype(BF16)
        sums_ref[0:1, :] += jnp.sum(err * err, axis=0, keepdims=True)
        sums_ref[1:2, :] += jnp.sum(dyf * nrm, axis=0, keepdims=True)
        sums_ref[2:3, :] += jnp.sum(dx2 * mx, axis=0, keepdims=True)

    row = pl.BlockSpec((1, d), lambda i: (0, 0))
    tile = pl.BlockSpec((tm, d), lambda i: (i, 0))
    return pl.pallas_call(
        body, name="loss_head",
        out_shape=(jax.ShapeDtypeStruct((s, d), F32), jax.ShapeDtypeStruct((s, d), BF16),
                   jax.ShapeDtypeStruct((3, d), F32)),
        grid=(s // tm,), in_specs=[tile, tile, row, row, tile],
        out_specs=(tile, tile, pl.BlockSpec((3, d), lambda i: (0, 0))),
        compiler_params=_cparams(("arbitrary",)),
    )(x, mixed, gate, gain_f, target)


def _silu_grad(z, sg):
    return sg * (1.0 + z * (1.0 - sg))


def _gated_norm_fwd(o, proj, z_off, gain, gate_inside, name):
    s, d = o.shape
    tm = min(512, s)
    zb = z_off // d

    def body(o_ref, z_ref, g_ref, y_ref):
        z = z_ref[...].astype(F32)
        sz = z * _sigmoid(z)
        u = o_ref[...] * sz if gate_inside else o_ref[...]
        r = lax.rsqrt(jnp.mean(u * u, axis=-1, keepdims=True) + NORM_EPS)
        y = u * r * g_ref[...]
        y_ref[...] = (y if gate_inside else y * sz).astype(BF16)

    tile = pl.BlockSpec((tm, d), lambda i: (i, 0))
    return pl.pallas_call(
        body, name=name, out_shape=jax.ShapeDtypeStruct((s, d), BF16), grid=(s // tm,),
        in_specs=[tile, pl.BlockSpec((tm, d), lambda i: (i, zb)), pl.BlockSpec((1, d), lambda i: (0, 0))],
        out_specs=tile, compiler_params=_cparams(("parallel",)),
    )(o, proj, gain)


def _gated_norm_bwd(dy_all, dy_blk, o, proj, z_off, gain, gate_inside, name):
    s, d = o.shape
    tm = min(512, s)
    zb = z_off // d

    def body(dy_ref, o_ref, z_ref, g_ref, do_ref, dz_ref, dg_ref):
        @pl.when(pl.program_id(0) == 0)
        def _():
            dg_ref[...] = jnp.zeros_like(dg_ref)

        z = z_ref[...].astype(F32)
        sg = _sigmoid(z)
        sz = z * sg
        ov, dy, g = o_ref[...], dy_ref[...].astype(F32), g_ref[...]
        u = ov * sz if gate_inside else ov
        r = lax.rsqrt(jnp.mean(u * u, axis=-1, keepdims=True) + NORM_EPS)
        nrm = u * r
        if gate_inside:
            dg_ref[...] += jnp.sum(dy * nrm, axis=0, keepdims=True)
            dn = dy * g
        else:
            dg_ref[...] += jnp.sum(dy * nrm * sz, axis=0, keepdims=True)
            dn = dy * g * sz
        du = r * (dn - nrm * jnp.mean(dn * nrm, axis=-1, keepdims=True))
        if gate_inside:
            do_ref[...] = du * sz
            dz_ref[...] = (du * ov * _silu_grad(z, sg)).astype(BF16)
        else:
            do_ref[...] = du
            dz_ref[...] = (dy * nrm * g * _silu_grad(z, sg)).astype(BF16)

    tile = pl.BlockSpec((tm, d), lambda i: (i, 0))
    row = pl.BlockSpec((1, d), lambda i: (0, 0))
    return pl.pallas_call(
        body, name=name,
        out_shape=(jax.ShapeDtypeStruct((s, d), F32), jax.ShapeDtypeStruct((s, d), BF16),
                   jax.ShapeDtypeStruct((1, d), F32)),
        grid=(s // tm,),
        in_specs=[pl.BlockSpec((tm, d), lambda i: (i, dy_blk)), tile, pl.BlockSpec((tm, d), lambda i: (i, zb)), row],
        out_specs=(tile, tile, row), compiler_params=_cparams(("arbitrary",)),
    )(dy_all, o, proj, gain)


def _sb_logits(qh, kb):
    z = lax.dot_general(qh, kb, NT, preferred_element_type=F32)
    neg_abs = lax.bitcast_convert_type(lax.bitcast_convert_type(z, jnp.uint32) | jnp.uint32(0x80000000), F32)
    lb = jnp.minimum(z, 0.0) - jnp.log(1.0 + jnp.exp(neg_abs))
    return lb, lb - z


def _attn_consts(tk):
    lane = lax.broadcasted_iota(jnp.int32, (1, LANE), 1)
    row = lax.broadcasted_iota(jnp.int32, (tk, tk), 0)
    col = lax.broadcasted_iota(jnp.int32, (tk, tk), 1)
    return (lane < HEAD_DIM, lane >= HEAD_DIM), row, col


def _band_mask(rows, tk):
    return lax.broadcasted_iota(jnp.int32, (rows, tk), 1) < lax.broadcasted_iota(jnp.int32, (rows, tk), 0)


def _attn_fwd(proj):
    s = proj.shape[0]
    tq, tk = min(ATTN_TQ, s), min(ATTN_TK, s)
    r = tq // tk

    def body(q_ref, k_ref, v_ref, o_ref, l_ref, acc_ref, run_ref):
        i = pl.program_id(1)
        head_mask, row, col = _attn_consts(tk)
        later = (row > col).astype(BF16)
        q = q_ref[...] * ATTN_SCALE
        qh = [jnp.where(m, q, jnp.zeros_like(q)) for m in head_mask]
        acc_ref[...] = jnp.zeros_like(acc_ref)
        run_ref[...] = jnp.zeros_like(run_ref)

        def block(j, lo, hi, band):
            start = pl.multiple_of(j * tk, tk)
            kb = k_ref[pl.ds(start, tk), :]
            vb = v_ref[pl.ds(start, tk), :]
            rows = slice(lo, hi)
            causal = _band_mask(hi - lo, tk) if band else None
            hs = range(2)
            logits = [_sb_logits(qh[h][rows], kb) for h in hs]
            lb = [logits[h][0] for h in hs]
            l1m = [logits[h][1] if causal is None else jnp.where(causal, logits[h][1], 0.0) for h in hs]
            tail = [jnp.dot(l1m[h].astype(BF16), later, preferred_element_type=F32) + run_ref[h, rows] for h in hs]
            w = [jnp.exp(lb[h] + tail[h]) for h in hs]
            if causal is not None:
                w = [jnp.where(causal, w[h], 0.0) for h in hs]
            vh = [jnp.where(head_mask[h], vb, jnp.zeros_like(vb)) for h in hs]
            acc_ref[rows, :] += (jnp.dot(w[0].astype(BF16), vh[0], preferred_element_type=F32)
                                 + jnp.dot(w[1].astype(BF16), vh[1], preferred_element_type=F32))
            for h in hs:
                run_ref[h, rows] += jnp.sum(l1m[h], axis=1, keepdims=True)

        for b in reversed(range(r)):
            block(i * r + b, b * tk, tq, True)
        n_full = i * r
        half = tq // 2

        def more(c):
            return jnp.logical_and(c[0] < n_full, c[1] > LOG_ZERO)

        def step_all(c):
            block(n_full - 1 - c[0], 0, tq, False)
            return c[0] + 1, jnp.max(run_ref[:, half:, :])

        def step_upper(c):
            block(n_full - 1 - c[0], 0, half, False)
            return c[0] + 1, jnp.max(run_ref[:, :half, :])

        seen_all, _ = lax.while_loop(more, step_all, (jnp.int32(0), jnp.max(run_ref[:, half:, :])))
        seen, _ = lax.while_loop(more, step_upper, (seen_all, jnp.max(run_ref[:, :half, :])))
        o_ref[...] = acc_ref[...]
        lane = lax.broadcasted_iota(jnp.int32, (1, LANE), 1)
        first = jnp.where(lane < 3 * HEAD_DIM // 4, n_full - seen, n_full - seen_all).astype(F32)
        l_ref[...] = jnp.where(lane < HEAD_DIM // 2, run_ref[0], jnp.where(lane < HEAD_DIM, first, run_ref[1]))

    kq, kk, kv = OFF_Q // LANE, OFF_K // LANE, OFF_V // LANE
    tile = pl.BlockSpec((tq, LANE), lambda p, i: (i, p))
    return pl.pallas_call(
        body, name="attn_fwd",
        out_shape=(jax.ShapeDtypeStruct((s, D_ATTN), F32), jax.ShapeDtypeStruct((s, D_ATTN), F32)),
        grid=(N_PAIRS, s // tq),
        in_specs=[pl.BlockSpec((tq, LANE), lambda p, i: (i, kq + p)),
                  pl.BlockSpec((s, LANE), lambda p, i: (0, kk + p)),
                  pl.BlockSpec((s, LANE), lambda p, i: (0, kv + p))],
        out_specs=(tile, tile),
        scratch_shapes=[pltpu.VMEM((tq, LANE), F32), pltpu.VMEM((2, tq, 1), F32)],
        compiler_params=_cparams(("parallel", "arbitrary")),
    )(proj, proj, proj)


def _attn_bwd(proj, do, lsum):
    s = proj.shape[0]
    tq, tk = min(ATTN_TQ, s), min(ATTN_TK, s)
    r = tq // tk

    def body(q_ref, k_ref, v_ref, do_ref, l_ref, dq_ref, dk_ref, dv_ref, dqacc_ref, dkacc_ref, dvacc_ref,
             passed_ref, pre_ref):
        i = pl.program_id(1)

        @pl.when(i == 0)
        def _():
            dkacc_ref[...] = jnp.zeros_like(dkacc_ref)
            dvacc_ref[...] = jnp.zeros_like(dvacc_ref)

        head_mask, row, col = _attn_consts(tk)
        later = (row > col).astype(BF16)
        earlier = (row < col).astype(BF16)
        q = q_ref[...] * ATTN_SCALE
        dov = do_ref[...].astype(BF16)
        qh = [jnp.where(m, q, jnp.zeros_like(q)) for m in head_mask]
        doh = [jnp.where(m, dov, jnp.zeros_like(dov)) for m in head_mask]
        lsum_v = l_ref[...]
        lh = [lsum_v[:, 0:1], lsum_v[:, HEAD_DIM:HEAD_DIM + 1]]
        n_full = i * r
        half = tq // 2
        quarter = HEAD_DIM // 4
        first_all = jnp.clip(jnp.max(lsum_v[0:8, 3 * quarter:HEAD_DIM]).astype(jnp.int32), 0, n_full)
        first = jnp.clip(jnp.max(lsum_v[0:8, 2 * quarter:3 * quarter]).astype(jnp.int32), 0, first_all)
        dqacc_ref[...] = jnp.zeros_like(dqacc_ref)
        passed_ref[...] = jnp.zeros_like(passed_ref)
        pre_ref[...] = jnp.zeros_like(pre_ref)

        def block(j, lo, hi, band):
            start = pl.multiple_of(j * tk, tk)
            kb = k_ref[pl.ds(start, tk), :]
            vb = v_ref[pl.ds(start, tk), :]
            rows = slice(lo, hi)
            causal = _band_mask(hi - lo, tk) if band else None
            hs = range(2)
            q_rows = [qh[h][rows] for h in hs]
            do_rows = [doh[h][rows] for h in hs]
            logits = [_sb_logits(q_rows[h], kb) for h in hs]
            lb = [logits[h][0] for h in hs]
            l1m = [logits[h][1] if causal is None else jnp.where(causal, logits[h][1], 0.0) for h in hs]
            da = [lax.dot_general(do_rows[h], vb, NT, preferred_element_type=F32) for h in hs]
            rs = [jnp.sum(l1m[h], axis=1, keepdims=True) for h in hs]
            right = [lh[h][rows] - passed_ref[h, rows] - rs[h] for h in hs]
            for h in hs:
                passed_ref[h, rows] += rs[h]
            tail = [jnp.dot(l1m[h].astype(BF16), later, preferred_element_type=F32) + right[h] for h in hs]
            a = [jnp.exp(lb[h] + tail[h]) for h in hs]
            if causal is not None:
                a = [jnp.where(causal, a[h], 0.0) for h in hs]
            g = [a[h] * da[h] for h in hs]
            pre = [jnp.dot(g[h].astype(BF16), earlier, preferred_element_type=F32) + pre_ref[h, rows] for h in hs]
            for h in hs:
                pre_ref[h, rows] += jnp.sum(g[h], axis=1, keepdims=True)
            dz = [g[h] - jnp.exp(lb[h]) * (g[h] + pre[h]) for h in hs]
            if causal is not None:
                dz = [jnp.where(causal, dz[h], 0.0) for h in hs]
            dzb = [dz[h].astype(BF16) for h in hs]
            kh = [jnp.where(head_mask[h], kb, jnp.zeros_like(kb)) * ATTN_SCALE for h in hs]
            dqacc_ref[rows, :] += (jnp.dot(dzb[0], kh[0], preferred_element_type=F32)
                                   + jnp.dot(dzb[1], kh[1], preferred_element_type=F32))
            dvacc_ref[pl.ds(start, tk), :] += (
                lax.dot_general(a[0].astype(BF16), do_rows[0], TN, preferred_element_type=F32)
                + lax.dot_general(a[1].astype(BF16), do_rows[1], TN, preferred_element_type=F32))
            dkacc_ref[pl.ds(start, tk), :] += (
                lax.dot_general(dzb[0], q_rows[0], TN, preferred_element_type=F32)
                + lax.dot_general(dzb[1], q_rows[1], TN, preferred_element_type=F32))

        def step_upper(j, carry):
            block(j, 0, half, False)
            return carry

        def step_all(j, carry):
            block(j, 0, tq, False)
            return carry

        lax.fori_loop(first, first_all, step_upper, 0)
        lax.fori_loop(first_all, n_full, step_all, 0)
        for b in range(r):
            block(n_full + b, b * tk, tq, True)
        dq_ref[...] = dqacc_ref[...].astype(BF16)

        @pl.when(i == pl.num_programs(1) - 1)
        def _():
            dk_ref[...] = dkacc_ref[...].astype(BF16)
            dv_ref[...] = dvacc_ref[...].astype(BF16)

    kq, kk, kv = OFF_Q // LANE, OFF_K // LANE, OFF_V // LANE
    tile = pl.BlockSpec((tq, LANE), lambda p, i: (i, p))
    full = pl.BlockSpec((s, LANE), lambda p, i: (0, p))
    shp = jax.ShapeDtypeStruct((s, D_ATTN), BF16)
    return pl.pallas_call(
        body, name="attn_bwd", out_shape=(shp, shp, shp), grid=(N_PAIRS, s // tq),
        in_specs=[pl.BlockSpec((tq, LANE), lambda p, i: (i, kq + p)),
                  pl.BlockSpec((s, LANE), lambda p, i: (0, kk + p)),
                  pl.BlockSpec((s, LANE), lambda p, i: (0, kv + p)),
                  tile, tile],
        out_specs=(tile, full, full),
        scratch_shapes=[pltpu.VMEM((tq, LANE), F32), pltpu.VMEM((s, LANE), F32), pltpu.VMEM((s, LANE), F32),
                        pltpu.VMEM((2, tq, 1), F32), pltpu.VMEM((2, tq, 1), F32)],
        compiler_params=_cparams(("parallel", "arbitrary")),
    )(proj, proj, proj, do, lsum)


def _shift_down(u, k, rows):
    return jnp.where(rows >= k, pltpu.roll(u, k, 0), 0.0)


def _shift_up(u, k, rows, s):
    return jnp.where(rows < s - k, pltpu.roll(u, s - k, 0), 0.0)


def _conv_fwd(proj, w, b):
    s = proj.shape[0]
    blk0 = OFF_XBC // LANE

    def body(u_ref, w_ref, b_ref, o_ref):
        u = u_ref[...].astype(F32)
        rows = lax.broadcasted_iota(jnp.int32, (s, 1), 0)
        pre = u * w_ref[CONV_K - 1:CONV_K, :] + b_ref[...]
        for k in range(1, CONV_K):
            pre += _shift_down(u, k, rows) * w_ref[CONV_K - 1 - k:CONV_K - k, :]
        o_ref[...] = pre * _sigmoid(pre)

    return pl.pallas_call(
        body, name="conv_fwd", out_shape=jax.ShapeDtypeStruct((s, D_XBC), F32), grid=(D_XBC // LANE,),
        in_specs=[pl.BlockSpec((s, LANE), lambda j: (0, blk0 + j)), pl.BlockSpec((CONV_K, LANE), lambda j: (0, j)),
                  pl.BlockSpec((1, LANE), lambda j: (0, j))],
        out_specs=pl.BlockSpec((s, LANE), lambda j: (0, j)), compiler_params=_cparams(("parallel",)),
    )(proj, w, b)


def _conv_bwd(proj, w, b, dact):
    s = proj.shape[0]
    blk0 = OFF_XBC // LANE

    def body(u_ref, w_ref, b_ref, da_ref, du_ref, dw_ref, db_ref):
        u = u_ref[...].astype(F32)
        rows = lax.broadcasted_iota(jnp.int32, (s, 1), 0)
        shifted = [u] + [_shift_down(u, k, rows) for k in range(1, CONV_K)]
        pre = b_ref[...] + shifted[0] * w_ref[CONV_K - 1:CONV_K, :]
        for k in range(1, CONV_K):
            pre += shifted[k] * w_ref[CONV_K - 1 - k:CONV_K - k, :]
        sg = _sigmoid(pre)
        dpre = da_ref[...] * _silu_grad(pre, sg)
        db_ref[...] = jnp.sum(dpre, axis=0, keepdims=True)
        du = dpre * w_ref[CONV_K - 1:CONV_K, :]
        for k in range(CONV_K):
            dw_ref[CONV_K - 1 - k:CONV_K - k, :] = jnp.sum(dpre * shifted[k], axis=0, keepdims=True)
            if k:
                du += _shift_up(dpre, k, rows, s) * w_ref[CONV_K - 1 - k:CONV_K - k, :]
        du_ref[...] = du.astype(BF16)

    col = pl.BlockSpec((s, LANE), lambda j: (0, j))
    return pl.pallas_call(
        body, name="conv_bwd",
        out_shape=(jax.ShapeDtypeStruct((s, D_XBC), BF16), jax.ShapeDtypeStruct((CONV_K, D_XBC), F32),
                   jax.ShapeDtypeStruct((1, D_XBC), F32)),
        grid=(D_XBC // LANE,),
        in_specs=[pl.BlockSpec((s, LANE), lambda j: (0, blk0 + j)), pl.BlockSpec((CONV_K, LANE), lambda j: (0, j)),
                  pl.BlockSpec((1, LANE), lambda j: (0, j)), col],
        out_specs=(col, pl.BlockSpec((CONV_K, LANE), lambda j: (0, j)), pl.BlockSpec((1, LANE), lambda j: (0, j))),
        compiler_params=_cparams(("parallel",)),
    )(proj, w, b, dact)


def _ssd_decays(dtraw_ref, bias_ref, dtt_ref, biast_ref, arow_ref, acol_ref):
    ln = CHUNK
    dt = _softplus(dtraw_ref[...] + bias_ref[...])
    r = lax.broadcasted_iota(jnp.int32, (ln, ln), 0)
    c = lax.broadcasted_iota(jnp.int32, (ln, ln), 1)
    ac = jnp.dot((r >= c).astype(F32), dt * arow_ref[...], preferred_element_type=F32, precision=HI)
    dtt = _softplus(dtt_ref[...] + biast_ref[...])
    act = jnp.dot(dtt * acol_ref[...], (r <= c).astype(F32), preferred_element_type=F32, precision=HI)
    return dt, ac, act, r >= c


def _pair_cols(m0, v, h0):
    return jnp.where(m0, v[:, h0:h0 + 1], v[:, h0 + 1:h0 + 2])


def _ssd_fwd(act, dtraw, dtt, bias, biast, arow, acol, dskip):
    s = act.shape[0]
    ln = CHUNK
    nc = s // ln

    def body(act_ref, dtraw_ref, dtt_ref, bias_ref, biast_ref, arow_ref, acol_ref, dsk_ref, y_ref, st_ref,
             state_ref):
        @pl.when(pl.program_id(0) == 0)
        def _():
            state_ref[...] = jnp.zeros_like(state_ref)

        dt, ac, act_t, lower = _ssd_decays(dtraw_ref, bias_ref, dtt_ref, biast_ref, arow_ref, acol_ref)
        lane = lax.broadcasted_iota(jnp.int32, (1, LANE), 1)
        m0 = lane < HEAD_DIM
        for g in range(N_GROUPS):
            bg32 = act_ref[:, D_SSM + g * D_STATE:D_SSM + (g + 1) * D_STATE]
            bg, bg_t = bg32.astype(BF16), bg32.T.astype(BF16)
            cg = act_ref[:, D_SSM + (N_GROUPS + g) * D_STATE:D_SSM + (N_GROUPS + g + 1) * D_STATE].astype(BF16)
            cb = lax.dot_general(cg, bg, NT, preferred_element_type=F32)
            for p in range(g * 4, g * 4 + 4):
                h0 = 2 * p
                xp = act_ref[:, p * LANE:(p + 1) * LANE]
                xdt = xp * _pair_cols(m0, dt, h0)
                acp = _pair_cols(m0, ac, h0)
                last = acp[ln - 1:ln, :]
                y = xp * dsk_ref[:, p * LANE:(p + 1) * LANE]
                for hh in range(2):
                    h = h0 + hh
                    dm = jnp.exp(jnp.where(lower, ac[:, h:h + 1] - act_t[h:h + 1, :], -jnp.inf))
                    mask = m0 if hh == 0 else jnp.logical_not(m0)
                    y += jnp.dot((cb * dm).astype(BF16), jnp.where(mask, xdt, 0.0).astype(BF16),
                                 preferred_element_type=F32)
                prev = state_ref[p]
                st_ref[0, p] = prev
                y += jnp.dot(cg, prev.astype(BF16), preferred_element_type=F32) * jnp.exp(acp)
                y_ref[:, p * LANE:(p + 1) * LANE] = y
                cs = jnp.dot(bg_t, (xdt * jnp.exp(last - acp)).astype(BF16), preferred_element_type=F32)
                state_ref[p] = prev * jnp.exp(last) + cs

    row = lambda w: pl.BlockSpec((1, w), lambda c: (0, 0))
    return pl.pallas_call(
        body, name="ssd_fwd",
        out_shape=(jax.ShapeDtypeStruct((s, D_SSM), F32),
                   jax.ShapeDtypeStruct((nc, N_PAIRS, LANE, D_STATE), F32)),
        grid=(nc,),
        in_specs=[pl.BlockSpec((ln, D_XBC), lambda c: (c, 0)), pl.BlockSpec((ln, LANE), lambda c: (c, 0)),
                  pl.BlockSpec((N_HEADS, ln), lambda c: (0, c)), row(LANE),
                  pl.BlockSpec((N_HEADS, 1), lambda c: (0, 0)), row(LANE),
                  pl.BlockSpec((N_HEADS, 1), lambda c: (0, 0)), row(D_SSM)],
        out_specs=(pl.BlockSpec((ln, D_SSM), lambda c: (c, 0)),
                   pl.BlockSpec((1, N_PAIRS, LANE, D_STATE), lambda c: (c, 0, 0, 0))),
        scratch_shapes=[pltpu.VMEM((N_PAIRS, LANE, D_STATE), F32)],
        compiler_params=_cparams(("arbitrary",)),
    )(act, dtraw, dtt, bias, biast, arow, acol, dskip)


def _ssd_bwd(act, dtraw, dtt, bias, biast, arow, acol, dskip, states, dy):
    s = act.shape[0]
    ln = CHUNK
    nc = s // ln

    def body(act_ref, dtraw_ref, dtt_ref, bias_ref, biast_ref, arow_ref, acol_ref, dsk_ref, st_ref, dy_ref,
             dact_ref, dldc_ref, dldr_ref, ddt_ref, dd_ref, dstate_ref):
        @pl.when(pl.program_id(0) == 0)
        def _():
            dstate_ref[...] = jnp.zeros_like(dstate_ref)
            dd_ref[...] = jnp.zeros_like(dd_ref)

        dt, ac, act_t, lower = _ssd_decays(dtraw_ref, bias_ref, dtt_ref, biast_ref, arow_ref, acol_ref)
        lane = lax.broadcasted_iota(jnp.int32, (1, LANE), 1)
        m0 = lane < HEAD_DIM
        halves = (m0, jnp.logical_not(m0))
        is_last = lax.broadcasted_iota(jnp.int32, (ln, 1), 0) == ln - 1
        sub = lax.broadcasted_iota(jnp.int32, (N_HEADS, 1), 0)
        earlier_eq = jnp.logical_not(lower) | (lax.broadcasted_iota(jnp.int32, (ln, ln), 0)
                                               == lax.broadcasted_iota(jnp.int32, (ln, ln), 1))
        dac_col = jnp.zeros((ln, LANE), F32)
        dac_row = jnp.zeros((N_HEADS, ln), F32)
        ddt_col = jnp.zeros((ln, LANE), F32)

        def half_sum(v, hh):
            return jnp.sum(jnp.where(halves[hh], v, 0.0), axis=1, keepdims=True)

        for g in range(N_GROUPS):
            b_lo, c_lo = D_SSM + g * D_STATE, D_SSM + (N_GROUPS + g) * D_STATE
            bg32 = act_ref[:, b_lo:b_lo + D_STATE]
            cg32 = act_ref[:, c_lo:c_lo + D_STATE]
            bg, cg = bg32.astype(BF16), cg32.astype(BF16)
            cg_t = cg32.T.astype(BF16)
            cb_t = lax.dot_general(bg, cg, NT, preferred_element_type=F32)
            dcb_t = jnp.zeros((ln, ln), F32)
            dbg = jnp.zeros((ln, D_STATE), F32)
            dcg = jnp.zeros((ln, D_STATE), F32)
            for p in range(g * 4, g * 4 + 4):
                h0 = 2 * p
                cols = slice(p * LANE, (p + 1) * LANE)
                xp = act_ref[:, cols]
                dyp = dy_ref[:, cols]
                dtp = _pair_cols(m0, dt, h0)
                acp = _pair_cols(m0, ac, h0)
                last = acp[ln - 1:ln, :]
                xdt = xp * dtp
                eac = jnp.exp(acp)
                dte = jnp.exp(last - acp)
                dec = jnp.exp(last)
                prev = st_ref[0, p]
                prev_b = prev.astype(BF16)
                ds = dstate_ref[p]
                ds_b = ds.astype(BF16)

                dd_ref[:, cols] += jnp.sum(dyp * xp, axis=0, keepdims=True)
                dx = dyp * dsk_ref[:, cols]
                zoff = jnp.dot(cg, prev_b, preferred_element_type=F32)
                dz_b = (dyp * eac).astype(BF16)
                dcg += lax.dot_general(dz_b, prev_b, NT, preferred_element_type=F32)
                dprev = jnp.dot(cg_t, dz_b, preferred_element_type=F32) + ds * dec
                wmat = jnp.dot(bg, ds_b, preferred_element_type=F32)
                xdte_b = (xdt * dte).astype(BF16)
                dbg += lax.dot_general(xdte_b, ds_b, NT, preferred_element_type=F32)
                dxdt = dte * wmat
                t_dte = xdt * wmat * dte
                t_ac = dyp * zoff * eac - t_dte
                at_last = jnp.sum(ds * prev, axis=0, keepdims=True) * dec + jnp.sum(t_dte, axis=0, keepdims=True)
                for hh in range(2):
                    h = h0 + hh
                    here = lane == h
                    dm_t = jnp.exp(jnp.where(earlier_eq, act_t[h:h + 1, :] - ac[:, h:h + 1], -jnp.inf))
                    mm_t = cb_t * dm_t
                    dyh = jnp.where(halves[hh], dyp, 0.0).astype(BF16)
                    xdth = jnp.where(halves[hh], xdt, 0.0).astype(BF16)
                    dmm_t = lax.dot_general(xdth, dyh, NT, preferred_element_type=F32)
                    dxdt += jnp.dot(mm_t.astype(BF16), dyh, preferred_element_type=F32)
                    gm_t = dmm_t * mm_t
                    dcb_t += dmm_t * dm_t
                    dac_col += jnp.where(here, half_sum(t_ac, hh) - jnp.sum(gm_t, axis=1, keepdims=True), 0.0)
                    dac_col += jnp.where(jnp.logical_and(is_last, here), half_sum(at_last, hh), 0.0)
                    dac_row += jnp.where(sub == h, jnp.sum(gm_t, axis=0, keepdims=True), 0.0)
                    ddt_col += jnp.where(here, half_sum(dxdt * xp, hh), 0.0)
                dact_ref[:, cols] = dx + dxdt * dtp
                dstate_ref[p] = dprev
            dcb_tb = dcb_t.astype(BF16)
            dact_ref[:, b_lo:b_lo + D_STATE] = dbg + jnp.dot(dcb_tb, cg, preferred_element_type=F32)
            dact_ref[:, c_lo:c_lo + D_STATE] = dcg + lax.dot_general(dcb_tb, bg, TN, preferred_element_type=F32)

        r = lax.broadcasted_iota(jnp.int32, (ln, ln), 0)
        c = lax.broadcasted_iota(jnp.int32, (ln, ln), 1)
        dldc_ref[...] = jnp.dot((r <= c).astype(F32), dac_col, preferred_element_type=F32, precision=HI)
        dldr_ref[...] = jnp.dot(dac_row, (r >= c).astype(F32), preferred_element_type=F32, precision=HI)
        ddt_ref[...] = ddt_col

    rev = lambda c: nc - 1 - c
    row = lambda w: pl.BlockSpec((1, w), lambda c: (0, 0))
    col16 = pl.BlockSpec((N_HEADS, 1), lambda c: (0, 0))
    chunk128 = pl.BlockSpec((ln, LANE), lambda c: (rev(c), 0))
    return pl.pallas_call(
        body, name="ssd_bwd",
        out_shape=(jax.ShapeDtypeStruct((s, D_XBC), F32), jax.ShapeDtypeStruct((s, LANE), F32),
                   jax.ShapeDtypeStruct((N_HEADS, s), F32), jax.ShapeDtypeStruct((s, LANE), F32),
                   jax.ShapeDtypeStruct((1, D_SSM), F32)),
        grid=(nc,),
        in_specs=[pl.BlockSpec((ln, D_XBC), lambda c: (rev(c), 0)), chunk128,
                  pl.BlockSpec((N_HEADS, ln), lambda c: (0, rev(c))), row(LANE), col16, row(LANE), col16,
                  row(D_SSM), pl.BlockSpec((1, N_PAIRS, LANE, D_STATE), lambda c: (rev(c), 0, 0, 0)),
                  pl.BlockSpec((ln, D_SSM), lambda c: (rev(c), 0))],
        out_specs=(pl.BlockSpec((ln, D_XBC), lambda c: (rev(c), 0)), chunk128,
                   pl.BlockSpec((N_HEADS, ln), lambda c: (0, rev(c))), chunk128, row(D_SSM)),
        scratch_shapes=[pltpu.VMEM((N_PAIRS, LANE, D_STATE), F32)],
        compiler_params=_cparams(("arbitrary",)),
    )(act, dtraw, dtt, bias, biast, arow, acol, dskip, states, dy)


def _dt_bwd(dtraw, bias, arow, dld_col, dld_row_t, ddt_col):
    s = dtraw.shape[0]
    tm = min(512, s)

    def body(raw_ref, bias_ref, a_ref, dc_ref, dr_ref, dd_ref, out_ref, sums_ref):
        @pl.when(pl.program_id(0) == 0)
        def _():
            sums_ref[...] = jnp.zeros_like(sums_ref)

        raw = raw_ref[...] + bias_ref[...]
        dld = dc_ref[...] + dr_ref[...]
        ddt = dld * a_ref[...] + dd_ref[...]
        draw = ddt * _sigmoid(raw)
        out_ref[...] = draw.astype(BF16)
        sums_ref[0:1, :] += jnp.sum(draw, axis=0, keepdims=True)
        sums_ref[1:2, :] += jnp.sum(dld * _softplus(raw), axis=0, keepdims=True)

    tile = pl.BlockSpec((tm, LANE), lambda i: (i, 0))
    row = pl.BlockSpec((1, LANE), lambda i: (0, 0))
    return pl.pallas_call(
        body, name="dt_bwd",
        out_shape=(jax.ShapeDtypeStruct((s, LANE), BF16), jax.ShapeDtypeStruct((2, LANE), F32)), grid=(s // tm,),
        in_specs=[tile, row, row, tile, tile, tile], out_specs=(tile, pl.BlockSpec((2, LANE), lambda i: (0, 0))),
        compiler_params=_cparams(("arbitrary",)),
    )(dtraw, bias, arow, dld_col, dld_row_t, ddt_col)


def _sum8(parts):
    nb, n = parts.shape

    def body(p_ref, o_ref):
        acc = p_ref[0:1, :]
        for b in range(1, nb):
            acc = acc + p_ref[b:b + 1, :]
        o_ref[...] = acc

    return pl.pallas_call(body, name="sum8", out_shape=jax.ShapeDtypeStruct((1, n), F32),
                          compiler_params=_cparams())(parts)


def _outer8(act_t, dmod):
    d, nb = act_t.shape
    n = dmod.shape[1]

    def body(a_ref, m_ref, o_ref):
        acc = a_ref[:, 0:1] * m_ref[0:1, :]
        for b in range(1, nb):
            acc = acc + a_ref[:, b:b + 1] * m_ref[b:b + 1, :]
        o_ref[...] = acc

    return pl.pallas_call(body, name="outer8", out_shape=jax.ShapeDtypeStruct((d, n), F32),
                          compiler_params=_cparams())(act_t, dmod)


def _pad_lanes(v, width=LANE):
    return jnp.pad(v, ((0, 0), (0, width - v.shape[1])))


def kernel(x, c, w_ada, b_ada, norm_in_gain, w_in, conv_w, conv_b, dt_bias, a_log, d_skip, sb_norm_gain, ssm_norm_gain, w_out, norm_f_gain, loss_target, m_w_ada, m_b_ada, m_norm_in_gain, m_w_in, m_conv_w, m_conv_b, m_dt_bias, m_a_log, m_d_skip, m_sb_norm_gain, m_ssm_norm_gain, m_w_out, m_norm_f_gain, v_w_ada, v_b_ada, v_norm_in_gain, v_w_in, v_conv_w, v_conv_b, v_dt_bias, v_a_log, v_d_skip, v_sb_norm_gain, v_ssm_norm_gain, v_w_out, v_norm_f_gain):
    ax, ay, ac_ = _coords()
    chip = 2 * ax + ay
    me = 2 * chip + ac_
    my_c = jnp.reshape(ac_, (1,)).astype(jnp.int32)
    x2d, tgt = x[0], loss_target[0]
    s = x2d.shape[0]
    ada_cols = w_ada.shape[2]
    cw_cols = conv_w.shape[2]
    in_cols = w_in.shape[2]
    out_rows = w_out.shape[1]

    small = jnp.concatenate([c, conv_w[0].reshape(1, CONV_K * cw_cols)], axis=1)
    small_all = _allgather8(small, "gather_cond")[:, 0, :]
    c_all = small_all[:, :D_MODEL]
    conv_w_full = (small_all[0::2, D_MODEL:].reshape(N_CHIPS, CONV_K, cw_cols)
                   .transpose(1, 0, 2).reshape(CONV_K, D_XBC))
    b_ada_shard = lax.dynamic_slice_in_dim(b_ada, chip * ada_cols, ada_cols, axis=1)
    mod_part, c_act_all = _ada_mod(c_all, w_ada[0], b_ada_shard)
    mod_all = _allgather8(mod_part.reshape(1, N_DEV * ada_cols), "gather_mod")[0::2, 0, :]
    mod_all = mod_all.reshape(N_CHIPS, N_DEV, ada_cols)
    mod = lax.dynamic_index_in_dim(mod_all, me, axis=1, keepdims=False).reshape(1, 3 * D_MODEL)
    shift, scale, gate = mod[:, :D_MODEL], mod[:, D_MODEL:2 * D_MODEL], mod[:, 2 * D_MODEL:]

    w_in_mine, w_out_mine = w_in[0].T.astype(BF16), w_out[0].astype(BF16)
    (w_in_all,) = _gather_shards([w_in_mine], "gather_w_in")
    w_in_all = lax.dynamic_update_slice(w_in_all, w_in_mine[None], (chip, 0, 0))
    w_in_t = w_in_all.reshape(D_PROJ, D_MODEL)
    w_zs_t = w_in_t[ZS_LO:]
    w_dt_t = jnp.pad(w_in_t[DT_LO:ZS_LO], ((0, LANE - N_HEADS), (0, 0)))

    h = _rms_mod_fwd(x2d, norm_in_gain, scale, shift)
    proj, w_out_all = _matmul_sum([(h, D_MODEL, 0, w_in_t, 0)], BF16, "in_proj_and_gather_w_out", 1024, 512,
                                  exchange=[jnp.broadcast_to(w_out_mine[None], (N_CHIPS,) + w_out_mine.shape)],
                                  trans_b=True, n_out=D_MAIN)
    w_out_all = lax.dynamic_update_slice(w_out_all, w_out_mine[None], (chip, 0, 0))
    w_out_full = w_out_all.reshape(N_CHIPS * out_rows, D_MODEL)
    proj_zs = _matmul(h, w_zs_t, BF16, "in_proj_zs", "nt", 1024, 512, 1024)
    dtraw = _matmul(h, w_dt_t, F32, "in_proj_dt", "nt", 1024, LANE, 1024)
    o_attn, lsum = _attn_fwd(proj)
    y_attn = _gated_norm_fwd(o_attn, proj, OFF_ZA, sb_norm_gain, False, "attn_gate_fwd")
    act = _conv_fwd(proj, conv_w_full, conv_b)
    a_neg = -jnp.exp(a_log)
    arow, acol = _pad_lanes(a_neg), a_neg.reshape(N_HEADS, 1)
    bias_row, bias_col = _pad_lanes(dt_bias), dt_bias.reshape(N_HEADS, 1)
    dtt = dtraw[:, :N_HEADS].T
    dskip_row = jnp.repeat(d_skip, HEAD_DIM, axis=1)
    ssd_args = (act, dtraw, dtt, bias_row, bias_col, arow, acol, dskip_row)
    y_ssd, states = _ssd_fwd(*ssd_args)
    y_ssm = _gated_norm_fwd(y_ssd, proj_zs, 0, ssm_norm_gain, True, "ssm_gate_fwd")
    mixed = _matmul_sum([(y_attn, D_ATTN, 0, w_out_full, 0), (y_ssm, D_SSM, 0, w_out_full, 1)], F32, "out_proj",
                        1024, 1024)

    dx2, dmixed, head_sums = _loss_head(x2d, mixed, gate, norm_f_gain.reshape(1, D_MODEL), tgt)
    g_w_out = _matmul_tn_pieces([y_attn, y_ssm], N_CHIPS * out_rows, dmixed, "out_proj_dw", 512)
    d_mix_in = _matmul(dmixed, w_out_full, BF16, "out_proj_dx", "nt", 1024, 1024, 1024)
    d_o, dz_attn, g_sb = _gated_norm_bwd(d_mix_in, 0, o_attn, proj, OFF_ZA, sb_norm_gain, False, "attn_gate_bwd")
    d_y, dz_ssm, g_ssm = _gated_norm_bwd(d_mix_in, 1, y_ssd, proj_zs, 0, ssm_norm_gain, True, "ssm_gate_bwd")
    dq, dk, dv = _attn_bwd(proj, d_o, lsum)
    dact, dld_col, dld_row, ddt_col, dd_cols = _ssd_bwd(*ssd_args, states, d_y)
    dxbc, g_conv_w, g_conv_b = _conv_bwd(proj, conv_w_full, conv_b, dact)
    ddtraw, dt_sums = _dt_bwd(dtraw, bias_row, arow, dld_col, _pad_lanes(dld_row.T), ddt_col)
    g_in_t = _matmul_tn_pieces([dq, dk, dv, dz_attn, dxbc], D_PROJ, h, "in_proj_dw", 256)
    g_in_t = _matmul_tn_rows(g_in_t, D_PROJ, ddtraw, h, DT_LO // LANE, "in_proj_dw_dt", LANE)
    g_zs_t = _matmul(dz_ssm, h, F32, "in_proj_dw_zs", "tn", 512, 1024, 4096)
    g_in_t = lax.dynamic_update_slice(g_in_t, g_zs_t, (ZS_LO, 0))
    dh_terms = [(dq, D_ATTN, 0, w_in_t, 0), (dk, D_ATTN, 0, w_in_t, 1), (dv, D_ATTN, 0, w_in_t, 2),
                (dz_attn, D_ATTN, 0, w_in_t, 3)]
    dh_terms += [(dxbc, 512, j, w_in_t, OFF_XBC // 512 + j) for j in range(D_XBC // 512)]
    dh_terms += [(dz_ssm, D_SSM, 0, w_zs_t, 0), (ddtraw, LANE, 0, w_dt_t, 0)]
    g_in_blocks = g_in_t.reshape(N_CHIPS, in_cols, D_MODEL)
    g_out_blocks = g_w_out.reshape(N_CHIPS, out_rows, D_MODEL)
    land_in, land_out = _send_to_sibling([g_in_blocks, g_out_blocks], "grads_to_sibling")
    chip_in = _add_my_half(g_in_blocks, land_in, my_c, "add_sibling_in")
    chip_out = _add_my_half(g_out_blocks, land_out, my_c, "add_sibling_out")
    dh, slots_in, slots_out = _matmul_sum(dh_terms, F32, "in_proj_dx_and_grads_between_chips", 512, 1024,
                                          exchange=[chip_in, chip_out])
    grad_x, in_sums = _rms_mod_bwd(x2d, dh, dx2, norm_in_gain, scale)

    g_a_log = dt_sums[1:2, :N_HEADS] * a_neg
    g_d_skip = jnp.sum(dd_cols.reshape(N_HEADS, HEAD_DIM), axis=1).reshape(1, N_HEADS)
    dmod = jnp.concatenate([in_sums[0:1], in_sums[1:2], head_sums[2:3]], axis=1)
    loss_part = 0.5 / D_MODEL * jnp.sum(head_sums[0:1], axis=1, keepdims=True)
    pieces = [dmod, in_sums[2:3], g_conv_w.reshape(1, CONV_K * D_XBC), g_conv_b, _pad_lanes(dt_sums[0:1, :N_HEADS]),
              _pad_lanes(g_a_log), _pad_lanes(g_d_skip), g_sb, g_ssm, head_sums[1:2], _pad_lanes(loss_part)]
    widths = [p.shape[1] for p in pieces]
    parts_all = _allgather8(jnp.concatenate(pieces, axis=1), "gather_small_grads")[:, 0, :]
    total = _sum8(parts_all)
    offs = [0]
    for w_ in widths:
        offs.append(offs[-1] + w_)
    tot = [total[:, offs[i]:offs[i + 1]] for i in range(len(pieces))]
    g_b_ada, g_norm_in, g_conv_w_full = tot[0], tot[1], tot[2].reshape(CONV_K, D_XBC)
    g_conv_b_t, g_dt_bias, g_a_log_t, g_d_skip_t = tot[3], tot[4][:, :N_HEADS], tot[5][:, :N_HEADS], tot[6][:, :N_HEADS]
    g_sb_t, g_ssm_t, g_norm_f, loss = tot[7], tot[8], tot[9], tot[10][0, 0]
    g_conv_w_shard = lax.dynamic_slice_in_dim(g_conv_w_full, chip * cw_cols, cw_cols, axis=1)
    dmod_shard = lax.dynamic_slice_in_dim(parts_all[:, :3 * D_MODEL], chip * ada_cols, ada_cols, axis=1)
    g_w_ada = _outer8(c_act_all.T, dmod_shard)

    own = lambda blocks: lax.dynamic_slice_in_dim(blocks, chip, 1, axis=0)
    slots_in = lax.dynamic_update_slice(slots_in, own(chip_in), (chip, 0, 0))
    slots_out = lax.dynamic_update_slice(slots_out, own(chip_out), (chip, 0, 0))
    half_in, half_out = _sum_slots(slots_in, "sum_chips_in"), _sum_slots(slots_out, "sum_chips_out")
    their_in, their_out = _swap_with_sibling([half_in, half_out], "grads_swap_sibling")
    south = ac_ == 0
    both = lambda mine, theirs: jnp.concatenate([jnp.where(south, mine, theirs), jnp.where(south, theirs, mine)],
                                                axis=1)
    g_w_in_t, g_w_out_shard = both(half_in, their_in), both(half_out, their_out)

    d_w_ada, nm_w_ada, nv_w_ada = _adamw(w_ada[0], g_w_ada, m_w_ada[0], v_w_ada[0], "adamw_w_ada")
    d_w_in, nm_w_in, nv_w_in = [r.T for r in _adamw(w_in[0].T, g_w_in_t, m_w_in[0].T, v_w_in[0].T, "adamw_w_in")]
    g_w_in = g_w_in_t.T
    d_w_out, nm_w_out, nv_w_out = _adamw(w_out[0], g_w_out_shard, m_w_out[0], v_w_out[0], "adamw_w_out")
    flat = lambda a: a.reshape(1, -1)
    small_w = [b_ada, norm_in_gain, conv_w[0], conv_b, dt_bias, a_log, d_skip, sb_norm_gain, ssm_norm_gain,
               norm_f_gain]
    small_m = [m_b_ada, m_norm_in_gain, m_conv_w[0], m_conv_b, m_dt_bias, m_a_log, m_d_skip, m_sb_norm_gain,
               m_ssm_norm_gain, m_norm_f_gain]
    small_v = [v_b_ada, v_norm_in_gain, v_conv_w[0], v_conv_b, v_dt_bias, v_a_log, v_d_skip, v_sb_norm_gain,
               v_ssm_norm_gain, v_norm_f_gain]
    small_g = [g_b_ada, g_norm_in, g_conv_w_shard, g_conv_b_t, g_dt_bias, g_a_log_t, g_d_skip_t, g_sb_t, g_ssm_t,
               g_norm_f]
    cat = lambda arrs: jnp.concatenate([flat(a) for a in arrs], axis=1)
    d_small, nm_small, nv_small = _adamw(cat(small_w), cat(small_g), cat(small_m), cat(small_v), "adamw_small")
    sizes = [a.size for a in small_w]
    soffs = [0]
    for n_ in sizes:
        soffs.append(soffs[-1] + n_)

    def split(packed):
        return [packed[0, soffs[i]:soffs[i + 1]].reshape(small_w[i].shape) for i in range(len(small_w))]

    def ordered(big_ada, big_in, big_out, smalls):
        (s_b_ada, s_norm_in, s_conv_w, s_conv_b, s_dt_bias, s_a_log, s_d_skip, s_sb, s_ssm, s_norm_f) = smalls
        return [big_ada[None], s_b_ada, s_norm_in, big_in[None], s_conv_w[None], s_conv_b, s_dt_bias, s_a_log,
                s_d_skip, s_sb, s_ssm, big_out[None], s_norm_f]

    grads = ordered(g_w_ada, g_w_in, g_w_out_shard,
                    [g.reshape(w_.shape) for g, w_ in zip(small_g, small_w)])
    deltas = ordered(d_w_ada, d_w_in, d_w_out, split(d_small))
    new_m = ordered(nm_w_ada, nm_w_in, nm_w_out, split(nm_small))
    new_v = ordered(nv_w_ada, nv_w_in, nv_w_out, split(nv_small))
    return (loss, grad_x[None], *grads, *deltas, *new_m, *new_v)
```

```python
import functools

import jax
import jax.numpy as jnp
from jax import lax
from jax.experimental import pallas as pl
from jax.experimental.pallas import tpu as pltpu

F32, BF16 = jnp.float32, jnp.bfloat16
MESH = pl.DeviceIdType.MESH
HI = lax.Precision.HIGHEST
NN = (((1,), (0,)), ((), ()))
NT = (((1,), (1,)), ((), ()))
TN = (((0,), (0,)), ((), ()))

D_MODEL = 1024
D_ATTN = 1024
D_SSM = 1024
HEAD_DIM = 64
N_HEADS = 16
N_PAIRS = 8
N_GROUPS = 2
D_STATE = 128
D_XBC = 1536
D_PROJ = 6672
D_MAIN = 5632
CONV_K = 4
CHUNK = 128
LANE = 128
N_CHIPS = 4
N_DEV = 8
NORM_EPS = 1e-6
ATTN_SCALE = HEAD_DIM ** -0.5
ATTN_TQ = 512
ATTN_TK = 256
LOG_ZERO = -110.0
ADAM_LR, ADAM_B1, ADAM_B2, ADAM_EPS, ADAM_WD, ADAM_STEP = 0.001, 0.9, 0.999, 1e-08, 0.01, 10
VMEM_LIMIT = 56 * 1024 * 1024

OFF_Q, OFF_K, OFF_V, OFF_ZA, OFF_XBC = 0, 1024, 2048, 3072, 4096
DT_LO = D_MAIN
ZS_LO = DT_LO + N_HEADS


def _cparams(sem=None):
    return pltpu.CompilerParams(dimension_semantics=sem, vmem_limit_bytes=VMEM_LIMIT)


def _sigmoid(x):
    return 1.0 / (1.0 + jnp.exp(-x))


def _softplus(x):
    return jnp.maximum(x, 0.0) + jnp.log(1.0 + jnp.exp(-jnp.abs(x)))


def _coords():
    return lax.axis_index("x"), lax.axis_index("y"), lax.axis_index("c")


def _allgather8(v, name):
    n = v.shape[-1]

    def body(v_ref, out_ref, send_sems, recv_sems, local_sem):
        x, y, c = _coords()
        me = 4 * x + 2 * y + c
        mine = pltpu.make_async_copy(v_ref, out_ref.at[me], local_sem)
        mine.start()
        sends, recvs = [], []
        for j in range(1, N_DEV):
            px = 1 - x if (j >> 2) & 1 else x
            py = 1 - y if (j >> 1) & 1 else y
            pc = 1 - c if j & 1 else c
            peer = (px, py, pc)
            sends.append(pltpu.make_async_remote_copy(
                src_ref=v_ref, dst_ref=out_ref.at[me], send_sem=send_sems.at[j - 1],
                recv_sem=recv_sems.at[j - 1], device_id=peer, device_id_type=MESH))
            recvs.append(pltpu.make_async_remote_copy(
                src_ref=v_ref, dst_ref=out_ref.at[4 * px + 2 * py + pc], send_sem=send_sems.at[j - 1],
                recv_sem=recv_sems.at[j - 1], device_id=peer, device_id_type=MESH))
        for s in sends:
            s.start()
        for r in recvs:
            r.wait_recv()
        for s in sends:
            s.wait_send()
        mine.wait()

    vm = pl.BlockSpec(memory_space=pltpu.VMEM)
    return pl.pallas_call(
        body, name=name, out_shape=jax.ShapeDtypeStruct((N_DEV, 1, n), F32),
        in_specs=[vm], out_specs=vm,
        scratch_shapes=[pltpu.SemaphoreType.DMA((N_DEV - 1,)), pltpu.SemaphoreType.DMA((N_DEV - 1,)),
                        pltpu.SemaphoreType.DMA(())],
    )(v)


def _other_chips(x, y):
    chips = [(1 - x, y), (x, 1 - y), (1 - x, 1 - y)]
    return chips, [2 * cx + cy for cx, cy in chips]


def _half_cols(width, which):
    half = width // 2
    return pl.ds(pl.multiple_of(which * half, half), half)


def _gather_shards(arrs, name):
    n = len(arrs)

    def body(*refs):
        ins, outs = refs[:n], refs[n:2 * n]
        send_sems, recv_sems = refs[2 * n:]
        x, y, c = _coords()
        k = 2 * x + y
        chips, chip_idx = _other_chips(x, y)
        sibling = (x, y, 1 - c)
        sends = []
        for a in range(n):
            mine = _half_cols(arrs[a].shape[-1], c)
            for j in range(3):
                cp = pltpu.make_async_remote_copy(
                    src_ref=ins[a].at[:, mine], dst_ref=outs[a].at[k, :, mine], send_sem=send_sems.at[6 * a + j],
                    recv_sem=recv_sems.at[6 * a + j], device_id=(*chips[j], c), device_id_type=MESH)
                cp.start()
                sends.append(cp)
        for a in range(n):
            mine = _half_cols(arrs[a].shape[-1], c)
            for j in range(3):
                landed = outs[a].at[chip_idx[j], :, mine]
                pltpu.make_async_remote_copy(
                    src_ref=landed, dst_ref=landed, send_sem=send_sems.at[6 * a + j],
                    recv_sem=recv_sems.at[6 * a + j], device_id=(*chips[j], c), device_id_type=MESH).wait_recv()
                fwd = pltpu.make_async_remote_copy(
                    src_ref=landed, dst_ref=landed, send_sem=send_sems.at[6 * a + 3 + j],
                    recv_sem=recv_sems.at[6 * a + 3 + j], device_id=sibling, device_id_type=MESH)
                fwd.start()
                sends.append(fwd)
        for a in range(n):
            theirs = _half_cols(arrs[a].shape[-1], 1 - c)
            for j in range(3):
                landed = outs[a].at[chip_idx[j], :, theirs]
                pltpu.make_async_remote_copy(
                    src_ref=landed, dst_ref=landed, send_sem=send_sems.at[6 * a + 3 + j],
                    recv_sem=recv_sems.at[6 * a + 3 + j], device_id=sibling, device_id_type=MESH).wait_recv()
        for cp in sends:
            cp.wait_send()

    hbm = pl.BlockSpec(memory_space=pl.ANY)
    return pl.pallas_call(
        body, name=name,
        out_shape=tuple(jax.ShapeDtypeStruct((N_CHIPS,) + a.shape, a.dtype) for a in arrs),
        in_specs=[hbm] * n, out_specs=tuple([hbm] * n),
        scratch_shapes=[pltpu.SemaphoreType.DMA((6 * n,)), pltpu.SemaphoreType.DMA((6 * n,))],
    )(*arrs)


def _front(small8, w_ada_shard, b_shard, w_shard):
    n = small8.shape[1]
    cols = w_ada_shard.shape[1]

    def body(small_ref, wa_ref, b_ref, w_ref, all_ref, mod_ref, act_ref, wout_ref, part_ref,
             send_a, recv_a, send_b, recv_b, send_w, recv_w):
        x, y, c = _coords()
        me = 4 * x + 2 * y + c
        k = 2 * x + y
        chips, chip_idx = _other_chips(x, y)
        sibling = (x, y, 1 - c)
        mine = _half_cols(w_shard.shape[-1], c)
        theirs = _half_cols(w_shard.shape[-1], 1 - c)

        def w_copy(j, landed, forward):
            if forward:
                ref = wout_ref.at[chip_idx[j], :, mine] if not landed else wout_ref.at[chip_idx[j], :, theirs]
                return pltpu.make_async_remote_copy(src_ref=ref, dst_ref=ref, send_sem=send_w.at[3 + j],
                                                    recv_sem=recv_w.at[3 + j], device_id=sibling, device_id_type=MESH)
            dst = wout_ref.at[chip_idx[j], :, mine] if landed else wout_ref.at[k, :, mine]
            src = dst if landed else w_ref.at[:, mine]
            return pltpu.make_async_remote_copy(src_ref=src, dst_ref=dst, send_sem=send_w.at[j], recv_sem=recv_w.at[j],
                                                device_id=(*chips[j], c), device_id_type=MESH)

        for j in range(3):
            w_copy(j, False, False).start()

        def gather8(src_ref, dst_ref, send_sems, recv_sems):
            slot = lambda d: pl.ds(pl.multiple_of(8 * d, 8), 8)
            dst_ref[slot(me), :] = src_ref[...]
            sends, recvs = [], []
            for j in range(1, N_DEV):
                px = 1 - x if (j >> 2) & 1 else x
                py = 1 - y if (j >> 1) & 1 else y
                pc = 1 - c if j & 1 else c
                sends.append(pltpu.make_async_remote_copy(
                    src_ref=src_ref, dst_ref=dst_ref.at[slot(me)], send_sem=send_sems.at[j - 1],
                    recv_sem=recv_sems.at[j - 1], device_id=(px, py, pc), device_id_type=MESH))
                recvs.append(pltpu.make_async_remote_copy(
                    src_ref=src_ref, dst_ref=dst_ref.at[slot(4 * px + 2 * py + pc)], send_sem=send_sems.at[j - 1],
                    recv_sem=recv_sems.at[j - 1], device_id=(px, py, pc), device_id_type=MESH))
            for s_ in sends:
                s_.start()
            for r_ in recvs:
                r_.wait_recv()
            for s_ in sends:
                s_.wait_send()

        gather8(small_ref, all_ref, send_a, recv_a)
        sub = lax.broadcasted_iota(jnp.int32, (8, 1), 0)
        cv = jnp.where(sub == 0, all_ref[0:8, :D_MODEL], 0.0)
        for d in range(1, N_DEV):
            cv = jnp.where(sub == d, all_ref[8 * d:8 * d + 8, :D_MODEL], cv)
        act = cv * _sigmoid(cv)
        act_ref[...] = act
        part_ref[...] = jnp.dot(act, wa_ref[...], preferred_element_type=F32, precision=HI) + b_ref[...]
        gather8(part_ref, mod_ref, send_b, recv_b)

        for j in range(3):
            w_copy(j, True, False).wait_recv()
            w_copy(j, False, True).start()
        for j in range(3):
            w_copy(j, True, True).wait_recv()
        for j in range(3):
            w_copy(j, False, False).wait_send()
            w_copy(j, False, True).wait_send()

    vm = pl.BlockSpec(memory_space=pltpu.VMEM)
    hbm = pl.BlockSpec(memory_space=pl.ANY)
    return pl.pallas_call(
        body, name="front_exchanges",
        out_shape=(jax.ShapeDtypeStruct((8 * N_DEV, n), F32), jax.ShapeDtypeStruct((8 * N_DEV, cols), F32),
                   jax.ShapeDtypeStruct((N_DEV, D_MODEL), F32),
                   jax.ShapeDtypeStruct((N_CHIPS,) + w_shard.shape, w_shard.dtype)),
        in_specs=[vm, vm, vm, hbm], out_specs=(vm, vm, vm, hbm),
        scratch_shapes=[pltpu.VMEM((N_DEV, cols), F32)] + [pltpu.SemaphoreType.DMA((N_DEV - 1,))] * 4
        + [pltpu.SemaphoreType.DMA((6,))] * 2,
        compiler_params=_cparams(),
    )(small8, w_ada_shard, b_shard, w_shard)


def _send_to_sibling(arrs, name):
    n = len(arrs)

    def body(*refs):
        ins, outs = refs[:n], refs[n:2 * n]
        send_sems, recv_sems = refs[2 * n:]
        x, y, c = _coords()
        cps = []
        for a in range(n):
            cp = pltpu.make_async_remote_copy(
                src_ref=ins[a].at[:, :, _half_cols(arrs[a].shape[-1], 1 - c)], dst_ref=outs[a],
                send_sem=send_sems.at[a], recv_sem=recv_sems.at[a], device_id=(x, y, 1 - c), device_id_type=MESH)
            cp.start()
            cps.append(cp)
        for cp in cps:
            cp.wait()

    hbm = pl.BlockSpec(memory_space=pl.ANY)
    return pl.pallas_call(
        body, name=name,
        out_shape=tuple(jax.ShapeDtypeStruct(a.shape[:-1] + (a.shape[-1] // 2,), a.dtype) for a in arrs),
        in_specs=[hbm] * n, out_specs=tuple([hbm] * n),
        scratch_shapes=[pltpu.SemaphoreType.DMA((n,)), pltpu.SemaphoreType.DMA((n,))],
    )(*arrs)


def _swap_with_sibling(arrs, name):
    n = len(arrs)

    def body(*refs):
        ins, outs = refs[:n], refs[n:2 * n]
        send_sems, recv_sems = refs[2 * n:]
        x, y, c = _coords()
        cps = []
        for a in range(n):
            cp = pltpu.make_async_remote_copy(
                src_ref=ins[a], dst_ref=outs[a], send_sem=send_sems.at[a], recv_sem=recv_sems.at[a],
                device_id=(x, y, 1 - c), device_id_type=MESH)
            cp.start()
            cps.append(cp)
        for cp in cps:
            cp.wait()

    hbm = pl.BlockSpec(memory_space=pl.ANY)
    return pl.pallas_call(
        body, name=name,
        out_shape=tuple(jax.ShapeDtypeStruct(a.shape, a.dtype) for a in arrs),
        in_specs=[hbm] * n, out_specs=tuple([hbm] * n),
        scratch_shapes=[pltpu.SemaphoreType.DMA((n,)), pltpu.SemaphoreType.DMA((n,))],
    )(*arrs)


def _row_tile(rows, cols, n_arrays):
    budget = VMEM_LIMIT // 2
    t = rows
    while t % 16 == 0 and t * cols * 4 * n_arrays * 2 > budget:
        t //= 2
    return t


def _add_my_half(g, landed, my_c, name):
    nb, r, cdim = g.shape
    half = cdim // 2
    tr = _row_tile(r, half, 3)

    def body(c_ref, g_ref, l_ref, o_ref):
        o_ref[...] = (g_ref[...] + l_ref[...]).astype(BF16)

    spec = pl.BlockSpec((None, tr, half), lambda b, i, c_ref: (b, i, 0))
    return pl.pallas_call(
        body, name=name, out_shape=jax.ShapeDtypeStruct((nb, r, half), BF16),
        grid_spec=pltpu.PrefetchScalarGridSpec(
            num_scalar_prefetch=1, grid=(nb, r // tr),
            in_specs=[pl.BlockSpec((None, tr, half), lambda b, i, c_ref: (b, i, c_ref[0])), spec],
            out_specs=spec),
        compiler_params=_cparams(("parallel", "parallel")),
    )(my_c, g, landed)


def _sum_slots(a, name):
    nb, r, cdim = a.shape
    tr = _row_tile(r, cdim, 4)

    def body(a_ref, o_ref):
        o_ref[...] = ((a_ref[0].astype(F32) + a_ref[1].astype(F32)) + a_ref[2].astype(F32)) + a_ref[3].astype(F32)

    return pl.pallas_call(
        body, name=name, out_shape=jax.ShapeDtypeStruct((r, cdim), F32), grid=(r // tr,),
        in_specs=[pl.BlockSpec((nb, tr, cdim), lambda i: (0, i, 0))],
        out_specs=pl.BlockSpec((tr, cdim), lambda i: (i, 0)),
        compiler_params=_cparams(("parallel",)),
    )(a)


def _adamw(w, g, m, v, name):
    r, cdim = w.shape
    tr = _row_tile(r, cdim, 7)
    tc = cdim
    if tr == r and r > 8:
        while tc % (2 * LANE) == 0 and r * tc * 4 * 7 * 2 > VMEM_LIMIT // 2:
            tc //= 2

    def body(w_ref, g_ref, m_ref, v_ref, d_ref, nm_ref, nv_ref):
        gv = g_ref[...]
        nm = ADAM_B1 * m_ref[...] + (1.0 - ADAM_B1) * gv
        nv = ADAM_B2 * v_ref[...] + (1.0 - ADAM_B2) * (gv * gv)
        m_hat = nm / (1.0 - ADAM_B1 ** ADAM_STEP)
        v_hat = nv / (1.0 - ADAM_B2 ** ADAM_STEP)
        d_ref[...] = -ADAM_LR * (m_hat / (jnp.sqrt(v_hat) + ADAM_EPS) + ADAM_WD * w_ref[...])
        nm_ref[...] = nm
        nv_ref[...] = nv

    spec = pl.BlockSpec((tr, tc), lambda i, j: (i, j))
    shp = jax.ShapeDtypeStruct((r, cdim), F32)
    return pl.pallas_call(
        body, name=name, out_shape=(shp, shp, shp), grid=(r // tr, cdim // tc),
        in_specs=[spec] * 4, out_specs=(spec, spec, spec),
        compiler_params=_cparams(("parallel", "parallel")),
    )(w, g, m, v)


def _matmul(a, b, out_dtype, name, mode, tm, tn, tk, extra=None, n_out=None):
    dims = {"nn": NN, "nt": NT, "tn": TN}[mode]
    if mode == "tn":
        kdim, m = a.shape
    else:
        m, kdim = a.shape
    n = n_out if n_out is not None else (b.shape[0] if mode == "nt" else b.shape[1])
    tm, tn, tk = min(tm, m), min(tn, n), min(tk, kdim)
    nk = kdim // tk
    a_spec = (pl.BlockSpec((tk, tm), lambda i, j, k: (k, i)) if mode == "tn"
              else pl.BlockSpec((tm, tk), lambda i, j, k: (i, k)))
    b_spec = (pl.BlockSpec((tn, tk), lambda i, j, k: (j, k)) if mode == "nt"
              else pl.BlockSpec((tk, tn), lambda i, j, k: (k, j)))
    in_specs, operands = [a_spec, b_spec], [a, b]
    if extra is not None:
        a2, b2 = extra
        k2 = a2.shape[0] if mode == "tn" else a2.shape[1]
        in_specs.append(pl.BlockSpec((k2, tm), lambda i, j, k: (0, i)) if mode == "tn"
                        else pl.BlockSpec((tm, k2), lambda i, j, k: (i, 0)))
        in_specs.append(pl.BlockSpec((tn, k2), lambda i, j, k: (j, 0)) if mode == "nt"
                        else pl.BlockSpec((k2, tn), lambda i, j, k: (0, j)))
        operands += [a2, b2]

    def body_one_block(*refs):
        acc = lax.dot_general(refs[0][...], refs[1][...], dims, preferred_element_type=F32)
        if extra is not None:
            acc += lax.dot_general(refs[2][...], refs[3][...], dims, preferred_element_type=F32)
        refs[-1][...] = acc.astype(out_dtype)

    if nk == 1:
        return pl.pallas_call(
            body_one_block, name=name, out_shape=jax.ShapeDtypeStruct((m, n), out_dtype), grid=(m // tm, n // tn, 1),
            in_specs=in_specs, out_specs=pl.BlockSpec((tm, tn), lambda i, j, k: (i, j)),
            compiler_params=_cparams(("parallel", "parallel", "arbitrary")),
        )(*operands)

    def body(*refs):
        if extra is not None:
            a_ref, b_ref, a2_ref, b2_ref, o_ref, acc_ref = refs
        else:
            a_ref, b_ref, o_ref, acc_ref = refs
        k = pl.program_id(2)

        @pl.when(k == 0)
        def _():
            if extra is not None:
                acc_ref[...] = lax.dot_general(a2_ref[...], b2_ref[...], dims, preferred_element_type=F32)
            else:
                acc_ref[...] = jnp.zeros_like(acc_ref)

        acc_ref[...] += lax.dot_general(a_ref[...], b_ref[...], dims, preferred_element_type=F32)

        @pl.when(k == nk - 1)
        def _():
            o_ref[...] = acc_ref[...].astype(out_dtype)

    return pl.pallas_call(
        body, name=name, out_shape=jax.ShapeDtypeStruct((m, n), out_dtype), grid=(m // tm, n // tn, nk),
        in_specs=in_specs, out_specs=pl.BlockSpec((tm, tn), lambda i, j, k: (i, j)),
        scratch_shapes=[pltpu.VMEM((tm, tn), F32)],
        compiler_params=_cparams(("parallel", "parallel", "arbitrary")),
    )(*operands)


def _matmul_sum(terms, out_dtype, name, tm, tn, exchange=None, trans_b=False, n_out=None):
    m = terms[0][0].shape[0]
    n = n_out if n_out is not None else terms[0][3].shape[0 if trans_b else 1]
    tm, tn = min(tm, m), min(tn, n)
    gm, gn = m // tm, n // tn
    nt = len(terms)
    dims = NT if trans_b else NN
    in_specs, operands = [], []
    for a, ka, ia, b, ib in terms:
        in_specs.append(pl.BlockSpec((tm, ka), functools.partial(lambda i, j, ia: (i, ia), ia=ia)))
        if trans_b:
            in_specs.append(pl.BlockSpec((tn, ka), functools.partial(lambda i, j, ib: (j, ib), ib=ib)))
        else:
            in_specs.append(pl.BlockSpec((ka, tn), functools.partial(lambda i, j, ib: (ib, j), ib=ib)))
        operands += [a, b]
    sent = [] if exchange is None else list(exchange)
    ns = len(sent)
    hbm = pl.BlockSpec(memory_space=pl.ANY)

    def body(*refs):
        o_ref = refs[2 * nt + ns]
        if ns:
            ins, outs = refs[2 * nt:2 * nt + ns], refs[2 * nt + ns + 1:2 * nt + 2 * ns + 1]
            send_sems, recv_sems = refs[2 * nt + 2 * ns + 1:]
            x, y, c = _coords()
            k = 2 * x + y
            chips, chip_idx = _other_chips(x, y)
            step = pl.program_id(0) * gn + pl.program_id(1)

            def copies(a, j, landed):
                dst = outs[a].at[chip_idx[j]] if landed else outs[a].at[k]
                src = dst if landed else ins[a].at[chip_idx[j]]
                return pltpu.make_async_remote_copy(
                    src_ref=src, dst_ref=dst, send_sem=send_sems.at[3 * a + j], recv_sem=recv_sems.at[3 * a + j],
                    device_id=(*chips[j], c), device_id_type=MESH)

            @pl.when(step == 0)
            def _():
                for a in range(ns):
                    for j in range(3):
                        copies(a, j, False).start()

        acc = lax.dot_general(refs[0][...], refs[1][...], dims, preferred_element_type=F32)
        for t in range(1, nt):
            acc += lax.dot_general(refs[2 * t][...], refs[2 * t + 1][...], dims, preferred_element_type=F32)
        o_ref[...] = acc.astype(out_dtype)

        if ns:
            @pl.when(step == gm * gn - 1)
            def _():
                for a in range(ns):
                    for j in range(3):
                        copies(a, j, True).wait_recv()
                for a in range(ns):
                    for j in range(3):
                        copies(a, j, False).wait_send()

    main = jax.ShapeDtypeStruct((m, n), out_dtype)
    tile = pl.BlockSpec((tm, tn), lambda i, j: (i, j))
    if not ns:
        return pl.pallas_call(
            body, name=name, out_shape=main, grid=(gm, gn), in_specs=in_specs, out_specs=tile,
            compiler_params=_cparams(("parallel", "parallel")),
        )(*operands)
    return pl.pallas_call(
        body, name=name, out_shape=(main, *[jax.ShapeDtypeStruct(a.shape, a.dtype) for a in sent]), grid=(gm, gn),
        in_specs=in_specs + [hbm] * ns, out_specs=(tile, *[hbm] * ns),
        scratch_shapes=[pltpu.SemaphoreType.DMA((3 * ns,)), pltpu.SemaphoreType.DMA((3 * ns,))],
        compiler_params=_cparams(("arbitrary", "arbitrary")),
    )(*operands, *sent)


def _matmul_tn_pieces(pieces, rows, b, name, tm):
    kdim, n = b.shape
    tm = min(tm, min(p.shape[1] for p in pieces))
    tiles = [p.shape[1] // tm for p in pieces]
    first = [sum(tiles[:i]) for i in range(len(pieces))]

    def body(*refs):
        b_ref, o_ref = refs[-2:]
        i = pl.program_id(0)
        for p in range(len(pieces)):
            @pl.when(jnp.logical_and(i >= first[p], i < first[p] + tiles[p]))
            def _(p=p):
                o_ref[...] = lax.dot_general(refs[p][...], b_ref[...], TN, preferred_element_type=F32)

    in_specs = [pl.BlockSpec((kdim, tm), functools.partial(lambda i, lo, cnt: (0, jnp.clip(i - lo, 0, cnt - 1)),
                                                           lo=first[p], cnt=tiles[p])) for p in range(len(pieces))]
    return pl.pallas_call(
        body, name=name, out_shape=jax.ShapeDtypeStruct((rows, n), F32), grid=(sum(tiles),),
        in_specs=in_specs + [pl.BlockSpec((kdim, n), lambda i: (0, 0))],
        out_specs=pl.BlockSpec((tm, n), lambda i: (i, 0)),
        compiler_params=_cparams(("arbitrary",)),
    )(*pieces, b)


def _matmul_tn_rows(buf, rows, a, b, row_blk, name, tm):
    kdim, m = a.shape
    n = b.shape[1]
    tm = min(tm, m)

    def body(*refs):
        a_ref, b_ref, o_ref = refs[-3:]
        o_ref[...] = lax.dot_general(a_ref[...], b_ref[...], TN, preferred_element_type=F32)

    in_specs = [pl.BlockSpec((kdim, tm), lambda i: (0, i)), pl.BlockSpec((kdim, n), lambda i: (0, 0))]
    operands = [a, b]
    if buf is not None:
        in_specs.insert(0, pl.BlockSpec(memory_space=pl.ANY))
        operands.insert(0, buf)
    return pl.pallas_call(
        body, name=name, out_shape=jax.ShapeDtypeStruct((rows, n), F32), grid=(m // tm,),
        in_specs=in_specs, out_specs=pl.BlockSpec((tm, n), lambda i: (row_blk + i, 0)),
        input_output_aliases={} if buf is None else {0: 0},
        compiler_params=_cparams(("parallel",)),
    )(*operands)


def _ada_mod(c_all, w_shard, b_shard):
    nb, d = c_all.shape
    cols = w_shard.shape[1]

    def body(c_ref, w_ref, b_ref, mod_ref, act_ref):
        cv = c_ref[...]
        act = cv * _sigmoid(cv)
        act_ref[...] = act
        mod_ref[...] = jnp.dot(act, w_ref[...], preferred_element_type=F32, precision=HI) + b_ref[...]

    return pl.pallas_call(
        body, name="ada_mod",
        out_shape=(jax.ShapeDtypeStruct((nb, cols), F32), jax.ShapeDtypeStruct((nb, d), F32)),
        compiler_params=_cparams(),
    )(c_all, w_shard, b_shard)


def _rms_mod_fwd(x, gain, scale, shift):
    s, d = x.shape
    tm = min(512, s)

    def body(x_ref, g_ref, sc_ref, sh_ref, h_ref):
        xv = x_ref[...]
        r = lax.rsqrt(jnp.mean(xv * xv, axis=-1, keepdims=True) + NORM_EPS)
        h_ref[...] = (xv * r * g_ref[...] * (1.0 + sc_ref[...]) + sh_ref[...]).astype(BF16)

    row = pl.BlockSpec((1, d), lambda i: (0, 0))
    tile = pl.BlockSpec((tm, d), lambda i: (i, 0))
    return pl.pallas_call(
        body, name="rms_mod_fwd", out_shape=jax.ShapeDtypeStruct((s, d), BF16), grid=(s // tm,),
        in_specs=[tile, row, row, row], out_specs=tile, compiler_params=_cparams(("parallel",)),
    )(x, gain, scale, shift)


def _rms_mod_bwd(x, dh, dres, gain, scale):
    s, d = x.shape
    tm = min(512, s)

    def body(x_ref, dh_ref, dres_ref, g_ref, sc_ref, dx_ref, sums_ref):
        @pl.when(pl.program_id(0) == 0)
        def _():
            sums_ref[...] = jnp.zeros_like(sums_ref)

        xv, dhv = x_ref[...], dh_ref[...]
        r = lax.rsqrt(jnp.mean(xv * xv, axis=-1, keepdims=True) + NORM_EPS)
        nrm = xv * r
        g, one_sc = g_ref[...], 1.0 + sc_ref[...]
        dn = dhv * g * one_sc
        dx_ref[...] = r * (dn - nrm * jnp.mean(dn * nrm, axis=-1, keepdims=True)) + dres_ref[...]
        dhn = dhv * nrm
        sums_ref[0:1, :] += jnp.sum(dhv, axis=0, keepdims=True)
        sums_ref[1:2, :] += jnp.sum(dhn * g, axis=0, keepdims=True)
        sums_ref[2:3, :] += jnp.sum(dhn * one_sc, axis=0, keepdims=True)

    row = pl.BlockSpec((1, d), lambda i: (0, 0))
    tile = pl.BlockSpec((tm, d), lambda i: (i, 0))
    return pl.pallas_call(
        body, name="rms_mod_bwd",
        out_shape=(jax.ShapeDtypeStruct((s, d), F32), jax.ShapeDtypeStruct((3, d), F32)), grid=(s // tm,),
        in_specs=[tile, tile, tile, row, row], out_specs=(tile, pl.BlockSpec((3, d), lambda i: (0, 0))),
        compiler_params=_cparams(("arbitrary",)),
    )(x, dh, dres, gain, scale)


def _loss_head(x, mixed, gate, gain_f, target):
    s, d = x.shape
    tm = min(512, s)

    def body(x_ref, mx_ref, gt_ref, gf_ref, t_ref, dx2_ref, dmx_ref, sums_ref):
        @pl.when(pl.program_id(0) == 0)
        def _():
            sums_ref[...] = jnp.zeros_like(sums_ref)

        mx, gt, gf = mx_ref[...], gt_ref[...], gf_ref[...]
        x2 = x_ref[...] + gt * mx
        r = lax.rsqrt(jnp.mean(x2 * x2, axis=-1, keepdims=True) + NORM_EPS)
        nrm = x2 * r
        err = nrm * gf - t_ref[...]
        dyf = err * (1.0 / d)
        dn = dyf * gf
        dx2 = r * (dn - nrm * jnp.mean(dn * nrm, axis=-1, keepdims=True))
        dx2_ref[...] = dx2
        dmx_ref[...] = (dx2 * gt).astype(BF16)
        sums_ref[0:1, :] += jnp.sum(err * err, axis=0, keepdims=True)
        sums_ref[1:2, :] += jnp.sum(dyf * nrm, axis=0, keepdims=True)
        sums_ref[2:3, :] += jnp.sum(dx2 * mx, axis=0, keepdims=True)

    row = pl.BlockSpec((1, d), lambda i: (0, 0))
    tile = pl.BlockSpec((tm, d), lambda i: (i, 0))
    return pl.pallas_call(
        body, name="loss_head",
        out_shape=(jax.ShapeDtypeStruct((s, d), F32), jax.ShapeDtypeStruct((s, d), BF16),
                   jax.ShapeDtypeStruct((3, d), F32)),
        grid=(s // tm,), in_specs=[tile, tile, row, row, tile],
        out_specs=(tile, tile, pl.BlockSpec((3, d), lambda i: (0, 0))),
        compiler_params=_cparams(("arbitrary",)),
    )(x, mixed, gate, gain_f, target)


def _silu_grad(z, sg):
    return sg * (1.0 + z * (1.0 - sg))


def _gated_norm_fwd(o, proj, z_off, gain, gate_inside, name):
    s, d = o.shape
    tm = min(512, s)
    zb = z_off // d

    def body(o_ref, z_ref, g_ref, y_ref):
        z = z_ref[...].astype(F32)
        sz = z * _sigmoid(z)
        u = o_ref[...] * sz if gate_inside else o_ref[...]
        r = lax.rsqrt(jnp.mean(u * u, axis=-1, keepdims=True) + NORM_EPS)
        y = u * r * g_ref[...]
        y_ref[...] = (y if gate_inside else y * sz).astype(BF16)

    tile = pl.BlockSpec((tm, d), lambda i: (i, 0))
    return pl.pallas_call(
        body, name=name, out_shape=jax.ShapeDtypeStruct((s, d), BF16), grid=(s // tm,),
        in_specs=[tile, pl.BlockSpec((tm, d), lambda i: (i, zb)), pl.BlockSpec((1, d), lambda i: (0, 0))],
        out_specs=tile, compiler_params=_cparams(("parallel",)),
    )(o, proj, gain)


def _gated_norm_bwd(dy_all, dy_blk, o, proj, z_off, gain, gate_inside, name):
    s, d = o.shape
    tm = min(512, s)
    zb = z_off // d

    def body(dy_ref, o_ref, z_ref, g_ref, do_ref, dz_ref, dg_ref):
        @pl.when(pl.program_id(0) == 0)
        def _():
            dg_ref[...] = jnp.zeros_like(dg_ref)

        z = z_ref[...].astype(F32)
        sg = _sigmoid(z)
        sz = z * sg
        ov, dy, g = o_ref[...], dy_ref[...].astype(F32), g_ref[...]
        u = ov * sz if gate_inside else ov
        r = lax.rsqrt(jnp.mean(u * u, axis=-1, keepdims=True) + NORM_EPS)
        nrm = u * r
        if gate_inside:
            dg_ref[...] += jnp.sum(dy * nrm, axis=0, keepdims=True)
            dn = dy * g
        else:
            dg_ref[...] += jnp.sum(dy * nrm * sz, axis=0, keepdims=True)
            dn = dy * g * sz
        du = r * (dn - nrm * jnp.mean(dn * nrm, axis=-1, keepdims=True))
        if gate_inside:
            do_ref[...] = du * sz
            dz_ref[...] = (du * ov * _silu_grad(z, sg)).astype(BF16)
        else:
            do_ref[...] = du
            dz_ref[...] = (dy * nrm * g * _silu_grad(z, sg)).astype(BF16)

    tile = pl.BlockSpec((tm, d), lambda i: (i, 0))
    row = pl.BlockSpec((1, d), lambda i: (0, 0))
    return pl.pallas_call(
        body, name=name,
        out_shape=(jax.ShapeDtypeStruct((s, d), F32), jax.ShapeDtypeStruct((s, d), BF16),
                   jax.ShapeDtypeStruct((1, d), F32)),
        grid=(s // tm,),
        in_specs=[pl.BlockSpec((tm, d), lambda i: (i, dy_blk)), tile, pl.BlockSpec((tm, d), lambda i: (i, zb)), row],
        out_specs=(tile, tile, row), compiler_params=_cparams(("arbitrary",)),
    )(dy_all, o, proj, gain)


def _sb_logits(qh, kb):
    z = lax.dot_general(qh, kb, NT, preferred_element_type=F32)
    neg_abs = lax.bitcast_convert_type(lax.bitcast_convert_type(z, jnp.uint32) | jnp.uint32(0x80000000), F32)
    lb = jnp.minimum(z, 0.0) - jnp.log(1.0 + jnp.exp(neg_abs))
    return lb, lb - z


def _attn_consts(tk):
    lane = lax.broadcasted_iota(jnp.int32, (1, LANE), 1)
    row = lax.broadcasted_iota(jnp.int32, (tk, tk), 0)
    col = lax.broadcasted_iota(jnp.int32, (tk, tk), 1)
    return (lane < HEAD_DIM, lane >= HEAD_DIM), row, col


def _band_mask(rows, tk):
    return lax.broadcasted_iota(jnp.int32, (rows, tk), 1) < lax.broadcasted_iota(jnp.int32, (rows, tk), 0)


def _attn_fwd(proj):
    s = proj.shape[0]
    tq, tk = min(ATTN_TQ, s), min(ATTN_TK, s)
    r = tq // tk

    def body(q_ref, k_ref, v_ref, o_ref, l_ref, acc_ref, run_ref):
        i = pl.program_id(1)
        head_mask, row, col = _attn_consts(tk)
        later = (row > col).astype(BF16)
        q = q_ref[...] * ATTN_SCALE
        qh = [jnp.where(m, q, jnp.zeros_like(q)) for m in head_mask]
        acc_ref[...] = jnp.zeros_like(acc_ref)
        run_ref[...] = jnp.zeros_like(run_ref)

        def block(j, lo, hi, band):
            start = pl.multiple_of(j * tk, tk)
            kb = k_ref[pl.ds(start, tk), :]
            vb = v_ref[pl.ds(start, tk), :]
            rows = slice(lo, hi)
            causal = _band_mask(hi - lo, tk) if band else None
            hs = range(2)
            logits = [_sb_logits(qh[h][rows], kb) for h in hs]
            lb = [logits[h][0] for h in hs]
            l1m = [logits[h][1] if causal is None else jnp.where(causal, logits[h][1], 0.0) for h in hs]
            tail = [jnp.dot(l1m[h].astype(BF16), later, preferred_element_type=F32) + run_ref[h, rows] for h in hs]
            w = [jnp.exp(lb[h] + tail[h]) for h in hs]
            if causal is not None:
                w = [jnp.where(causal, w[h], 0.0) for h in hs]
            vh = [jnp.where(head_mask[h], vb, jnp.zeros_like(vb)) for h in hs]
            acc_ref[rows, :] += (jnp.dot(w[0].astype(BF16), vh[0], preferred_element_type=F32)
                                 + jnp.dot(w[1].astype(BF16), vh[1], preferred_element_type=F32))
            for h in hs:
                run_ref[h, rows] += jnp.sum(l1m[h], axis=1, keepdims=True)

        for b in reversed(range(r)):
            block(i * r + b, b * tk, tq, True)
        n_full = i * r
        half = tq // 2

        def more(c):
            return jnp.logical_and(c[0] < n_full, c[1] > LOG_ZERO)

        def step_all(c):
            block(n_full - 1 - c[0], 0, tq, False)
            return c[0] + 1, jnp.max(run_ref[:, half:, :])

        def step_upper(c):
            block(n_full - 1 - c[0], 0, half, False)
            return c[0] + 1, jnp.max(run_ref[:, :half, :])

        seen_all, _ = lax.while_loop(more, step_all, (jnp.int32(0), jnp.max(run_ref[:, half:, :])))
        seen, _ = lax.while_loop(more, step_upper, (seen_all, jnp.max(run_ref[:, :half, :])))
        o_ref[...] = acc_ref[...]
        lane = lax.broadcasted_iota(jnp.int32, (1, LANE), 1)
        first = jnp.where(lane < 3 * HEAD_DIM // 4, n_full - seen, n_full - seen_all).astype(F32)
        l_ref[...] = jnp.where(lane < HEAD_DIM // 2, run_ref[0], jnp.where(lane < HEAD_DIM, first, run_ref[1]))

    kq, kk, kv = OFF_Q // LANE, OFF_K // LANE, OFF_V // LANE
    tile = pl.BlockSpec((tq, LANE), lambda p, i: (i, p))
    return pl.pallas_call(
        body, name="attn_fwd",
        out_shape=(jax.ShapeDtypeStruct((s, D_ATTN), F32), jax.ShapeDtypeStruct((s, D_ATTN), F32)),
        grid=(N_PAIRS, s // tq),
        in_specs=[pl.BlockSpec((tq, LANE), lambda p, i: (i, kq + p)),
                  pl.BlockSpec((s, LANE), lambda p, i: (0, kk + p)),
                  pl.BlockSpec((s, LANE), lambda p, i: (0, kv + p))],
        out_specs=(tile, tile),
        scratch_shapes=[pltpu.VMEM((tq, LANE), F32), pltpu.VMEM((2, tq, 1), F32)],
        compiler_params=_cparams(("parallel", "arbitrary")),
    )(proj, proj, proj)


def _attn_bwd(proj, do, lsum):
    s = proj.shape[0]
    tq, tk = min(ATTN_TQ, s), min(ATTN_TK, s)
    r = tq // tk

    def body(q_ref, k_ref, v_ref, do_ref, l_ref, dq_ref, dk_ref, dv_ref, dqacc_ref, dkacc_ref, dvacc_ref,
             passed_ref, pre_ref):
        i = pl.program_id(1)

        @pl.when(i == 0)
        def _():
            dkacc_ref[...] = jnp.zeros_like(dkacc_ref)
            dvacc_ref[...] = jnp.zeros_like(dvacc_ref)

        head_mask, row, col = _attn_consts(tk)
        later = (row > col).astype(BF16)
        earlier = (row < col).astype(BF16)
        q = q_ref[...] * ATTN_SCALE
        dov = do_ref[...].astype(BF16)
        qh = [jnp.where(m, q, jnp.zeros_like(q)) for m in head_mask]
        doh = [jnp.where(m, dov, jnp.zeros_like(dov)) for m in head_mask]
        lsum_v = l_ref[...]
        lh = [lsum_v[:, 0:1], lsum_v[:, HEAD_DIM:HEAD_DIM + 1]]
        n_full = i * r
        half = tq // 2
        quarter = HEAD_DIM // 4
        first_all = jnp.clip(jnp.max(lsum_v[0:8, 3 * quarter:HEAD_DIM]).astype(jnp.int32), 0, n_full)
        first = jnp.clip(jnp.max(lsum_v[0:8, 2 * quarter:3 * quarter]).astype(jnp.int32), 0, first_all)
        dqacc_ref[...] = jnp.zeros_like(dqacc_ref)
        passed_ref[...] = jnp.zeros_like(passed_ref)
        pre_ref[...] = jnp.zeros_like(pre_ref)

        def block(j, lo, hi, band):
            start = pl.multiple_of(j * tk, tk)
            kb = k_ref[pl.ds(start, tk), :]
            vb = v_ref[pl.ds(start, tk), :]
            rows = slice(lo, hi)
            causal = _band_mask(hi - lo, tk) if band else None
            hs = range(2)
            q_rows = [qh[h][rows] for h in hs]
            do_rows = [doh[h][rows] for h in hs]
            logits = [_sb_logits(q_rows[h], kb) for h in hs]
            lb = [logits[h][0] for h in hs]
            l1m = [logits[h][1] if causal is None else jnp.where(causal, logits[h][1], 0.0) for h in hs]
            da = [lax.dot_general(do_rows[h], vb, NT, preferred_element_type=F32) for h in hs]
            rs = [jnp.sum(l1m[h], axis=1, keepdims=True) for h in hs]
            right = [lh[h][rows] - passed_ref[h, rows] - rs[h] for h in hs]
            for h in hs:
                passed_ref[h, rows] += rs[h]
            tail = [jnp.dot(l1m[h].astype(BF16), later, preferred_element_type=F32) + right[h] for h in hs]
            a = [jnp.exp(lb[h] + tail[h]) for h in hs]
            if causal is not None:
                a = [jnp.where(causal, a[h], 0.0) for h in hs]
            g = [a[h] * da[h] for h in hs]
            pre = [jnp.dot(g[h].astype(BF16), earlier, preferred_element_type=F32) + pre_ref[h, rows] for h in hs]
            for h in hs:
                pre_ref[h, rows] += jnp.sum(g[h], axis=1, keepdims=True)
            dz = [g[h] - jnp.exp(lb[h]) * (g[h] + pre[h]) for h in hs]
            if causal is not None:
                dz = [jnp.where(causal, dz[h], 0.0) for h in hs]
            dzb = [dz[h].astype(BF16) for h in hs]
            kh = [jnp.where(head_mask[h], kb, jnp.zeros_like(kb)) * ATTN_SCALE for h in hs]
            dqacc_ref[rows, :] += (jnp.dot(dzb[0], kh[0], preferred_element_type=F32)
                                   + jnp.dot(dzb[1], kh[1], preferred_element_type=F32))
            dvacc_ref[pl.ds(start, tk), :] += (
                lax.dot_general(a[0].astype(BF16), do_rows[0], TN, preferred_element_type=F32)
                + lax.dot_general(a[1].astype(BF16), do_rows[1], TN, preferred_element_type=F32))
            dkacc_ref[pl.ds(start, tk), :] += (
                lax.dot_general(dzb[0], q_rows[0], TN, preferred_element_type=F32)
                + lax.dot_general(dzb[1], q_rows[1], TN, preferred_element_type=F32))

        def step_upper(j, carry):
            block(j, 0, half, False)
            return carry

        def step_all(j, carry):
            block(j, 0, tq, False)
            return carry

        lax.fori_loop(first, first_all, step_upper, 0)
        lax.fori_loop(first_all, n_full, step_all, 0)
        for b in range(r):
            block(n_full + b, b * tk, tq, True)
        dq_ref[...] = dqacc_ref[...].astype(BF16)

        @pl.when(i == pl.num_programs(1) - 1)
        def _():
            dk_ref[...] = dkacc_ref[...].astype(BF16)
            dv_ref[...] = dvacc_ref[...].astype(BF16)

    kq, kk, kv = OFF_Q // LANE, OFF_K // LANE, OFF_V // LANE
    tile = pl.BlockSpec((tq, LANE), lambda p, i: (i, p))
    full = pl.BlockSpec((s, LANE), lambda p, i: (0, p))
    shp = jax.ShapeDtypeStruct((s, D_ATTN), BF16)
    return pl.pallas_call(
        body, name="attn_bwd", out_shape=(shp, shp, shp), grid=(N_PAIRS, s // tq),
        in_specs=[pl.BlockSpec((tq, LANE), lambda p, i: (i, kq + p)),
                  pl.BlockSpec((s, LANE), lambda p, i: (0, kk + p)),
                  pl.BlockSpec((s, LANE), lambda p, i: (0, kv + p)),
                  tile, tile],
        out_specs=(tile, full, full),
        scratch_shapes=[pltpu.VMEM((tq, LANE), F32), pltpu.VMEM((s, LANE), F32), pltpu.VMEM((s, LANE), F32),
                        pltpu.VMEM((2, tq, 1), F32), pltpu.VMEM((2, tq, 1), F32)],
        compiler_params=_cparams(("parallel", "arbitrary")),
    )(proj, proj, proj, do, lsum)


def _shift_down(u, k, rows):
    return jnp.where(rows >= k, pltpu.roll(u, k, 0), 0.0)


def _shift_up(u, k, rows, s):
    return jnp.where(rows < s - k, pltpu.roll(u, s - k, 0), 0.0)


def _conv_fwd(proj, w, b):
    s = proj.shape[0]
    blk0 = OFF_XBC // LANE

    def body(u_ref, w_ref, b_ref, o_ref):
        u = u_ref[...].astype(F32)
        rows = lax.broadcasted_iota(jnp.int32, (s, 1), 0)
        pre = u * w_ref[CONV_K - 1:CONV_K, :] + b_ref[...]
        for k in range(1, CONV_K):
            pre += _shift_down(u, k, rows) * w_ref[CONV_K - 1 - k:CONV_K - k, :]
        o_ref[...] = pre * _sigmoid(pre)

    return pl.pallas_call(
        body, name="conv_fwd", out_shape=jax.ShapeDtypeStruct((s, D_XBC), F32), grid=(D_XBC // LANE,),
        in_specs=[pl.BlockSpec((s, LANE), lambda j: (0, blk0 + j)), pl.BlockSpec((CONV_K, LANE), lambda j: (0, j)),
                  pl.BlockSpec((1, LANE), lambda j: (0, j))],
        out_specs=pl.BlockSpec((s, LANE), lambda j: (0, j)), compiler_params=_cparams(("parallel",)),
    )(proj, w, b)


def _conv_bwd(proj, w, b, dact):
    s = proj.shape[0]
    blk0 = OFF_XBC // LANE

    def body(u_ref, w_ref, b_ref, da_ref, du_ref, dw_ref, db_ref):
        u = u_ref[...].astype(F32)
        rows = lax.broadcasted_iota(jnp.int32, (s, 1), 0)
        shifted = [u] + [_shift_down(u, k, rows) for k in range(1, CONV_K)]
        pre = b_ref[...] + shifted[0] * w_ref[CONV_K - 1:CONV_K, :]
        for k in range(1, CONV_K):
            pre += shifted[k] * w_ref[CONV_K - 1 - k:CONV_K - k, :]
        sg = _sigmoid(pre)
        dpre = da_ref[...] * _silu_grad(pre, sg)
        db_ref[...] = jnp.sum(dpre, axis=0, keepdims=True)
        du = dpre * w_ref[CONV_K - 1:CONV_K, :]
        for k in range(CONV_K):
            dw_ref[CONV_K - 1 - k:CONV_K - k, :] = jnp.sum(dpre * shifted[k], axis=0, keepdims=True)
            if k:
                du += _shift_up(dpre, k, rows, s) * w_ref[CONV_K - 1 - k:CONV_K - k, :]
        du_ref[...] = du.astype(BF16)

    col = pl.BlockSpec((s, LANE), lambda j: (0, j))
    return pl.pallas_call(
        body, name="conv_bwd",
        out_shape=(jax.ShapeDtypeStruct((s, D_XBC), BF16), jax.ShapeDtypeStruct((CONV_K, D_XBC), F32),
                   jax.ShapeDtypeStruct((1, D_XBC), F32)),
        grid=(D_XBC // LANE,),
        in_specs=[pl.BlockSpec((s, LANE), lambda j: (0, blk0 + j)), pl.BlockSpec((CONV_K, LANE), lambda j: (0, j)),
                  pl.BlockSpec((1, LANE), lambda j: (0, j)), col],
        out_specs=(col, pl.BlockSpec((CONV_K, LANE), lambda j: (0, j)), pl.BlockSpec((1, LANE), lambda j: (0, j))),
        compiler_params=_cparams(("parallel",)),
    )(proj, w, b, dact)


def _ssd_decays(dtraw_ref, bias_ref, dtt_ref, biast_ref, arow_ref, acol_ref):
    ln = CHUNK
    dt = _softplus(dtraw_ref[...] + bias_ref[...])
    r = lax.broadcasted_iota(jnp.int32, (ln, ln), 0)
    c = lax.broadcasted_iota(jnp.int32, (ln, ln), 1)
    ac = jnp.dot((r >= c).astype(F32), dt * arow_ref[...], preferred_element_type=F32, precision=HI)
    dtt = _softplus(dtt_ref[...] + biast_ref[...])
    act = jnp.dot(dtt * acol_ref[...], (r <= c).astype(F32), preferred_element_type=F32, precision=HI)
    return dt, ac, act, r >= c


def _pair_cols(m0, v, h0):
    return jnp.where(m0, v[:, h0:h0 + 1], v[:, h0 + 1:h0 + 2])


def _ssd_fwd(act, dtraw, dtt, bias, biast, arow, acol, dskip):
    s = act.shape[0]
    ln = CHUNK
    nc = s // ln

    def body(act_ref, dtraw_ref, dtt_ref, bias_ref, biast_ref, arow_ref, acol_ref, dsk_ref, y_ref, st_ref,
             state_ref):
        @pl.when(pl.program_id(0) == 0)
        def _():
            state_ref[...] = jnp.zeros_like(state_ref)

        dt, ac, act_t, lower = _ssd_decays(dtraw_ref, bias_ref, dtt_ref, biast_ref, arow_ref, acol_ref)
        lane = lax.broadcasted_iota(jnp.int32, (1, LANE), 1)
        m0 = lane < HEAD_DIM
        for g in range(N_GROUPS):
            bg32 = act_ref[:, D_SSM + g * D_STATE:D_SSM + (g + 1) * D_STATE]
            bg, bg_t = bg32.astype(BF16), bg32.T.astype(BF16)
            cg = act_ref[:, D_SSM + (N_GROUPS + g) * D_STATE:D_SSM + (N_GROUPS + g + 1) * D_STATE].astype(BF16)
            cb = lax.dot_general(cg, bg, NT, preferred_element_type=F32)
            for p in range(g * 4, g * 4 + 4):
                h0 = 2 * p
                xp = act_ref[:, p * LANE:(p + 1) * LANE]
                xdt = xp * _pair_cols(m0, dt, h0)
                acp = _pair_cols(m0, ac, h0)
                last = acp[ln - 1:ln, :]
                y = xp * dsk_ref[:, p * LANE:(p + 1) * LANE]
                for hh in range(2):
                    h = h0 + hh
                    dm = jnp.exp(jnp.where(lower, ac[:, h:h + 1] - act_t[h:h + 1, :], -jnp.inf))
                    mask = m0 if hh == 0 else jnp.logical_not(m0)
                    y += jnp.dot((cb * dm).astype(BF16), jnp.where(mask, xdt, 0.0).astype(BF16),
                                 preferred_element_type=F32)
                prev = state_ref[p]
                st_ref[0, p] = prev
                y += jnp.dot(cg, prev.astype(BF16), preferred_element_type=F32) * jnp.exp(acp)
                y_ref[:, p * LANE:(p + 1) * LANE] = y
                cs = jnp.dot(bg_t, (xdt * jnp.exp(last - acp)).astype(BF16), preferred_element_type=F32)
                state_ref[p] = prev * jnp.exp(last) + cs

    row = lambda w: pl.BlockSpec((1, w), lambda c: (0, 0))
    return pl.pallas_call(
        body, name="ssd_fwd",
        out_shape=(jax.ShapeDtypeStruct((s, D_SSM), F32),
                   jax.ShapeDtypeStruct((nc, N_PAIRS, LANE, D_STATE), F32)),
        grid=(nc,),
        in_specs=[pl.BlockSpec((ln, D_XBC), lambda c: (c, 0)), pl.BlockSpec((ln, LANE), lambda c: (c, 0)),
                  pl.BlockSpec((N_HEADS, ln), lambda c: (0, c)), row(LANE),
                  pl.BlockSpec((N_HEADS, 1), lambda c: (0, 0)), row(LANE),
                  pl.BlockSpec((N_HEADS, 1), lambda c: (0, 0)), row(D_SSM)],
        out_specs=(pl.BlockSpec((ln, D_SSM), lambda c: (c, 0)),
                   pl.BlockSpec((1, N_PAIRS, LANE, D_STATE), lambda c: (c, 0, 0, 0))),
        scratch_shapes=[pltpu.VMEM((N_PAIRS, LANE, D_STATE), F32)],
        compiler_params=_cparams(("arbitrary",)),
    )(act, dtraw, dtt, bias, biast, arow, acol, dskip)


def _ssd_bwd(act, dtraw, dtt, bias, biast, arow, acol, dskip, states, dy):
    s = act.shape[0]
    ln = CHUNK
    nc = s // ln

    def body(act_ref, dtraw_ref, dtt_ref, bias_ref, biast_ref, arow_ref, acol_ref, dsk_ref, st_ref, dy_ref,
             dact_ref, dldc_ref, dldr_ref, ddt_ref, dd_ref, dstate_ref):
        @pl.when(pl.program_id(0) == 0)
        def _():
            dstate_ref[...] = jnp.zeros_like(dstate_ref)
            dd_ref[...] = jnp.zeros_like(dd_ref)

        dt, ac, act_t, lower = _ssd_decays(dtraw_ref, bias_ref, dtt_ref, biast_ref, arow_ref, acol_ref)
        lane = lax.broadcasted_iota(jnp.int32, (1, LANE), 1)
        m0 = lane < HEAD_DIM
        halves = (m0, jnp.logical_not(m0))
        is_last = lax.broadcasted_iota(jnp.int32, (ln, 1), 0) == ln - 1
        sub = lax.broadcasted_iota(jnp.int32, (N_HEADS, 1), 0)
        earlier_eq = jnp.logical_not(lower) | (lax.broadcasted_iota(jnp.int32, (ln, ln), 0)
                                               == lax.broadcasted_iota(jnp.int32, (ln, ln), 1))
        dac_col = jnp.zeros((ln, LANE), F32)
        dac_row = jnp.zeros((N_HEADS, ln), F32)
        ddt_col = jnp.zeros((ln, LANE), F32)

        def half_sum(v, hh):
            return jnp.sum(jnp.where(halves[hh], v, 0.0), axis=1, keepdims=True)

        for g in range(N_GROUPS):
            b_lo, c_lo = D_SSM + g * D_STATE, D_SSM + (N_GROUPS + g) * D_STATE
            bg32 = act_ref[:, b_lo:b_lo + D_STATE]
            cg32 = act_ref[:, c_lo:c_lo + D_STATE]
            bg, cg = bg32.astype(BF16), cg32.astype(BF16)
            cg_t = cg32.T.astype(BF16)
            cb_t = lax.dot_general(bg, cg, NT, preferred_element_type=F32)
            dcb_t = jnp.zeros((ln, ln), F32)
            dbg = jnp.zeros((ln, D_STATE), F32)
            dcg = jnp.zeros((ln, D_STATE), F32)
            for p in range(g * 4, g * 4 + 4):
                h0 = 2 * p
                cols = slice(p * LANE, (p + 1) * LANE)
                xp = act_ref[:, cols]
                dyp = dy_ref[:, cols]
                dtp = _pair_cols(m0, dt, h0)
                acp = _pair_cols(m0, ac, h0)
                last = acp[ln - 1:ln, :]
                xdt = xp * dtp
                eac = jnp.exp(acp)
                dte = jnp.exp(last - acp)
                dec = jnp.exp(last)
                prev = st_ref[0, p]
                prev_b = prev.astype(BF16)
                ds = dstate_ref[p]
                ds_b = ds.astype(BF16)

                dd_ref[:, cols] += jnp.sum(dyp * xp, axis=0, keepdims=True)
                dx = dyp * dsk_ref[:, cols]
                zoff = jnp.dot(cg, prev_b, preferred_element_type=F32)
                dz_b = (dyp * eac).astype(BF16)
                dcg += lax.dot_general(dz_b, prev_b, NT, preferred_element_type=F32)
                dprev = jnp.dot(cg_t, dz_b, preferred_element_type=F32) + ds * dec
                wmat = jnp.dot(bg, ds_b, preferred_element_type=F32)
                xdte_b = (xdt * dte).astype(BF16)
                dbg += lax.dot_general(xdte_b, ds_b, NT, preferred_element_type=F32)
                dxdt = dte * wmat
                t_dte = xdt * wmat * dte
                t_ac = dyp * zoff * eac - t_dte
                at_last = jnp.sum(ds * prev, axis=0, keepdims=True) * dec + jnp.sum(t_dte, axis=0, keepdims=True)
                for hh in range(2):
                    h = h0 + hh
                    here = lane == h
                    dm_t = jnp.exp(jnp.where(earlier_eq, act_t[h:h + 1, :] - ac[:, h:h + 1], -jnp.inf))
                    mm_t = cb_t * dm_t
                    dyh = jnp.where(halves[hh], dyp, 0.0).astype(BF16)
                    xdth = jnp.where(halves[hh], xdt, 0.0).astype(BF16)
                    dmm_t = lax.dot_general(xdth, dyh, NT, preferred_element_type=F32)
                    dxdt += jnp.dot(mm_t.astype(BF16), dyh, preferred_element_type=F32)
                    gm_t = dmm_t * mm_t
                    dcb_t += dmm_t * dm_t
                    dac_col += jnp.where(here, half_sum(t_ac, hh) - jnp.sum(gm_t, axis=1, keepdims=True), 0.0)
                    dac_col += jnp.where(jnp.logical_and(is_last, here), half_sum(at_last, hh), 0.0)
                    dac_row += jnp.where(sub == h, jnp.sum(gm_t, axis=0, keepdims=True), 0.0)
                    ddt_col += jnp.where(here, half_sum(dxdt * xp, hh), 0.0)
                dact_ref[:, cols] = dx + dxdt * dtp
                dstate_ref[p] = dprev
            dcb_tb = dcb_t.astype(BF16)
            dact_ref[:, b_lo:b_lo + D_STATE] = dbg + jnp.dot(dcb_tb, cg, preferred_element_type=F32)
            dact_ref[:, c_lo:c_lo + D_STATE] = dcg + lax.dot_general(dcb_tb, bg, TN, preferred_element_type=F32)

        r = lax.broadcasted_iota(jnp.int32, (ln, ln), 0)
        c = lax.broadcasted_iota(jnp.int32, (ln, ln), 1)
        dldc_ref[...] = jnp.dot((r <= c).astype(F32), dac_col, preferred_element_type=F32, precision=HI)
        dldr_ref[...] = jnp.dot(dac_row, (r >= c).astype(F32), preferred_element_type=F32, precision=HI)
        ddt_ref[...] = ddt_col

    rev = lambda c: nc - 1 - c
    row = lambda w: pl.BlockSpec((1, w), lambda c: (0, 0))
    col16 = pl.BlockSpec((N_HEADS, 1), lambda c: (0, 0))
    chunk128 = pl.BlockSpec((ln, LANE), lambda c: (rev(c), 0))
    return pl.pallas_call(
        body, name="ssd_bwd",
        out_shape=(jax.ShapeDtypeStruct((s, D_XBC), F32), jax.ShapeDtypeStruct((s, LANE), F32),
                   jax.ShapeDtypeStruct((N_HEADS, s), F32), jax.ShapeDtypeStruct((s, LANE), F32),
                   jax.ShapeDtypeStruct((1, D_SSM), F32)),
        grid=(nc,),
        in_specs=[pl.BlockSpec((ln, D_XBC), lambda c: (rev(c), 0)), chunk128,
                  pl.BlockSpec((N_HEADS, ln), lambda c: (0, rev(c))), row(LANE), col16, row(LANE), col16,
                  row(D_SSM), pl.BlockSpec((1, N_PAIRS, LANE, D_STATE), lambda c: (rev(c), 0, 0, 0)),
                  pl.BlockSpec((ln, D_SSM), lambda c: (rev(c), 0))],
        out_specs=(pl.BlockSpec((ln, D_XBC), lambda c: (rev(c), 0)), chunk128,
                   pl.BlockSpec((N_HEADS, ln), lambda c: (0, rev(c))), chunk128, row(D_SSM)),
        scratch_shapes=[pltpu.VMEM((N_PAIRS, LANE, D_STATE), F32)],
        compiler_params=_cparams(("arbitrary",)),
    )(act, dtraw, dtt, bias, biast, arow, acol, dskip, states, dy)


def _dt_bwd(dtraw, bias, arow, dld_col, dld_row_t, ddt_col):
    s = dtraw.shape[0]
    tm = min(512, s)

    def body(raw_ref, bias_ref, a_ref, dc_ref, dr_ref, dd_ref, out_ref, sums_ref):
        @pl.when(pl.program_id(0) == 0)
        def _():
            sums_ref[...] = jnp.zeros_like(sums_ref)

        raw = raw_ref[...] + bias_ref[...]
        dld = dc_ref[...] + dr_ref[...]
        ddt = dld * a_ref[...] + dd_ref[...]
        draw = ddt * _sigmoid(raw)
        out_ref[...] = draw.astype(BF16)
        sums_ref[0:1, :] += jnp.sum(draw, axis=0, keepdims=True)
        sums_ref[1:2, :] += jnp.sum(dld * _softplus(raw), axis=0, keepdims=True)

    tile = pl.BlockSpec((tm, LANE), lambda i: (i, 0))
    row = pl.BlockSpec((1, LANE), lambda i: (0, 0))
    return pl.pallas_call(
        body, name="dt_bwd",
        out_shape=(jax.ShapeDtypeStruct((s, LANE), BF16), jax.ShapeDtypeStruct((2, LANE), F32)), grid=(s // tm,),
        in_specs=[tile, row, row, tile, tile, tile], out_specs=(tile, pl.BlockSpec((2, LANE), lambda i: (0, 0))),
        compiler_params=_cparams(("arbitrary",)),
    )(dtraw, bias, arow, dld_col, dld_row_t, ddt_col)


def _sum8(parts):
    nb, n = parts.shape

    def body(p_ref, o_ref):
        acc = p_ref[0:1, :]
        for b in range(1, nb):
            acc = acc + p_ref[b:b + 1, :]
        o_ref[...] = acc

    return pl.pallas_call(body, name="sum8", out_shape=jax.ShapeDtypeStruct((1, n), F32),
                          compiler_params=_cparams())(parts)


def _outer8(act_t, dmod):
    d, nb = act_t.shape
    n = dmod.shape[1]

    def body(a_ref, m_ref, o_ref):
        acc = a_ref[:, 0:1] * m_ref[0:1, :]
        for b in range(1, nb):
            acc = acc + a_ref[:, b:b + 1] * m_ref[b:b + 1, :]
        o_ref[...] = acc

    return pl.pallas_call(body, name="outer8", out_shape=jax.ShapeDtypeStruct((d, n), F32),
                          compiler_params=_cparams())(act_t, dmod)


def _pad_lanes(v, width=LANE):
    return jnp.pad(v, ((0, 0), (0, width - v.shape[1])))


def kernel(x, c, w_ada, b_ada, norm_in_gain, w_in, conv_w, conv_b, dt_bias, a_log, d_skip, sb_norm_gain, ssm_norm_gain, w_out, norm_f_gain, loss_target, m_w_ada, m_b_ada, m_norm_in_gain, m_w_in, m_conv_w, m_conv_b, m_dt_bias, m_a_log, m_d_skip, m_sb_norm_gain, m_ssm_norm_gain, m_w_out, m_norm_f_gain, v_w_ada, v_b_ada, v_norm_in_gain, v_w_in, v_conv_w, v_conv_b, v_dt_bias, v_a_log, v_d_skip, v_sb_norm_gain, v_ssm_norm_gain, v_w_out, v_norm_f_gain):
    ax, ay, ac_ = _coords()
    chip = 2 * ax + ay
    me = 2 * chip + ac_
    my_c = jnp.reshape(ac_, (1,)).astype(jnp.int32)
    x2d, tgt = x[0], loss_target[0]
    s = x2d.shape[0]
    ada_cols = w_ada.shape[2]
    cw_cols = conv_w.shape[2]
    in_cols = w_in.shape[2]
    out_rows = w_out.shape[1]

    small = jnp.concatenate([c, conv_w[0].reshape(1, CONV_K * cw_cols)], axis=1)
    b_ada_shard = lax.dynamic_slice_in_dim(b_ada, chip * ada_cols, ada_cols, axis=1)
    w_in_mine, w_out_mine = w_in[0].T.astype(BF16), w_out[0].astype(BF16)
    small_rows, mod_rows, c_act_all, w_in_all = _front(jnp.broadcast_to(small, (8, small.shape[1])), w_ada[0],
                                                       b_ada_shard, w_in_mine)
    small_all = small_rows[0::8]
    conv_w_full = (small_all[0::2, D_MODEL:].reshape(N_CHIPS, CONV_K, cw_cols)
                   .transpose(1, 0, 2).reshape(CONV_K, D_XBC))
    mod_all = mod_rows.reshape(N_DEV, N_DEV, ada_cols)[0::2]
    mod = lax.dynamic_index_in_dim(mod_all, me, axis=1, keepdims=False).reshape(1, 3 * D_MODEL)
    shift, scale, gate = mod[:, :D_MODEL], mod[:, D_MODEL:2 * D_MODEL], mod[:, 2 * D_MODEL:]

    w_in_all = lax.dynamic_update_slice(w_in_all, w_in_mine[None], (chip, 0, 0))
    w_in_t = w_in_all.reshape(D_PROJ, D_MODEL)
    w_zs_t = w_in_t[ZS_LO:]
    w_dt_t = jnp.pad(w_in_t[DT_LO:ZS_LO], ((0, LANE - N_HEADS), (0, 0)))

    h = _rms_mod_fwd(x2d, norm_in_gain, scale, shift)
    proj, w_out_all = _matmul_sum([(h, D_MODEL, 0, w_in_t, 0)], BF16, "in_proj_and_gather_w_out", 1024, 512,
                                  exchange=[jnp.broadcast_to(w_out_mine[None], (N_CHIPS,) + w_out_mine.shape)],
                                  trans_b=True, n_out=D_MAIN)
    w_out_all = lax.dynamic_update_slice(w_out_all, w_out_mine[None], (chip, 0, 0))
    w_out_full = w_out_all.reshape(N_CHIPS * out_rows, D_MODEL)
    proj_zs = _matmul(h, w_zs_t, BF16, "in_proj_zs", "nt", 1024, 512, 1024)
    dtraw = _matmul(h, w_dt_t, F32, "in_proj_dt", "nt", 1024, LANE, 1024)
    o_attn, lsum = _attn_fwd(proj)
    y_attn = _gated_norm_fwd(o_attn, proj, OFF_ZA, sb_norm_gain, False, "attn_gate_fwd")
    act = _conv_fwd(proj, conv_w_full, conv_b)
    a_neg = -jnp.exp(a_log)
    arow, acol = _pad_lanes(a_neg), a_neg.reshape(N_HEADS, 1)
    bias_row, bias_col = _pad_lanes(dt_bias), dt_bias.reshape(N_HEADS, 1)
    dtt = dtraw[:, :N_HEADS].T
    dskip_row = jnp.repeat(d_skip, HEAD_DIM, axis=1)
    ssd_args = (act, dtraw, dtt, bias_row, bias_col, arow, acol, dskip_row)
    y_ssd, states = _ssd_fwd(*ssd_args)
    y_ssm = _gated_norm_fwd(y_ssd, proj_zs, 0, ssm_norm_gain, True, "ssm_gate_fwd")
    mixed = _matmul_sum([(y_attn, D_ATTN, 0, w_out_full, 0), (y_ssm, D_SSM, 0, w_out_full, 1)], F32, "out_proj",
                        1024, 1024)

    dx2, dmixed, head_sums = _loss_head(x2d, mixed, gate, norm_f_gain.reshape(1, D_MODEL), tgt)
    g_w_out = _matmul_tn_pieces([y_attn, y_ssm], N_CHIPS * out_rows, dmixed, "out_proj_dw", 512)
    d_mix_in = _matmul(dmixed, w_out_full, BF16, "out_proj_dx", "nt", 1024, 1024, 1024)
    d_o, dz_attn, g_sb = _gated_norm_bwd(d_mix_in, 0, o_attn, proj, OFF_ZA, sb_norm_gain, False, "attn_gate_bwd")
    d_y, dz_ssm, g_ssm = _gated_norm_bwd(d_mix_in, 1, y_ssd, proj_zs, 0, ssm_norm_gain, True, "ssm_gate_bwd")
    dq, dk, dv = _attn_bwd(proj, d_o, lsum)
    dact, dld_col, dld_row, ddt_col, dd_cols = _ssd_bwd(*ssd_args, states, d_y)
    dxbc, g_conv_w, g_conv_b = _conv_bwd(proj, conv_w_full, conv_b, dact)
    ddtraw, dt_sums = _dt_bwd(dtraw, bias_row, arow, dld_col, _pad_lanes(dld_row.T), ddt_col)
    g_in_t = _matmul_tn_pieces([dq, dk, dv, dz_attn, dxbc], D_PROJ, h, "in_proj_dw", 256)
    g_in_t = _matmul_tn_rows(g_in_t, D_PROJ, ddtraw, h, DT_LO // LANE, "in_proj_dw_dt", LANE)
    g_zs_t = _matmul(dz_ssm, h, F32, "in_proj_dw_zs", "tn", 512, 1024, 4096)
    g_in_t = lax.dynamic_update_slice(g_in_t, g_zs_t, (ZS_LO, 0))
    dh_terms = [(dq, D_ATTN, 0, w_in_t, 0), (dk, D_ATTN, 0, w_in_t, 1), (dv, D_ATTN, 0, w_in_t, 2),
                (dz_attn, D_ATTN, 0, w_in_t, 3)]
    dh_terms += [(dxbc, 512, j, w_in_t, OFF_XBC // 512 + j) for j in range(D_XBC // 512)]
    dh_terms += [(dz_ssm, D_SSM, 0, w_zs_t, 0), (ddtraw, LANE, 0, w_dt_t, 0)]
    g_in_blocks = g_in_t.reshape(N_CHIPS, in_cols, D_MODEL)
    g_out_blocks = g_w_out.reshape(N_CHIPS, out_rows, D_MODEL)
    land_in, land_out = _send_to_sibling([g_in_blocks, g_out_blocks], "grads_to_sibling")
    chip_in = _add_my_half(g_in_blocks, land_in, my_c, "add_sibling_in")
    chip_out = _add_my_half(g_out_blocks, land_out, my_c, "add_sibling_out")
    dh, slots_in, slots_out = _matmul_sum(dh_terms, F32, "in_proj_dx_and_grads_between_chips", 512, 1024,
                                          exchange=[chip_in, chip_out])
    grad_x, in_sums = _rms_mod_bwd(x2d, dh, dx2, norm_in_gain, scale)

    g_a_log = dt_sums[1:2, :N_HEADS] * a_neg
    g_d_skip = jnp.sum(dd_cols.reshape(N_HEADS, HEAD_DIM), axis=1).reshape(1, N_HEADS)
    dmod = jnp.concatenate([in_sums[0:1], in_sums[1:2], head_sums[2:3]], axis=1)
    loss_part = 0.5 / D_MODEL * jnp.sum(head_sums[0:1], axis=1, keepdims=True)
    pieces = [dmod, in_sums[2:3], g_conv_w.reshape(1, CONV_K * D_XBC), g_conv_b, _pad_lanes(dt_sums[0:1, :N_HEADS]),
              _pad_lanes(g_a_log), _pad_lanes(g_d_skip), g_sb, g_ssm, head_sums[1:2], _pad_lanes(loss_part)]
    widths = [p.shape[1] for p in pieces]
    parts_all = _allgather8(jnp.concatenate(pieces, axis=1), "gather_small_grads")[:, 0, :]
    total = _sum8(parts_all)
    offs = [0]
    for w_ in widths:
        offs.append(offs[-1] + w_)
    tot = [total[:, offs[i]:offs[i + 1]] for i in range(len(pieces))]
    g_b_ada, g_norm_in, g_conv_w_full = tot[0], tot[1], tot[2].reshape(CONV_K, D_XBC)
    g_conv_b_t, g_dt_bias, g_a_log_t, g_d_skip_t = tot[3], tot[4][:, :N_HEADS], tot[5][:, :N_HEADS], tot[6][:, :N_HEADS]
    g_sb_t, g_ssm_t, g_norm_f, loss = tot[7], tot[8], tot[9], tot[10][0, 0]
    g_conv_w_shard = lax.dynamic_slice_in_dim(g_conv_w_full, chip * cw_cols, cw_cols, axis=1)
    dmod_shard = lax.dynamic_slice_in_dim(parts_all[:, :3 * D_MODEL], chip * ada_cols, ada_cols, axis=1)
    g_w_ada = _outer8(c_act_all.T, dmod_shard)

    own = lambda blocks: lax.dynamic_slice_in_dim(blocks, chip, 1, axis=0)
    slots_in = lax.dynamic_update_slice(slots_in, own(chip_in), (chip, 0, 0))
    slots_out = lax.dynamic_update_slice(slots_out, own(chip_out), (chip, 0, 0))
    half_in, half_out = _sum_slots(slots_in, "sum_chips_in"), _sum_slots(slots_out, "sum_chips_out")
    their_in, their_out = _swap_with_sibling([half_in, half_out], "grads_swap_sibling")
    south = ac_ == 0
    both = lambda mine, theirs: jnp.concatenate([jnp.where(south, mine, theirs), jnp.where(south, theirs, mine)],
                                                axis=1)
    g_w_in_t, g_w_out_shard = both(half_in, their_in), both(half_out, their_out)

    d_w_ada, nm_w_ada, nv_w_ada = _adamw(w_ada[0], g_w_ada, m_w_ada[0], v_w_ada[0], "adamw_w_ada")
    d_w_in, nm_w_in, nv_w_in = [r.T for r in _adamw(w_in[0].T, g_w_in_t, m_w_in[0].T, v_w_in[0].T, "adamw_w_in")]
    g_w_in = g_w_in_t.T
    d_w_out, nm_w_out, nv_w_out = _adamw(w_out[0], g_w_out_shard, m_w_out[0], v_w_out[0], "adamw_w_out")
    flat = lambda a: a.reshape(1, -1)
    small_w = [b_ada, norm_in_gain, conv_w[0], conv_b, dt_bias, a_log, d_skip, sb_norm_gain, ssm_norm_gain,
               norm_f_gain]
    small_m = [m_b_ada, m_norm_in_gain, m_conv_w[0], m_conv_b, m_dt_bias, m_a_log, m_d_skip, m_sb_norm_gain,
               m_ssm_norm_gain, m_norm_f_gain]
    small_v = [v_b_ada, v_norm_in_gain, v_conv_w[0], v_conv_b, v_dt_bias, v_a_log, v_d_skip, v_sb_norm_gain,
               v_ssm_norm_gain, v_norm_f_gain]
    small_g = [g_b_ada, g_norm_in, g_conv_w_shard, g_conv_b_t, g_dt_bias, g_a_log_t, g_d_skip_t, g_sb_t, g_ssm_t,
               g_norm_f]
    cat = lambda arrs: jnp.concatenate([flat(a) for a in arrs], axis=1)
    d_small, nm_small, nv_small = _adamw(cat(small_w), cat(small_g), cat(small_m), cat(small_v), "adamw_small")
    sizes = [a.size for a in small_w]
    soffs = [0]
    for n_ in sizes:
        soffs.append(soffs[-1] + n_)

    def split(packed):
        return [packed[0, soffs[i]:soffs[i + 1]].reshape(small_w[i].shape) for i in range(len(small_w))]

    def ordered(big_ada, big_in, big_out, smalls):
        (s_b_ada, s_norm_in, s_conv_w, s_conv_b, s_dt_bias, s_a_log, s_d_skip, s_sb, s_ssm, s_norm_f) = smalls
        return [big_ada[None], s_b_ada, s_norm_in, big_in[None], s_conv_w[None], s_conv_b, s_dt_bias, s_a_log,
                s_d_skip, s_sb, s_ssm, big_out[None], s_norm_f]

    grads = ordered(g_w_ada, g_w_in, g_w_out_shard,
                    [g.reshape(w_.shape) for g, w_ in zip(small_g, small_w)])
    deltas = ordered(d_w_ada, d_w_in, d_w_out, split(d_small))
    new_m = ordered(nm_w_ada, nm_w_in, nm_w_out, split(nm_small))
    new_v = ordered(nv_w_ada, nv_w_in, nv_w_out, split(nv_small))
    return (loss, grad_x[None], *grads, *deltas, *new_m, *new_v)
```

```python
import functools

import jax
import jax.numpy as jnp
from jax import lax
from jax.experimental import pallas as pl
from jax.experimental.pallas import tpu as pltpu

F32, BF16 = jnp.float32, jnp.bfloat16
MESH = pl.DeviceIdType.MESH
HI = lax.Precision.HIGHEST
NN = (((1,), (0,)), ((), ()))
NT = (((1,), (1,)), ((), ()))
TN = (((0,), (0,)), ((), ()))

D_MODEL = 1024
D_ATTN = 1024
D_SSM = 1024
HEAD_DIM = 64
N_HEADS = 16
N_PAIRS = 8
N_GROUPS = 2
D_STATE = 128
D_XBC = 1536
D_PROJ = 6672
D_MAIN = 5632
CONV_K = 4
CHUNK = 128
LANE = 128
N_CHIPS = 4
N_DEV = 8
NORM_EPS = 1e-6
ATTN_SCALE = HEAD_DIM ** -0.5
ATTN_TQ = 512
ATTN_TK = 256
LOG_ZERO = -110.0
ADAM_LR, ADAM_B1, ADAM_B2, ADAM_EPS, ADAM_WD, ADAM_STEP = 0.001, 0.9, 0.999, 1e-08, 0.01, 10
VMEM_LIMIT = 56 * 1024 * 1024

OFF_Q, OFF_K, OFF_V, OFF_ZA, OFF_XBC = 0, 1024, 2048, 3072, 4096
DT_LO = D_MAIN
ZS_LO = DT_LO + N_HEADS


def _cparams(sem=None):
    return pltpu.CompilerParams(dimension_semantics=sem, vmem_limit_bytes=VMEM_LIMIT)


def _sigmoid(x):
    return 1.0 / (1.0 + jnp.exp(-x))


def _softplus(x):
    return jnp.maximum(x, 0.0) + jnp.log(1.0 + jnp.exp(-jnp.abs(x)))


def _coords():
    return lax.axis_index("x"), lax.axis_index("y"), lax.axis_index("c")


def _allgather8(v, name):
    n = v.shape[-1]

    def body(v_ref, out_ref, send_sems, recv_sems, local_sem):
        x, y, c = _coords()
        me = 4 * x + 2 * y + c
        mine = pltpu.make_async_copy(v_ref, out_ref.at[me], local_sem)
        mine.start()
        sends, recvs = [], []
        for j in range(1, N_DEV):
            px = 1 - x if (j >> 2) & 1 else x
            py = 1 - y if (j >> 1) & 1 else y
            pc = 1 - c if j & 1 else c
            peer = (px, py, pc)
            sends.append(pltpu.make_async_remote_copy(
                src_ref=v_ref, dst_ref=out_ref.at[me], send_sem=send_sems.at[j - 1],
                recv_sem=recv_sems.at[j - 1], device_id=peer, device_id_type=MESH))
            recvs.append(pltpu.make_async_remote_copy(
                src_ref=v_ref, dst_ref=out_ref.at[4 * px + 2 * py + pc], send_sem=send_sems.at[j - 1],
                recv_sem=recv_sems.at[j - 1], device_id=peer, device_id_type=MESH))
        for s in sends:
            s.start()
        for r in recvs:
            r.wait_recv()
        for s in sends:
            s.wait_send()
        mine.wait()

    vm = pl.BlockSpec(memory_space=pltpu.VMEM)
    return pl.pallas_call(
        body, name=name, out_shape=jax.ShapeDtypeStruct((N_DEV, 1, n), F32),
        in_specs=[vm], out_specs=vm,
        scratch_shapes=[pltpu.SemaphoreType.DMA((N_DEV - 1,)), pltpu.SemaphoreType.DMA((N_DEV - 1,)),
                        pltpu.SemaphoreType.DMA(())],
    )(v)


def _other_chips(x, y):
    chips = [(1 - x, y), (x, 1 - y), (1 - x, 1 - y)]
    return chips, [2 * cx + cy for cx, cy in chips]


def _half_cols(width, which):
    half = width // 2
    return pl.ds(pl.multiple_of(which * half, half), half)


def _gather_shards(arrs, name):
    n = len(arrs)

    def body(*refs):
        ins, outs = refs[:n], refs[n:2 * n]
        send_sems, recv_sems = refs[2 * n:]
        x, y, c = _coords()
        k = 2 * x + y
        chips, chip_idx = _other_chips(x, y)
        sibling = (x, y, 1 - c)
        sends = []
        for a in range(n):
            mine = _half_cols(arrs[a].shape[-1], c)
            for j in range(3):
                cp = pltpu.make_async_remote_copy(
                    src_ref=ins[a].at[:, mine], dst_ref=outs[a].at[k, :, mine], send_sem=send_sems.at[6 * a + j],
                    recv_sem=recv_sems.at[6 * a + j], device_id=(*chips[j], c), device_id_type=MESH)
                cp.start()
                sends.append(cp)
        for a in range(n):
            mine = _half_cols(arrs[a].shape[-1], c)
            for j in range(3):
                landed = outs[a].at[chip_idx[j], :, mine]
                pltpu.make_async_remote_copy(
                    src_ref=landed, dst_ref=landed, send_sem=send_sems.at[6 * a + j],
                    recv_sem=recv_sems.at[6 * a + j], device_id=(*chips[j], c), device_id_type=MESH).wait_recv()
                fwd = pltpu.make_async_remote_copy(
                    src_ref=landed, dst_ref=landed, send_sem=send_sems.at[6 * a + 3 + j],
                    recv_sem=recv_sems.at[6 * a + 3 + j], device_id=sibling, device_id_type=MESH)
                fwd.start()
                sends.append(fwd)
        for a in range(n):
            theirs = _half_cols(arrs[a].shape[-1], 1 - c)
            for j in range(3):
                landed = outs[a].at[chip_idx[j], :, theirs]
                pltpu.make_async_remote_copy(
                    src_ref=landed, dst_ref=landed, send_sem=send_sems.at[6 * a + 3 + j],
                    recv_sem=recv_sems.at[6 * a + 3 + j], device_id=sibling, device_id_type=MESH).wait_recv()
        for cp in sends:
            cp.wait_send()

    hbm = pl.BlockSpec(memory_space=pl.ANY)
    return pl.pallas_call(
        body, name=name,
        out_shape=tuple(jax.ShapeDtypeStruct((N_CHIPS,) + a.shape, a.dtype) for a in arrs),
        in_specs=[hbm] * n, out_specs=tuple([hbm] * n),
        scratch_shapes=[pltpu.SemaphoreType.DMA((6 * n,)), pltpu.SemaphoreType.DMA((6 * n,))],
    )(*arrs)


def _front(small8, w_ada_shard, b_shard, w_shard):
    n = small8.shape[1]
    cols = w_ada_shard.shape[1]

    def body(small_ref, wa_ref, b_ref, w_ref, all_ref, mod_ref, act_ref, wout_ref, part_ref,
             send_a, recv_a, send_b, recv_b, send_w, recv_w):
        x, y, c = _coords()
        me = 4 * x + 2 * y + c
        k = 2 * x + y
        chips, chip_idx = _other_chips(x, y)
        sibling = (x, y, 1 - c)
        mine = _half_cols(w_shard.shape[-1], c)
        theirs = _half_cols(w_shard.shape[-1], 1 - c)

        def w_copy(j, landed, forward):
            if forward:
                ref = wout_ref.at[chip_idx[j], :, mine] if not landed else wout_ref.at[chip_idx[j], :, theirs]
                return pltpu.make_async_remote_copy(src_ref=ref, dst_ref=ref, send_sem=send_w.at[3 + j],
                                                    recv_sem=recv_w.at[3 + j], device_id=sibling, device_id_type=MESH)
            dst = wout_ref.at[chip_idx[j], :, mine] if landed else wout_ref.at[k, :, mine]
            src = dst if landed else w_ref.at[:, mine]
            return pltpu.make_async_remote_copy(src_ref=src, dst_ref=dst, send_sem=send_w.at[j], recv_sem=recv_w.at[j],
                                                device_id=(*chips[j], c), device_id_type=MESH)

        def gather8(src_ref, dst_ref, send_sems, recv_sems):
            slot = lambda d: pl.ds(pl.multiple_of(8 * d, 8), 8)
            dst_ref[slot(me), :] = src_ref[...]
            sends, recvs = [], []
            for j in range(1, N_DEV):
                px = 1 - x if (j >> 2) & 1 else x
                py = 1 - y if (j >> 1) & 1 else y
                pc = 1 - c if j & 1 else c
                sends.append(pltpu.make_async_remote_copy(
                    src_ref=src_ref, dst_ref=dst_ref.at[slot(me)], send_sem=send_sems.at[j - 1],
                    recv_sem=recv_sems.at[j - 1], device_id=(px, py, pc), device_id_type=MESH))
                recvs.append(pltpu.make_async_remote_copy(
                    src_ref=src_ref, dst_ref=dst_ref.at[slot(4 * px + 2 * py + pc)], send_sem=send_sems.at[j - 1],
                    recv_sem=recv_sems.at[j - 1], device_id=(px, py, pc), device_id_type=MESH))
            for s_ in sends:
                s_.start()
            for r_ in recvs:
                r_.wait_recv()
            for s_ in sends:
                s_.wait_send()

        gather8(small_ref, all_ref, send_a, recv_a)
        for j in range(3):
            w_copy(j, False, False).start()
        sub = lax.broadcasted_iota(jnp.int32, (8, 1), 0)
        cv = jnp.where(sub == 0, all_ref[0:8, :D_MODEL], 0.0)
        for d in range(1, N_DEV):
            cv = jnp.where(sub == d, all_ref[8 * d:8 * d + 8, :D_MODEL], cv)
        act = cv * _sigmoid(cv)
        act_ref[...] = act
        part_ref[...] = jnp.dot(act, wa_ref[...], preferred_element_type=F32, precision=HI) + b_ref[...]
        gather8(part_ref, mod_ref, send_b, recv_b)

        for j in range(3):
            w_copy(j, True, False).wait_recv()
            w_copy(j, False, True).start()
        for j in range(3):
            w_copy(j, True, True).wait_recv()
        for j in range(3):
            w_copy(j, False, False).wait_send()
            w_copy(j, False, True).wait_send()

    vm = pl.BlockSpec(memory_space=pltpu.VMEM)
    hbm = pl.BlockSpec(memory_space=pl.ANY)
    return pl.pallas_call(
        body, name="front_exchanges",
        out_shape=(jax.ShapeDtypeStruct((8 * N_DEV, n), F32), jax.ShapeDtypeStruct((8 * N_DEV, cols), F32),
                   jax.ShapeDtypeStruct((N_DEV, D_MODEL), F32),
                   jax.ShapeDtypeStruct((N_CHIPS,) + w_shard.shape, w_shard.dtype)),
        in_specs=[vm, vm, vm, hbm], out_specs=(vm, vm, vm, hbm),
        scratch_shapes=[pltpu.VMEM((N_DEV, cols), F32)] + [pltpu.SemaphoreType.DMA((N_DEV - 1,))] * 4
        + [pltpu.SemaphoreType.DMA((6,))] * 2,
        compiler_params=_cparams(),
    )(small8, w_ada_shard, b_shard, w_shard)


def _send_to_sibling(arrs, name):
    n = len(arrs)

    def body(*refs):
        ins, outs = refs[:n], refs[n:2 * n]
        send_sems, recv_sems = refs[2 * n:]
        x, y, c = _coords()
        cps = []
        for a in range(n):
            cp = pltpu.make_async_remote_copy(
                src_ref=ins[a].at[:, :, _half_cols(arrs[a].shape[-1], 1 - c)], dst_ref=outs[a],
                send_sem=send_sems.at[a], recv_sem=recv_sems.at[a], device_id=(x, y, 1 - c), device_id_type=MESH)
            cp.start()
            cps.append(cp)
        for cp in cps:
            cp.wait()

    hbm = pl.BlockSpec(memory_space=pl.ANY)
    return pl.pallas_call(
        body, name=name,
        out_shape=tuple(jax.ShapeDtypeStruct(a.shape[:-1] + (a.shape[-1] // 2,), a.dtype) for a in arrs),
        in_specs=[hbm] * n, out_specs=tuple([hbm] * n),
        scratch_shapes=[pltpu.SemaphoreType.DMA((n,)), pltpu.SemaphoreType.DMA((n,))],
    )(*arrs)


def _swap_with_sibling(arrs, name):
    n = len(arrs)

    def body(*refs):
        ins, outs = refs[:n], refs[n:2 * n]
        send_sems, recv_sems = refs[2 * n:]
        x, y, c = _coords()
        cps = []
        for a in range(n):
            cp = pltpu.make_async_remote_copy(
                src_ref=ins[a], dst_ref=outs[a], send_sem=send_sems.at[a], recv_sem=recv_sems.at[a],
                device_id=(x, y, 1 - c), device_id_type=MESH)
            cp.start()
            cps.append(cp)
        for cp in cps:
            cp.wait()

    hbm = pl.BlockSpec(memory_space=pl.ANY)
    return pl.pallas_call(
        body, name=name,
        out_shape=tuple(jax.ShapeDtypeStruct(a.shape, a.dtype) for a in arrs),
        in_specs=[hbm] * n, out_specs=tuple([hbm] * n),
        scratch_shapes=[pltpu.SemaphoreType.DMA((n,)), pltpu.SemaphoreType.DMA((n,))],
    )(*arrs)


def _row_tile(rows, cols, n_arrays):
    budget = VMEM_LIMIT // 2
    t = rows
    while t % 16 == 0 and t * cols * 4 * n_arrays * 2 > budget:
        t //= 2
    return t


def _add_my_half(g, landed, my_c, name):
    nb, r, cdim = g.shape
    half = cdim // 2
    tr = _row_tile(r, half, 3)

    def body(c_ref, g_ref, l_ref, o_ref):
        o_ref[...] = (g_ref[...] + l_ref[...]).astype(BF16)

    spec = pl.BlockSpec((None, tr, half), lambda b, i, c_ref: (b, i, 0))
    return pl.pallas_call(
        body, name=name, out_shape=jax.ShapeDtypeStruct((nb, r, half), BF16),
        grid_spec=pltpu.PrefetchScalarGridSpec(
            num_scalar_prefetch=1, grid=(nb, r // tr),
            in_specs=[pl.BlockSpec((None, tr, half), lambda b, i, c_ref: (b, i, c_ref[0])), spec],
            out_specs=spec),
        compiler_params=_cparams(("parallel", "parallel")),
    )(my_c, g, landed)


def _sum_slots(a, name):
    nb, r, cdim = a.shape
    tr = _row_tile(r, cdim, 4)

    def body(a_ref, o_ref):
        o_ref[...] = ((a_ref[0].astype(F32) + a_ref[1].astype(F32)) + a_ref[2].astype(F32)) + a_ref[3].astype(F32)

    return pl.pallas_call(
        body, name=name, out_shape=jax.ShapeDtypeStruct((r, cdim), F32), grid=(r // tr,),
        in_specs=[pl.BlockSpec((nb, tr, cdim), lambda i: (0, i, 0))],
        out_specs=pl.BlockSpec((tr, cdim), lambda i: (i, 0)),
        compiler_params=_cparams(("parallel",)),
    )(a)


def _adamw(w, g, m, v, name):
    r, cdim = w.shape
    tr = _row_tile(r, cdim, 7)
    tc = cdim
    if tr == r and r > 8:
        while tc % (2 * LANE) == 0 and r * tc * 4 * 7 * 2 > VMEM_LIMIT // 2:
            tc //= 2

    def body(w_ref, g_ref, m_ref, v_ref, d_ref, nm_ref, nv_ref):
        gv = g_ref[...]
        nm = ADAM_B1 * m_ref[...] + (1.0 - ADAM_B1) * gv
        nv = ADAM_B2 * v_ref[...] + (1.0 - ADAM_B2) * (gv * gv)
        m_hat = nm / (1.0 - ADAM_B1 ** ADAM_STEP)
        v_hat = nv / (1.0 - ADAM_B2 ** ADAM_STEP)
        d_ref[...] = -ADAM_LR * (m_hat / (jnp.sqrt(v_hat) + ADAM_EPS) + ADAM_WD * w_ref[...])
        nm_ref[...] = nm
        nv_ref[...] = nv

    spec = pl.BlockSpec((tr, tc), lambda i, j: (i, j))
    shp = jax.ShapeDtypeStruct((r, cdim), F32)
    return pl.pallas_call(
        body, name=name, out_shape=(shp, shp, shp), grid=(r // tr, cdim // tc),
        in_specs=[spec] * 4, out_specs=(spec, spec, spec),
        compiler_params=_cparams(("parallel", "parallel")),
    )(w, g, m, v)


def _matmul(a, b, out_dtype, name, mode, tm, tn, tk, extra=None, n_out=None):
    dims = {"nn": NN, "nt": NT, "tn": TN}[mode]
    if mode == "tn":
        kdim, m = a.shape
    else:
        m, kdim = a.shape
    n = n_out if n_out is not None else (b.shape[0] if mode == "nt" else b.shape[1])
    tm, tn, tk = min(tm, m), min(tn, n), min(tk, kdim)
    nk = kdim // tk
    a_spec = (pl.BlockSpec((tk, tm), lambda i, j, k: (k, i)) if mode == "tn"
              else pl.BlockSpec((tm, tk), lambda i, j, k: (i, k)))
    b_spec = (pl.BlockSpec((tn, tk), lambda i, j, k: (j, k)) if mode == "nt"
              else pl.BlockSpec((tk, tn), lambda i, j, k: (k, j)))
    in_specs, operands = [a_spec, b_spec], [a, b]
    if extra is not None:
        a2, b2 = extra
        k2 = a2.shape[0] if mode == "tn" else a2.shape[1]
        in_specs.append(pl.BlockSpec((k2, tm), lambda i, j, k: (0, i)) if mode == "tn"
                        else pl.BlockSpec((tm, k2), lambda i, j, k: (i, 0)))
        in_specs.append(pl.BlockSpec((tn, k2), lambda i, j, k: (j, 0)) if mode == "nt"
                        else pl.BlockSpec((k2, tn), lambda i, j, k: (0, j)))
        operands += [a2, b2]

    def body_one_block(*refs):
        acc = lax.dot_general(refs[0][...], refs[1][...], dims, preferred_element_type=F32)
        if extra is not None:
            acc += lax.dot_general(refs[2][...], refs[3][...], dims, preferred_element_type=F32)
        refs[-1][...] = acc.astype(out_dtype)

    if nk == 1:
        return pl.pallas_call(
            body_one_block, name=name, out_shape=jax.ShapeDtypeStruct((m, n), out_dtype), grid=(m // tm, n // tn, 1),
            in_specs=in_specs, out_specs=pl.BlockSpec((tm, tn), lambda i, j, k: (i, j)),
            compiler_params=_cparams(("parallel", "parallel", "arbitrary")),
        )(*operands)

    def body(*refs):
        if extra is not None:
            a_ref, b_ref, a2_ref, b2_ref, o_ref, acc_ref = refs
        else:
            a_ref, b_ref, o_ref, acc_ref = refs
        k = pl.program_id(2)

        @pl.when(k == 0)
        def _():
            if extra is not None:
                acc_ref[...] = lax.dot_general(a2_ref[...], b2_ref[...], dims, preferred_element_type=F32)
            else:
                acc_ref[...] = jnp.zeros_like(acc_ref)

        acc_ref[...] += lax.dot_general(a_ref[...], b_ref[...], dims, preferred_element_type=F32)

        @pl.when(k == nk - 1)
        def _():
            o_ref[...] = acc_ref[...].astype(out_dtype)

    return pl.pallas_call(
        body, name=name, out_shape=jax.ShapeDtypeStruct((m, n), out_dtype), grid=(m // tm, n // tn, nk),
        in_specs=in_specs, out_specs=pl.BlockSpec((tm, tn), lambda i, j, k: (i, j)),
        scratch_shapes=[pltpu.VMEM((tm, tn), F32)],
        compiler_params=_cparams(("parallel", "parallel", "arbitrary")),
    )(*operands)


def _matmul_sum(terms, out_dtype, name, tm, tn, exchange=None, trans_b=False, n_out=None):
    m = terms[0][0].shape[0]
    n = n_out if n_out is not None else terms[0][3].shape[0 if trans_b else 1]
    tm, tn = min(tm, m), min(tn, n)
    gm, gn = m // tm, n // tn
    nt = len(terms)
    dims = NT if trans_b else NN
    in_specs, operands = [], []
    for a, ka, ia, b, ib in terms:
        in_specs.append(pl.BlockSpec((tm, ka), functools.partial(lambda i, j, ia: (i, ia), ia=ia)))
        if trans_b:
            in_specs.append(pl.BlockSpec((tn, ka), functools.partial(lambda i, j, ib: (j, ib), ib=ib)))
        else:
            in_specs.append(pl.BlockSpec((ka, tn), functools.partial(lambda i, j, ib: (ib, j), ib=ib)))
        operands += [a, b]
    sent = [] if exchange is None else list(exchange)
    ns = len(sent)
    hbm = pl.BlockSpec(memory_space=pl.ANY)

    def body(*refs):
        o_ref = refs[2 * nt + ns]
        if ns:
            ins, outs = refs[2 * nt:2 * nt + ns], refs[2 * nt + ns + 1:2 * nt + 2 * ns + 1]
            send_sems, recv_sems = refs[2 * nt + 2 * ns + 1:]
            x, y, c = _coords()
            k = 2 * x + y
            chips, chip_idx = _other_chips(x, y)
            step = pl.program_id(0) * gn + pl.program_id(1)

            def copies(a, j, landed):
                dst = outs[a].at[chip_idx[j]] if landed else outs[a].at[k]
                src = dst if landed else ins[a].at[chip_idx[j]]
                return pltpu.make_async_remote_copy(
                    src_ref=src, dst_ref=dst, send_sem=send_sems.at[3 * a + j], recv_sem=recv_sems.at[3 * a + j],
                    device_id=(*chips[j], c), device_id_type=MESH)

            @pl.when(step == 0)
            def _():
                for a in range(ns):
                    for j in range(3):
                        copies(a, j, False).start()

        acc = lax.dot_general(refs[0][...], refs[1][...], dims, preferred_element_type=F32)
        for t in range(1, nt):
            acc += lax.dot_general(refs[2 * t][...], refs[2 * t + 1][...], dims, preferred_element_type=F32)
        o_ref[...] = acc.astype(out_dtype)

        if ns:
            @pl.when(step == gm * gn - 1)
            def _():
                for a in range(ns):
                    for j in range(3):
                        copies(a, j, True).wait_recv()
                for a in range(ns):
                    for j in range(3):
                        copies(a, j, False).wait_send()

    main = jax.ShapeDtypeStruct((m, n), out_dtype)
    tile = pl.BlockSpec((tm, tn), lambda i, j: (i, j))
    if not ns:
        return pl.pallas_call(
            body, name=name, out_shape=main, grid=(gm, gn), in_specs=in_specs, out_specs=tile,
            compiler_params=_cparams(("parallel", "parallel")),
        )(*operands)
    return pl.pallas_call(
        body, name=name, out_shape=(main, *[jax.ShapeDtypeStruct(a.shape, a.dtype) for a in sent]), grid=(gm, gn),
        in_specs=in_specs + [hbm] * ns, out_specs=(tile, *[hbm] * ns),
        scratch_shapes=[pltpu.SemaphoreType.DMA((3 * ns,)), pltpu.SemaphoreType.DMA((3 * ns,))],
        compiler_params=_cparams(("arbitrary", "arbitrary")),
    )(*operands, *sent)


def _matmul_tn_pieces(pieces, rows, b, name, tm):
    kdim, n = b.shape
    tm = min(tm, min(p.shape[1] for p in pieces))
    tiles = [p.shape[1] // tm for p in pieces]
    first = [sum(tiles[:i]) for i in range(len(pieces))]

    def body(*refs):
        b_ref, o_ref = refs[-2:]
        i = pl.program_id(0)
        for p in range(len(pieces)):
            @pl.when(jnp.logical_and(i >= first[p], i < first[p] + tiles[p]))
            def _(p=p):
                o_ref[...] = lax.dot_general(refs[p][...], b_ref[...], TN, preferred_element_type=F32)

    in_specs = [pl.BlockSpec((kdim, tm), functools.partial(lambda i, lo, cnt: (0, jnp.clip(i - lo, 0, cnt - 1)),
                                                           lo=first[p], cnt=tiles[p])) for p in range(len(pieces))]
    return pl.pallas_call(
        body, name=name, out_shape=jax.ShapeDtypeStruct((rows, n), F32), grid=(sum(tiles),),
        in_specs=in_specs + [pl.BlockSpec((kdim, n), lambda i: (0, 0))],
        out_specs=pl.BlockSpec((tm, n), lambda i: (i, 0)),
        compiler_params=_cparams(("arbitrary",)),
    )(*pieces, b)


def _matmul_tn_rows(buf, rows, a, b, row_blk, name, tm):
    kdim, m = a.shape
    n = b.shape[1]
    tm = min(tm, m)

    def body(*refs):
        a_ref, b_ref, o_ref = refs[-3:]
        o_ref[...] = lax.dot_general(a_ref[...], b_ref[...], TN, preferred_element_type=F32)

    in_specs = [pl.BlockSpec((kdim, tm), lambda i: (0, i)), pl.BlockSpec((kdim, n), lambda i: (0, 0))]
    operands = [a, b]
    if buf is not None:
        in_specs.insert(0, pl.BlockSpec(memory_space=pl.ANY))
        operands.insert(0, buf)
    return pl.pallas_call(
        body, name=name, out_shape=jax.ShapeDtypeStruct((rows, n), F32), grid=(m // tm,),
        in_specs=in_specs, out_specs=pl.BlockSpec((tm, n), lambda i: (row_blk + i, 0)),
        input_output_aliases={} if buf is None else {0: 0},
        compiler_params=_cparams(("parallel",)),
    )(*operands)


def _ada_mod(c_all, w_shard, b_shard):
    nb, d = c_all.shape
    cols = w_shard.shape[1]

    def body(c_ref, w_ref, b_ref, mod_ref, act_ref):
        cv = c_ref[...]
        act = cv * _sigmoid(cv)
        act_ref[...] = act
        mod_ref[...] = jnp.dot(act, w_ref[...], preferred_element_type=F32, precision=HI) + b_ref[...]

    return pl.pallas_call(
        body, name="ada_mod",
        out_shape=(jax.ShapeDtypeStruct((nb, cols), F32), jax.ShapeDtypeStruct((nb, d), F32)),
        compiler_params=_cparams(),
    )(c_all, w_shard, b_shard)


def _rms_mod_fwd(x, gain, scale, shift):
    s, d = x.shape
    tm = min(512, s)

    def body(x_ref, g_ref, sc_ref, sh_ref, h_ref):
        xv = x_ref[...]
        r = lax.rsqrt(jnp.mean(xv * xv, axis=-1, keepdims=True) + NORM_EPS)
        h_ref[...] = (xv * r * g_ref[...] * (1.0 + sc_ref[...]) + sh_ref[...]).astype(BF16)

    row = pl.BlockSpec((1, d), lambda i: (0, 0))
    tile = pl.BlockSpec((tm, d), lambda i: (i, 0))
    return pl.pallas_call(
        body, name="rms_mod_fwd", out_shape=jax.ShapeDtypeStruct((s, d), BF16), grid=(s // tm,),
        in_specs=[tile, row, row, row], out_specs=tile, compiler_params=_cparams(("parallel",)),
    )(x, gain, scale, shift)


def _rms_mod_bwd(x, dh, dres, gain, scale):
    s, d = x.shape
    tm = min(512, s)

    def body(x_ref, dh_ref, dres_ref, g_ref, sc_ref, dx_ref, sums_ref):
        @pl.when(pl.program_id(0) == 0)
        def _():
            sums_ref[...] = jnp.zeros_like(sums_ref)

        xv, dhv = x_ref[...], dh_ref[...]
        r = lax.rsqrt(jnp.mean(xv * xv, axis=-1, keepdims=True) + NORM_EPS)
        nrm = xv * r
        g, one_sc = g_ref[...], 1.0 + sc_ref[...]
        dn = dhv * g * one_sc
        dx_ref[...] = r * (dn - nrm * jnp.mean(dn * nrm, axis=-1, keepdims=True)) + dres_ref[...]
        dhn = dhv * nrm
        sums_ref[0:1, :] += jnp.sum(dhv, axis=0, keepdims=True)
        sums_ref[1:2, :] += jnp.sum(dhn * g, axis=0, keepdims=True)
        sums_ref[2:3, :] += jnp.sum(dhn * one_sc, axis=0, keepdims=True)

    row = pl.BlockSpec((1, d), lambda i: (0, 0))
    tile = pl.BlockSpec((tm, d), lambda i: (i, 0))
    return pl.pallas_call(
        body, name="rms_mod_bwd",
        out_shape=(jax.ShapeDtypeStruct((s, d), F32), jax.ShapeDtypeStruct((3, d), F32)), grid=(s // tm,),
        in_specs=[tile, tile, tile, row, row], out_specs=(tile, pl.BlockSpec((3, d), lambda i: (0, 0))),
        compiler_params=_cparams(("arbitrary",)),
    )(x, dh, dres, gain, scale)


def _loss_head(x, mixed, gate, gain_f, target):
    s, d = x.shape
    tm = min(512, s)

    def body(x_ref, mx_ref, gt_ref, gf_ref, t_ref, dx2_ref, dmx_ref, sums_ref):
        @pl.when(pl.program_id(0) == 0)
        def _():
            sums_ref[...] = jnp.zeros_like(sums_ref)

        mx, gt, gf = mx_ref[...], gt_ref[...], gf_ref[...]
        x2 = x_ref[...] + gt * mx
        r = lax.rsqrt(jnp.mean(x2 * x2, axis=-1, keepdims=True) + NORM_EPS)
        nrm = x2 * r
        err = nrm * gf - t_ref[...]
        dyf = err * (1.0 / d)
        dn = dyf * gf
        dx2 = r * (dn - nrm * jnp.mean(dn * nrm, axis=-1, keepdims=True))
        dx2_ref[...] = dx2
        dmx_ref[...] = (dx2 * gt).astype(BF16)
        sums_ref[0:1, :] += jnp.sum(err * err, axis=0, keepdims=True)
        sums_ref[1:2, :] += jnp.sum(dyf * nrm, axis=0, keepdims=True)
        sums_ref[2:3, :] += jnp.sum(dx2 * mx, axis=0, keepdims=True)

    row = pl.BlockSpec((1, d), lambda i: (0, 0))
    tile = pl.BlockSpec((tm, d), lambda i: (i, 0))
    return pl.pallas_call(
        body, name="loss_head",
        out_shape=(jax.ShapeDtypeStruct((s, d), F32), jax.ShapeDtypeStruct((s, d), BF16),
                   jax.ShapeDtypeStruct((3, d), F32)),
        grid=(s // tm,), in_specs=[tile, tile, row, row, tile],
        out_specs=(tile, tile, pl.BlockSpec((3, d), lambda i: (0, 0))),
        compiler_params=_cparams(("arbitrary",)),
    )(x, mixed, gate, gain_f, target)


def _silu_grad(z, sg):
    return sg * (1.0 + z * (1.0 - sg))


def _gated_norm_fwd(o, proj, z_off, gain, gate_inside, name):
    s, d = o.shape
    tm = min(512, s)
    zb = z_off // d

    def body(o_ref, z_ref, g_ref, y_ref):
        z = z_ref[...].astype(F32)
        sz = z * _sigmoid(z)
        u = o_ref[...] * sz if gate_inside else o_ref[...]
        r = lax.rsqrt(jnp.mean(u * u, axis=-1, keepdims=True) + NORM_EPS)
        y = u * r * g_ref[...]
        y_ref[...] = (y if gate_inside else y * sz).astype(BF16)

    tile = pl.BlockSpec((tm, d), lambda i: (i, 0))
    return pl.pallas_call(
        body, name=name, out_shape=jax.ShapeDtypeStruct((s, d), BF16), grid=(s // tm,),
        in_specs=[tile, pl.BlockSpec((tm, d), lambda i: (i, zb)), pl.BlockSpec((1, d), lambda i: (0, 0))],
        out_specs=tile, compiler_params=_cparams(("parallel",)),
    )(o, proj, gain)


def _gated_norm_bwd(dy_all, dy_blk, o, proj, z_off, gain, gate_inside, name):
    s, d = o.shape
    tm = min(512, s)
    zb = z_off // d

    def body(dy_ref, o_ref, z_ref, g_ref, do_ref, dz_ref, dg_ref):
        @pl.when(pl.program_id(0) == 0)
        def _():
            dg_ref[...] = jnp.zeros_like(dg_ref)

        z = z_ref[...].astype(F32)
        sg = _sigmoid(z)
        sz = z * sg
        ov, dy, g = o_ref[...], dy_ref[...].astype(F32), g_ref[...]
        u = ov * sz if gate_inside else ov
        r = lax.rsqrt(jnp.mean(u * u, axis=-1, keepdims=True) + NORM_EPS)
        nrm = u * r
        if gate_inside:
            dg_ref[...] += jnp.sum(dy * nrm, axis=0, keepdims=True)
            dn = dy * g
        else:
            dg_ref[...] += jnp.sum(dy * nrm * sz, axis=0, keepdims=True)
            dn = dy * g * sz
        du = r * (dn - nrm * jnp.mean(dn * nrm, axis=-1, keepdims=True))
        if gate_inside:
            do_ref[...] = du * sz
            dz_ref[...] = (du * ov * _silu_grad(z, sg)).astype(BF16)
        else:
            do_ref[...] = du
            dz_ref[...] = (dy * nrm * g * _silu_grad(z, sg)).astype(BF16)

    tile = pl.BlockSpec((tm, d), lambda i: (i, 0))
    row = pl.BlockSpec((1, d), lambda i: (0, 0))
    return pl.pallas_call(
        body, name=name,
        out_shape=(jax.ShapeDtypeStruct((s, d), F32), jax.ShapeDtypeStruct((s, d), BF16),
                   jax.ShapeDtypeStruct((1, d), F32)),
        grid=(s // tm,),
        in_specs=[pl.BlockSpec((tm, d), lambda i: (i, dy_blk)), tile, pl.BlockSpec((tm, d), lambda i: (i, zb)), row],
        out_specs=(tile, tile, row), compiler_params=_cparams(("arbitrary",)),
    )(dy_all, o, proj, gain)


def _sb_logits(qh, kb):
    z = lax.dot_general(qh, kb, NT, preferred_element_type=F32)
    neg_abs = lax.bitcast_convert_type(lax.bitcast_convert_type(z, jnp.uint32) | jnp.uint32(0x80000000), F32)
    lb = jnp.minimum(z, 0.0) - jnp.log(1.0 + jnp.exp(neg_abs))
    return lb, lb - z


def _attn_consts(tk):
    lane = lax.broadcasted_iota(jnp.int32, (1, LANE), 1)
    row = lax.broadcasted_iota(jnp.int32, (tk, tk), 0)
    col = lax.broadcasted_iota(jnp.int32, (tk, tk), 1)
    return (lane < HEAD_DIM, lane >= HEAD_DIM), row, col


def _band_mask(rows, tk):
    return lax.broadcasted_iota(jnp.int32, (rows, tk), 1) < lax.broadcasted_iota(jnp.int32, (rows, tk), 0)


def _attn_fwd(proj):
    s = proj.shape[0]
    tq, tk = min(ATTN_TQ, s), min(ATTN_TK, s)
    r = tq // tk

    def body(q_ref, k_ref, v_ref, o_ref, l_ref, acc_ref, run_ref):
        i = pl.program_id(1)
        head_mask, row, col = _attn_consts(tk)
        later = (row > col).astype(BF16)
        q = q_ref[...] * ATTN_SCALE
        qh = [jnp.where(m, q, jnp.zeros_like(q)) for m in head_mask]
        acc_ref[...] = jnp.zeros_like(acc_ref)
        run_ref[...] = jnp.zeros_like(run_ref)

        def block(j, lo, hi, band):
            start = pl.multiple_of(j * tk, tk)
            kb = k_ref[pl.ds(start, tk), :]
            vb = v_ref[pl.ds(start, tk), :]
            rows = slice(lo, hi)
            causal = _band_mask(hi - lo, tk) if band else None
            hs = range(2)
            logits = [_sb_logits(qh[h][rows], kb) for h in hs]
            lb = [logits[h][0] for h in hs]
            l1m = [logits[h][1] if causal is None else jnp.where(causal, logits[h][1], 0.0) for h in hs]
            tail = [jnp.dot(l1m[h].astype(BF16), later, preferred_element_type=F32) + run_ref[h, rows] for h in hs]
            w = [jnp.exp(lb[h] + tail[h]) for h in hs]
            if causal is not None:
                w = [jnp.where(causal, w[h], 0.0) for h in hs]
            vh = [jnp.where(head_mask[h], vb, jnp.zeros_like(vb)) for h in hs]
            acc_ref[rows, :] += (jnp.dot(w[0].astype(BF16), vh[0], preferred_element_type=F32)
                                 + jnp.dot(w[1].astype(BF16), vh[1], preferred_element_type=F32))
            for h in hs:
                run_ref[h, rows] += jnp.sum(l1m[h], axis=1, keepdims=True)

        for b in reversed(range(r)):
            block(i * r + b, b * tk, tq, True)
        n_full = i * r
        half = tq // 2

        def more(c):
            return jnp.logical_and(c[0] < n_full, c[1] > LOG_ZERO)

        def step_all(c):
            block(n_full - 1 - c[0], 0, tq, False)
            return c[0] + 1, jnp.max(run_ref[:, half:, :])

        def step_upper(c):
            block(n_full - 1 - c[0], 0, half, False)
            return c[0] + 1, jnp.max(run_ref[:, :half, :])

        seen_all, _ = lax.while_loop(more, step_all, (jnp.int32(0), jnp.max(run_ref[:, half:, :])))
        seen, _ = lax.while_loop(more, step_upper, (seen_all, jnp.max(run_ref[:, :half, :])))
        o_ref[...] = acc_ref[...]
        lane = lax.broadcasted_iota(jnp.int32, (1, LANE), 1)
        first = jnp.where(lane < 3 * HEAD_DIM // 4, n_full - seen, n_full - seen_all).astype(F32)
        l_ref[...] = jnp.where(lane < HEAD_DIM // 2, run_ref[0], jnp.where(lane < HEAD_DIM, first, run_ref[1]))

    kq, kk, kv = OFF_Q // LANE, OFF_K // LANE, OFF_V // LANE
    tile = pl.BlockSpec((tq, LANE), lambda p, i: (i, p))
    return pl.pallas_call(
        body, name="attn_fwd",
        out_shape=(jax.ShapeDtypeStruct((s, D_ATTN), F32), jax.ShapeDtypeStruct((s, D_ATTN), F32)),
        grid=(N_PAIRS, s // tq),
        in_specs=[pl.BlockSpec((tq, LANE), lambda p, i: (i, kq + p)),
                  pl.BlockSpec((s, LANE), lambda p, i: (0, kk + p)),
                  pl.BlockSpec((s, LANE), lambda p, i: (0, kv + p))],
        out_specs=(tile, tile),
        scratch_shapes=[pltpu.VMEM((tq, LANE), F32), pltpu.VMEM((2, tq, 1), F32)],
        compiler_params=_cparams(("parallel", "arbitrary")),
    )(proj, proj, proj)


def _attn_bwd(proj, do, lsum):
    s = proj.shape[0]
    tq, tk = min(ATTN_TQ, s), min(ATTN_TK, s)
    r = tq // tk

    def body(q_ref, k_ref, v_ref, do_ref, l_ref, dq_ref, dk_ref, dv_ref, dqacc_ref, dkacc_ref, dvacc_ref,
             passed_ref, pre_ref):
        i = pl.program_id(1)

        @pl.when(i == 0)
        def _():
            dkacc_ref[...] = jnp.zeros_like(dkacc_ref)
            dvacc_ref[...] = jnp.zeros_like(dvacc_ref)

        head_mask, row, col = _attn_consts(tk)
        later = (row > col).astype(BF16)
        earlier = (row < col).astype(BF16)
        q = q_ref[...] * ATTN_SCALE
        dov = do_ref[...].astype(BF16)
        qh = [jnp.where(m, q, jnp.zeros_like(q)) for m in head_mask]
        doh = [jnp.where(m, dov, jnp.zeros_like(dov)) for m in head_mask]
        lsum_v = l_ref[...]
        lh = [lsum_v[:, 0:1], lsum_v[:, HEAD_DIM:HEAD_DIM + 1]]
        n_full = i * r
        half = tq // 2
        quarter = HEAD_DIM // 4
        first_all = jnp.clip(jnp.max(lsum_v[0:8, 3 * quarter:HEAD_DIM]).astype(jnp.int32), 0, n_full)
        first = jnp.clip(jnp.max(lsum_v[0:8, 2 * quarter:3 * quarter]).astype(jnp.int32), 0, first_all)
        dqacc_ref[...] = jnp.zeros_like(dqacc_ref)
        passed_ref[...] = jnp.zeros_like(passed_ref)
        pre_ref[...] = jnp.zeros_like(pre_ref)

        def block(j, lo, hi, band):
            start = pl.multiple_of(j * tk, tk)
            kb = k_ref[pl.ds(start, tk), :]
            vb = v_ref[pl.ds(start, tk), :]
            rows = slice(lo, hi)
            causal = _band_mask(hi - lo, tk) if band else None
            hs = range(2)
            q_rows = [qh[h][rows] for h in hs]
            do_rows = [doh[h][rows] for h in hs]
            logits = [_sb_logits(q_rows[h], kb) for h in hs]
            lb = [logits[h][0] for h in hs]
            l1m = [logits[h][1] if causal is None else jnp.where(causal, logits[h][1], 0.0) for h in hs]
            da = [lax.dot_general(do_rows[h], vb, NT, preferred_element_type=F32) for h in hs]
            rs = [jnp.sum(l1m[h], axis=1, keepdims=True) for h in hs]
            right = [lh[h][rows] - passed_ref[h, rows] - rs[h] for h in hs]
            for h in hs:
                passed_ref[h, rows] += rs[h]
            tail = [jnp.dot(l1m[h].astype(BF16), later, preferred_element_type=F32) + right[h] for h in hs]
            a = [jnp.exp(lb[h] + tail[h]) for h in hs]
            if causal is not None:
                a = [jnp.where(causal, a[h], 0.0) for h in hs]
            g = [a[h] * da[h] for h in hs]
            pre = [jnp.dot(g[h].astype(BF16), earlier, preferred_element_type=F32) + pre_ref[h, rows] for h in hs]
            for h in hs:
                pre_ref[h, rows] += jnp.sum(g[h], axis=1, keepdims=True)
            dz = [g[h] - jnp.exp(lb[h]) * (g[h] + pre[h]) for h in hs]
            if causal is not None:
                dz = [jnp.where(causal, dz[h], 0.0) for h in hs]
            dzb = [dz[h].astype(BF16) for h in hs]
            kh = [jnp.where(head_mask[h], kb, jnp.zeros_like(kb)) * ATTN_SCALE for h in hs]
            dqacc_ref[rows, :] += (jnp.dot(dzb[0], kh[0], preferred_element_type=F32)
                                   + jnp.dot(dzb[1], kh[1], preferred_element_type=F32))
            dvacc_ref[pl.ds(start, tk), :] += (
                lax.dot_general(a[0].astype(BF16), do_rows[0], TN, preferred_element_type=F32)
                + lax.dot_general(a[1].astype(BF16), do_rows[1], TN, preferred_element_type=F32))
            dkacc_ref[pl.ds(start, tk), :] += (
                lax.dot_general(dzb[0], q_rows[0], TN, preferred_element_type=F32)
                + lax.dot_general(dzb[1], q_rows[1], TN, preferred_element_type=F32))

        def step_upper(j, carry):
            block(j, 0, half, False)
            return carry

        def step_all(j, carry):
            block(j, 0, tq, False)
            return carry

        lax.fori_loop(first, first_all, step_upper, 0)
        lax.fori_loop(first_all, n_full, step_all, 0)
        for b in range(r):
            block(n_full + b, b * tk, tq, True)
        dq_ref[...] = dqacc_ref[...].astype(BF16)

        @pl.when(i == pl.num_programs(1) - 1)
        def _():
            dk_ref[...] = dkacc_ref[...].astype(BF16)
            dv_ref[...] = dvacc_ref[...].astype(BF16)

    kq, kk, kv = OFF_Q // LANE, OFF_K // LANE, OFF_V // LANE
    tile = pl.BlockSpec((tq, LANE), lambda p, i: (i, p))
    full = pl.BlockSpec((s, LANE), lambda p, i: (0, p))
    shp = jax.ShapeDtypeStruct((s, D_ATTN), BF16)
    return pl.pallas_call(
        body, name="attn_bwd", out_shape=(shp, shp, shp), grid=(N_PAIRS, s // tq),
        in_specs=[pl.BlockSpec((tq, LANE), lambda p, i: (i, kq + p)),
                  pl.BlockSpec((s, LANE), lambda p, i: (0, kk + p)),
                  pl.BlockSpec((s, LANE), lambda p, i: (0, kv + p)),
                  tile, tile],
        out_specs=(tile, full, full),
        scratch_shapes=[pltpu.VMEM((tq, LANE), F32), pltpu.VMEM((s, LANE), F32), pltpu.VMEM((s, LANE), F32),
                        pltpu.VMEM((2, tq, 1), F32), pltpu.VMEM((2, tq, 1), F32)],
        compiler_params=_cparams(("parallel", "arbitrary")),
    )(proj, proj, proj, do, lsum)


def _shift_down(u, k, rows):
    return jnp.where(rows >= k, pltpu.roll(u, k, 0), 0.0)


def _shift_up(u, k, rows, s):
    return jnp.where(rows < s - k, pltpu.roll(u, s - k, 0), 0.0)


def _conv_fwd(proj, w, b):
    s = proj.shape[0]
    blk0 = OFF_XBC // LANE

    def body(u_ref, w_ref, b_ref, o_ref):
        u = u_ref[...].astype(F32)
        rows = lax.broadcasted_iota(jnp.int32, (s, 1), 0)
        pre = u * w_ref[CONV_K - 1:CONV_K, :] + b_ref[...]
        for k in range(1, CONV_K):
            pre += _shift_down(u, k, rows) * w_ref[CONV_K - 1 - k:CONV_K - k, :]
        o_ref[...] = pre * _sigmoid(pre)

    return pl.pallas_call(
        body, name="conv_fwd", out_shape=jax.ShapeDtypeStruct((s, D_XBC), F32), grid=(D_XBC // LANE,),
        in_specs=[pl.BlockSpec((s, LANE), lambda j: (0, blk0 + j)), pl.BlockSpec((CONV_K, LANE), lambda j: (0, j)),
                  pl.BlockSpec((1, LANE), lambda j: (0, j))],
        out_specs=pl.BlockSpec((s, LANE), lambda j: (0, j)), compiler_params=_cparams(("parallel",)),
    )(proj, w, b)


def _conv_bwd(proj, w, b, dact):
    s = proj.shape[0]
    blk0 = OFF_XBC // LANE

    def body(u_ref, w_ref, b_ref, da_ref, du_ref, dw_ref, db_ref):
        u = u_ref[...].astype(F32)
        rows = lax.broadcasted_iota(jnp.int32, (s, 1), 0)
        shifted = [u] + [_shift_down(u, k, rows) for k in range(1, CONV_K)]
        pre = b_ref[...] + shifted[0] * w_ref[CONV_K - 1:CONV_K, :]
        for k in range(1, CONV_K):
            pre += shifted[k] * w_ref[CONV_K - 1 - k:CONV_K - k, :]
        sg = _sigmoid(pre)
        dpre = da_ref[...] * _silu_grad(pre, sg)
        db_ref[...] = jnp.sum(dpre, axis=0, keepdims=True)
        du = dpre * w_ref[CONV_K - 1:CONV_K, :]
        for k in range(CONV_K):
            dw_ref[CONV_K - 1 - k:CONV_K - k, :] = jnp.sum(dpre * shifted[k], axis=0, keepdims=True)
            if k:
                du += _shift_up(dpre, k, rows, s) * w_ref[CONV_K - 1 - k:CONV_K - k, :]
        du_ref[...] = du.astype(BF16)

    col = pl.BlockSpec((s, LANE), lambda j: (0, j))
    return pl.pallas_call(
        body, name="conv_bwd",
        out_shape=(jax.ShapeDtypeStruct((s, D_XBC), BF16), jax.ShapeDtypeStruct((CONV_K, D_XBC), F32),
                   jax.ShapeDtypeStruct((1, D_XBC), F32)),
        grid=(D_XBC // LANE,),
        in_specs=[pl.BlockSpec((s, LANE), lambda j: (0, blk0 + j)), pl.BlockSpec((CONV_K, LANE), lambda j: (0, j)),
                  pl.BlockSpec((1, LANE), lambda j: (0, j)), col],
        out_specs=(col, pl.BlockSpec((CONV_K, LANE), lambda j: (0, j)), pl.BlockSpec((1, LANE), lambda j: (0, j))),
        compiler_params=_cparams(("parallel",)),
    )(proj, w, b, dact)


def _ssd_decays(dtraw_ref, bias_ref, dtt_ref, biast_ref, arow_ref, acol_ref):
    ln = CHUNK
    dt = _softplus(dtraw_ref[...] + bias_ref[...])
    r = lax.broadcasted_iota(jnp.int32, (ln, ln), 0)
    c = lax.broadcasted_iota(jnp.int32, (ln, ln), 1)
    ac = jnp.dot((r >= c).astype(F32), dt * arow_ref[...], preferred_element_type=F32, precision=HI)
    dtt = _softplus(dtt_ref[...] + biast_ref[...])
    act = jnp.dot(dtt * acol_ref[...], (r <= c).astype(F32), preferred_element_type=F32, precision=HI)
    return dt, ac, act, r >= c


def _pair_cols(m0, v, h0):
    return jnp.where(m0, v[:, h0:h0 + 1], v[:, h0 + 1:h0 + 2])


def _ssd_fwd(act, dtraw, dtt, bias, biast, arow, acol, dskip):
    s = act.shape[0]
    ln = CHUNK
    nc = s // ln

    def body(act_ref, dtraw_ref, dtt_ref, bias_ref, biast_ref, arow_ref, acol_ref, dsk_ref, y_ref, st_ref,
             state_ref):
        @pl.when(pl.program_id(0) == 0)
        def _():
            state_ref[...] = jnp.zeros_like(state_ref)

        dt, ac, act_t, lower = _ssd_decays(dtraw_ref, bias_ref, dtt_ref, biast_ref, arow_ref, acol_ref)
        lane = lax.broadcasted_iota(jnp.int32, (1, LANE), 1)
        m0 = lane < HEAD_DIM
        for g in range(N_GROUPS):
            bg32 = act_ref[:, D_SSM + g * D_STATE:D_SSM + (g + 1) * D_STATE]
            bg, bg_t = bg32.astype(BF16), bg32.T.astype(BF16)
            cg = act_ref[:, D_SSM + (N_GROUPS + g) * D_STATE:D_SSM + (N_GROUPS + g + 1) * D_STATE].astype(BF16)
            cb = lax.dot_general(cg, bg, NT, preferred_element_type=F32)
            for p in range(g * 4, g * 4 + 4):
                h0 = 2 * p
                xp = act_ref[:, p * LANE:(p + 1) * LANE]
                xdt = xp * _pair_cols(m0, dt, h0)
                acp = _pair_cols(m0, ac, h0)
                last = acp[ln - 1:ln, :]
                y = xp * dsk_ref[:, p * LANE:(p + 1) * LANE]
                for hh in range(2):
                    h = h0 + hh
                    dm = jnp.exp(jnp.where(lower, ac[:, h:h + 1] - act_t[h:h + 1, :], -jnp.inf))
                    mask = m0 if hh == 0 else jnp.logical_not(m0)
                    y += jnp.dot((cb * dm).astype(BF16), jnp.where(mask, xdt, 0.0).astype(BF16),
                                 preferred_element_type=F32)
                prev = state_ref[p]
                st_ref[0, p] = prev
                y += jnp.dot(cg, prev.astype(BF16), preferred_element_type=F32) * jnp.exp(acp)
                y_ref[:, p * LANE:(p + 1) * LANE] = y
                cs = jnp.dot(bg_t, (xdt * jnp.exp(last - acp)).astype(BF16), preferred_element_type=F32)
                state_ref[p] = prev * jnp.exp(last) + cs

    row = lambda w: pl.BlockSpec((1, w), lambda c: (0, 0))
    return pl.pallas_call(
        body, name="ssd_fwd",
        out_shape=(jax.ShapeDtypeStruct((s, D_SSM), F32),
                   jax.ShapeDtypeStruct((nc, N_PAIRS, LANE, D_STATE), F32)),
        grid=(nc,),
        in_specs=[pl.BlockSpec((ln, D_XBC), lambda c: (c, 0)), pl.BlockSpec((ln, LANE), lambda c: (c, 0)),
                  pl.BlockSpec((N_HEADS, ln), lambda c: (0, c)), row(LANE),
                  pl.BlockSpec((N_HEADS, 1), lambda c: (0, 0)), row(LANE),
                  pl.BlockSpec((N_HEADS, 1), lambda c: (0, 0)), row(D_SSM)],
        out_specs=(pl.BlockSpec((ln, D_SSM), lambda c: (c, 0)),
                   pl.BlockSpec((1, N_PAIRS, LANE, D_STATE), lambda c: (c, 0, 0, 0))),
        scratch_shapes=[pltpu.VMEM((N_PAIRS, LANE, D_STATE), F32)],
        compiler_params=_cparams(("arbitrary",)),
    )(act, dtraw, dtt, bias, biast, arow, acol, dskip)


def _ssd_bwd(act, dtraw, dtt, bias, biast, arow, acol, dskip, states, dy):
    s = act.shape[0]
    ln = CHUNK
    nc = s // ln

    def body(act_ref, dtraw_ref, dtt_ref, bias_ref, biast_ref, arow_ref, acol_ref, dsk_ref, st_ref, dy_ref,
             dact_ref, dldc_ref, dldr_ref, ddt_ref, dd_ref, dstate_ref):
        @pl.when(pl.program_id(0) == 0)
        def _():
            dstate_ref[...] = jnp.zeros_like(dstate_ref)
            dd_ref[...] = jnp.zeros_like(dd_ref)

        dt, ac, act_t, lower = _ssd_decays(dtraw_ref, bias_ref, dtt_ref, biast_ref, arow_ref, acol_ref)
        lane = lax.broadcasted_iota(jnp.int32, (1, LANE), 1)
        m0 = lane < HEAD_DIM
        halves = (m0, jnp.logical_not(m0))
        is_last = lax.broadcasted_iota(jnp.int32, (ln, 1), 0) == ln - 1
        sub = lax.broadcasted_iota(jnp.int32, (N_HEADS, 1), 0)
        earlier_eq = jnp.logical_not(lower) | (lax.broadcasted_iota(jnp.int32, (ln, ln), 0)
                                               == lax.broadcasted_iota(jnp.int32, (ln, ln), 1))
        dac_col = jnp.zeros((ln, LANE), F32)
        dac_row = jnp.zeros((N_HEADS, ln), F32)
        ddt_col = jnp.zeros((ln, LANE), F32)

        def half_sum(v, hh):
            return jnp.sum(jnp.where(halves[hh], v, 0.0), axis=1, keepdims=True)

        for g in range(N_GROUPS):
            b_lo, c_lo = D_SSM + g * D_STATE, D_SSM + (N_GROUPS + g) * D_STATE
            bg32 = act_ref[:, b_lo:b_lo + D_STATE]
            cg32 = act_ref[:, c_lo:c_lo + D_STATE]
            bg, cg = bg32.astype(BF16), cg32.astype(BF16)
            cg_t = cg32.T.astype(BF16)
            cb_t = lax.dot_general(bg, cg, NT, preferred_element_type=F32)
            dcb_t = jnp.zeros((ln, ln), F32)
            dbg = jnp.zeros((ln, D_STATE), F32)
            dcg = jnp.zeros((ln, D_STATE), F32)
            for p in range(g * 4, g * 4 + 4):
                h0 = 2 * p
                cols = slice(p * LANE, (p + 1) * LANE)
                xp = act_ref[:, cols]
                dyp = dy_ref[:, cols]
                dtp = _pair_cols(m0, dt, h0)
                acp = _pair_cols(m0, ac, h0)
                last = acp[ln - 1:ln, :]
                xdt = xp * dtp
                eac = jnp.exp(acp)
                dte = jnp.exp(last - acp)
                dec = jnp.exp(last)
                prev = st_ref[0, p]
                prev_b = prev.astype(BF16)
                ds = dstate_ref[p]
                ds_b = ds.astype(BF16)

                dd_ref[:, cols] += jnp.sum(dyp * xp, axis=0, keepdims=True)
                dx = dyp * dsk_ref[:, cols]
                zoff = jnp.dot(cg, prev_b, preferred_element_type=F32)
                dz_b = (dyp * eac).astype(BF16)
                dcg += lax.dot_general(dz_b, prev_b, NT, preferred_element_type=F32)
                dprev = jnp.dot(cg_t, dz_b, preferred_element_type=F32) + ds * dec
                wmat = jnp.dot(bg, ds_b, preferred_element_type=F32)
                xdte_b = (xdt * dte).astype(BF16)
                dbg += lax.dot_general(xdte_b, ds_b, NT, preferred_element_type=F32)
                dxdt = dte * wmat
                t_dte = xdt * wmat * dte
                t_ac = dyp * zoff * eac - t_dte
                at_last = jnp.sum(ds * prev, axis=0, keepdims=True) * dec + jnp.sum(t_dte, axis=0, keepdims=True)
                for hh in range(2):
                    h = h0 + hh
                    here = lane == h
                    dm_t = jnp.exp(jnp.where(earlier_eq, act_t[h:h + 1, :] - ac[:, h:h + 1], -jnp.inf))
                    mm_t = cb_t * dm_t
                    dyh = jnp.where(halves[hh], dyp, 0.0).astype(BF16)
                    xdth = jnp.where(halves[hh], xdt, 0.0).astype(BF16)
                    dmm_t = lax.dot_general(xdth, dyh, NT, preferred_element_type=F32)
                    dxdt += jnp.dot(mm_t.astype(BF16), dyh, preferred_element_type=F32)
                    gm_t = dmm_t * mm_t
                    dcb_t += dmm_t * dm_t
                    dac_col += jnp.where(here, half_sum(t_ac, hh) - jnp.sum(gm_t, axis=1, keepdims=True), 0.0)
                    dac_col += jnp.where(jnp.logical_and(is_last, here), half_sum(at_last, hh), 0.0)
                    dac_row += jnp.where(sub == h, jnp.sum(gm_t, axis=0, keepdims=True), 0.0)
                    ddt_col += jnp.where(here, half_sum(dxdt * xp, hh), 0.0)
                dact_ref[:, cols] = dx + dxdt * dtp
                dstate_ref[p] = dprev
            dcb_tb = dcb_t.astype(BF16)
            dact_ref[:, b_lo:b_lo + D_STATE] = dbg + jnp.dot(dcb_tb, cg, preferred_element_type=F32)
            dact_ref[:, c_lo:c_lo + D_STATE] = dcg + lax.dot_general(dcb_tb, bg, TN, preferred_element_type=F32)

        r = lax.broadcasted_iota(jnp.int32, (ln, ln), 0)
        c = lax.broadcasted_iota(jnp.int32, (ln, ln), 1)
        dldc_ref[...] = jnp.dot((r <= c).astype(F32), dac_col, preferred_element_type=F32, precision=HI)
        dldr_ref[...] = jnp.dot(dac_row, (r >= c).astype(F32), preferred_element_type=F32, precision=HI)
        ddt_ref[...] = ddt_col

    rev = lambda c: nc - 1 - c
    row = lambda w: pl.BlockSpec((1, w), lambda c: (0, 0))
    col16 = pl.BlockSpec((N_HEADS, 1), lambda c: (0, 0))
    chunk128 = pl.BlockSpec((ln, LANE), lambda c: (rev(c), 0))
    return pl.pallas_call(
        body, name="ssd_bwd",
        out_shape=(jax.ShapeDtypeStruct((s, D_XBC), F32), jax.ShapeDtypeStruct((s, LANE), F32),
                   jax.ShapeDtypeStruct((N_HEADS, s), F32), jax.ShapeDtypeStruct((s, LANE), F32),
                   jax.ShapeDtypeStruct((1, D_SSM), F32)),
        grid=(nc,),
        in_specs=[pl.BlockSpec((ln, D_XBC), lambda c: (rev(c), 0)), chunk128,
                  pl.BlockSpec((N_HEADS, ln), lambda c: (0, rev(c))), row(LANE), col16, row(LANE), col16,
                  row(D_SSM), pl.BlockSpec((1, N_PAIRS, LANE, D_STATE), lambda c: (rev(c), 0, 0, 0)),
                  pl.BlockSpec((ln, D_SSM), lambda c: (rev(c), 0))],
        out_specs=(pl.BlockSpec((ln, D_XBC), lambda c: (rev(c), 0)), chunk128,
                   pl.BlockSpec((N_HEADS, ln), lambda c: (0, rev(c))), chunk128, row(D_SSM)),
        scratch_shapes=[pltpu.VMEM((N_PAIRS, LANE, D_STATE), F32)],
        compiler_params=_cparams(("arbitrary",)),
    )(act, dtraw, dtt, bias, biast, arow, acol, dskip, states, dy)


def _dt_bwd(dtraw, bias, arow, dld_col, dld_row_t, ddt_col):
    s = dtraw.shape[0]
    tm = min(512, s)

    def body(raw_ref, bias_ref, a_ref, dc_ref, dr_ref, dd_ref, out_ref, sums_ref):
        @pl.when(pl.program_id(0) == 0)
        def _():
            sums_ref[...] = jnp.zeros_like(sums_ref)

        raw = raw_ref[...] + bias_ref[...]
        dld = dc_ref[...] + dr_ref[...]
        ddt = dld * a_ref[...] + dd_ref[...]
        draw = ddt * _sigmoid(raw)
        out_ref[...] = draw.astype(BF16)
        sums_ref[0:1, :] += jnp.sum(draw, axis=0, keepdims=True)
        sums_ref[1:2, :] += jnp.sum(dld * _softplus(raw), axis=0, keepdims=True)

    tile = pl.BlockSpec((tm, LANE), lambda i: (i, 0))
    row = pl.BlockSpec((1, LANE), lambda i: (0, 0))
    return pl.pallas_call(
        body, name="dt_bwd",
        out_shape=(jax.ShapeDtypeStruct((s, LANE), BF16), jax.ShapeDtypeStruct((2, LANE), F32)), grid=(s // tm,),
        in_specs=[tile, row, row, tile, tile, tile], out_specs=(tile, pl.BlockSpec((2, LANE), lambda i: (0, 0))),
        compiler_params=_cparams(("arbitrary",)),
    )(dtraw, bias, arow, dld_col, dld_row_t, ddt_col)


def _sum8(parts):
    nb, n = parts.shape

    def body(p_ref, o_ref):
        acc = p_ref[0:1, :]
        for b in range(1, nb):
            acc = acc + p_ref[b:b + 1, :]
        o_ref[...] = acc

    return pl.pallas_call(body, name="sum8", out_shape=jax.ShapeDtypeStruct((1, n), F32),
                          compiler_params=_cparams())(parts)


def _outer8(act_t, dmod):
    d, nb = act_t.shape
    n = dmod.shape[1]

    def body(a_ref, m_ref, o_ref):
        acc = a_ref[:, 0:1] * m_ref[0:1, :]
        for b in range(1, nb):
            acc = acc + a_ref[:, b:b + 1] * m_ref[b:b + 1, :]
        o_ref[...] = acc

    return pl.pallas_call(body, name="outer8", out_shape=jax.ShapeDtypeStruct((d, n), F32),
                          compiler_params=_cparams())(act_t, dmod)


def _pad_lanes(v, width=LANE):
    return jnp.pad(v, ((0, 0), (0, width - v.shape[1])))


def kernel(x, c, w_ada, b_ada, norm_in_gain, w_in, conv_w, conv_b, dt_bias, a_log, d_skip, sb_norm_gain, ssm_norm_gain, w_out, norm_f_gain, loss_target, m_w_ada, m_b_ada, m_norm_in_gain, m_w_in, m_conv_w, m_conv_b, m_dt_bias, m_a_log, m_d_skip, m_sb_norm_gain, m_ssm_norm_gain, m_w_out, m_norm_f_gain, v_w_ada, v_b_ada, v_norm_in_gain, v_w_in, v_conv_w, v_conv_b, v_dt_bias, v_a_log, v_d_skip, v_sb_norm_gain, v_ssm_norm_gain, v_w_out, v_norm_f_gain):
    ax, ay, ac_ = _coords()
    chip = 2 * ax + ay
    me = 2 * chip + ac_
    my_c = jnp.reshape(ac_, (1,)).astype(jnp.int32)
    x2d, tgt = x[0], loss_target[0]
    s = x2d.shape[0]
    ada_cols = w_ada.shape[2]
    cw_cols = conv_w.shape[2]
    in_cols = w_in.shape[2]
    out_rows = w_out.shape[1]

    small = jnp.concatenate([c, conv_w[0].reshape(1, CONV_K * cw_cols)], axis=1)
    b_ada_shard = lax.dynamic_slice_in_dim(b_ada, chip * ada_cols, ada_cols, axis=1)
    w_in_mine, w_out_mine = w_in[0].T.astype(BF16), w_out[0].astype(BF16)
    small_rows, mod_rows, c_act_all, w_in_all = _front(jnp.broadcast_to(small, (8, small.shape[1])), w_ada[0],
                                                       b_ada_shard, w_in_mine)
    small_all = small_rows[0::8]
    conv_w_full = (small_all[0::2, D_MODEL:].reshape(N_CHIPS, CONV_K, cw_cols)
                   .transpose(1, 0, 2).reshape(CONV_K, D_XBC))
    mod_all = mod_rows.reshape(N_DEV, N_DEV, ada_cols)[0::2]
    mod = lax.dynamic_index_in_dim(mod_all, me, axis=1, keepdims=False).reshape(1, 3 * D_MODEL)
    shift, scale, gate = mod[:, :D_MODEL], mod[:, D_MODEL:2 * D_MODEL], mod[:, 2 * D_MODEL:]

    w_in_all = lax.dynamic_update_slice(w_in_all, w_in_mine[None], (chip, 0, 0))
    w_in_t = w_in_all.reshape(D_PROJ, D_MODEL)
    w_zs_t = w_in_t[ZS_LO:]
    w_dt_t = jnp.pad(w_in_t[DT_LO:ZS_LO], ((0, LANE - N_HEADS), (0, 0)))

    h = _rms_mod_fwd(x2d, norm_in_gain, scale, shift)
    proj, w_out_all = _matmul_sum([(h, D_MODEL, 0, w_in_t, 0)], BF16, "in_proj_and_gather_w_out", 1024, 512,
                                  exchange=[jnp.broadcast_to(w_out_mine[None], (N_CHIPS,) + w_out_mine.shape)],
                                  trans_b=True, n_out=D_MAIN)
    w_out_all = lax.dynamic_update_slice(w_out_all, w_out_mine[None], (chip, 0, 0))
    w_out_full = w_out_all.reshape(N_CHIPS * out_rows, D_MODEL)
    proj_zs = _matmul(h, w_zs_t, BF16, "in_proj_zs", "nt", 1024, 512, 1024)
    dtraw = _matmul(h, w_dt_t, F32, "in_proj_dt", "nt", 1024, LANE, 1024)
    o_attn, lsum = _attn_fwd(proj)
    y_attn = _gated_norm_fwd(o_attn, proj, OFF_ZA, sb_norm_gain, False, "attn_gate_fwd")
    act = _conv_fwd(proj, conv_w_full, conv_b)
    a_neg = -jnp.exp(a_log)
    arow, acol = _pad_lanes(a_neg), a_neg.reshape(N_HEADS, 1)
    bias_row, bias_col = _pad_lanes(dt_bias), dt_bias.reshape(N_HEADS, 1)
    dtt = dtraw[:, :N_HEADS].T
    dskip_row = jnp.repeat(d_skip, HEAD_DIM, axis=1)
    ssd_args = (act, dtraw, dtt, bias_row, bias_col, arow, acol, dskip_row)
    y_ssd, states = _ssd_fwd(*ssd_args)
    y_ssm = _gated_norm_fwd(y_ssd, proj_zs, 0, ssm_norm_gain, True, "ssm_gate_fwd")
    mixed = _matmul_sum([(y_attn, D_ATTN, 0, w_out_full, 0), (y_ssm, D_SSM, 0, w_out_full, 1)], F32, "out_proj",
                        1024, 1024)

    dx2, dmixed, head_sums = _loss_head(x2d, mixed, gate, norm_f_gain.reshape(1, D_MODEL), tgt)
    g_w_out = _matmul_tn_pieces([y_attn, y_ssm], N_CHIPS * out_rows, dmixed, "out_proj_dw", 512)
    d_mix_in = _matmul(dmixed, w_out_full, BF16, "out_proj_dx", "nt", 1024, 1024, 1024)
    d_o, dz_attn, g_sb = _gated_norm_bwd(d_mix_in, 0, o_attn, proj, OFF_ZA, sb_norm_gain, False, "attn_gate_bwd")
    d_y, dz_ssm, g_ssm = _gated_norm_bwd(d_mix_in, 1, y_ssd, proj_zs, 0, ssm_norm_gain, True, "ssm_gate_bwd")
    dq, dk, dv = _attn_bwd(proj, d_o, lsum)
    dact, dld_col, dld_row, ddt_col, dd_cols = _ssd_bwd(*ssd_args, states, d_y)
    dxbc, g_conv_w, g_conv_b = _conv_bwd(proj, conv_w_full, conv_b, dact)
    ddtraw, dt_sums = _dt_bwd(dtraw, bias_row, arow, dld_col, _pad_lanes(dld_row.T), ddt_col)
    g_in_t = _matmul_tn_pieces([dq, dk, dv, dz_attn, dxbc], D_PROJ, h, "in_proj_dw", 256)
    g_in_t = _matmul_tn_rows(g_in_t, D_PROJ, ddtraw, h, DT_LO // LANE, "in_proj_dw_dt", LANE)
    g_zs_t = _matmul(dz_ssm, h, F32, "in_proj_dw_zs", "tn", 512, 1024, 4096)
    g_in_t = lax.dynamic_update_slice(g_in_t, g_zs_t, (ZS_LO, 0))
    dh_terms = [(dq, D_ATTN, 0, w_in_t, 0), (dk, D_ATTN, 0, w_in_t, 1), (dv, D_ATTN, 0, w_in_t, 2),
                (dz_attn, D_ATTN, 0, w_in_t, 3)]
    dh_terms += [(dxbc, 512, j, w_in_t, OFF_XBC // 512 + j) for j in range(D_XBC // 512)]
    dh_terms += [(dz_ssm, D_SSM, 0, w_zs_t, 0), (ddtraw, LANE, 0, w_dt_t, 0)]
    g_in_blocks = g_in_t.reshape(N_CHIPS, in_cols, D_MODEL)
    g_out_blocks = g_w_out.reshape(N_CHIPS, out_rows, D_MODEL)
    land_in, land_out = _send_to_sibling([g_in_blocks, g_out_blocks], "grads_to_sibling")
    chip_in = _add_my_half(g_in_blocks, land_in, my_c, "add_sibling_in")
    chip_out = _add_my_half(g_out_blocks, land_out, my_c, "add_sibling_out")
    dh, slots_in, slots_out = _matmul_sum(dh_terms, F32, "in_proj_dx_and_grads_between_chips", 512, 1024,
                                          exchange=[chip_in, chip_out])
    grad_x, in_sums = _rms_mod_bwd(x2d, dh, dx2, norm_in_gain, scale)

    g_a_log = dt_sums[1:2, :N_HEADS] * a_neg
    g_d_skip = jnp.sum(dd_cols.reshape(N_HEADS, HEAD_DIM), axis=1).reshape(1, N_HEADS)
    dmod = jnp.concatenate([in_sums[0:1], in_sums[1:2], head_sums[2:3]], axis=1)
    loss_part = 0.5 / D_MODEL * jnp.sum(head_sums[0:1], axis=1, keepdims=True)
    pieces = [dmod, in_sums[2:3], g_conv_w.reshape(1, CONV_K * D_XBC), g_conv_b, _pad_lanes(dt_sums[0:1, :N_HEADS]),
              _pad_lanes(g_a_log), _pad_lanes(g_d_skip), g_sb, g_ssm, head_sums[1:2], _pad_lanes(loss_part)]
    widths = [p.shape[1] for p in pieces]
    parts_all = _allgather8(jnp.concatenate(pieces, axis=1), "gather_small_grads")[:, 0, :]
    total = _sum8(parts_all)
    offs = [0]
    for w_ in widths:
        offs.append(offs[-1] + w_)
    tot = [total[:, offs[i]:offs[i + 1]] for i in range(len(pieces))]
    g_b_ada, g_norm_in, g_conv_w_full = tot[0], tot[1], tot[2].reshape(CONV_K, D_XBC)
    g_conv_b_t, g_dt_bias, g_a_log_t, g_d_skip_t = tot[3], tot[4][:, :N_HEADS], tot[5][:, :N_HEADS], tot[6][:, :N_HEADS]
    g_sb_t, g_ssm_t, g_norm_f, loss = tot[7], tot[8], tot[9], tot[10][0, 0]
    g_conv_w_shard = lax.dynamic_slice_in_dim(g_conv_w_full, chip * cw_cols, cw_cols, axis=1)
    dmod_shard = lax.dynamic_slice_in_dim(parts_all[:, :3 * D_MODEL], chip * ada_cols, ada_cols, axis=1)
    g_w_ada = _outer8(c_act_all.T, dmod_shard)

    own = lambda blocks: lax.dynamic_slice_in_dim(blocks, chip, 1, axis=0)
    slots_in = lax.dynamic_update_slice(slots_in, own(chip_in), (chip, 0, 0))
    slots_out = lax.dynamic_update_slice(slots_out, own(chip_out), (chip, 0, 0))
    half_in, half_out = _sum_slots(slots_in, "sum_chips_in"), _sum_slots(slots_out, "sum_chips_out")
    their_in, their_out = _swap_with_sibling([half_in, half_out], "grads_swap_sibling")
    south = ac_ == 0
    both = lambda mine, theirs: jnp.concatenate([jnp.where(south, mine, theirs), jnp.where(south, theirs, mine)],
                                                axis=1)
    g_w_in_t, g_w_out_shard = both(half_in, their_in), both(half_out, their_out)

    d_w_ada, nm_w_ada, nv_w_ada = _adamw(w_ada[0], g_w_ada, m_w_ada[0], v_w_ada[0], "adamw_w_ada")
    d_w_in, nm_w_in, nv_w_in = [r.T for r in _adamw(w_in[0].T, g_w_in_t, m_w_in[0].T, v_w_in[0].T, "adamw_w_in")]
    g_w_in = g_w_in_t.T
    d_w_out, nm_w_out, nv_w_out = _adamw(w_out[0], g_w_out_shard, m_w_out[0], v_w_out[0], "adamw_w_out")
    flat = lambda a: a.reshape(1, -1)
    small_w = [b_ada, norm_in_gain, conv_w[0], conv_b, dt_bias, a_log, d_skip, sb_norm_gain, ssm_norm_gain,
               norm_f_gain]
    small_m = [m_b_ada, m_norm_in_gain, m_conv_w[0], m_conv_b, m_dt_bias, m_a_log, m_d_skip, m_sb_norm_gain,
               m_ssm_norm_gain, m_norm_f_gain]
    small_v = [v_b_ada, v_norm_in_gain, v_conv_w[0], v_conv_b, v_dt_bias, v_a_log, v_d_skip, v_sb_norm_gain,
               v_ssm_norm_gain, v_norm_f_gain]
    small_g = [g_b_ada, g_norm_in, g_conv_w_shard, g_conv_b_t, g_dt_bias, g_a_log_t, g_d_skip_t, g_sb_t, g_ssm_t,
               g_norm_f]
    cat = lambda arrs: jnp.concatenate([flat(a) for a in arrs], axis=1)
    d_small, nm_small, nv_small = _adamw(cat(small_w), cat(small_g), cat(small_m), cat(small_v), "adamw_small")
    sizes = [a.size for a in small_w]
    soffs = [0]
    for n_ in sizes:
        soffs.append(soffs[-1] + n_)

    def split(packed):
        return [packed[0, soffs[i]:soffs[i + 1]].reshape(small_w[i].shape) for i in range(len(small_w))]

    def ordered(big_ada, big_in, big_out, smalls):
        (s_b_ada, s_norm_in, s_conv_w, s_conv_b, s_dt_bias, s_a_log, s_d_skip, s_sb, s_ssm, s_norm_f) = smalls
        return [big_ada[None], s_b_ada, s_norm_in, big_in[None], s_conv_w[None], s_conv_b, s_dt_bias, s_a_log,
                s_d_skip, s_sb, s_ssm, big_out[None], s_norm_f]

    grads = ordered(g_w_ada, g_w_in, g_w_out_shard,
                    [g.reshape(w_.shape) for g, w_ in zip(small_g, small_w)])
    deltas = ordered(d_w_ada, d_w_in, d_w_out, split(d_small))
    new_m = ordered(nm_w_ada, nm_w_in, nm_w_out, split(nm_small))
    new_v = ordered(nv_w_ada, nv_w_in, nv_w_out, split(nv_small))
    return (loss, grad_x[None], *grads, *deltas, *new_m, *new_v)
```

```python
import functools

import jax
import jax.numpy as jnp
from jax import lax
from jax.experimental import pallas as pl
from jax.experimental.pallas import tpu as pltpu

F32, BF16 = jnp.float32, jnp.bfloat16
MESH = pl.DeviceIdType.MESH
HI = lax.Precision.HIGHEST
NN = (((1,), (0,)), ((), ()))
NT = (((1,), (1,)), ((), ()))
TN = (((0,), (0,)), ((), ()))

D_MODEL = 1024
D_ATTN = 1024
D_SSM = 1024
HEAD_DIM = 64
N_HEADS = 16
N_PAIRS = 8
N_GROUPS = 2
D_STATE = 128
D_XBC = 1536
D_PROJ = 6672
D_MAIN = 5632
CONV_K = 4
CHUNK = 128
LANE = 128
N_CHIPS = 4
N_DEV = 8
NORM_EPS = 1e-6
ATTN_SCALE = HEAD_DIM ** -0.5
ATTN_TQ = 512
ATTN_TK = 256
LOG_ZERO = -110.0
ADAM_LR, ADAM_B1, ADAM_B2, ADAM_EPS, ADAM_WD, ADAM_STEP = 0.001, 0.9, 0.999, 1e-08, 0.01, 10
VMEM_LIMIT = 56 * 1024 * 1024

OFF_Q, OFF_K, OFF_V, OFF_ZA, OFF_XBC = 0, 1024, 2048, 3072, 4096
DT_LO = D_MAIN
ZS_LO = DT_LO + N_HEADS


def _cparams(sem=None):
    return pltpu.CompilerParams(dimension_semantics=sem, vmem_limit_bytes=VMEM_LIMIT)


def _sigmoid(x):
    return 1.0 / (1.0 + jnp.exp(-x))


def _softplus(x):
    return jnp.maximum(x, 0.0) + jnp.log(1.0 + jnp.exp(-jnp.abs(x)))


def _coords():
    return lax.axis_index("x"), lax.axis_index("y"), lax.axis_index("c")


def _allgather8(v, name):
    n = v.shape[-1]

    def body(v_ref, out_ref, send_sems, recv_sems, local_sem):
        x, y, c = _coords()
        me = 4 * x + 2 * y + c
        mine = pltpu.make_async_copy(v_ref, out_ref.at[me], local_sem)
        mine.start()
        sends, recvs = [], []
        for j in range(1, N_DEV):
            px = 1 - x if (j >> 2) & 1 else x
            py = 1 - y if (j >> 1) & 1 else y
            pc = 1 - c if j & 1 else c
            peer = (px, py, pc)
            sends.append(pltpu.make_async_remote_copy(
                src_ref=v_ref, dst_ref=out_ref.at[me], send_sem=send_sems.at[j - 1],
                recv_sem=recv_sems.at[j - 1], device_id=peer, device_id_type=MESH))
            recvs.append(pltpu.make_async_remote_copy(
                src_ref=v_ref, dst_ref=out_ref.at[4 * px + 2 * py + pc], send_sem=send_sems.at[j - 1],
                recv_sem=recv_sems.at[j - 1], device_id=peer, device_id_type=MESH))
        for s in sends:
            s.start()
        for r in recvs:
            r.wait_recv()
        for s in sends:
            s.wait_send()
        mine.wait()

    vm = pl.BlockSpec(memory_space=pltpu.VMEM)
    return pl.pallas_call(
        body, name=name, out_shape=jax.ShapeDtypeStruct((N_DEV, 1, n), F32),
        in_specs=[vm], out_specs=vm,
        scratch_shapes=[pltpu.SemaphoreType.DMA((N_DEV - 1,)), pltpu.SemaphoreType.DMA((N_DEV - 1,)),
                        pltpu.SemaphoreType.DMA(())],
    )(v)


def _other_chips(x, y):
    chips = [(1 - x, y), (x, 1 - y), (1 - x, 1 - y)]
    return chips, [2 * cx + cy for cx, cy in chips]


def _half_cols(width, which):
    half = width // 2
    return pl.ds(pl.multiple_of(which * half, half), half)


def _gather_shards(arrs, name):
    n = len(arrs)

    def body(*refs):
        ins, outs = refs[:n], refs[n:2 * n]
        send_sems, recv_sems = refs[2 * n:]
        x, y, c = _coords()
        k = 2 * x + y
        chips, chip_idx = _other_chips(x, y)
        sibling = (x, y, 1 - c)
        sends = []
        for a in range(n):
            mine = _half_cols(arrs[a].shape[-1], c)
            for j in range(3):
                cp = pltpu.make_async_remote_copy(
                    src_ref=ins[a].at[:, mine], dst_ref=outs[a].at[k, :, mine], send_sem=send_sems.at[6 * a + j],
                    recv_sem=recv_sems.at[6 * a + j], device_id=(*chips[j], c), device_id_type=MESH)
                cp.start()
                sends.append(cp)
        for a in range(n):
            mine = _half_cols(arrs[a].shape[-1], c)
            for j in range(3):
                landed = outs[a].at[chip_idx[j], :, mine]
                pltpu.make_async_remote_copy(
                    src_ref=landed, dst_ref=landed, send_sem=send_sems.at[6 * a + j],
                    recv_sem=recv_sems.at[6 * a + j], device_id=(*chips[j], c), device_id_type=MESH).wait_recv()
                fwd = pltpu.make_async_remote_copy(
                    src_ref=landed, dst_ref=landed, send_sem=send_sems.at[6 * a + 3 + j],
                    recv_sem=recv_sems.at[6 * a + 3 + j], device_id=sibling, device_id_type=MESH)
                fwd.start()
                sends.append(fwd)
        for a in range(n):
            theirs = _half_cols(arrs[a].shape[-1], 1 - c)
            for j in range(3):
                landed = outs[a].at[chip_idx[j], :, theirs]
                pltpu.make_async_remote_copy(
                    src_ref=landed, dst_ref=landed, send_sem=send_sems.at[6 * a + 3 + j],
                    recv_sem=recv_sems.at[6 * a + 3 + j], device_id=sibling, device_id_type=MESH).wait_recv()
        for cp in sends:
            cp.wait_send()

    hbm = pl.BlockSpec(memory_space=pl.ANY)
    return pl.pallas_call(
        body, name=name,
        out_shape=tuple(jax.ShapeDtypeStruct((N_CHIPS,) + a.shape, a.dtype) for a in arrs),
        in_specs=[hbm] * n, out_specs=tuple([hbm] * n),
        scratch_shapes=[pltpu.SemaphoreType.DMA((6 * n,)), pltpu.SemaphoreType.DMA((6 * n,))],
    )(*arrs)


def _front(small8, w_ada_shard, b_shard, w_shard):
    n = small8.shape[1]
    cols = w_ada_shard.shape[1]

    def body(small_ref, wa_ref, b_ref, w_ref, all_ref, mod_ref, act_ref, wout_ref, part_ref,
             send_a, recv_a, send_b, recv_b, send_w, recv_w):
        x, y, c = _coords()
        me = 4 * x + 2 * y + c
        k = 2 * x + y
        chips, chip_idx = _other_chips(x, y)
        sibling = (x, y, 1 - c)
        mine = _half_cols(w_shard.shape[-1], c)
        theirs = _half_cols(w_shard.shape[-1], 1 - c)

        def w_copy(j, landed, forward):
            if forward:
                ref = wout_ref.at[chip_idx[j], :, mine] if not landed else wout_ref.at[chip_idx[j], :, theirs]
                return pltpu.make_async_remote_copy(src_ref=ref, dst_ref=ref, send_sem=send_w.at[3 + j],
                                                    recv_sem=recv_w.at[3 + j], device_id=sibling, device_id_type=MESH)
            dst = wout_ref.at[chip_idx[j], :, mine] if landed else wout_ref.at[k, :, mine]
            src = dst if landed else w_ref.at[:, mine]
            return pltpu.make_async_remote_copy(src_ref=src, dst_ref=dst, send_sem=send_w.at[j], recv_sem=recv_w.at[j],
                                                device_id=(*chips[j], c), device_id_type=MESH)

        for j in range(3):
            w_copy(j, False, False).start()

        def gather8(src_ref, dst_ref, send_sems, recv_sems):
            slot = lambda d: pl.ds(pl.multiple_of(8 * d, 8), 8)
            dst_ref[slot(me), :] = src_ref[...]
            sends, recvs = [], []
            for j in range(1, N_DEV):
                px = 1 - x if (j >> 2) & 1 else x
                py = 1 - y if (j >> 1) & 1 else y
                pc = 1 - c if j & 1 else c
                sends.append(pltpu.make_async_remote_copy(
                    src_ref=src_ref, dst_ref=dst_ref.at[slot(me)], send_sem=send_sems.at[j - 1],
                    recv_sem=recv_sems.at[j - 1], device_id=(px, py, pc), device_id_type=MESH))
                recvs.append(pltpu.make_async_remote_copy(
                    src_ref=src_ref, dst_ref=dst_ref.at[slot(4 * px + 2 * py + pc)], send_sem=send_sems.at[j - 1],
                    recv_sem=recv_sems.at[j - 1], device_id=(px, py, pc), device_id_type=MESH))
            for s_ in sends:
                s_.start()
            for r_ in recvs:
                r_.wait_recv()
            for s_ in sends:
                s_.wait_send()

        gather8(small_ref, all_ref, send_a, recv_a)
        sub = lax.broadcasted_iota(jnp.int32, (8, 1), 0)
        cv = jnp.where(sub == 0, all_ref[0:8, :D_MODEL], 0.0)
        for d in range(1, N_DEV):
            cv = jnp.where(sub == d, all_ref[8 * d:8 * d + 8, :D_MODEL], cv)
        act = cv * _sigmoid(cv)
        act_ref[...] = act
        part_ref[...] = jnp.dot(act, wa_ref[...], preferred_element_type=F32, precision=HI) + b_ref[...]
        gather8(part_ref, mod_ref, send_b, recv_b)

        for j in range(3):
            w_copy(j, True, False).wait_recv()
            w_copy(j, False, True).start()
        for j in range(3):
            w_copy(j, True, True).wait_recv()
        for j in range(3):
            w_copy(j, False, False).wait_send()
            w_copy(j, False, True).wait_send()

    vm = pl.BlockSpec(memory_space=pltpu.VMEM)
    hbm = pl.BlockSpec(memory_space=pl.ANY)
    return pl.pallas_call(
        body, name="front_exchanges",
        out_shape=(jax.ShapeDtypeStruct((8 * N_DEV, n), F32), jax.ShapeDtypeStruct((8 * N_DEV, cols), F32),
                   jax.ShapeDtypeStruct((N_DEV, D_MODEL), F32),
                   jax.ShapeDtypeStruct((N_CHIPS,) + w_shard.shape, w_shard.dtype)),
        in_specs=[vm, vm, vm, hbm], out_specs=(vm, vm, vm, hbm),
        scratch_shapes=[pltpu.VMEM((N_DEV, cols), F32)] + [pltpu.SemaphoreType.DMA((N_DEV - 1,))] * 4
        + [pltpu.SemaphoreType.DMA((6,))] * 2,
        compiler_params=_cparams(),
    )(small8, w_ada_shard, b_shard, w_shard)


def _send_to_sibling(arrs, name):
    n = len(arrs)

    def body(*refs):
        ins, outs = refs[:n], refs[n:2 * n]
        send_sems, recv_sems = refs[2 * n:]
        x, y, c = _coords()
        cps = []
        for a in range(n):
            cp = pltpu.make_async_remote_copy(
                src_ref=ins[a].at[:, :, _half_cols(arrs[a].shape[-1], 1 - c)], dst_ref=outs[a],
                send_sem=send_sems.at[a], recv_sem=recv_sems.at[a], device_id=(x, y, 1 - c), device_id_type=MESH)
            cp.start()
            cps.append(cp)
        for cp in cps:
            cp.wait()

    hbm = pl.BlockSpec(memory_space=pl.ANY)
    return pl.pallas_call(
        body, name=name,
        out_shape=tuple(jax.ShapeDtypeStruct(a.shape[:-1] + (a.shape[-1] // 2,), a.dtype) for a in arrs),
        in_specs=[hbm] * n, out_specs=tuple([hbm] * n),
        scratch_shapes=[pltpu.SemaphoreType.DMA((n,)), pltpu.SemaphoreType.DMA((n,))],
    )(*arrs)


def _swap_with_sibling(arrs, name):
    n = len(arrs)

    def body(*refs):
        ins, outs = refs[:n], refs[n:2 * n]
        send_sems, recv_sems = refs[2 * n:]
        x, y, c = _coords()
        cps = []
        for a in range(n):
            cp = pltpu.make_async_remote_copy(
                src_ref=ins[a], dst_ref=outs[a], send_sem=send_sems.at[a], recv_sem=recv_sems.at[a],
                device_id=(x, y, 1 - c), device_id_type=MESH)
            cp.start()
            cps.append(cp)
        for cp in cps:
            cp.wait()

    hbm = pl.BlockSpec(memory_space=pl.ANY)
    return pl.pallas_call(
        body, name=name,
        out_shape=tuple(jax.ShapeDtypeStruct(a.shape, a.dtype) for a in arrs),
        in_specs=[hbm] * n, out_specs=tuple([hbm] * n),
        scratch_shapes=[pltpu.SemaphoreType.DMA((n,)), pltpu.SemaphoreType.DMA((n,))],
    )(*arrs)


def _row_tile(rows, cols, n_arrays):
    budget = VMEM_LIMIT // 2
    t = rows
    while t % 16 == 0 and t * cols * 4 * n_arrays * 2 > budget:
        t //= 2
    return t


def _add_my_half(g, landed, my_c, name):
    nb, r, cdim = g.shape
    half = cdim // 2
    tr = _row_tile(r, half, 3)

    def body(c_ref, g_ref, l_ref, o_ref):
        o_ref[...] = (g_ref[...] + l_ref[...]).astype(BF16)

    spec = pl.BlockSpec((None, tr, half), lambda b, i, c_ref: (b, i, 0))
    return pl.pallas_call(
        body, name=name, out_shape=jax.ShapeDtypeStruct((nb, r, half), BF16),
        grid_spec=pltpu.PrefetchScalarGridSpec(
            num_scalar_prefetch=1, grid=(nb, r // tr),
            in_specs=[pl.BlockSpec((None, tr, half), lambda b, i, c_ref: (b, i, c_ref[0])), spec],
            out_specs=spec),
        compiler_params=_cparams(("parallel", "parallel")),
    )(my_c, g, landed)


def _sum_slots(a, name):
    nb, r, cdim = a.shape
    tr = _row_tile(r, cdim, 4)

    def body(a_ref, o_ref):
        o_ref[...] = ((a_ref[0].astype(F32) + a_ref[1].astype(F32)) + a_ref[2].astype(F32)) + a_ref[3].astype(F32)

    return pl.pallas_call(
        body, name=name, out_shape=jax.ShapeDtypeStruct((r, cdim), F32), grid=(r // tr,),
        in_specs=[pl.BlockSpec((nb, tr, cdim), lambda i: (0, i, 0))],
        out_specs=pl.BlockSpec((tr, cdim), lambda i: (i, 0)),
        compiler_params=_cparams(("parallel",)),
    )(a)


def _adamw(w, g, m, v, name):
    r, cdim = w.shape
    tr = _row_tile(r, cdim, 7)
    tc = cdim
    if tr == r and r > 8:
        while tc % (2 * LANE) == 0 and r * tc * 4 * 7 * 2 > VMEM_LIMIT // 2:
            tc //= 2

    def body(w_ref, g_ref, m_ref, v_ref, d_ref, nm_ref, nv_ref):
        gv = g_ref[...]
        nm = ADAM_B1 * m_ref[...] + (1.0 - ADAM_B1) * gv
        nv = ADAM_B2 * v_ref[...] + (1.0 - ADAM_B2) * (gv * gv)
        m_hat = nm / (1.0 - ADAM_B1 ** ADAM_STEP)
        v_hat = nv / (1.0 - ADAM_B2 ** ADAM_STEP)
        d_ref[...] = -ADAM_LR * (m_hat / (jnp.sqrt(v_hat) + ADAM_EPS) + ADAM_WD * w_ref[...])
        nm_ref[...] = nm
        nv_ref[...] = nv

    spec = pl.BlockSpec((tr, tc), lambda i, j: (i, j))
    shp = jax.ShapeDtypeStruct((r, cdim), F32)
    return pl.pallas_call(
        body, name=name, out_shape=(shp, shp, shp), grid=(r // tr, cdim // tc),
        in_specs=[spec] * 4, out_specs=(spec, spec, spec),
        compiler_params=_cparams(("parallel", "parallel")),
    )(w, g, m, v)


def _matmul(a, b, out_dtype, name, mode, tm, tn, tk, extra=None, n_out=None):
    dims = {"nn": NN, "nt": NT, "tn": TN}[mode]
    if mode == "tn":
        kdim, m = a.shape
    else:
        m, kdim = a.shape
    n = n_out if n_out is not None else (b.shape[0] if mode == "nt" else b.shape[1])
    tm, tn, tk = min(tm, m), min(tn, n), min(tk, kdim)
    nk = kdim // tk
    a_spec = (pl.BlockSpec((tk, tm), lambda i, j, k: (k, i)) if mode == "tn"
              else pl.BlockSpec((tm, tk), lambda i, j, k: (i, k)))
    b_spec = (pl.BlockSpec((tn, tk), lambda i, j, k: (j, k)) if mode == "nt"
              else pl.BlockSpec((tk, tn), lambda i, j, k: (k, j)))
    in_specs, operands = [a_spec, b_spec], [a, b]
    if extra is not None:
        a2, b2 = extra
        k2 = a2.shape[0] if mode == "tn" else a2.shape[1]
        in_specs.append(pl.BlockSpec((k2, tm), lambda i, j, k: (0, i)) if mode == "tn"
                        else pl.BlockSpec((tm, k2), lambda i, j, k: (i, 0)))
        in_specs.append(pl.BlockSpec((tn, k2), lambda i, j, k: (j, 0)) if mode == "nt"
                        else pl.BlockSpec((k2, tn), lambda i, j, k: (0, j)))
        operands += [a2, b2]

    def body_one_block(*refs):
        acc = lax.dot_general(refs[0][...], refs[1][...], dims, preferred_element_type=F32)
        if extra is not None:
            acc += lax.dot_general(refs[2][...], refs[3][...], dims, preferred_element_type=F32)
        refs[-1][...] = acc.astype(out_dtype)

    if nk == 1:
        return pl.pallas_call(
            body_one_block, name=name, out_shape=jax.ShapeDtypeStruct((m, n), out_dtype), grid=(m // tm, n // tn, 1),
            in_specs=in_specs, out_specs=pl.BlockSpec((tm, tn), lambda i, j, k: (i, j)),
            compiler_params=_cparams(("parallel", "parallel", "arbitrary")),
        )(*operands)

    def body(*refs):
        if extra is not None:
            a_ref, b_ref, a2_ref, b2_ref, o_ref, acc_ref = refs
        else:
            a_ref, b_ref, o_ref, acc_ref = refs
        k = pl.program_id(2)

        @pl.when(k == 0)
        def _():
            if extra is not None:
                acc_ref[...] = lax.dot_general(a2_ref[...], b2_ref[...], dims, preferred_element_type=F32)
            else:
                acc_ref[...] = jnp.zeros_like(acc_ref)

        acc_ref[...] += lax.dot_general(a_ref[...], b_ref[...], dims, preferred_element_type=F32)

        @pl.when(k == nk - 1)
        def _():
            o_ref[...] = acc_ref[...].astype(out_dtype)

    return pl.pallas_call(
        body, name=name, out_shape=jax.ShapeDtypeStruct((m, n), out_dtype), grid=(m // tm, n // tn, nk),
        in_specs=in_specs, out_specs=pl.BlockSpec((tm, tn), lambda i, j, k: (i, j)),
        scratch_shapes=[pltpu.VMEM((tm, tn), F32)],
        compiler_params=_cparams(("parallel", "parallel", "arbitrary")),
    )(*operands)


def _matmul_sum(terms, out_dtype, name, tm, tn, exchange=None, trans_b=False, n_out=None):
    m = terms[0][0].shape[0]
    n = n_out if n_out is not None else terms[0][3].shape[0 if trans_b else 1]
    tm, tn = min(tm, m), min(tn, n)
    gm, gn = m // tm, n // tn
    nt = len(terms)
    dims = NT if trans_b else NN
    in_specs, operands = [], []
    for a, ka, ia, b, ib in terms:
        in_specs.append(pl.BlockSpec((tm, ka), functools.partial(lambda i, j, ia: (i, ia), ia=ia)))
        if trans_b:
            in_specs.append(pl.BlockSpec((tn, ka), functools.partial(lambda i, j, ib: (j, ib), ib=ib)))
        else:
            in_specs.append(pl.BlockSpec((ka, tn), functools.partial(lambda i, j, ib: (ib, j), ib=ib)))
        operands += [a, b]
    sent = [] if exchange is None else list(exchange)
    ns = len(sent)
    hbm = pl.BlockSpec(memory_space=pl.ANY)

    def body(*refs):
        o_ref = refs[2 * nt + ns]
        if ns:
            ins, outs = refs[2 * nt:2 * nt + ns], refs[2 * nt + ns + 1:2 * nt + 2 * ns + 1]
            send_sems, recv_sems = refs[2 * nt + 2 * ns + 1:]
            x, y, c = _coords()
            k = 2 * x + y
            chips, chip_idx = _other_chips(x, y)
            step = pl.program_id(0) * gn + pl.program_id(1)

            def copies(a, j, landed):
                dst = outs[a].at[chip_idx[j]] if landed else outs[a].at[k]
                src = dst if landed else ins[a].at[chip_idx[j]]
                return pltpu.make_async_remote_copy(
                    src_ref=src, dst_ref=dst, send_sem=send_sems.at[3 * a + j], recv_sem=recv_sems.at[3 * a + j],
                    device_id=(*chips[j], c), device_id_type=MESH)

            @pl.when(step == 0)
            def _():
                for a in range(ns):
                    for j in range(3):
                        copies(a, j, False).start()

        acc = lax.dot_general(refs[0][...], refs[1][...], dims, preferred_element_type=F32)
        for t in range(1, nt):
            acc += lax.dot_general(refs[2 * t][...], refs[2 * t + 1][...], dims, preferred_element_type=F32)
        o_ref[...] = acc.astype(out_dtype)

        if ns:
            @pl.when(step == gm * gn - 1)
            def _():
                for a in range(ns):
                    for j in range(3):
                        copies(a, j, True).wait_recv()
                for a in range(ns):
                    for j in range(3):
                        copies(a, j, False).wait_send()

    main = jax.ShapeDtypeStruct((m, n), out_dtype)
    tile = pl.BlockSpec((tm, tn), lambda i, j: (i, j))
    if not ns:
        return pl.pallas_call(
            body, name=name, out_shape=main, grid=(gm, gn), in_specs=in_specs, out_specs=tile,
            compiler_params=_cparams(("parallel", "parallel")),
        )(*operands)
    return pl.pallas_call(
        body, name=name, out_shape=(main, *[jax.ShapeDtypeStruct(a.shape, a.dtype) for a in sent]), grid=(gm, gn),
        in_specs=in_specs + [hbm] * ns, out_specs=(tile, *[hbm] * ns),
        scratch_shapes=[pltpu.SemaphoreType.DMA((3 * ns,)), pltpu.SemaphoreType.DMA((3 * ns,))],
        compiler_params=_cparams(("arbitrary", "arbitrary")),
    )(*operands, *sent)


def _matmul_tn_pieces(pieces, rows, b, name, tm):
    kdim, n = b.shape
    tm = min(tm, min(p.shape[1] for p in pieces))
    tiles = [p.shape[1] // tm for p in pieces]
    first = [sum(tiles[:i]) for i in range(len(pieces))]

    def body(*refs):
        b_ref, o_ref = refs[-2:]
        i = pl.program_id(0)
        for p in range(len(pieces)):
            @pl.when(jnp.logical_and(i >= first[p], i < first[p] + tiles[p]))
            def _(p=p):
                o_ref[...] = lax.dot_general(refs[p][...], b_ref[...], TN, preferred_element_type=F32)

    in_specs = [pl.BlockSpec((kdim, tm), functools.partial(lambda i, lo, cnt: (0, jnp.clip(i - lo, 0, cnt - 1)),
                                                           lo=first[p], cnt=tiles[p])) for p in range(len(pieces))]
    return pl.pallas_call(
        body, name=name, out_shape=jax.ShapeDtypeStruct((rows, n), F32), grid=(sum(tiles),),
        in_specs=in_specs + [pl.BlockSpec((kdim, n), lambda i: (0, 0))],
        out_specs=pl.BlockSpec((tm, n), lambda i: (i, 0)),
        compiler_params=_cparams(("arbitrary",)),
    )(*pieces, b)


def _matmul_tn_rows(buf, rows, a, b, row_blk, name, tm):
    kdim, m = a.shape
    n = b.shape[1]
    tm = min(tm, m)

    def body(*refs):
        a_ref, b_ref, o_ref = refs[-3:]
        o_ref[...] = lax.dot_general(a_ref[...], b_ref[...], TN, preferred_element_type=F32)

    in_specs = [pl.BlockSpec((kdim, tm), lambda i: (0, i)), pl.BlockSpec((kdim, n), lambda i: (0, 0))]
    operands = [a, b]
    if buf is not None:
        in_specs.insert(0, pl.BlockSpec(memory_space=pl.ANY))
        operands.insert(0, buf)
    return pl.pallas_call(
        body, name=name, out_shape=jax.ShapeDtypeStruct((rows, n), F32), grid=(m // tm,),
        in_specs=in_specs, out_specs=pl.BlockSpec((tm, n), lambda i: (row_blk + i, 0)),
        input_output_aliases={} if buf is None else {0: 0},
        compiler_params=_cparams(("parallel",)),
    )(*operands)


def _ada_mod(c_all, w_shard, b_shard):
    nb, d = c_all.shape
    cols = w_shard.shape[1]

    def body(c_ref, w_ref, b_ref, mod_ref, act_ref):
        cv = c_ref[...]
        act = cv * _sigmoid(cv)
        act_ref[...] = act
        mod_ref[...] = jnp.dot(act, w_ref[...], preferred_element_type=F32, precision=HI) + b_ref[...]

    return pl.pallas_call(
        body, name="ada_mod",
        out_shape=(jax.ShapeDtypeStruct((nb, cols), F32), jax.ShapeDtypeStruct((nb, d), F32)),
        compiler_params=_cparams(),
    )(c_all, w_shard, b_shard)


def _rms_mod_fwd(x, gain, scale, shift):
    s, d = x.shape
    tm = min(512, s)

    def body(x_ref, g_ref, sc_ref, sh_ref, h_ref):
        xv = x_ref[...]
        r = lax.rsqrt(jnp.mean(xv * xv, axis=-1, keepdims=True) + NORM_EPS)
        h_ref[...] = (xv * r * g_ref[...] * (1.0 + sc_ref[...]) + sh_ref[...]).astype(BF16)

    row = pl.BlockSpec((1, d), lambda i: (0, 0))
    tile = pl.BlockSpec((tm, d), lambda i: (i, 0))
    return pl.pallas_call(
        body, name="rms_mod_fwd", out_shape=jax.ShapeDtypeStruct((s, d), BF16), grid=(s // tm,),
        in_specs=[tile, row, row, row], out_specs=tile, compiler_params=_cparams(("parallel",)),
    )(x, gain, scale, shift)


def _rms_mod_bwd(x, dh, dres, gain, scale):
    s, d = x.shape
    tm = min(512, s)

    def body(x_ref, dh_ref, dres_ref, g_ref, sc_ref, dx_ref, sums_ref):
        @pl.when(pl.program_id(0) == 0)
        def _():
            sums_ref[...] = jnp.zeros_like(sums_ref)

        xv, dhv = x_ref[...], dh_ref[...].astype(F32)
        r = lax.rsqrt(jnp.mean(xv * xv, axis=-1, keepdims=True) + NORM_EPS)
        nrm = xv * r
        g, one_sc = g_ref[...], 1.0 + sc_ref[...]
        dn = dhv * g * one_sc
        dx_ref[...] = r * (dn - nrm * jnp.mean(dn * nrm, axis=-1, keepdims=True)) + dres_ref[...]
        dhn = dhv * nrm
        sums_ref[0:1, :] += jnp.sum(dhv, axis=0, keepdims=True)
        sums_ref[1:2, :] += jnp.sum(dhn * g, axis=0, keepdims=True)
        sums_ref[2:3, :] += jnp.sum(dhn * one_sc, axis=0, keepdims=True)

    row = pl.BlockSpec((1, d), lambda i: (0, 0))
    tile = pl.BlockSpec((tm, d), lambda i: (i, 0))
    return pl.pallas_call(
        body, name="rms_mod_bwd",
        out_shape=(jax.ShapeDtypeStruct((s, d), F32), jax.ShapeDtypeStruct((3, d), F32)), grid=(s // tm,),
        in_specs=[tile, tile, tile, row, row], out_specs=(tile, pl.BlockSpec((3, d), lambda i: (0, 0))),
        compiler_params=_cparams(("arbitrary",)),
    )(x, dh, dres, gain, scale)


def _loss_head(x, mixed, gate, gain_f, target):
    s, d = x.shape
    tm = min(512, s)

    def body(x_ref, mx_ref, gt_ref, gf_ref, t_ref, dx2_ref, dmx_ref, sums_ref):
        @pl.when(pl.program_id(0) == 0)
        def _():
            sums_ref[...] = jnp.zeros_like(sums_ref)

        mx, gt, gf = mx_ref[...], gt_ref[...], gf_ref[...]
        x2 = x_ref[...] + gt * mx
        r = lax.rsqrt(jnp.mean(x2 * x2, axis=-1, keepdims=True) + NORM_EPS)
        nrm = x2 * r
        err = nrm * gf - t_ref[...]
        dyf = err * (1.0 / d)
        dn = dyf * gf
        dx2 = r * (dn - nrm * jnp.mean(dn * nrm, axis=-1, keepdims=True))
        dx2_ref[...] = dx2
        dmx_ref[...] = (dx2 * gt).astype(BF16)
        sums_ref[0:1, :] += jnp.sum(err * err, axis=0, keepdims=True)
        sums_ref[1:2, :] += jnp.sum(dyf * nrm, axis=0, keepdims=True)
        sums_ref[2:3, :] += jnp.sum(dx2 * mx, axis=0, keepdims=True)

    row = pl.BlockSpec((1, d), lambda i: (0, 0))
    tile = pl.BlockSpec((tm, d), lambda i: (i, 0))
    return pl.pallas_call(
        body, name="loss_head",
        out_shape=(jax.ShapeDtypeStruct((s, d), F32), jax.ShapeDtypeStruct((s, d), BF16),
                   jax.ShapeDtypeStruct((3, d), F32)),
        grid=(s // tm,), in_specs=[tile, tile, row, row, tile],
        out_specs=(tile, tile, pl.BlockSpec((3, d), lambda i: (0, 0))),
        compiler_params=_cparams(("arbitrary",)),
    )(x, mixed, gate, gain_f, target)


def _silu_grad(z, sg):
    return sg * (1.0 + z * (1.0 - sg))


def _gated_norm_fwd(o, proj, z_off, gain, gate_inside, name):
    s, d = o.shape
    tm = min(512, s)
    zb = z_off // d

    def body(o_ref, z_ref, g_ref, y_ref):
        z = z_ref[...].astype(F32)
        sz = z * _sigmoid(z)
        u = o_ref[...] * sz if gate_inside else o_ref[...]
        r = lax.rsqrt(jnp.mean(u * u, axis=-1, keepdims=True) + NORM_EPS)
        y = u * r * g_ref[...]
        y_ref[...] = (y if gate_inside else y * sz).astype(BF16)

    tile = pl.BlockSpec((tm, d), lambda i: (i, 0))
    return pl.pallas_call(
        body, name=name, out_shape=jax.ShapeDtypeStruct((s, d), BF16), grid=(s // tm,),
        in_specs=[tile, pl.BlockSpec((tm, d), lambda i: (i, zb)), pl.BlockSpec((1, d), lambda i: (0, 0))],
        out_specs=tile, compiler_params=_cparams(("parallel",)),
    )(o, proj, gain)


def _gated_norm_bwd(dy_all, dy_blk, o, proj, z_off, gain, gate_inside, name):
    s, d = o.shape
    tm = min(512, s)
    zb = z_off // d

    def body(dy_ref, o_ref, z_ref, g_ref, do_ref, dz_ref, dg_ref):
        @pl.when(pl.program_id(0) == 0)
        def _():
            dg_ref[...] = jnp.zeros_like(dg_ref)

        z = z_ref[...].astype(F32)
        sg = _sigmoid(z)
        sz = z * sg
        ov, dy, g = o_ref[...], dy_ref[...].astype(F32), g_ref[...]
        u = ov * sz if gate_inside else ov
        r = lax.rsqrt(jnp.mean(u * u, axis=-1, keepdims=True) + NORM_EPS)
        nrm = u * r
        if gate_inside:
            dg_ref[...] += jnp.sum(dy * nrm, axis=0, keepdims=True)
            dn = dy * g
        else:
            dg_ref[...] += jnp.sum(dy * nrm * sz, axis=0, keepdims=True)
            dn = dy * g * sz
        du = r * (dn - nrm * jnp.mean(dn * nrm, axis=-1, keepdims=True))
        if gate_inside:
            do_ref[...] = du * sz
            dz_ref[...] = (du * ov * _silu_grad(z, sg)).astype(BF16)
        else:
            do_ref[...] = du
            dz_ref[...] = (dy * nrm * g * _silu_grad(z, sg)).astype(BF16)

    tile = pl.BlockSpec((tm, d), lambda i: (i, 0))
    row = pl.BlockSpec((1, d), lambda i: (0, 0))
    return pl.pallas_call(
        body, name=name,
        out_shape=(jax.ShapeDtypeStruct((s, d), F32), jax.ShapeDtypeStruct((s, d), BF16),
                   jax.ShapeDtypeStruct((1, d), F32)),
        grid=(s // tm,),
        in_specs=[pl.BlockSpec((tm, d), lambda i: (i, dy_blk)), tile, pl.BlockSpec((tm, d), lambda i: (i, zb)), row],
        out_specs=(tile, tile, row), compiler_params=_cparams(("arbitrary",)),
    )(dy_all, o, proj, gain)


def _sb_logits(qh, kb):
    z = lax.dot_general(qh, kb, NT, preferred_element_type=F32)
    neg_abs = lax.bitcast_convert_type(lax.bitcast_convert_type(z, jnp.uint32) | jnp.uint32(0x80000000), F32)
    lb = jnp.minimum(z, 0.0) - jnp.log(1.0 + jnp.exp(neg_abs))
    return lb, lb - z


def _attn_consts(tk):
    lane = lax.broadcasted_iota(jnp.int32, (1, LANE), 1)
    row = lax.broadcasted_iota(jnp.int32, (tk, tk), 0)
    col = lax.broadcasted_iota(jnp.int32, (tk, tk), 1)
    return (lane < HEAD_DIM, lane >= HEAD_DIM), row, col


def _band_mask(rows, tk):
    return lax.broadcasted_iota(jnp.int32, (rows, tk), 1) < lax.broadcasted_iota(jnp.int32, (rows, tk), 0)


def _attn_fwd(proj):
    s = proj.shape[0]
    tq, tk = min(ATTN_TQ, s), min(ATTN_TK, s)
    r = tq // tk

    def body(q_ref, k_ref, v_ref, o_ref, l_ref, acc_ref, run_ref):
        i = pl.program_id(1)
        head_mask, row, col = _attn_consts(tk)
        later = (row > col).astype(BF16)
        q = q_ref[...] * ATTN_SCALE
        qh = [jnp.where(m, q, jnp.zeros_like(q)) for m in head_mask]
        acc_ref[...] = jnp.zeros_like(acc_ref)
        run_ref[...] = jnp.zeros_like(run_ref)

        def block(j, lo, hi, band):
            start = pl.multiple_of(j * tk, tk)
            kb = k_ref[pl.ds(start, tk), :]
            vb = v_ref[pl.ds(start, tk), :]
            rows = slice(lo, hi)
            causal = _band_mask(hi - lo, tk) if band else None
            hs = range(2)
            logits = [_sb_logits(qh[h][rows], kb) for h in hs]
            lb = [logits[h][0] for h in hs]
            l1m = [logits[h][1] if causal is None else jnp.where(causal, logits[h][1], 0.0) for h in hs]
            tail = [jnp.dot(l1m[h].astype(BF16), later, preferred_element_type=F32) + run_ref[h, rows] for h in hs]
            w = [jnp.exp(lb[h] + tail[h]) for h in hs]
            if causal is not None:
                w = [jnp.where(causal, w[h], 0.0) for h in hs]
            vh = [jnp.where(head_mask[h], vb, jnp.zeros_like(vb)) for h in hs]
            acc_ref[rows, :] += (jnp.dot(w[0].astype(BF16), vh[0], preferred_element_type=F32)
                                 + jnp.dot(w[1].astype(BF16), vh[1], preferred_element_type=F32))
            for h in hs:
                run_ref[h, rows] += jnp.sum(l1m[h], axis=1, keepdims=True)

        for b in reversed(range(r)):
            block(i * r + b, b * tk, tq, True)
        n_full = i * r
        half = tq // 2

        def more(c):
            return jnp.logical_and(c[0] < n_full, c[1] > LOG_ZERO)

        def step_all(c):
            block(n_full - 1 - c[0], 0, tq, False)
            return c[0] + 1, jnp.max(run_ref[:, half:, :])

        def step_upper(c):
            block(n_full - 1 - c[0], 0, half, False)
            return c[0] + 1, jnp.max(run_ref[:, :half, :])

        seen_all, _ = lax.while_loop(more, step_all, (jnp.int32(0), jnp.max(run_ref[:, half:, :])))
        seen, _ = lax.while_loop(more, step_upper, (seen_all, jnp.max(run_ref[:, :half, :])))
        o_ref[...] = acc_ref[...]
        lane = lax.broadcasted_iota(jnp.int32, (1, LANE), 1)
        first = jnp.where(lane < 3 * HEAD_DIM // 4, n_full - seen, n_full - seen_all).astype(F32)
        l_ref[...] = jnp.where(lane < HEAD_DIM // 2, run_ref[0], jnp.where(lane < HEAD_DIM, first, run_ref[1]))

    kq, kk, kv = OFF_Q // LANE, OFF_K // LANE, OFF_V // LANE
    tile = pl.BlockSpec((tq, LANE), lambda p, i: (i, p))
    return pl.pallas_call(
        body, name="attn_fwd",
        out_shape=(jax.ShapeDtypeStruct((s, D_ATTN), F32), jax.ShapeDtypeStruct((s, D_ATTN), F32)),
        grid=(N_PAIRS, s // tq),
        in_specs=[pl.BlockSpec((tq, LANE), lambda p, i: (i, kq + p)),
                  pl.BlockSpec((s, LANE), lambda p, i: (0, kk + p)),
                  pl.BlockSpec((s, LANE), lambda p, i: (0, kv + p))],
        out_specs=(tile, tile),
        scratch_shapes=[pltpu.VMEM((tq, LANE), F32), pltpu.VMEM((2, tq, 1), F32)],
        compiler_params=_cparams(("parallel", "arbitrary")),
    )(proj, proj, proj)


def _attn_bwd(proj, do, lsum):
    s = proj.shape[0]
    tq, tk = min(ATTN_TQ, s), min(ATTN_TK, s)
    r = tq // tk

    def body(q_ref, k_ref, v_ref, do_ref, l_ref, dq_ref, dk_ref, dv_ref, dqacc_ref, dkacc_ref, dvacc_ref,
             passed_ref, pre_ref):
        i = pl.program_id(1)

        @pl.when(i == 0)
        def _():
            dkacc_ref[...] = jnp.zeros_like(dkacc_ref)
            dvacc_ref[...] = jnp.zeros_like(dvacc_ref)

        head_mask, row, col = _attn_consts(tk)
        later = (row > col).astype(BF16)
        earlier = (row < col).astype(BF16)
        q = q_ref[...] * ATTN_SCALE
        dov = do_ref[...].astype(BF16)
        qh = [jnp.where(m, q, jnp.zeros_like(q)) for m in head_mask]
        doh = [jnp.where(m, dov, jnp.zeros_like(dov)) for m in head_mask]
        lsum_v = l_ref[...]
        lh = [lsum_v[:, 0:1], lsum_v[:, HEAD_DIM:HEAD_DIM + 1]]
        n_full = i * r
        half = tq // 2
        quarter = HEAD_DIM // 4
        first_all = jnp.clip(jnp.max(lsum_v[0:8, 3 * quarter:HEAD_DIM]).astype(jnp.int32), 0, n_full)
        first = jnp.clip(jnp.max(lsum_v[0:8, 2 * quarter:3 * quarter]).astype(jnp.int32), 0, first_all)
        dqacc_ref[...] = jnp.zeros_like(dqacc_ref)
        passed_ref[...] = jnp.zeros_like(passed_ref)
        pre_ref[...] = jnp.zeros_like(pre_ref)

        def block(j, lo, hi, band):
            start = pl.multiple_of(j * tk, tk)
            kb = k_ref[pl.ds(start, tk), :]
            vb = v_ref[pl.ds(start, tk), :]
            rows = slice(lo, hi)
            causal = _band_mask(hi - lo, tk) if band else None
            hs = range(2)
            q_rows = [qh[h][rows] for h in hs]
            do_rows = [doh[h][rows] for h in hs]
            logits = [_sb_logits(q_rows[h], kb) for h in hs]
            lb = [logits[h][0] for h in hs]
            l1m = [logits[h][1] if causal is None else jnp.where(causal, logits[h][1], 0.0) for h in hs]
            da = [lax.dot_general(do_rows[h], vb, NT, preferred_element_type=F32) for h in hs]
            rs = [jnp.sum(l1m[h], axis=1, keepdims=True) for h in hs]
            right = [lh[h][rows] - passed_ref[h, rows] - rs[h] for h in hs]
            for h in hs:
                passed_ref[h, rows] += rs[h]
            tail = [jnp.dot(l1m[h].astype(BF16), later, preferred_element_type=F32) + right[h] for h in hs]
            a = [jnp.exp(lb[h] + tail[h]) for h in hs]
            if causal is not None:
                a = [jnp.where(causal, a[h], 0.0) for h in hs]
            g = [a[h] * da[h] for h in hs]
            pre = [jnp.dot(g[h].astype(BF16), earlier, preferred_element_type=F32) + pre_ref[h, rows] for h in hs]
            for h in hs:
                pre_ref[h, rows] += jnp.sum(g[h], axis=1, keepdims=True)
            dz = [g[h] - jnp.exp(lb[h]) * (g[h] + pre[h]) for h in hs]
            if causal is not None:
                dz = [jnp.where(causal, dz[h], 0.0) for h in hs]
            dzb = [dz[h].astype(BF16) for h in hs]
            kh = [jnp.where(head_mask[h], kb, jnp.zeros_like(kb)) * ATTN_SCALE for h in hs]
            dqacc_ref[rows, :] += (jnp.dot(dzb[0], kh[0], preferred_element_type=F32)
                                   + jnp.dot(dzb[1], kh[1], preferred_element_type=F32))
            dvacc_ref[pl.ds(start, tk), :] += (
                lax.dot_general(a[0].astype(BF16), do_rows[0], TN, preferred_element_type=F32)
                + lax.dot_general(a[1].astype(BF16), do_rows[1], TN, preferred_element_type=F32))
            dkacc_ref[pl.ds(start, tk), :] += (
                lax.dot_general(dzb[0], q_rows[0], TN, preferred_element_type=F32)
                + lax.dot_general(dzb[1], q_rows[1], TN, preferred_element_type=F32))

        def step_upper(j, carry):
            block(j, 0, half, False)
            return carry

        def step_all(j, carry):
            block(j, 0, tq, False)
            return carry

        lax.fori_loop(first, first_all, step_upper, 0)
        lax.fori_loop(first_all, n_full, step_all, 0)
        for b in range(r):
            block(n_full + b, b * tk, tq, True)
        dq_ref[...] = dqacc_ref[...].astype(BF16)

        @pl.when(i == pl.num_programs(1) - 1)
        def _():
            dk_ref[...] = dkacc_ref[...].astype(BF16)
            dv_ref[...] = dvacc_ref[...].astype(BF16)

    kq, kk, kv = OFF_Q // LANE, OFF_K // LANE, OFF_V // LANE
    tile = pl.BlockSpec((tq, LANE), lambda p, i: (i, p))
    full = pl.BlockSpec((s, LANE), lambda p, i: (0, p))
    shp = jax.ShapeDtypeStruct((s, D_ATTN), BF16)
    return pl.pallas_call(
        body, name="attn_bwd", out_shape=(shp, shp, shp), grid=(N_PAIRS, s // tq),
        in_specs=[pl.BlockSpec((tq, LANE), lambda p, i: (i, kq + p)),
                  pl.BlockSpec((s, LANE), lambda p, i: (0, kk + p)),
                  pl.BlockSpec((s, LANE), lambda p, i: (0, kv + p)),
                  tile, tile],
        out_specs=(tile, full, full),
        scratch_shapes=[pltpu.VMEM((tq, LANE), F32), pltpu.VMEM((s, LANE), F32), pltpu.VMEM((s, LANE), F32),
                        pltpu.VMEM((2, tq, 1), F32), pltpu.VMEM((2, tq, 1), F32)],
        compiler_params=_cparams(("parallel", "arbitrary")),
    )(proj, proj, proj, do, lsum)


def _shift_down(u, k, rows):
    return jnp.where(rows >= k, pltpu.roll(u, k, 0), 0.0)


def _shift_up(u, k, rows, s):
    return jnp.where(rows < s - k, pltpu.roll(u, s - k, 0), 0.0)


def _conv_fwd(proj, w, b):
    s = proj.shape[0]
    blk0 = OFF_XBC // LANE

    def body(u_ref, w_ref, b_ref, o_ref):
        u = u_ref[...].astype(F32)
        rows = lax.broadcasted_iota(jnp.int32, (s, 1), 0)
        pre = u * w_ref[CONV_K - 1:CONV_K, :] + b_ref[...]
        for k in range(1, CONV_K):
            pre += _shift_down(u, k, rows) * w_ref[CONV_K - 1 - k:CONV_K - k, :]
        o_ref[...] = pre * _sigmoid(pre)

    return pl.pallas_call(
        body, name="conv_fwd", out_shape=jax.ShapeDtypeStruct((s, D_XBC), F32), grid=(D_XBC // LANE,),
        in_specs=[pl.BlockSpec((s, LANE), lambda j: (0, blk0 + j)), pl.BlockSpec((CONV_K, LANE), lambda j: (0, j)),
                  pl.BlockSpec((1, LANE), lambda j: (0, j))],
        out_specs=pl.BlockSpec((s, LANE), lambda j: (0, j)), compiler_params=_cparams(("parallel",)),
    )(proj, w, b)


def _conv_bwd(proj, w, b, dact):
    s = proj.shape[0]
    blk0 = OFF_XBC // LANE

    def body(u_ref, w_ref, b_ref, da_ref, du_ref, dw_ref, db_ref):
        u = u_ref[...].astype(F32)
        rows = lax.broadcasted_iota(jnp.int32, (s, 1), 0)
        shifted = [u] + [_shift_down(u, k, rows) for k in range(1, CONV_K)]
        pre = b_ref[...] + shifted[0] * w_ref[CONV_K - 1:CONV_K, :]
        for k in range(1, CONV_K):
            pre += shifted[k] * w_ref[CONV_K - 1 - k:CONV_K - k, :]
        sg = _sigmoid(pre)
        dpre = da_ref[...] * _silu_grad(pre, sg)
        db_ref[...] = jnp.sum(dpre, axis=0, keepdims=True)
        du = dpre * w_ref[CONV_K - 1:CONV_K, :]
        for k in range(CONV_K):
            dw_ref[CONV_K - 1 - k:CONV_K - k, :] = jnp.sum(dpre * shifted[k], axis=0, keepdims=True)
            if k:
                du += _shift_up(dpre, k, rows, s) * w_ref[CONV_K - 1 - k:CONV_K - k, :]
        du_ref[...] = du.astype(BF16)

    col = pl.BlockSpec((s, LANE), lambda j: (0, j))
    return pl.pallas_call(
        body, name="conv_bwd",
        out_shape=(jax.ShapeDtypeStruct((s, D_XBC), BF16), jax.ShapeDtypeStruct((CONV_K, D_XBC), F32),
                   jax.ShapeDtypeStruct((1, D_XBC), F32)),
        grid=(D_XBC // LANE,),
        in_specs=[pl.BlockSpec((s, LANE), lambda j: (0, blk0 + j)), pl.BlockSpec((CONV_K, LANE), lambda j: (0, j)),
                  pl.BlockSpec((1, LANE), lambda j: (0, j)), col],
        out_specs=(col, pl.BlockSpec((CONV_K, LANE), lambda j: (0, j)), pl.BlockSpec((1, LANE), lambda j: (0, j))),
        compiler_params=_cparams(("parallel",)),
    )(proj, w, b, dact)


def _ssd_decays(dtraw_ref, bias_ref, dtt_ref, biast_ref, arow_ref, acol_ref):
    ln = CHUNK
    dt = _softplus(dtraw_ref[...] + bias_ref[...])
    r = lax.broadcasted_iota(jnp.int32, (ln, ln), 0)
    c = lax.broadcasted_iota(jnp.int32, (ln, ln), 1)
    ac = jnp.dot((r >= c).astype(F32), dt * arow_ref[...], preferred_element_type=F32, precision=HI)
    dtt = _softplus(dtt_ref[...] + biast_ref[...])
    act = jnp.dot(dtt * acol_ref[...], (r <= c).astype(F32), preferred_element_type=F32, precision=HI)
    return dt, ac, act, r >= c


def _pair_cols(m0, v, h0):
    return jnp.where(m0, v[:, h0:h0 + 1], v[:, h0 + 1:h0 + 2])


def _ssd_fwd(act, dtraw, dtt, bias, biast, arow, acol, dskip):
    s = act.shape[0]
    ln = CHUNK
    nc = s // ln

    def body(act_ref, dtraw_ref, dtt_ref, bias_ref, biast_ref, arow_ref, acol_ref, dsk_ref, y_ref, st_ref,
             state_ref):
        @pl.when(pl.program_id(0) == 0)
        def _():
            state_ref[...] = jnp.zeros_like(state_ref)

        dt, ac, act_t, lower = _ssd_decays(dtraw_ref, bias_ref, dtt_ref, biast_ref, arow_ref, acol_ref)
        lane = lax.broadcasted_iota(jnp.int32, (1, LANE), 1)
        m0 = lane < HEAD_DIM
        for g in range(N_GROUPS):
            bg32 = act_ref[:, D_SSM + g * D_STATE:D_SSM + (g + 1) * D_STATE]
            bg, bg_t = bg32.astype(BF16), bg32.T.astype(BF16)
            cg = act_ref[:, D_SSM + (N_GROUPS + g) * D_STATE:D_SSM + (N_GROUPS + g + 1) * D_STATE].astype(BF16)
            cb = lax.dot_general(cg, bg, NT, preferred_element_type=F32)
            for p in range(g * 4, g * 4 + 4):
                h0 = 2 * p
                xp = act_ref[:, p * LANE:(p + 1) * LANE]
                xdt = xp * _pair_cols(m0, dt, h0)
                acp = _pair_cols(m0, ac, h0)
                last = acp[ln - 1:ln, :]
                y = xp * dsk_ref[:, p * LANE:(p + 1) * LANE]
                for hh in range(2):
                    h = h0 + hh
                    dm = jnp.exp(jnp.where(lower, ac[:, h:h + 1] - act_t[h:h + 1, :], -jnp.inf))
                    mask = m0 if hh == 0 else jnp.logical_not(m0)
                    y += jnp.dot((cb * dm).astype(BF16), jnp.where(mask, xdt, 0.0).astype(BF16),
                                 preferred_element_type=F32)
                prev = state_ref[p]
                st_ref[0, p] = prev
                y += jnp.dot(cg, prev.astype(BF16), preferred_element_type=F32) * jnp.exp(acp)
                y_ref[:, p * LANE:(p + 1) * LANE] = y
                cs = jnp.dot(bg_t, (xdt * jnp.exp(last - acp)).astype(BF16), preferred_element_type=F32)
                state_ref[p] = prev * jnp.exp(last) + cs

    row = lambda w: pl.BlockSpec((1, w), lambda c: (0, 0))
    return pl.pallas_call(
        body, name="ssd_fwd",
        out_shape=(jax.ShapeDtypeStruct((s, D_SSM), F32),
                   jax.ShapeDtypeStruct((nc, N_PAIRS, LANE, D_STATE), F32)),
        grid=(nc,),
        in_specs=[pl.BlockSpec((ln, D_XBC), lambda c: (c, 0)), pl.BlockSpec((ln, LANE), lambda c: (c, 0)),
                  pl.BlockSpec((N_HEADS, ln), lambda c: (0, c)), row(LANE),
                  pl.BlockSpec((N_HEADS, 1), lambda c: (0, 0)), row(LANE),
                  pl.BlockSpec((N_HEADS, 1), lambda c: (0, 0)), row(D_SSM)],
        out_specs=(pl.BlockSpec((ln, D_SSM), lambda c: (c, 0)),
                   pl.BlockSpec((1, N_PAIRS, LANE, D_STATE), lambda c: (c, 0, 0, 0))),
        scratch_shapes=[pltpu.VMEM((N_PAIRS, LANE, D_STATE), F32)],
        compiler_params=_cparams(("arbitrary",)),
    )(act, dtraw, dtt, bias, biast, arow, acol, dskip)


def _ssd_bwd(act, dtraw, dtt, bias, biast, arow, acol, dskip, states, dy):
    s = act.shape[0]
    ln = CHUNK
    nc = s // ln

    def body(act_ref, dtraw_ref, dtt_ref, bias_ref, biast_ref, arow_ref, acol_ref, dsk_ref, st_ref, dy_ref,
             dact_ref, dldc_ref, dldr_ref, ddt_ref, dd_ref, dstate_ref):
        @pl.when(pl.program_id(0) == 0)
        def _():
            dstate_ref[...] = jnp.zeros_like(dstate_ref)
            dd_ref[...] = jnp.zeros_like(dd_ref)

        dt, ac, act_t, lower = _ssd_decays(dtraw_ref, bias_ref, dtt_ref, biast_ref, arow_ref, acol_ref)
        lane = lax.broadcasted_iota(jnp.int32, (1, LANE), 1)
        m0 = lane < HEAD_DIM
        halves = (m0, jnp.logical_not(m0))
        is_last = lax.broadcasted_iota(jnp.int32, (ln, 1), 0) == ln - 1
        sub = lax.broadcasted_iota(jnp.int32, (N_HEADS, 1), 0)
        earlier_eq = jnp.logical_not(lower) | (lax.broadcasted_iota(jnp.int32, (ln, ln), 0)
                                               == lax.broadcasted_iota(jnp.int32, (ln, ln), 1))
        dac_col = jnp.zeros((ln, LANE), F32)
        dac_row = jnp.zeros((N_HEADS, ln), F32)
        ddt_col = jnp.zeros((ln, LANE), F32)

        def half_sum(v, hh):
            return jnp.sum(jnp.where(halves[hh], v, 0.0), axis=1, keepdims=True)

        for g in range(N_GROUPS):
            b_lo, c_lo = D_SSM + g * D_STATE, D_SSM + (N_GROUPS + g) * D_STATE
            bg32 = act_ref[:, b_lo:b_lo + D_STATE]
            cg32 = act_ref[:, c_lo:c_lo + D_STATE]
            bg, cg = bg32.astype(BF16), cg32.astype(BF16)
            cg_t = cg32.T.astype(BF16)
            cb_t = lax.dot_general(bg, cg, NT, preferred_element_type=F32)
            dcb_t = jnp.zeros((ln, ln), F32)
            dbg = jnp.zeros((ln, D_STATE), F32)
            dcg = jnp.zeros((ln, D_STATE), F32)
            for p in range(g * 4, g * 4 + 4):
                h0 = 2 * p
                cols = slice(p * LANE, (p + 1) * LANE)
                xp = act_ref[:, cols]
                dyp = dy_ref[:, cols]
                dtp = _pair_cols(m0, dt, h0)
                acp = _pair_cols(m0, ac, h0)
                last = acp[ln - 1:ln, :]
                xdt = xp * dtp
                eac = jnp.exp(acp)
                dte = jnp.exp(last - acp)
                dec = jnp.exp(last)
                prev = st_ref[0, p]
                prev_b = prev.astype(BF16)
                ds = dstate_ref[p]
                ds_b = ds.astype(BF16)

                dd_ref[:, cols] += jnp.sum(dyp * xp, axis=0, keepdims=True)
                dx = dyp * dsk_ref[:, cols]
                zoff = jnp.dot(cg, prev_b, preferred_element_type=F32)
                dz_b = (dyp * eac).astype(BF16)
                dcg += lax.dot_general(dz_b, prev_b, NT, preferred_element_type=F32)
                dprev = jnp.dot(cg_t, dz_b, preferred_element_type=F32) + ds * dec
                wmat = jnp.dot(bg, ds_b, preferred_element_type=F32)
                xdte_b = (xdt * dte).astype(BF16)
                dbg += lax.dot_general(xdte_b, ds_b, NT, preferred_element_type=F32)
                dxdt = dte * wmat
                t_dte = xdt * wmat * dte
                t_ac = dyp * zoff * eac - t_dte
                at_last = jnp.sum(ds * prev, axis=0, keepdims=True) * dec + jnp.sum(t_dte, axis=0, keepdims=True)
                for hh in range(2):
                    h = h0 + hh
                    here = lane == h
                    dm_t = jnp.exp(jnp.where(earlier_eq, act_t[h:h + 1, :] - ac[:, h:h + 1], -jnp.inf))
                    mm_t = cb_t * dm_t
                    dyh = jnp.where(halves[hh], dyp, 0.0).astype(BF16)
                    xdth = jnp.where(halves[hh], xdt, 0.0).astype(BF16)
                    dmm_t = lax.dot_general(xdth, dyh, NT, preferred_element_type=F32)
                    dxdt += jnp.dot(mm_t.astype(BF16), dyh, preferred_element_type=F32)
                    gm_t = dmm_t * mm_t
                    dcb_t += dmm_t * dm_t
                    dac_col += jnp.where(here, half_sum(t_ac, hh) - jnp.sum(gm_t, axis=1, keepdims=True), 0.0)
                    dac_col += jnp.where(jnp.logical_and(is_last, here), half_sum(at_last, hh), 0.0)
                    dac_row += jnp.where(sub == h, jnp.sum(gm_t, axis=0, keepdims=True), 0.0)
                    ddt_col += jnp.where(here, half_sum(dxdt * xp, hh), 0.0)
                dact_ref[:, cols] = dx + dxdt * dtp
                dstate_ref[p] = dprev
            dcb_tb = dcb_t.astype(BF16)
            dact_ref[:, b_lo:b_lo + D_STATE] = dbg + jnp.dot(dcb_tb, cg, preferred_element_type=F32)
            dact_ref[:, c_lo:c_lo + D_STATE] = dcg + lax.dot_general(dcb_tb, bg, TN, preferred_element_type=F32)

        r = lax.broadcasted_iota(jnp.int32, (ln, ln), 0)
        c = lax.broadcasted_iota(jnp.int32, (ln, ln), 1)
        dldc_ref[...] = jnp.dot((r <= c).astype(F32), dac_col, preferred_element_type=F32, precision=HI)
        dldr_ref[...] = jnp.dot(dac_row, (r >= c).astype(F32), preferred_element_type=F32, precision=HI)
        ddt_ref[...] = ddt_col

    rev = lambda c: nc - 1 - c
    row = lambda w: pl.BlockSpec((1, w), lambda c: (0, 0))
    col16 = pl.BlockSpec((N_HEADS, 1), lambda c: (0, 0))
    chunk128 = pl.BlockSpec((ln, LANE), lambda c: (rev(c), 0))
    return pl.pallas_call(
        body, name="ssd_bwd",
        out_shape=(jax.ShapeDtypeStruct((s, D_XBC), F32), jax.ShapeDtypeStruct((s, LANE), F32),
                   jax.ShapeDtypeStruct((N_HEADS, s), F32), jax.ShapeDtypeStruct((s, LANE), F32),
                   jax.ShapeDtypeStruct((1, D_SSM), F32)),
        grid=(nc,),
        in_specs=[pl.BlockSpec((ln, D_XBC), lambda c: (rev(c), 0)), chunk128,
                  pl.BlockSpec((N_HEADS, ln), lambda c: (0, rev(c))), row(LANE), col16, row(LANE), col16,
                  row(D_SSM), pl.BlockSpec((1, N_PAIRS, LANE, D_STATE), lambda c: (rev(c), 0, 0, 0)),
                  pl.BlockSpec((ln, D_SSM), lambda c: (rev(c), 0))],
        out_specs=(pl.BlockSpec((ln, D_XBC), lambda c: (rev(c), 0)), chunk128,
                   pl.BlockSpec((N_HEADS, ln), lambda c: (0, rev(c))), chunk128, row(D_SSM)),
        scratch_shapes=[pltpu.VMEM((N_PAIRS, LANE, D_STATE), F32)],
        compiler_params=_cparams(("arbitrary",)),
    )(act, dtraw, dtt, bias, biast, arow, acol, dskip, states, dy)


def _dt_bwd(dtraw, bias, arow, dld_col, dld_row_t, ddt_col):
    s = dtraw.shape[0]
    tm = min(512, s)

    def body(raw_ref, bias_ref, a_ref, dc_ref, dr_ref, dd_ref, out_ref, sums_ref):
        @pl.when(pl.program_id(0) == 0)
        def _():
            sums_ref[...] = jnp.zeros_like(sums_ref)

        raw = raw_ref[...] + bias_ref[...]
        dld = dc_ref[...] + dr_ref[...]
        ddt = dld * a_ref[...] + dd_ref[...]
        draw = ddt * _sigmoid(raw)
        out_ref[...] = draw.astype(BF16)
        sums_ref[0:1, :] += jnp.sum(draw, axis=0, keepdims=True)
        sums_ref[1:2, :] += jnp.sum(dld * _softplus(raw), axis=0, keepdims=True)

    tile = pl.BlockSpec((tm, LANE), lambda i: (i, 0))
    row = pl.BlockSpec((1, LANE), lambda i: (0, 0))
    return pl.pallas_call(
        body, name="dt_bwd",
        out_shape=(jax.ShapeDtypeStruct((s, LANE), BF16), jax.ShapeDtypeStruct((2, LANE), F32)), grid=(s // tm,),
        in_specs=[tile, row, row, tile, tile, tile], out_specs=(tile, pl.BlockSpec((2, LANE), lambda i: (0, 0))),
        compiler_params=_cparams(("arbitrary",)),
    )(dtraw, bias, arow, dld_col, dld_row_t, ddt_col)


def _sum8(parts):
    nb, n = parts.shape

    def body(p_ref, o_ref):
        acc = p_ref[0:1, :]
        for b in range(1, nb):
            acc = acc + p_ref[b:b + 1, :]
        o_ref[...] = acc

    return pl.pallas_call(body, name="sum8", out_shape=jax.ShapeDtypeStruct((1, n), F32),
                          compiler_params=_cparams())(parts)


def _outer8(act_t, dmod):
    d, nb = act_t.shape
    n = dmod.shape[1]

    def body(a_ref, m_ref, o_ref):
        acc = a_ref[:, 0:1] * m_ref[0:1, :]
        for b in range(1, nb):
            acc = acc + a_ref[:, b:b + 1] * m_ref[b:b + 1, :]
        o_ref[...] = acc

    return pl.pallas_call(body, name="outer8", out_shape=jax.ShapeDtypeStruct((d, n), F32),
                          compiler_params=_cparams())(act_t, dmod)


def _pad_lanes(v, width=LANE):
    return jnp.pad(v, ((0, 0), (0, width - v.shape[1])))


def kernel(x, c, w_ada, b_ada, norm_in_gain, w_in, conv_w, conv_b, dt_bias, a_log, d_skip, sb_norm_gain, ssm_norm_gain, w_out, norm_f_gain, loss_target, m_w_ada, m_b_ada, m_norm_in_gain, m_w_in, m_conv_w, m_conv_b, m_dt_bias, m_a_log, m_d_skip, m_sb_norm_gain, m_ssm_norm_gain, m_w_out, m_norm_f_gain, v_w_ada, v_b_ada, v_norm_in_gain, v_w_in, v_conv_w, v_conv_b, v_dt_bias, v_a_log, v_d_skip, v_sb_norm_gain, v_ssm_norm_gain, v_w_out, v_norm_f_gain):
    ax, ay, ac_ = _coords()
    chip = 2 * ax + ay
    me = 2 * chip + ac_
    my_c = jnp.reshape(ac_, (1,)).astype(jnp.int32)
    x2d, tgt = x[0], loss_target[0]
    s = x2d.shape[0]
    ada_cols = w_ada.shape[2]
    cw_cols = conv_w.shape[2]
    in_cols = w_in.shape[2]
    out_rows = w_out.shape[1]

    small = jnp.concatenate([c, conv_w[0].reshape(1, CONV_K * cw_cols)], axis=1)
    b_ada_shard = lax.dynamic_slice_in_dim(b_ada, chip * ada_cols, ada_cols, axis=1)
    w_in_mine, w_out_mine = w_in[0].T.astype(BF16), w_out[0].astype(BF16)
    small_rows, mod_rows, c_act_all, w_in_all = _front(jnp.broadcast_to(small, (8, small.shape[1])), w_ada[0],
                                                       b_ada_shard, w_in_mine)
    small_all = small_rows[0::8]
    conv_w_full = (small_all[0::2, D_MODEL:].reshape(N_CHIPS, CONV_K, cw_cols)
                   .transpose(1, 0, 2).reshape(CONV_K, D_XBC))
    mod_all = mod_rows.reshape(N_DEV, N_DEV, ada_cols)[0::2]
    mod = lax.dynamic_index_in_dim(mod_all, me, axis=1, keepdims=False).reshape(1, 3 * D_MODEL)
    shift, scale, gate = mod[:, :D_MODEL], mod[:, D_MODEL:2 * D_MODEL], mod[:, 2 * D_MODEL:]

    w_in_all = lax.dynamic_update_slice(w_in_all, w_in_mine[None], (chip, 0, 0))
    w_in_t = w_in_all.reshape(D_PROJ, D_MODEL)
    w_zs_t = w_in_t[ZS_LO:]
    w_dt_t = jnp.pad(w_in_t[DT_LO:ZS_LO], ((0, LANE - N_HEADS), (0, 0)))

    h = _rms_mod_fwd(x2d, norm_in_gain, scale, shift)
    proj, w_out_all = _matmul_sum([(h, D_MODEL, 0, w_in_t, 0)], BF16, "in_proj_and_gather_w_out", 1024, 512,
                                  exchange=[jnp.broadcast_to(w_out_mine[None], (N_CHIPS,) + w_out_mine.shape)],
                                  trans_b=True, n_out=D_MAIN)
    w_out_all = lax.dynamic_update_slice(w_out_all, w_out_mine[None], (chip, 0, 0))
    w_out_full = w_out_all.reshape(N_CHIPS * out_rows, D_MODEL)
    proj_zs = _matmul(h, w_zs_t, BF16, "in_proj_zs", "nt", 1024, 512, 1024)
    dtraw = _matmul(h, w_dt_t, F32, "in_proj_dt", "nt", 1024, LANE, 1024)
    o_attn, lsum = _attn_fwd(proj)
    y_attn = _gated_norm_fwd(o_attn, proj, OFF_ZA, sb_norm_gain, False, "attn_gate_fwd")
    act = _conv_fwd(proj, conv_w_full, conv_b)
    a_neg = -jnp.exp(a_log)
    arow, acol = _pad_lanes(a_neg), a_neg.reshape(N_HEADS, 1)
    bias_row, bias_col = _pad_lanes(dt_bias), dt_bias.reshape(N_HEADS, 1)
    dtt = dtraw[:, :N_HEADS].T
    dskip_row = jnp.repeat(d_skip, HEAD_DIM, axis=1)
    ssd_args = (act, dtraw, dtt, bias_row, bias_col, arow, acol, dskip_row)
    y_ssd, states = _ssd_fwd(*ssd_args)
    y_ssm = _gated_norm_fwd(y_ssd, proj_zs, 0, ssm_norm_gain, True, "ssm_gate_fwd")
    mixed = _matmul_sum([(y_attn, D_ATTN, 0, w_out_full, 0), (y_ssm, D_SSM, 0, w_out_full, 1)], F32, "out_proj",
                        1024, 1024)

    dx2, dmixed, head_sums = _loss_head(x2d, mixed, gate, norm_f_gain.reshape(1, D_MODEL), tgt)
    g_w_out = _matmul_tn_pieces([y_attn, y_ssm], N_CHIPS * out_rows, dmixed, "out_proj_dw", 512)
    d_mix_in = _matmul(dmixed, w_out_full, BF16, "out_proj_dx", "nt", 1024, 1024, 1024)
    d_o, dz_attn, g_sb = _gated_norm_bwd(d_mix_in, 0, o_attn, proj, OFF_ZA, sb_norm_gain, False, "attn_gate_bwd")
    d_y, dz_ssm, g_ssm = _gated_norm_bwd(d_mix_in, 1, y_ssd, proj_zs, 0, ssm_norm_gain, True, "ssm_gate_bwd")
    dq, dk, dv = _attn_bwd(proj, d_o, lsum)
    dact, dld_col, dld_row, ddt_col, dd_cols = _ssd_bwd(*ssd_args, states, d_y)
    dxbc, g_conv_w, g_conv_b = _conv_bwd(proj, conv_w_full, conv_b, dact)
    ddtraw, dt_sums = _dt_bwd(dtraw, bias_row, arow, dld_col, _pad_lanes(dld_row.T), ddt_col)
    g_in_t = _matmul_tn_pieces([dq, dk, dv, dz_attn, dxbc], D_PROJ, h, "in_proj_dw", 256)
    g_in_t = _matmul_tn_rows(g_in_t, D_PROJ, ddtraw, h, DT_LO // LANE, "in_proj_dw_dt", LANE)
    g_zs_t = _matmul(dz_ssm, h, F32, "in_proj_dw_zs", "tn", 512, 1024, 4096)
    g_in_t = lax.dynamic_update_slice(g_in_t, g_zs_t, (ZS_LO, 0))
    dh_terms = [(dq, D_ATTN, 0, w_in_t, 0), (dk, D_ATTN, 0, w_in_t, 1), (dv, D_ATTN, 0, w_in_t, 2),
                (dz_attn, D_ATTN, 0, w_in_t, 3)]
    dh_terms += [(dxbc, 512, j, w_in_t, OFF_XBC // 512 + j) for j in range(D_XBC // 512)]
    dh_terms += [(dz_ssm, D_SSM, 0, w_zs_t, 0), (ddtraw, LANE, 0, w_dt_t, 0)]
    g_in_blocks = g_in_t.reshape(N_CHIPS, in_cols, D_MODEL)
    g_out_blocks = g_w_out.reshape(N_CHIPS, out_rows, D_MODEL)
    land_in, land_out = _send_to_sibling([g_in_blocks, g_out_blocks], "grads_to_sibling")
    chip_in = _add_my_half(g_in_blocks, land_in, my_c, "add_sibling_in")
    chip_out = _add_my_half(g_out_blocks, land_out, my_c, "add_sibling_out")
    dh, slots_in, slots_out = _matmul_sum(dh_terms, BF16, "in_proj_dx_and_grads_between_chips", 512, 1024,
                                          exchange=[chip_in, chip_out])
    grad_x, in_sums = _rms_mod_bwd(x2d, dh, dx2, norm_in_gain, scale)

    g_a_log = dt_sums[1:2, :N_HEADS] * a_neg
    g_d_skip = jnp.sum(dd_cols.reshape(N_HEADS, HEAD_DIM), axis=1).reshape(1, N_HEADS)
    dmod = jnp.concatenate([in_sums[0:1], in_sums[1:2], head_sums[2:3]], axis=1)
    loss_part = 0.5 / D_MODEL * jnp.sum(head_sums[0:1], axis=1, keepdims=True)
    pieces = [dmod, in_sums[2:3], g_conv_w.reshape(1, CONV_K * D_XBC), g_conv_b, _pad_lanes(dt_sums[0:1, :N_HEADS]),
              _pad_lanes(g_a_log), _pad_lanes(g_d_skip), g_sb, g_ssm, head_sums[1:2], _pad_lanes(loss_part)]
    widths = [p.shape[1] for p in pieces]
    parts_all = _allgather8(jnp.concatenate(pieces, axis=1), "gather_small_grads")[:, 0, :]
    total = _sum8(parts_all)
    offs = [0]
    for w_ in widths:
        offs.append(offs[-1] + w_)
    tot = [total[:, offs[i]:offs[i + 1]] for i in range(len(pieces))]
    g_b_ada, g_norm_in, g_conv_w_full = tot[0], tot[1], tot[2].reshape(CONV_K, D_XBC)
    g_conv_b_t, g_dt_bias, g_a_log_t, g_d_skip_t = tot[3], tot[4][:, :N_HEADS], tot[5][:, :N_HEADS], tot[6][:, :N_HEADS]
    g_sb_t, g_ssm_t, g_norm_f, loss = tot[7], tot[8], tot[9], tot[10][0, 0]
    g_conv_w_shard = lax.dynamic_slice_in_dim(g_conv_w_full, chip * cw_cols, cw_cols, axis=1)
    dmod_shard = lax.dynamic_slice_in_dim(parts_all[:, :3 * D_MODEL], chip * ada_cols, ada_cols, axis=1)
    g_w_ada = _outer8(c_act_all.T, dmod_shard)

    own = lambda blocks: lax.dynamic_slice_in_dim(blocks, chip, 1, axis=0)
    slots_in = lax.dynamic_update_slice(slots_in, own(chip_in), (chip, 0, 0))
    slots_out = lax.dynamic_update_slice(slots_out, own(chip_out), (chip, 0, 0))
    half_in, half_out = _sum_slots(slots_in, "sum_chips_in"), _sum_slots(slots_out, "sum_chips_out")
    their_in, their_out = _swap_with_sibling([half_in, half_out], "grads_swap_sibling")
    south = ac_ == 0
    both = lambda mine, theirs: jnp.concatenate([jnp.where(south, mine, theirs), jnp.where(south, theirs, mine)],
                                                axis=1)
    g_w_in_t, g_w_out_shard = both(half_in, their_in), both(half_out, their_out)

    d_w_ada, nm_w_ada, nv_w_ada = _adamw(w_ada[0], g_w_ada, m_w_ada[0], v_w_ada[0], "adamw_w_ada")
    d_w_in, nm_w_in, nv_w_in = [r.T for r in _adamw(w_in[0].T, g_w_in_t, m_w_in[0].T, v_w_in[0].T, "adamw_w_in")]
    g_w_in = g_w_in_t.T
    d_w_out, nm_w_out, nv_w_out = _adamw(w_out[0], g_w_out_shard, m_w_out[0], v_w_out[0], "adamw_w_out")
    flat = lambda a: a.reshape(1, -1)
    small_w = [b_ada, norm_in_gain, conv_w[0], conv_b, dt_bias, a_log, d_skip, sb_norm_gain, ssm_norm_gain,
               norm_f_gain]
    small_m = [m_b_ada, m_norm_in_gain, m_conv_w[0], m_conv_b, m_dt_bias, m_a_log, m_d_skip, m_sb_norm_gain,
               m_ssm_norm_gain, m_norm_f_gain]
    small_v = [v_b_ada, v_norm_in_gain, v_conv_w[0], v_conv_b, v_dt_bias, v_a_log, v_d_skip, v_sb_norm_gain,
               v_ssm_norm_gain, v_norm_f_gain]
    small_g = [g_b_ada, g_norm_in, g_conv_w_shard, g_conv_b_t, g_dt_bias, g_a_log_t, g_d_skip_t, g_sb_t, g_ssm_t,
               g_norm_f]
    cat = lambda arrs: jnp.concatenate([flat(a) for a in arrs], axis=1)
    d_small, nm_small, nv_small = _adamw(cat(small_w), cat(small_g), cat(small_m), cat(small_v), "adamw_small")
    sizes = [a.size for a in small_w]
    soffs = [0]
    for n_ in sizes:
        soffs.append(soffs[-1] + n_)

    def split(packed):
        return [packed[0, soffs[i]:soffs[i + 1]].reshape(small_w[i].shape) for i in range(len(small_w))]

    def ordered(big_ada, big_in, big_out, smalls):
        (s_b_ada, s_norm_in, s_conv_w, s_conv_b, s_dt_bias, s_a_log, s_d_skip, s_sb, s_ssm, s_norm_f) = smalls
        return [big_ada[None], s_b_ada, s_norm_in, big_in[None], s_conv_w[None], s_conv_b, s_dt_bias, s_a_log,
                s_d_skip, s_sb, s_ssm, big_out[None], s_norm_f]

    grads = ordered(g_w_ada, g_w_in, g_w_out_shard,
                    [g.reshape(w_.shape) for g, w_ in zip(small_g, small_w)])
    deltas = ordered(d_w_ada, d_w_in, d_w_out, split(d_small))
    new_m = ordered(nm_w_ada, nm_w_in, nm_w_out, split(nm_small))
    new_v = ordered(nv_w_ada, nv_w_in, nv_w_out, split(nv_small))
    return (loss, grad_x[None], *grads, *deltas, *new_m, *new_v)
```
